```python
import jax, jax.numpy as jnp
from jax import lax
import numpy as np

D_MODEL = 1024
BATCH = 8
SEQ = 2048
DEPTH = 1
DEC_BATCH = 32
DEC_SEQ = 16
PAST_LEN = 1024

CHUNK = 64
D_CONV = D_MODEL // 2
D_ATTN = D_MODEL - D_CONV
N_HEADS = 8
HEAD_DIM = D_ATTN // N_HEADS
CONV_WIDTH = 31
D_FF = -(-8 * D_MODEL // (3 * 256)) * 256
IN_COLS = 2 * D_CONV + 3 * D_ATTN + N_HEADS
Q_BLOCK = 128
EPS = 1e-6
NEG_INF = -1e30

kernel_name = 'hybrid_conformer_conv_fox_adaln_step'


def _rmsnorm(x, g):
    x32 = x.astype(jnp.float32)
    y = x32 * lax.rsqrt(jnp.mean(x32 * x32, axis=-1, keepdims=True) + EPS)
    return (y * g.astype(jnp.float32)).astype(x.dtype)


def _modulation(c, w_ada, b_ada):
    mod = jax.nn.silu(c) @ w_ada + b_ada
    return jnp.split(mod[:, None, :], 6, axis=-1)


def _mixer_inputs(x, shift, scale, norm1_g, w_in, b_f, q_norm_g, k_norm_g):
    b, t = x.shape[0], x.shape[1]
    h = _rmsnorm(x, norm1_g) * (1 + scale) + shift
    z = h @ w_in
    u_a, u_g, q, k, v, f = jnp.split(
        z, [D_CONV, 2 * D_CONV, 2 * D_CONV + D_ATTN, 2 * D_CONV + 2 * D_ATTN,
            2 * D_CONV + 3 * D_ATTN], axis=-1)
    u = u_a * jax.nn.sigmoid(u_g)
    q = _rmsnorm(q.reshape(b, t, N_HEADS, HEAD_DIM), q_norm_g)
    k = _rmsnorm(k.reshape(b, t, N_HEADS, HEAD_DIM), k_norm_g)
    v = v.reshape(b, t, N_HEADS, HEAD_DIM)
    logf = jax.nn.log_sigmoid((f + b_f).astype(jnp.float32))
    return u, q, k, v, logf


def _conv_module(u_hist, conv_w, conv_b, ln_g, ln_b):
    y = lax.conv_general_dilated(
        u_hist, conv_w[:, None, :], window_strides=(1,), padding='VALID',
        dimension_numbers=('NWC', 'WIO', 'NWC'), feature_group_count=D_CONV) + conv_b
    y32 = y.astype(jnp.float32)
    mu = jnp.mean(y32, axis=-1, keepdims=True)
    var = jnp.mean(jnp.square(y32 - mu), axis=-1, keepdims=True)
    y32 = (y32 - mu) * lax.rsqrt(var + EPS) * ln_g.astype(jnp.float32) + ln_b.astype(jnp.float32)
    return jax.nn.silu(y32).astype(u_hist.dtype)


def _attend(q, k, v, cum_q, cum_k, q_pos, k_pos):
    s = jnp.einsum('bqhd,bkhd->bhqk', q, k, preferred_element_type=jnp.float32) * (HEAD_DIM ** -0.5)
    s = s + (cum_q[..., :, None] - cum_k[..., None, :])
    mask = k_pos[None, :] <= q_pos[:, None]
    s = jnp.where(mask, s, NEG_INF)
    p = jax.nn.softmax(s, axis=-1)
    return jnp.einsum('bhqk,bkhd->bqhd', p.astype(v.dtype), v)


def _prompt_attention(q, k, v, logf):
    b, s = q.shape[0], q.shape[1]
    nb = s // Q_BLOCK
    cum = jnp.cumsum(logf, axis=1).transpose(0, 2, 1)
    qb = q.reshape(b, nb, Q_BLOCK, N_HEADS, HEAD_DIM).transpose(1, 0, 2, 3, 4)
    cb = cum.reshape(b, N_HEADS, nb, Q_BLOCK).transpose(2, 0, 1, 3)
    pb = jnp.arange(s).reshape(nb, Q_BLOCK)
    k_pos = jnp.arange(s)
    out = lax.map(lambda a: _attend(a[0], k, v, a[1], cum, a[2], k_pos), (qb, cb, pb))
    return out.transpose(1, 0, 2, 3, 4).reshape(b, s, D_ATTN)


def _sample_attention(q, k, v, logf, ck, cv, clf):
    b, t = q.shape[0], q.shape[1]
    p_len = ck.shape[1]
    k_all = jnp.concatenate([ck, k], axis=1)
    v_all = jnp.concatenate([cv, v], axis=1)
    lf_all = jnp.concatenate([clf.astype(jnp.float32), logf], axis=1)
    cum = jnp.cumsum(lf_all, axis=1).transpose(0, 2, 1)
    q_pos = p_len + jnp.arange(t)
    k_pos = jnp.arange(p_len + t)
    out = _attend(q, k_all, v_all, cum[:, :, p_len:], cum, q_pos, k_pos)
    return out.reshape(b, t, D_ATTN)


def _layer(x, c, past, w_ada, b_ada, norm1_g, w_in, b_f, q_norm_g, k_norm_g, conv_w, conv_b,
           conv_ln_g, conv_ln_b, w_out, norm2_g, w_gate, w_up, w_down):
    sh1, sc1, g1, sh2, sc2, g2 = _modulation(c, w_ada, b_ada)
    u, q, k, v, logf = _mixer_inputs(x, sh1, sc1, norm1_g, w_in, b_f, q_norm_g, k_norm_g)
    if past is None:
        u_hist = jnp.pad(u, ((0, 0), (CONV_WIDTH - 1, 0), (0, 0)))
        attn = _prompt_attention(q, k, v, logf)
    else:
        conv_state, ck, cv, clf = past
        u_hist = jnp.concatenate([conv_state.astype(u.dtype), u], axis=1)
        attn = _sample_attention(q, k, v, logf, ck, cv, clf)
    conv_out = _conv_module(u_hist, conv_w, conv_b, conv_ln_g, conv_ln_b)
    new_conv = u_hist[:, u_hist.shape[1] - (CONV_WIDTH - 1):]
    mix = jnp.concatenate([conv_out, attn.astype(conv_out.dtype)], axis=-1) @ w_out
    x = x + g1 * mix
    h = _rmsnorm(x, norm2_g) * (1 + sc2) + sh2
    x = x + g2 * ((jax.nn.silu(h @ w_gate) * (h @ w_up)) @ w_down)
    return x, (k, v, logf, new_conv)


def setup_inputs(seed: int = 0) -> dict:
    key = jax.random.key(seed)
    ks = jax.random.split(key, 24)

    def nrm(k, shape, scale):
        return jax.random.normal(k, shape, jnp.float32) * scale

    L = DEPTH
    return {
        'x_prompt': nrm(ks[0], (BATCH, SEQ, D_MODEL), 1.0),
        'x_sample': nrm(ks[1], (DEC_BATCH, DEC_SEQ, D_MODEL), 1.0),
        'cache_k': nrm(ks[2], (L, DEC_BATCH, PAST_LEN, N_HEADS, HEAD_DIM), 1.0),
        'cache_v': nrm(ks[3], (L, DEC_BATCH, PAST_LEN, N_HEADS, HEAD_DIM), 1.0),
        'cache_logf': jax.nn.log_sigmoid(2.0 + nrm(ks[4], (L, DEC_BATCH, PAST_LEN, N_HEADS), 1.0)),
        'state_conv': nrm(ks[5], (L, DEC_BATCH, CONV_WIDTH - 1, D_CONV), 0.5),
        'c_prompt': nrm(ks[6], (BATCH, D_MODEL), 1.0),
        'c_sample': nrm(ks[7], (DEC_BATCH, D_MODEL), 1.0),
        'w_ada': nrm(ks[8], (L, D_MODEL, 6 * D_MODEL), D_MODEL ** -0.5),
        'b_ada': nrm(ks[9], (L, 6 * D_MODEL), 0.02),
        'norm1_g': 1.0 + nrm(ks[10], (L, D_MODEL), 0.02),
        'w_in': nrm(ks[11], (L, D_MODEL, IN_COLS), D_MODEL ** -0.5),
        'b_f': 2.0 + nrm(ks[12], (L, N_HEADS), 0.5),
        'q_norm_g': 1.0 + nrm(ks[13], (L, N_HEADS, HEAD_DIM), 0.02),
        'k_norm_g': 1.0 + nrm(ks[14], (L, N_HEADS, HEAD_DIM), 0.02),
        'conv_w': nrm(ks[15], (L, CONV_WIDTH, D_CONV), CONV_WIDTH ** -0.5),
        'conv_b': nrm(ks[16], (L, D_CONV), 0.02),
        'conv_ln_g': 1.0 + nrm(ks[17], (L, D_CONV), 0.02),
        'conv_ln_b': nrm(ks[18], (L, D_CONV), 0.02),
        'w_out': nrm(ks[19], (L, D_MODEL, D_MODEL), D_MODEL ** -0.5),
        'norm2_g': 1.0 + nrm(ks[20], (L, D_MODEL), 0.02),
        'w_gate': nrm(ks[21], (L, D_MODEL, D_FF), D_MODEL ** -0.5),
        'w_up': nrm(ks[22], (L, D_MODEL, D_FF), D_MODEL ** -0.5),
        'w_down': nrm(ks[23], (L, D_FF, D_MODEL), D_FF ** -0.5),
    }


def reference(x_prompt, x_sample, cache_k, cache_v, cache_logf, state_conv, c_prompt, c_sample,
              w_ada, b_ada, norm1_g, w_in, b_f, q_norm_g, k_norm_g, conv_w, conv_b, conv_ln_g,
              conv_ln_b, w_out, norm2_g, w_gate, w_up, w_down):
    yp, ys = x_prompt, x_sample
    st_p, st_s = [], []
    for l in range(DEPTH):
        w = (w_ada[l], b_ada[l], norm1_g[l], w_in[l], b_f[l], q_norm_g[l], k_norm_g[l], conv_w[l],
             conv_b[l], conv_ln_g[l], conv_ln_b[l], w_out[l], norm2_g[l], w_gate[l], w_up[l], w_down[l])
        yp, sp = _layer(yp, c_prompt, None, *w)
        ys, ss = _layer(ys, c_sample, (state_conv[l], cache_k[l], cache_v[l], cache_logf[l]), *w)
        st_p.append(sp)
        st_s.append(ss)
    k_prompt = jnp.stack([s[0] for s in st_p])
    v_prompt = jnp.stack([s[1] for s in st_p])
    logf_prompt = jnp.stack([s[2] for s in st_p])
    conv_prompt = jnp.stack([s[3] for s in st_p])
    k_sample = jnp.stack([s[0] for s in st_s])
    v_sample = jnp.stack([s[1] for s in st_s])
    logf_sample = jnp.stack([s[2] for s in st_s])
    conv_sample = jnp.stack([s[3] for s in st_s])
    return (yp, ys, k_prompt, v_prompt, logf_prompt, conv_prompt, k_sample, v_sample, logf_sample, conv_sample)
```

```python
import functools

import jax
import jax.numpy as jnp
from jax import lax
from jax.experimental import pallas as pl
from jax.experimental.pallas import tpu as pltpu

F32 = jnp.float32
BF16 = jnp.bfloat16

N_HEADS = 8
HEAD_DIM = 64
CONV_WIDTH = 31
EPS = 1e-6
NEG_INF = -1e30

LANES = 128
MXU_DIM = 256
VMEM_LIMIT_BYTES = 56 * 1024 * 1024

N_PARTS = 3
N_FEAT = N_PARTS * N_HEADS
HIST = 32
CONV_ROWS = 32

ROW_TILE = 512
Q_TILE = 256
FF_CHUNK = 256


def _dot(a, b):
    return jnp.dot(a, b, preferred_element_type=F32)


def _dot_nt(a, b):
    return lax.dot_general(a, b, (((1,), (1,)), ((), ())), preferred_element_type=F32)


def _split3(x):
    hi = x.astype(BF16).astype(F32)
    r = x - hi
    mid = r.astype(BF16).astype(F32)
    lo = (r - mid).astype(BF16).astype(F32)
    return hi, mid, lo


def _log_sigmoid(x):
    return jnp.minimum(x, 0.0) - jnp.log1p(jnp.exp(-jnp.abs(x)))


def _adaln_rmsnorm(x, g, scale, shift):
    y = x * lax.rsqrt(jnp.mean(x * x, axis=-1, keepdims=True) + EPS)
    return (y * g) * (1.0 + scale) + shift


def _const_spec(shape):
    n = len(shape)
    return pl.BlockSpec(shape, lambda *_: (0,) * n, pipeline_mode=pl.Buffered(1))


def _modulation_kernel(c_ref, w_ref, b_ref, o_ref):
    c = c_ref[...]
    a = (c * jax.nn.sigmoid(c)).astype(BF16)
    o_ref[...] = _dot(a, w_ref[...].astype(BF16)) + b_ref[...]


def _modulation(c, w_ada, b_ada):
    nb, d = c.shape
    n = w_ada.shape[1]
    tn = 768
    return pl.pallas_call(
        _modulation_kernel,
        grid=(n // tn,),
        in_specs=[pl.BlockSpec((nb, d), lambda j: (0, 0)),
                  pl.BlockSpec((d, tn), lambda j: (0, j)),
                  pl.BlockSpec((1, tn), lambda j: (0, j))],
        out_specs=pl.BlockSpec((nb, tn), lambda j: (0, j)),
        out_shape=jax.ShapeDtypeStruct((nb, n), F32),
        compiler_params=pltpu.CompilerParams(dimension_semantics=("arbitrary",),
                                             vmem_limit_bytes=VMEM_LIMIT_BYTES),
        name="modulation",
    )(c, w_ada, b_ada.reshape(1, n))


def _qkvf_feature_major(hb, wqkvf_t, qg_col, kg_col, bf_col):
    d_attn = N_HEADS * HEAD_DIM
    zt = _dot_nt(wqkvf_t, hb)
    r = zt.shape[1]

    def head_rms(z, g_col):
        z3 = z.reshape(N_HEADS, HEAD_DIM, r)
        ms = jnp.mean(z3 * z3, axis=1, keepdims=True)
        return (z3 * lax.rsqrt(ms + EPS)).reshape(d_attn, r) * g_col

    q_t = head_rms(zt[0:d_attn], qg_col)
    k_t = head_rms(zt[d_attn:2 * d_attn], kg_col)
    v_t = zt[2 * d_attn:3 * d_attn]
    lf_t = _log_sigmoid(zt[3 * d_attn:3 * d_attn + N_HEADS] + bf_col)
    return q_t, k_t, v_t, lf_t


def _conv_ln_swish(load_rows, cw_ref, cb, ln_g, ln_b):
    acc = cb + cw_ref[0:1, :] * load_rows(0)
    for k in range(1, CONV_WIDTH):
        acc = acc + cw_ref[k:k + 1, :] * load_rows(k)
    mu = jnp.mean(acc, axis=-1, keepdims=True)
    cen = acc - mu
    var = jnp.mean(cen * cen, axis=-1, keepdims=True)
    y = cen * lax.rsqrt(var + EPS) * ln_g + ln_b
    return y * jax.nn.sigmoid(y)


def _inproj_prompt_kernel(x_ref, sh_ref, sc_ref, n1g_ref, wglu_ref, wqkvf_ref, bf_ref, qg_ref, kg_ref,
                          utri_ref, cw_ref, cb_ref, lng_ref, lnb_ref,
                          kt_ref, vt_ref, lft_ref, cst_ref, q_ref, qb_ref, kbt_ref, co_ref,
                          ubuf_ref, carry_ref):
    t = pl.program_id(1)
    tm = x_ref.shape[1]
    d_conv = co_ref.shape[2]
    d_attn = N_HEADS * HEAD_DIM

    @pl.when(t == 0)
    def _():
        ubuf_ref[0:HIST, :] = jnp.zeros((HIST, d_conv), F32)
        carry_ref[...] = jnp.zeros(carry_ref.shape, F32)

    hb = _adaln_rmsnorm(x_ref[0], n1g_ref[...], sc_ref[0, 0], sh_ref[0, 0]).astype(BF16)

    zg = _dot(hb, wglu_ref[...])
    ubuf_ref[HIST:HIST + tm, :] = zg[:, :d_conv] * jax.nn.sigmoid(zg[:, d_conv:])

    q_t, k_t, v_t, lf_t = _qkvf_feature_major(hb, wqkvf_ref[...], qg_ref[...], kg_ref[...], bf_ref[...])
    kt_ref[0] = k_t
    vt_ref[0] = v_t
    lft_ref[0] = lf_t

    nblk = tm // MXU_DIM
    parts = jnp.concatenate(_split3(lf_t), axis=0)
    stacked = jnp.concatenate([parts[:, i * MXU_DIM:(i + 1) * MXU_DIM] for i in range(nblk)], axis=0)
    local = _dot(stacked.astype(BF16), utri_ref[...])
    carry = carry_ref[:, 0:1]
    cums = []
    for i in range(nblk):
        loc = local[i * N_FEAT:(i + 1) * N_FEAT]
        cums.append(loc + carry)
        carry = carry + loc[:, MXU_DIM - 1:MXU_DIM]
    carry_ref[...] = jnp.broadcast_to(carry, carry_ref.shape)
    cum_parts = jnp.concatenate(cums, axis=1)
    cum_t = cum_parts[0:8] + cum_parts[8:16] + cum_parts[16:24]

    c_hi, c_mid, c_lo = _split3(cum_t)
    ones = jnp.ones((N_FEAT, tm), F32)
    zeros = jnp.zeros((LANES - 2 * N_FEAT, tm), F32)
    kbt_ref[0] = jnp.concatenate([ones, -c_hi, -c_mid, -c_lo, zeros], axis=0).astype(BF16)
    qfeat_t = jnp.concatenate([c_hi, c_mid, c_lo, ones, zeros], axis=0)
    q_ref[0] = (q_t * (HEAD_DIM ** -0.5)).T.astype(BF16)
    qb_ref[0] = qfeat_t.T.astype(BF16)

    cb, ln_g, ln_b = cb_ref[...], lng_ref[...], lnb_ref[...]
    first = HIST - (CONV_WIDTH - 1)
    for c in range(tm // CONV_ROWS):
        base = c * CONV_ROWS + first
        y = _conv_ln_swish(lambda k: ubuf_ref[base + k:base + k + CONV_ROWS, :], cw_ref, cb, ln_g, ln_b)
        co_ref[0, c * CONV_ROWS:(c + 1) * CONV_ROWS, :] = y.astype(BF16)

    @pl.when(t == pl.num_programs(1) - 1)
    def _():
        cst_ref[0] = ubuf_ref[HIST + tm - (CONV_WIDTH - 1):HIST + tm, :]

    ubuf_ref[0:HIST, :] = ubuf_ref[tm:tm + HIST, :]


def _inproj_prompt(x, mod4, n1g, wglu, wqkvf_t, bf_col, qg_col, kg_col, utri, conv_w, conv_b, ln_g, ln_b):
    b, s, d = x.shape
    tm = ROW_TILE
    d_conv = conv_w.shape[1]
    d_attn = N_HEADS * HEAD_DIM
    mod_spec = lambda j: pl.BlockSpec((1, 1, 1, d), lambda i, t: (i, j, 0, 0))
    out_shape = (
        jax.ShapeDtypeStruct((b, d_attn, s), F32),
        jax.ShapeDtypeStruct((b, d_attn, s), F32),
        jax.ShapeDtypeStruct((b, N_HEADS, s), F32),
        jax.ShapeDtypeStruct((b, CONV_WIDTH - 1, d_conv), F32),
        jax.ShapeDtypeStruct((b, s, d_attn), BF16),
        jax.ShapeDtypeStruct((b, s, LANES), BF16),
        jax.ShapeDtypeStruct((b, LANES, s), BF16),
        jax.ShapeDtypeStruct((b, s, d_conv), BF16),
    )
    out_specs = (
        pl.BlockSpec((1, d_attn, tm), lambda i, t: (i, 0, t)),
        pl.BlockSpec((1, d_attn, tm), lambda i, t: (i, 0, t)),
        pl.BlockSpec((1, N_HEADS, tm), lambda i, t: (i, 0, t)),
        pl.BlockSpec((1, CONV_WIDTH - 1, d_conv), lambda i, t: (i, 0, 0)),
        pl.BlockSpec((1, tm, d_attn), lambda i, t: (i, t, 0)),
        pl.BlockSpec((1, tm, LANES), lambda i, t: (i, t, 0)),
        pl.BlockSpec((1, LANES, tm), lambda i, t: (i, 0, t)),
        pl.BlockSpec((1, tm, d_conv), lambda i, t: (i, t, 0)),
    )
    in_specs = [
        pl.BlockSpec((1, tm, d), lambda i, t: (i, t, 0)),
        mod_spec(0), mod_spec(1),
        _const_spec(n1g.shape), _const_spec(wglu.shape), _const_spec(wqkvf_t.shape),
        _const_spec(bf_col.shape), _const_spec(qg_col.shape), _const_spec(kg_col.shape),
        _const_spec(utri.shape), _const_spec(conv_w.shape), _const_spec(conv_b.shape),
        _const_spec(ln_g.shape), _const_spec(ln_b.shape),
    ]
    return pl.pallas_call(
        _inproj_prompt_kernel,
        grid=(b, s // tm),
        in_specs=in_specs,
        out_specs=out_specs,
        out_shape=out_shape,
        scratch_shapes=[pltpu.VMEM((tm + HIST, d_conv), F32), pltpu.VMEM((N_FEAT, LANES), F32)],
        compiler_params=pltpu.CompilerParams(dimension_semantics=("arbitrary", "arbitrary"),
                                             vmem_limit_bytes=VMEM_LIMIT_BYTES),
        name="inproj_prompt",
    )(x, mod4, mod4, n1g, wglu, wqkvf_t, bf_col, qg_col, kg_col, utri, conv_w, conv_b, ln_g, ln_b)


def _attn_prompt_kernel(q_ref, qb_ref, kt_ref, vt_ref, kbt_ref, o_ref,
                        kp_ref, vp_ref, m_ref, l_ref, acc_ref):
    pair = pl.program_id(1)
    qi = pl.program_id(2)
    tq = q_ref.shape[1]
    nblk = kp_ref.shape[0]

    @pl.when(qi == 0)
    def _():
        for i in range(nblk):
            cols = slice(i * tq, (i + 1) * tq)
            kp_ref[i, 0:LANES, :] = kt_ref[0, :, cols].astype(BF16)
            kp_ref[i, LANES:2 * LANES, :] = kbt_ref[0, :, cols]
            vp_ref[i] = vt_ref[0, :, cols].astype(BF16)

    lane = lax.broadcasted_iota(jnp.int32, (tq, LANES), 1)
    q = q_ref[0].astype(F32)
    qb = qb_ref[0].astype(F32)
    rows = []
    for j in range(2):
        head = 2 * pair + j
        qm = jnp.where((lane >= j * HEAD_DIM) & (lane < (j + 1) * HEAD_DIM), q, 0.0)
        bm = jnp.where((lane % N_HEADS) == head, qb, 0.0)
        rows.append(jnp.concatenate([qm, bm], axis=1))
    q2 = jnp.concatenate(rows, axis=0).astype(BF16)

    m_ref[...] = jnp.full(m_ref.shape, NEG_INF, F32)
    l_ref[...] = jnp.zeros(l_ref.shape, F32)
    acc_ref[...] = jnp.zeros(acc_ref.shape, F32)

    def block(j, masked):
        s = _dot(q2, kp_ref[j])
        if masked:
            row = lax.broadcasted_iota(jnp.int32, s.shape, 0)
            col = lax.broadcasted_iota(jnp.int32, s.shape, 1)
            row = jnp.where(row >= tq, row - tq, row)
            s = jnp.where(col <= row, s, NEG_INF)
        m_old = m_ref[...]
        m_new = jnp.maximum(m_old, jnp.max(s, axis=-1, keepdims=True))
        alpha = jnp.exp(m_old - m_new)
        p = jnp.exp(s - m_new)
        l_ref[...] = alpha * l_ref[...] + jnp.sum(p, axis=-1, keepdims=True)
        acc_ref[...] = alpha * acc_ref[...] + _dot_nt(p.astype(BF16), vp_ref[j])
        m_ref[...] = m_new

    def body(j, carry):
        block(j, False)
        return carry

    lax.fori_loop(0, qi, body, 0)
    block(qi, True)

    out = acc_ref[...] / l_ref[...]
    o_ref[0] = jnp.where(lane < HEAD_DIM, out[:tq], out[tq:]).astype(BF16)


def _attn_prompt(q, qb, kt, vt, kbt):
    b, s, d_attn = q.shape
    tq = Q_TILE
    nblk = s // tq
    n_pairs = d_attn // LANES
    return pl.pallas_call(
        _attn_prompt_kernel,
        grid=(b, n_pairs, nblk),
        in_specs=[
            pl.BlockSpec((1, tq, LANES), lambda i, p, j: (i, j, p)),
            pl.BlockSpec((1, tq, LANES), lambda i, p, j: (i, j, 0)),
            pl.BlockSpec((1, LANES, s), lambda i, p, j: (i, p, 0)),
            pl.BlockSpec((1, LANES, s), lambda i, p, j: (i, p, 0)),
            pl.BlockSpec((1, LANES, s), lambda i, p, j: (i, 0, 0)),
        ],
        out_specs=pl.BlockSpec((1, tq, LANES), lambda i, p, j: (i, j, p)),
        out_shape=jax.ShapeDtypeStruct((b, s, d_attn), BF16),
        scratch_shapes=[
            pltpu.VMEM((nblk, 2 * LANES, tq), BF16),
            pltpu.VMEM((nblk, LANES, tq), BF16),
            pltpu.VMEM((2 * tq, 1), F32),
            pltpu.VMEM((2 * tq, 1), F32),
            pltpu.VMEM((2 * tq, LANES), F32),
        ],
        compiler_params=pltpu.CompilerParams(dimension_semantics=("arbitrary", "arbitrary", "arbitrary"),
                                             vmem_limit_bytes=VMEM_LIMIT_BYTES),
        name="attn_prompt",
    )(q, qb, kt, vt, kbt)


def _inproj_sample_kernel(x_ref, sh_ref, sc_ref, n1g_ref, wglu_ref, wqkvf_ref, bf_ref, qg_ref, kg_ref,
                          bdtri_ref, cw_ref, cb_ref, lng_ref, lnb_ref, st_ref,
                          k_ref, v_ref, lf_ref, nst_ref, q_ref, qb_ref, kb_ref, co_ref,
                          u_ref, hist_ref, cof_ref):
    nb, tt, d = x_ref.shape
    r = nb * tt
    d_conv = co_ref.shape[1]
    n_state = CONV_WIDTH - 1

    h = _adaln_rmsnorm(x_ref[...], n1g_ref[...], sc_ref[:, 0], sh_ref[:, 0])
    hb = h.reshape(r, d).astype(BF16)

    n_chunks = d_conv // LANES
    zg = _dot(hb, wglu_ref[...])
    u = zg[:, :d_conv] * jax.nn.sigmoid(zg[:, d_conv:])
    for c in range(n_chunks):
        u_ref[c] = u[:, c * LANES:(c + 1) * LANES]

    q_t, k_t, v_t, lf_t = _qkvf_feature_major(hb, wqkvf_ref[...], qg_ref[...], kg_ref[...], bf_ref[...])
    k_ref[...] = k_t.T
    v_ref[...] = v_t.T
    q_ref[...] = (q_t * (HEAD_DIM ** -0.5)).T.astype(BF16)
    lf = jnp.concatenate([lf_t, jnp.zeros((LANES - N_HEADS, r), F32)], axis=0).T
    lf_ref[...] = lf

    lane = lax.broadcasted_iota(jnp.int32, (r, LANES), 1)
    hi, mid, lo = _split3(lf)
    packed = hi + pltpu.roll(mid, N_HEADS, 1) + pltpu.roll(lo, 2 * N_HEADS, 1)
    c = _dot(bdtri_ref[...], packed.astype(BF16))
    cn = jnp.where(lane < N_HEADS,
                   c + pltpu.roll(c, LANES - N_HEADS, 1) + pltpu.roll(c, LANES - 2 * N_HEADS, 1), 0.0)
    hi, mid, lo = _split3(cn)
    p = hi + pltpu.roll(mid, N_HEADS, 1) + pltpu.roll(lo, 2 * N_HEADS, 1)
    qb_ref[...] = (p + jnp.where((lane >= N_FEAT) & (lane < 2 * N_FEAT), 1.0, 0.0)).astype(BF16)
    kb_ref[...] = (jnp.where(lane < N_FEAT, 1.0, 0.0) - pltpu.roll(p, N_FEAT, 1)).astype(BF16)

    hist_ref[0:n_state] = st_ref[...]
    for t in range(tt):
        for c in range(n_chunks):
            hist_ref[n_state + t, :, c * LANES:(c + 1) * LANES] = u_ref[c, pl.ds(t, nb, stride=tt), :]
    nst_ref[...] = hist_ref[tt:tt + n_state]
    cb, ln_g, ln_b = cb_ref[...], lng_ref[...], lnb_ref[...]
    for t in range(tt):
        y = _conv_ln_swish(lambda k: hist_ref[t + k], cw_ref, cb, ln_g, ln_b)
        for c in range(n_chunks):
            cof_ref[c, pl.ds(t, nb, stride=tt), :] = y[:, c * LANES:(c + 1) * LANES]
    co_ref[...] = jnp.concatenate([cof_ref[c] for c in range(n_chunks)], axis=1).astype(BF16)


def _inproj_sample(x, mod4, n1g, wglu, wqkvf_t, bf_col, qg_col, kg_col, bdtri, conv_w, conv_b, ln_g, ln_b,
                   state_t):
    nb, tt, d = x.shape
    r = nb * tt
    d_conv = conv_w.shape[1]
    d_attn = N_HEADS * HEAD_DIM
    n_state = CONV_WIDTH - 1
    mod_spec = lambda j: pl.BlockSpec((nb, 1, 1, d), lambda i: (0, j, 0, 0))
    full = lambda shape: pl.BlockSpec(shape, lambda i: (0,) * len(shape))
    out_shape = (
        jax.ShapeDtypeStruct((r, d_attn), F32),
        jax.ShapeDtypeStruct((r, d_attn), F32),
        jax.ShapeDtypeStruct((r, LANES), F32),
        jax.ShapeDtypeStruct((n_state, nb, d_conv), F32),
        jax.ShapeDtypeStruct((r, d_attn), BF16),
        jax.ShapeDtypeStruct((r, LANES), BF16),
        jax.ShapeDtypeStruct((r, LANES), BF16),
        jax.ShapeDtypeStruct((r, d_conv), BF16),
    )
    args = (x, mod4, mod4, n1g, wglu, wqkvf_t, bf_col, qg_col, kg_col, bdtri, conv_w, conv_b, ln_g, ln_b,
            state_t)
    in_specs = [full(x.shape), mod_spec(0), mod_spec(1)] + [full(a.shape) for a in args[3:]]
    return pl.pallas_call(
        _inproj_sample_kernel,
        grid=(1,),
        in_specs=in_specs,
        out_specs=tuple(full(o.shape) for o in out_shape),
        out_shape=out_shape,
        scratch_shapes=[pltpu.VMEM((d_conv // LANES, r, LANES), F32),
                        pltpu.VMEM((n_state + tt, nb, d_conv), F32),
                        pltpu.VMEM((d_conv // LANES, r, LANES), F32)],
        compiler_params=pltpu.CompilerParams(dimension_semantics=("arbitrary",),
                                             vmem_limit_bytes=VMEM_LIMIT_BYTES),
        name="inproj_sample",
    )(*args)


def _attn_sample_kernel(q_ref, qb_ref, kn_ref, vn_ref, kbn_ref, kt_ref, vt_ref, clft_ref, ltri_ref, o_ref):
    tt = q_ref.shape[1]
    d_attn = q_ref.shape[2]
    p_len = kt_ref.shape[2]
    nblk = p_len // MXU_DIM
    rows = N_HEADS * tt

    parts = jnp.concatenate(_split3(clft_ref[0]), axis=0)
    blocks = [parts[:, i * MXU_DIM:(i + 1) * MXU_DIM] for i in range(nblk)]
    local = _dot(jnp.concatenate(blocks, axis=0).astype(BF16), ltri_ref[...])
    off = jnp.zeros((N_FEAT, 1), F32)
    sufs = [None] * nblk
    for i in reversed(range(nblk)):
        loc = local[i * N_FEAT:(i + 1) * N_FEAT]
        sufs[i] = loc + off
        off = off + loc[:, 0:1] + blocks[i][:, 0:1]
    suf_parts = jnp.concatenate(sufs, axis=1)
    ck_rel = suf_parts[0:8] + suf_parts[8:16] + suf_parts[16:24]
    c_hi, c_mid, c_lo = _split3(ck_rel)
    kbt = jnp.concatenate([jnp.ones((N_FEAT, p_len), F32), c_hi, c_mid, c_lo,
                           jnp.zeros((LANES - 2 * N_FEAT, p_len), F32)], axis=0)
    k_all = jnp.concatenate([kt_ref[0].astype(BF16), kbt.astype(BF16)], axis=0)

    lane_q = lax.broadcasted_iota(jnp.int32, (tt, d_attn), 1)
    lane_b = lax.broadcasted_iota(jnp.int32, (tt, LANES), 1)
    q = q_ref[0].astype(F32)
    qb = qb_ref[0].astype(F32)
    stack = []
    for h in range(N_HEADS):
        qm = jnp.where((lane_q >= h * HEAD_DIM) & (lane_q < (h + 1) * HEAD_DIM), q, 0.0)
        bm = jnp.where((lane_b % N_HEADS) == h, qb, 0.0)
        stack.append(jnp.concatenate([qm, bm], axis=1))
    qs = jnp.concatenate(stack, axis=0).astype(BF16)

    s_c = _dot(qs, k_all)

    pad = jnp.zeros((LANES - tt, d_attn + LANES), F32)
    kn = jnp.concatenate([jnp.concatenate([kn_ref[0], kbn_ref[0].astype(F32)], axis=1), pad], axis=0)
    s_n = _dot_nt(qs, kn.astype(BF16))
    row = lax.broadcasted_iota(jnp.int32, s_n.shape, 0)
    col = lax.broadcasted_iota(jnp.int32, s_n.shape, 1)
    s_n = jnp.where(col <= (row % tt), s_n, NEG_INF)

    m = jnp.maximum(jnp.max(s_c, axis=-1, keepdims=True), jnp.max(s_n, axis=-1, keepdims=True))
    p_c = jnp.exp(s_c - m)
    p_n = jnp.exp(s_n - m)
    l = jnp.sum(p_c, axis=-1, keepdims=True) + jnp.sum(p_n, axis=-1, keepdims=True)
    vn = jnp.concatenate([vn_ref[0], jnp.zeros((LANES - tt, d_attn), F32)], axis=0).astype(BF16)
    o = _dot_nt(p_c.astype(BF16), vt_ref[0].astype(BF16)) + _dot(p_n.astype(BF16), vn)
    o = o / l

    out = jnp.zeros((tt, d_attn), F32)
    for h in range(N_HEADS):
        out = jnp.where((lane_q >= h * HEAD_DIM) & (lane_q < (h + 1) * HEAD_DIM), o[h * tt:(h + 1) * tt], out)
    o_ref[0] = out.astype(BF16)


def _attn_sample(q, qb, kn, vn, kbn, kt, vt, clft, ltri):
    nb, tt, d_attn = q.shape
    p_len = kt.shape[2]
    per_req = lambda shape: pl.BlockSpec((1,) + shape, lambda i: (i, 0, 0))
    return pl.pallas_call(
        _attn_sample_kernel,
        grid=(nb,),
        in_specs=[per_req((tt, d_attn)), per_req((tt, LANES)), per_req((tt, d_attn)), per_req((tt, d_attn)),
                  per_req((tt, LANES)), per_req((d_attn, p_len)), per_req((d_attn, p_len)),
                  per_req((N_HEADS, p_len)), _const_spec(ltri.shape)],
        out_specs=per_req((tt, d_attn)),
        out_shape=jax.ShapeDtypeStruct((nb, tt, d_attn), BF16),
        compiler_params=pltpu.CompilerParams(dimension_semantics=("arbitrary",),
                                             vmem_limit_bytes=VMEM_LIMIT_BYTES),
        name="attn_sample",
    )(q, qb, kn, vn, kbn, kt, vt, clft, ltri)


def _outproj_ffn_kernel(x_ref, co_ref, at_ref, g1_ref, sh2_ref, sc2_ref, g2_ref, n2g_ref,
                        wo_ref, wg_ref, wu_ref, wd_ref, y_ref, acc_ref):
    nb, tt, d = x_ref.shape
    r = nb * tt
    d_conv = co_ref.shape[2]
    co = co_ref[...].reshape(r, d_conv)
    at = at_ref[...].reshape(r, at_ref.shape[2])
    mix = _dot(co, wo_ref[0:d_conv, :]) + _dot(at, wo_ref[d_conv:, :])
    x1 = x_ref[...] + g1_ref[:, 0] * mix.reshape(nb, tt, d)
    hb = _adaln_rmsnorm(x1, n2g_ref[...], sc2_ref[:, 0], sh2_ref[:, 0]).reshape(r, d).astype(BF16)
    for c in range(wg_ref.shape[0]):
        g = _dot(hb, wg_ref[c])
        a = ((g * jax.nn.sigmoid(g)) * _dot(hb, wu_ref[c])).astype(BF16)
        part = _dot(a, wd_ref[c])
        if c == 0:
            acc_ref[...] = part
        else:
            acc_ref[...] += part
    y_ref[...] = x1 + g2_ref[:, 0] * acc_ref[...].reshape(nb, tt, d)


def _outproj_ffn(x, co, at, mod4, n2g, wo, wg3, wu3, wd3, bb, tt):
    nb, s, d = x.shape
    d_conv = co.shape[2]
    d_attn = at.shape[2]
    mod_spec = lambda j: pl.BlockSpec((bb, 1, 1, d), lambda i, t: (i, j, 0, 0))
    row_spec = lambda w: pl.BlockSpec((bb, tt, w), lambda i, t: (i, t, 0))
    return pl.pallas_call(
        _outproj_ffn_kernel,
        grid=(nb // bb, s // tt),
        in_specs=[row_spec(d), row_spec(d_conv), row_spec(d_attn),
                  mod_spec(2), mod_spec(3), mod_spec(4), mod_spec(5),
                  _const_spec(n2g.shape), _const_spec(wo.shape), _const_spec(wg3.shape),
                  _const_spec(wu3.shape), _const_spec(wd3.shape)],
        out_specs=row_spec(d),
        out_shape=jax.ShapeDtypeStruct((nb, s, d), F32),
        scratch_shapes=[pltpu.VMEM((bb * tt, d), F32)],
        compiler_params=pltpu.CompilerParams(dimension_semantics=("arbitrary", "arbitrary"),
                                             vmem_limit_bytes=VMEM_LIMIT_BYTES),
        name="outproj_ffn",
    )(x, co, at, mod4, mod4, mod4, mod4, n2g, wo, wg3, wu3, wd3)


def _tri(n, rel):
    i = lax.broadcasted_iota(jnp.int32, (n, n), 0)
    j = lax.broadcasted_iota(jnp.int32, (n, n), 1)
    return rel(i, j).astype(BF16)


def _layer(xp, xs, cache_k, cache_v, cache_logf, state_conv, c_all, w_ada, b_ada, norm1_g, w_in, b_f,
           q_norm_g, k_norm_g, conv_w, conv_b, conv_ln_g, conv_ln_b, w_out, norm2_g, w_gate, w_up, w_down):
    b, s, d = xp.shape
    nb, tt, _ = xs.shape
    d_conv = conv_w.shape[1]
    d_attn = N_HEADS * HEAD_DIM
    d_ff = w_gate.shape[1]
    p_len = cache_k.shape[1]

    mod4 = _modulation(c_all, w_ada, b_ada).reshape(b + nb, 6, 1, d)
    mod_p, mod_s = mod4[:b], mod4[b:]

    wglu = w_in[:, :2 * d_conv].astype(BF16)
    wqkvf_t = jnp.pad(w_in[:, 2 * d_conv:].T, ((0, LANES - N_HEADS), (0, 0))).astype(BF16)
    bf_col = b_f.reshape(N_HEADS, 1)
    qg_col = q_norm_g.reshape(d_attn, 1)
    kg_col = k_norm_g.reshape(d_attn, 1)
    row = lambda v: v.reshape(1, -1)
    n1g, n2g, cb, ln_g, ln_b = row(norm1_g), row(norm2_g), row(conv_b), row(conv_ln_g), row(conv_ln_b)
    wo = w_out.astype(BF16)
    n_ff = d_ff // FF_CHUNK
    wg3 = w_gate.astype(BF16).reshape(d, n_ff, FF_CHUNK).transpose(1, 0, 2)
    wu3 = w_up.astype(BF16).reshape(d, n_ff, FF_CHUNK).transpose(1, 0, 2)
    wd3 = w_down.astype(BF16).reshape(n_ff, FF_CHUNK, d)
    utri = _tri(MXU_DIM, lambda i, j: i <= j)
    ltri = _tri(MXU_DIM, lambda i, j: i > j)
    bdtri = _tri(nb * tt, lambda i, j: (i // tt == j // tt) & (j <= i))

    kt, vt, lft, cst, q, qb, kbt, co = _inproj_prompt(
        xp, mod_p, n1g, wglu, wqkvf_t, bf_col, qg_col, kg_col, utri, conv_w, cb, ln_g, ln_b)
    at = _attn_prompt(q, qb, kt, vt, kbt)
    yp = _outproj_ffn(xp, co, at, mod_p, n2g, wo, wg3, wu3, wd3, 1, ROW_TILE)
    k_p = kt.reshape(b, N_HEADS, HEAD_DIM, s).transpose(0, 3, 1, 2)
    v_p = vt.reshape(b, N_HEADS, HEAD_DIM, s).transpose(0, 3, 1, 2)
    lf_p = lft.transpose(0, 2, 1)

    state_t = state_conv.transpose(1, 0, 2)
    k_s, v_s, lf_s, nst, q_s, qb_s, kb_s, co_s = _inproj_sample(
        xs, mod_s, n1g, wglu, wqkvf_t, bf_col, qg_col, kg_col, bdtri, conv_w, cb, ln_g, ln_b, state_t)
    r3 = lambda a: a.reshape(nb, tt, a.shape[-1])
    ckt = cache_k.transpose(0, 2, 3, 1).reshape(nb, d_attn, p_len)
    cvt = cache_v.transpose(0, 2, 3, 1).reshape(nb, d_attn, p_len)
    clft = cache_logf.transpose(0, 2, 1)
    at_s = _attn_sample(r3(q_s), r3(qb_s), r3(k_s), r3(v_s), r3(kb_s), ckt, cvt, clft, ltri)
    ys = _outproj_ffn(xs, r3(co_s), at_s, mod_s, n2g, wo, wg3, wu3, wd3, nb, tt)
    k_sn = k_s.reshape(nb, tt, N_HEADS, HEAD_DIM)
    v_sn = v_s.reshape(nb, tt, N_HEADS, HEAD_DIM)
    lf_sn = lf_s[:, :N_HEADS].reshape(nb, tt, N_HEADS)
    return yp, ys, (k_p, v_p, lf_p, cst), (k_sn, v_sn, lf_sn, nst.transpose(1, 0, 2))


def kernel(x_prompt, x_sample, cache_k, cache_v, cache_logf, state_conv, c_prompt, c_sample, w_ada, b_ada,
           norm1_g, w_in, b_f, q_norm_g, k_norm_g, conv_w, conv_b, conv_ln_g, conv_ln_b, w_out, norm2_g,
           w_gate, w_up, w_down):
    depth = w_ada.shape[0]
    c_all = jnp.concatenate([c_prompt, c_sample], axis=0)
    yp, ys = x_prompt, x_sample
    st_p, st_s = [], []
    for l in range(depth):
        yp, ys, sp, ss = _layer(
            yp, ys, cache_k[l], cache_v[l], cache_logf[l], state_conv[l], c_all, w_ada[l], b_ada[l],
            norm1_g[l], w_in[l], b_f[l], q_norm_g[l], k_norm_g[l], conv_w[l], conv_b[l], conv_ln_g[l],
            conv_ln_b[l], w_out[l], norm2_g[l], w_gate[l], w_up[l], w_down[l])
        st_p.append(sp)
        st_s.append(ss)
    stack = lambda xs: xs[0][None] if len(xs) == 1 else jnp.stack(xs)
    outs_p = [stack([s[i] for s in st_p]) for i in range(4)]
    outs_s = [stack([s[i] for s in st_s]) for i in range(4)]
    return (yp, ys, *outs_p, *outs_s)
```

```python
import functools

import jax
import jax.numpy as jnp
from jax import lax
from jax.experimental import pallas as pl
from jax.experimental.pallas import tpu as pltpu

F32 = jnp.float32
BF16 = jnp.bfloat16

N_HEADS = 8
HEAD_DIM = 64
CONV_WIDTH = 31
EPS = 1e-6
NEG_INF = -1e30

LANES = 128
MXU_DIM = 256
VMEM_LIMIT_BYTES = 56 * 1024 * 1024

N_PARTS = 3
N_FEAT = N_PARTS * N_HEADS
HIST = 32
CONV_ROWS = 32

ROW_TILE = 512
Q_TILE = 256
FF_CHUNK = 256


def _dot(a, b):
    return jnp.dot(a, b, preferred_element_type=F32)


def _dot_nt(a, b):
    return lax.dot_general(a, b, (((1,), (1,)), ((), ())), preferred_element_type=F32)


def _split3(x):
    hi = x.astype(BF16).astype(F32)
    r = x - hi
    mid = r.astype(BF16).astype(F32)
    lo = (r - mid).astype(BF16).astype(F32)
    return hi, mid, lo


def _log_sigmoid(x):
    return jnp.minimum(x, 0.0) - jnp.log1p(jnp.exp(-jnp.abs(x)))


def _adaln_rmsnorm(x, g, scale, shift):
    y = x * lax.rsqrt(jnp.mean(x * x, axis=-1, keepdims=True) + EPS)
    return (y * g) * (1.0 + scale) + shift


def _const_spec(shape):
    n = len(shape)
    return pl.BlockSpec(shape, lambda *_: (0,) * n, pipeline_mode=pl.Buffered(1))


def _modulation_kernel(c_ref, w_ref, b_ref, o_ref):
    c = c_ref[...]
    a = (c * jax.nn.sigmoid(c)).astype(BF16)
    o_ref[...] = _dot(a, w_ref[...].astype(BF16)) + b_ref[...]


def _modulation(c, w_ada, b_ada):
    nb, d = c.shape
    n = w_ada.shape[1]
    tn = 768
    return pl.pallas_call(
        _modulation_kernel,
        grid=(n // tn,),
        in_specs=[pl.BlockSpec((nb, d), lambda j: (0, 0)),
                  pl.BlockSpec((d, tn), lambda j: (0, j)),
                  pl.BlockSpec((1, tn), lambda j: (0, j))],
        out_specs=pl.BlockSpec((nb, tn), lambda j: (0, j)),
        out_shape=jax.ShapeDtypeStruct((nb, n), F32),
        compiler_params=pltpu.CompilerParams(dimension_semantics=("arbitrary",),
                                             vmem_limit_bytes=VMEM_LIMIT_BYTES),
        name="modulation",
    )(c, w_ada, b_ada.reshape(1, n))


def _qkvf_feature_major(hb, wqkvf_t, qg_col, kg_col, bf_col):
    d_attn = N_HEADS * HEAD_DIM
    zt = _dot_nt(wqkvf_t, hb)
    r = zt.shape[1]

    def head_rms(z, g_col):
        z3 = z.reshape(N_HEADS, HEAD_DIM, r)
        ms = jnp.mean(z3 * z3, axis=1, keepdims=True)
        return (z3 * lax.rsqrt(ms + EPS)).reshape(d_attn, r) * g_col

    q_t = head_rms(zt[0:d_attn], qg_col)
    k_t = head_rms(zt[d_attn:2 * d_attn], kg_col)
    v_t = zt[2 * d_attn:3 * d_attn]
    lf_t = _log_sigmoid(zt[3 * d_attn:3 * d_attn + N_HEADS] + bf_col)
    return q_t, k_t, v_t, lf_t


def _conv_ln_swish(load_rows, cw_ref, cb, ln_g, ln_b):
    acc = cb + cw_ref[0:1, :] * load_rows(0)
    for k in range(1, CONV_WIDTH):
        acc = acc + cw_ref[k:k + 1, :] * load_rows(k)
    mu = jnp.mean(acc, axis=-1, keepdims=True)
    cen = acc - mu
    var = jnp.mean(cen * cen, axis=-1, keepdims=True)
    y = cen * lax.rsqrt(var + EPS) * ln_g + ln_b
    return y * jax.nn.sigmoid(y)


def _inproj_prompt_kernel(x_ref, sh_ref, sc_ref, n1g_ref, wglu_ref, wqkvf_ref, bf_ref, qg_ref, kg_ref,
                          utri_ref, cw_ref, cb_ref, lng_ref, lnb_ref,
                          kt_ref, vt_ref, lft_ref, cst_ref, qt_ref, qft_ref, k_ref, kb_ref, co_ref,
                          ubuf_ref, carry_ref):
    t = pl.program_id(1)
    tm = x_ref.shape[1]
    d_conv = co_ref.shape[2]
    d_attn = N_HEADS * HEAD_DIM

    @pl.when(t == 0)
    def _():
        ubuf_ref[0:HIST, :] = jnp.zeros((HIST, d_conv), F32)
        carry_ref[...] = jnp.zeros(carry_ref.shape, F32)

    hb = _adaln_rmsnorm(x_ref[0], n1g_ref[...], sc_ref[0, 0], sh_ref[0, 0]).astype(BF16)

    zg = _dot(hb, wglu_ref[...])
    ubuf_ref[HIST:HIST + tm, :] = zg[:, :d_conv] * jax.nn.sigmoid(zg[:, d_conv:])

    q_t, k_t, v_t, lf_t = _qkvf_feature_major(hb, wqkvf_ref[...], qg_ref[...], kg_ref[...], bf_ref[...])
    kt_ref[0] = k_t
    vt_ref[0] = v_t
    lft_ref[0] = lf_t

    nblk = tm // MXU_DIM
    parts = jnp.concatenate(_split3(lf_t), axis=0)
    stacked = jnp.concatenate([parts[:, i * MXU_DIM:(i + 1) * MXU_DIM] for i in range(nblk)], axis=0)
    local = _dot(stacked.astype(BF16), utri_ref[...])
    carry = carry_ref[:, 0:1]
    cums = []
    for i in range(nblk):
        loc = local[i * N_FEAT:(i + 1) * N_FEAT]
        cums.append(loc + carry)
        carry = carry + loc[:, MXU_DIM - 1:MXU_DIM]
    carry_ref[...] = jnp.broadcast_to(carry, carry_ref.shape)
    cum_parts = jnp.concatenate(cums, axis=1)
    cum_t = cum_parts[0:8] + cum_parts[8:16] + cum_parts[16:24]

    c_hi, c_mid, c_lo = _split3(cum_t)
    ones = jnp.ones((N_FEAT, tm), F32)
    zeros = jnp.zeros((LANES - 2 * N_FEAT, tm), F32)
    kfeat_t = jnp.concatenate([ones, -c_hi, -c_mid, -c_lo, zeros], axis=0)
    qft_ref[0] = jnp.concatenate([c_hi, c_mid, c_lo, ones, zeros], axis=0).astype(BF16)
    qt_ref[0] = (q_t * (HEAD_DIM ** -0.5)).astype(BF16)
    k_ref[0] = k_t.T.astype(BF16)
    kb_ref[0] = kfeat_t.T.astype(BF16)

    cb, ln_g, ln_b = cb_ref[...], lng_ref[...], lnb_ref[...]
    first = HIST - (CONV_WIDTH - 1)
    for c in range(tm // CONV_ROWS):
        base = c * CONV_ROWS + first
        y = _conv_ln_swish(lambda k: ubuf_ref[base + k:base + k + CONV_ROWS, :], cw_ref, cb, ln_g, ln_b)
        co_ref[0, c * CONV_ROWS:(c + 1) * CONV_ROWS, :] = y.astype(BF16)

    @pl.when(t == pl.num_programs(1) - 1)
    def _():
        cst_ref[0] = ubuf_ref[HIST + tm - (CONV_WIDTH - 1):HIST + tm, :]

    ubuf_ref[0:HIST, :] = ubuf_ref[tm:tm + HIST, :]


def _inproj_prompt(x, mod4, n1g, wglu, wqkvf_t, bf_col, qg_col, kg_col, utri, conv_w, conv_b, ln_g, ln_b):
    b, s, d = x.shape
    tm = ROW_TILE
    d_conv = conv_w.shape[1]
    d_attn = N_HEADS * HEAD_DIM
    mod_spec = lambda j: pl.BlockSpec((1, 1, 1, d), lambda i, t: (i, j, 0, 0))
    out_shape = (
        jax.ShapeDtypeStruct((b, d_attn, s), F32),
        jax.ShapeDtypeStruct((b, d_attn, s), F32),
        jax.ShapeDtypeStruct((b, N_HEADS, s), F32),
        jax.ShapeDtypeStruct((b, CONV_WIDTH - 1, d_conv), F32),
        jax.ShapeDtypeStruct((b, d_attn, s), BF16),
        jax.ShapeDtypeStruct((b, LANES, s), BF16),
        jax.ShapeDtypeStruct((b, s, d_attn), BF16),
        jax.ShapeDtypeStruct((b, s, LANES), BF16),
        jax.ShapeDtypeStruct((b, s, d_conv), BF16),
    )
    out_specs = (
        pl.BlockSpec((1, d_attn, tm), lambda i, t: (i, 0, t)),
        pl.BlockSpec((1, d_attn, tm), lambda i, t: (i, 0, t)),
        pl.BlockSpec((1, N_HEADS, tm), lambda i, t: (i, 0, t)),
        pl.BlockSpec((1, CONV_WIDTH - 1, d_conv), lambda i, t: (i, 0, 0)),
        pl.BlockSpec((1, d_attn, tm), lambda i, t: (i, 0, t)),
        pl.BlockSpec((1, LANES, tm), lambda i, t: (i, 0, t)),
        pl.BlockSpec((1, tm, d_attn), lambda i, t: (i, t, 0)),
        pl.BlockSpec((1, tm, LANES), lambda i, t: (i, t, 0)),
        pl.BlockSpec((1, tm, d_conv), lambda i, t: (i, t, 0)),
    )
    in_specs = [
        pl.BlockSpec((1, tm, d), lambda i, t: (i, t, 0)),
        mod_spec(0), mod_spec(1),
        _const_spec(n1g.shape), _const_spec(wglu.shape), _const_spec(wqkvf_t.shape),
        _const_spec(bf_col.shape), _const_spec(qg_col.shape), _const_spec(kg_col.shape),
        _const_spec(utri.shape), _const_spec(conv_w.shape), _const_spec(conv_b.shape),
        _const_spec(ln_g.shape), _const_spec(ln_b.shape),
    ]
    return pl.pallas_call(
        _inproj_prompt_kernel,
        grid=(b, s // tm),
        in_specs=in_specs,
        out_specs=out_specs,
        out_shape=out_shape,
        scratch_shapes=[pltpu.VMEM((tm + HIST, d_conv), F32), pltpu.VMEM((N_FEAT, LANES), F32)],
        compiler_params=pltpu.CompilerParams(dimension_semantics=("arbitrary", "arbitrary"),
                                             vmem_limit_bytes=VMEM_LIMIT_BYTES),
        name="inproj_prompt",
    )(x, mod4, mod4, n1g, wglu, wqkvf_t, bf_col, qg_col, kg_col, utri, conv_w, conv_b, ln_g, ln_b)


def _attn_prompt_kernel(qt_ref, qft_ref, k_ref, kb_ref, vt_ref, o_ref,
                        kp_ref, vp_ref, q2_ref, m_ref, l_ref, acc_ref):
    qi = pl.program_id(1)
    tq = qt_ref.shape[2]
    n_pairs, nblk = vp_ref.shape[0], vp_ref.shape[1]

    @pl.when(qi == 0)
    def _():
        for p in range(n_pairs):
            kp_ref[p, :, 0:LANES] = k_ref[0, :, p * LANES:(p + 1) * LANES]
            kp_ref[p, :, LANES:2 * LANES] = kb_ref[0]
            for i in range(nblk):
                vp_ref[p, i] = vt_ref[0, p * LANES:(p + 1) * LANES, i * tq:(i + 1) * tq].astype(BF16)

    frow = lax.broadcasted_iota(jnp.int32, (LANES, tq), 0)
    feat = qft_ref[0].astype(F32)
    for p in range(n_pairs):
        qp = qt_ref[0, p * LANES:(p + 1) * LANES, :].astype(F32)
        cols = []
        for j in range(2):
            qm = jnp.where((frow >= j * HEAD_DIM) & (frow < (j + 1) * HEAD_DIM), qp, 0.0)
            fm = jnp.where((frow % N_HEADS) == 2 * p + j, feat, 0.0)
            cols.append(jnp.concatenate([qm, fm], axis=0))
        q2_ref[p] = jnp.concatenate(cols, axis=1).astype(BF16)

    m_ref[...] = jnp.full(m_ref.shape, NEG_INF, F32)
    l_ref[...] = jnp.zeros(l_ref.shape, F32)
    acc_ref[...] = jnp.zeros(acc_ref.shape, F32)

    def block(j, masked):
        start = pl.multiple_of(j * tq, tq)
        for p in range(n_pairs):
            s = _dot(kp_ref[p, pl.ds(start, tq), :], q2_ref[p])
            if masked:
                key = lax.broadcasted_iota(jnp.int32, s.shape, 0)
                qry = lax.broadcasted_iota(jnp.int32, s.shape, 1)
                qry = jnp.where(qry >= tq, qry - tq, qry)
                s = jnp.where(key <= qry, s, NEG_INF)
            m_old = m_ref[p]
            m_new = jnp.maximum(m_old, jnp.max(s, axis=0, keepdims=True))
            alpha = jnp.exp(m_old - m_new)
            pr = jnp.exp(s - m_new)
            l_ref[p] = alpha * l_ref[p] + jnp.sum(pr, axis=0, keepdims=True)
            acc_ref[p] = alpha * acc_ref[p] + _dot(vp_ref[p, j], pr.astype(BF16))
            m_ref[p] = m_new

    def body(j, carry):
        block(j, False)
        return carry

    lax.fori_loop(0, qi, body, 0)
    block(qi, True)

    for p in range(n_pairs):
        out_t = acc_ref[p] / l_ref[p]
        pair_t = jnp.where(frow < HEAD_DIM, out_t[:, :tq], out_t[:, tq:])
        o_ref[0, :, p * LANES:(p + 1) * LANES] = pair_t.T.astype(BF16)


def _attn_prompt(qt, qft, k, kb, vt):
    b, d_attn, s = qt.shape
    tq = Q_TILE
    nblk = s // tq
    n_pairs = d_attn // LANES
    return pl.pallas_call(
        _attn_prompt_kernel,
        grid=(b, nblk),
        in_specs=[
            pl.BlockSpec((1, d_attn, tq), lambda i, j: (i, 0, j)),
            pl.BlockSpec((1, LANES, tq), lambda i, j: (i, 0, j)),
            pl.BlockSpec((1, s, d_attn), lambda i, j: (i, 0, 0)),
            pl.BlockSpec((1, s, LANES), lambda i, j: (i, 0, 0)),
            pl.BlockSpec((1, d_attn, s), lambda i, j: (i, 0, 0)),
        ],
        out_specs=pl.BlockSpec((1, tq, d_attn), lambda i, j: (i, j, 0)),
        out_shape=jax.ShapeDtypeStruct((b, s, d_attn), BF16),
        scratch_shapes=[
            pltpu.VMEM((n_pairs, s, 2 * LANES), BF16),
            pltpu.VMEM((n_pairs, nblk, LANES, tq), BF16),
            pltpu.VMEM((n_pairs, 2 * LANES, 2 * tq), BF16),
            pltpu.VMEM((n_pairs, 1, 2 * tq), F32),
            pltpu.VMEM((n_pairs, 1, 2 * tq), F32),
            pltpu.VMEM((n_pairs, LANES, 2 * tq), F32),
        ],
        compiler_params=pltpu.CompilerParams(dimension_semantics=("arbitrary", "arbitrary"),
                                             vmem_limit_bytes=VMEM_LIMIT_BYTES),
        name="attn_prompt",
    )(qt, qft, k, kb, vt)


def _inproj_sample_kernel(x_ref, sh_ref, sc_ref, n1g_ref, wglu_ref, wqkvf_ref, bf_ref, qg_ref, kg_ref,
                          bdtri_ref, cw_ref, cb_ref, lng_ref, lnb_ref, st_ref,
                          k_ref, v_ref, lf_ref, nst_ref, q_ref, qb_ref, kb_ref, co_ref,
                          u_ref, hist_ref, cof_ref):
    nb, tt, d = x_ref.shape
    r = nb * tt
    d_conv = co_ref.shape[1]
    n_state = CONV_WIDTH - 1

    h = _adaln_rmsnorm(x_ref[...], n1g_ref[...], sc_ref[:, 0], sh_ref[:, 0])
    hb = h.reshape(r, d).astype(BF16)

    n_chunks = d_conv // LANES
    zg = _dot(hb, wglu_ref[...])
    u = zg[:, :d_conv] * jax.nn.sigmoid(zg[:, d_conv:])
    for c in range(n_chunks):
        u_ref[c] = u[:, c * LANES:(c + 1) * LANES]

    q_t, k_t, v_t, lf_t = _qkvf_feature_major(hb, wqkvf_ref[...], qg_ref[...], kg_ref[...], bf_ref[...])
    k_ref[...] = k_t.T
    v_ref[...] = v_t.T
    q_ref[...] = (q_t * (HEAD_DIM ** -0.5)).T.astype(BF16)
    lf = jnp.concatenate([lf_t, jnp.zeros((LANES - N_HEADS, r), F32)], axis=0).T
    lf_ref[...] = lf

    lane = lax.broadcasted_iota(jnp.int32, (r, LANES), 1)
    hi, mid, lo = _split3(lf)
    packed = hi + pltpu.roll(mid, N_HEADS, 1) + pltpu.roll(lo, 2 * N_HEADS, 1)
    c = _dot(bdtri_ref[...], packed.astype(BF16))
    cn = jnp.where(lane < N_HEADS,
                   c + pltpu.roll(c, LANES - N_HEADS, 1) + pltpu.roll(c, LANES - 2 * N_HEADS, 1), 0.0)
    hi, mid, lo = _split3(cn)
    p = hi + pltpu.roll(mid, N_HEADS, 1) + pltpu.roll(lo, 2 * N_HEADS, 1)
    qb_ref[...] = (p + jnp.where((lane >= N_FEAT) & (lane < 2 * N_FEAT), 1.0, 0.0)).astype(BF16)
    kb_ref[...] = (jnp.where(lane < N_FEAT, 1.0, 0.0) - pltpu.roll(p, N_FEAT, 1)).astype(BF16)

    hist_ref[0:n_state] = st_ref[...]
    for t in range(tt):
        for c in range(n_chunks):
            hist_ref[n_state + t, :, c * LANES:(c + 1) * LANES] = u_ref[c, pl.ds(t, nb, stride=tt), :]
    nst_ref[...] = hist_ref[tt:tt + n_state]
    cb, ln_g, ln_b = cb_ref[...], lng_ref[...], lnb_ref[...]
    for t in range(tt):
        y = _conv_ln_swish(lambda k: hist_ref[t + k], cw_ref, cb, ln_g, ln_b)
        for c in range(n_chunks):
            cof_ref[c, pl.ds(t, nb, stride=tt), :] = y[:, c * LANES:(c + 1) * LANES]
    co_ref[...] = jnp.concatenate([cof_ref[c] for c in range(n_chunks)], axis=1).astype(BF16)


def _inproj_sample(x, mod4, n1g, wglu, wqkvf_t, bf_col, qg_col, kg_col, bdtri, conv_w, conv_b, ln_g, ln_b,
                   state_t):
    nb, tt, d = x.shape
    r = nb * tt
    d_conv = conv_w.shape[1]
    d_attn = N_HEADS * HEAD_DIM
    n_state = CONV_WIDTH - 1
    mod_spec = lambda j: pl.BlockSpec((nb, 1, 1, d), lambda i: (0, j, 0, 0))
    full = lambda shape: pl.BlockSpec(shape, lambda i: (0,) * len(shape))
    out_shape = (
        jax.ShapeDtypeStruct((r, d_attn), F32),
        jax.ShapeDtypeStruct((r, d_attn), F32),
        jax.ShapeDtypeStruct((r, LANES), F32),
        jax.ShapeDtypeStruct((n_state, nb, d_conv), F32),
        jax.ShapeDtypeStruct((r, d_attn), BF16),
        jax.ShapeDtypeStruct((r, LANES), BF16),
        jax.ShapeDtypeStruct((r, LANES), BF16),
        jax.ShapeDtypeStruct((r, d_conv), BF16),
    )
    args = (x, mod4, mod4, n1g, wglu, wqkvf_t, bf_col, qg_col, kg_col, bdtri, conv_w, conv_b, ln_g, ln_b,
            state_t)
    in_specs = [full(x.shape), mod_spec(0), mod_spec(1)] + [full(a.shape) for a in args[3:]]
    return pl.pallas_call(
        _inproj_sample_kernel,
        grid=(1,),
        in_specs=in_specs,
        out_specs=tuple(full(o.shape) for o in out_shape),
        out_shape=out_shape,
        scratch_shapes=[pltpu.VMEM((d_conv // LANES, r, LANES), F32),
                        pltpu.VMEM((n_state + tt, nb, d_conv), F32),
                        pltpu.VMEM((d_conv // LANES, r, LANES), F32)],
        compiler_params=pltpu.CompilerParams(dimension_semantics=("arbitrary",),
                                             vmem_limit_bytes=VMEM_LIMIT_BYTES),
        name="inproj_sample",
    )(*args)


def _attn_sample_kernel(q_ref, qb_ref, kn_ref, vn_ref, kbn_ref, kt_ref, vt_ref, clft_ref, ltri_ref, o_ref):
    tt = q_ref.shape[1]
    d_attn = q_ref.shape[2]
    p_len = kt_ref.shape[2]
    nblk = p_len // MXU_DIM
    rows = N_HEADS * tt

    parts = jnp.concatenate(_split3(clft_ref[0]), axis=0)
    blocks = [parts[:, i * MXU_DIM:(i + 1) * MXU_DIM] for i in range(nblk)]
    local = _dot(jnp.concatenate(blocks, axis=0).astype(BF16), ltri_ref[...])
    off = jnp.zeros((N_FEAT, 1), F32)
    sufs = [None] * nblk
    for i in reversed(range(nblk)):
        loc = local[i * N_FEAT:(i + 1) * N_FEAT]
        sufs[i] = loc + off
        off = off + loc[:, 0:1] + blocks[i][:, 0:1]
    suf_parts = jnp.concatenate(sufs, axis=1)
    ck_rel = suf_parts[0:8] + suf_parts[8:16] + suf_parts[16:24]
    c_hi, c_mid, c_lo = _split3(ck_rel)
    kbt = jnp.concatenate([jnp.ones((N_FEAT, p_len), F32), c_hi, c_mid, c_lo,
                           jnp.zeros((LANES - 2 * N_FEAT, p_len), F32)], axis=0)
    k_all = jnp.concatenate([kt_ref[0].astype(BF16), kbt.astype(BF16)], axis=0)

    lane_q = lax.broadcasted_iota(jnp.int32, (tt, d_attn), 1)
    lane_b = lax.broadcasted_iota(jnp.int32, (tt, LANES), 1)
    q = q_ref[0].astype(F32)
    qb = qb_ref[0].astype(F32)
    stack = []
    for h in range(N_HEADS):
        qm = jnp.where((lane_q >= h * HEAD_DIM) & (lane_q < (h + 1) * HEAD_DIM), q, 0.0)
        bm = jnp.where((lane_b % N_HEADS) == h, qb, 0.0)
        stack.append(jnp.concatenate([qm, bm], axis=1))
    qs = jnp.concatenate(stack, axis=0).astype(BF16)

    s_c = _dot(qs, k_all)

    pad = jnp.zeros((LANES - tt, d_attn + LANES), F32)
    kn = jnp.concatenate([jnp.concatenate([kn_ref[0], kbn_ref[0].astype(F32)], axis=1), pad], axis=0)
    s_n = _dot_nt(qs, kn.astype(BF16))
    row = lax.broadcasted_iota(jnp.int32, s_n.shape, 0)
    col = lax.broadcasted_iota(jnp.int32, s_n.shape, 1)
    s_n = jnp.where(col <= (row % tt), s_n, NEG_INF)

    m = jnp.maximum(jnp.max(s_c, axis=-1, keepdims=True), jnp.max(s_n, axis=-1, keepdims=True))
    p_c = jnp.exp(s_c - m)
    p_n = jnp.exp(s_n - m)
    l = jnp.sum(p_c, axis=-1, keepdims=True) + jnp.sum(p_n, axis=-1, keepdims=True)
    vn = jnp.concatenate([vn_ref[0], jnp.zeros((LANES - tt, d_attn), F32)], axis=0).astype(BF16)
    o = _dot_nt(p_c.astype(BF16), vt_ref[0].astype(BF16)) + _dot(p_n.astype(BF16), vn)
    o = o / l

    out = jnp.zeros((tt, d_attn), F32)
    for h in range(N_HEADS):
        out = jnp.where((lane_q >= h * HEAD_DIM) & (lane_q < (h + 1) * HEAD_DIM), o[h * tt:(h + 1) * tt], out)
    o_ref[0] = out.astype(BF16)


def _attn_sample(q, qb, kn, vn, kbn, kt, vt, clft, ltri):
    nb, tt, d_attn = q.shape
    p_len = kt.shape[2]
    per_req = lambda shape: pl.BlockSpec((1,) + shape, lambda i: (i, 0, 0))
    return pl.pallas_call(
        _attn_sample_kernel,
        grid=(nb,),
        in_specs=[per_req((tt, d_attn)), per_req((tt, LANES)), per_req((tt, d_attn)), per_req((tt, d_attn)),
                  per_req((tt, LANES)), per_req((d_attn, p_len)), per_req((d_attn, p_len)),
                  per_req((N_HEADS, p_len)), _const_spec(ltri.shape)],
        out_specs=per_req((tt, d_attn)),
        out_shape=jax.ShapeDtypeStruct((nb, tt, d_attn), BF16),
        compiler_params=pltpu.CompilerParams(dimension_semantics=("arbitrary",),
                                             vmem_limit_bytes=VMEM_LIMIT_BYTES),
        name="attn_sample",
    )(q, qb, kn, vn, kbn, kt, vt, clft, ltri)


def _outproj_ffn_kernel(x_ref, co_ref, at_ref, g1_ref, sh2_ref, sc2_ref, g2_ref, n2g_ref,
                        wo_ref, wg_ref, wu_ref, wd_ref, y_ref, acc_ref):
    nb, tt, d = x_ref.shape
    r = nb * tt
    d_conv = co_ref.shape[2]
    co = co_ref[...].reshape(r, d_conv)
    at = at_ref[...].reshape(r, at_ref.shape[2])
    mix = _dot(co, wo_ref[0:d_conv, :]) + _dot(at, wo_ref[d_conv:, :])
    x1 = x_ref[...] + g1_ref[:, 0] * mix.reshape(nb, tt, d)
    hb = _adaln_rmsnorm(x1, n2g_ref[...], sc2_ref[:, 0], sh2_ref[:, 0]).reshape(r, d).astype(BF16)
    for c in range(wg_ref.shape[0]):
        g = _dot(hb, wg_ref[c])
        a = ((g * jax.nn.sigmoid(g)) * _dot(hb, wu_ref[c])).astype(BF16)
        part = _dot(a, wd_ref[c])
        if c == 0:
            acc_ref[...] = part
        else:
            acc_ref[...] += part
    y_ref[...] = x1 + g2_ref[:, 0] * acc_ref[...].reshape(nb, tt, d)


def _outproj_ffn(x, co, at, mod4, n2g, wo, wg3, wu3, wd3, bb, tt):
    nb, s, d = x.shape
    d_conv = co.shape[2]
    d_attn = at.shape[2]
    mod_spec = lambda j: pl.BlockSpec((bb, 1, 1, d), lambda i, t: (i, j, 0, 0))
    row_spec = lambda w: pl.BlockSpec((bb, tt, w), lambda i, t: (i, t, 0))
    return pl.pallas_call(
        _outproj_ffn_kernel,
        grid=(nb // bb, s // tt),
        in_specs=[row_spec(d), row_spec(d_conv), row_spec(d_attn),
                  mod_spec(2), mod_spec(3), mod_spec(4), mod_spec(5),
                  _const_spec(n2g.shape), _const_spec(wo.shape), _const_spec(wg3.shape),
                  _const_spec(wu3.shape), _const_spec(wd3.shape)],
        out_specs=row_spec(d),
        out_shape=jax.ShapeDtypeStruct((nb, s, d), F32),
        scratch_shapes=[pltpu.VMEM((bb * tt, d), F32)],
        compiler_params=pltpu.CompilerParams(dimension_semantics=("arbitrary", "arbitrary"),
                                             vmem_limit_bytes=VMEM_LIMIT_BYTES),
        name="outproj_ffn",
    )(x, co, at, mod4, mod4, mod4, mod4, n2g, wo, wg3, wu3, wd3)


def _tri(n, rel):
    i = lax.broadcasted_iota(jnp.int32, (n, n), 0)
    j = lax.broadcasted_iota(jnp.int32, (n, n), 1)
    return rel(i, j).astype(BF16)


def _layer(xp, xs, cache_k, cache_v, cache_logf, state_conv, c_all, w_ada, b_ada, norm1_g, w_in, b_f,
           q_norm_g, k_norm_g, conv_w, conv_b, conv_ln_g, conv_ln_b, w_out, norm2_g, w_gate, w_up, w_down):
    b, s, d = xp.shape
    nb, tt, _ = xs.shape
    d_conv = conv_w.shape[1]
    d_attn = N_HEADS * HEAD_DIM
    d_ff = w_gate.shape[1]
    p_len = cache_k.shape[1]

    mod4 = _modulation(c_all, w_ada, b_ada).reshape(b + nb, 6, 1, d)
    mod_p, mod_s = mod4[:b], mod4[b:]

    wglu = w_in[:, :2 * d_conv].astype(BF16)
    wqkvf_t = jnp.pad(w_in[:, 2 * d_conv:].T, ((0, LANES - N_HEADS), (0, 0))).astype(BF16)
    bf_col = b_f.reshape(N_HEADS, 1)
    qg_col = q_norm_g.reshape(d_attn, 1)
    kg_col = k_norm_g.reshape(d_attn, 1)
    row = lambda v: v.reshape(1, -1)
    n1g, n2g, cb, ln_g, ln_b = row(norm1_g), row(norm2_g), row(conv_b), row(conv_ln_g), row(conv_ln_b)
    wo = w_out.astype(BF16)
    n_ff = d_ff // FF_CHUNK
    wg3 = w_gate.astype(BF16).reshape(d, n_ff, FF_CHUNK).transpose(1, 0, 2)
    wu3 = w_up.astype(BF16).reshape(d, n_ff, FF_CHUNK).transpose(1, 0, 2)
    wd3 = w_down.astype(BF16).reshape(n_ff, FF_CHUNK, d)
    utri = _tri(MXU_DIM, lambda i, j: i <= j)
    ltri = _tri(MXU_DIM, lambda i, j: i > j)
    bdtri = _tri(nb * tt, lambda i, j: (i // tt == j // tt) & (j <= i))

    kt, vt, lft, cst, qt, qft, k, kb, co = _inproj_prompt(
        xp, mod_p, n1g, wglu, wqkvf_t, bf_col, qg_col, kg_col, utri, conv_w, cb, ln_g, ln_b)
    at = _attn_prompt(qt, qft, k, kb, vt)
    yp = _outproj_ffn(xp, co, at, mod_p, n2g, wo, wg3, wu3, wd3, 1, ROW_TILE)
    k_p = kt.reshape(b, N_HEADS, HEAD_DIM, s).transpose(0, 3, 1, 2)
    v_p = vt.reshape(b, N_HEADS, HEAD_DIM, s).transpose(0, 3, 1, 2)
    lf_p = lft.transpose(0, 2, 1)

    state_t = state_conv.transpose(1, 0, 2)
    k_s, v_s, lf_s, nst, q_s, qb_s, kb_s, co_s = _inproj_sample(
        xs, mod_s, n1g, wglu, wqkvf_t, bf_col, qg_col, kg_col, bdtri, conv_w, cb, ln_g, ln_b, state_t)
    r3 = lambda a: a.reshape(nb, tt, a.shape[-1])
    ckt = cache_k.transpose(0, 2, 3, 1).reshape(nb, d_attn, p_len)
    cvt = cache_v.transpose(0, 2, 3, 1).reshape(nb, d_attn, p_len)
    clft = cache_logf.transpose(0, 2, 1)
    at_s = _attn_sample(r3(q_s), r3(qb_s), r3(k_s), r3(v_s), r3(kb_s), ckt, cvt, clft, ltri)
    ys = _outproj_ffn(xs, r3(co_s), at_s, mod_s, n2g, wo, wg3, wu3, wd3, nb, tt)
    k_sn = k_s.reshape(nb, tt, N_HEADS, HEAD_DIM)
    v_sn = v_s.reshape(nb, tt, N_HEADS, HEAD_DIM)
    lf_sn = lf_s[:, :N_HEADS].reshape(nb, tt, N_HEADS)
    return yp, ys, (k_p, v_p, lf_p, cst), (k_sn, v_sn, lf_sn, nst.transpose(1, 0, 2))


def kernel(x_prompt, x_sample, cache_k, cache_v, cache_logf, state_conv, c_prompt, c_sample, w_ada, b_ada,
           norm1_g, w_in, b_f, q_norm_g, k_norm_g, conv_w, conv_b, conv_ln_g, conv_ln_b, w_out, norm2_g,
           w_gate, w_up, w_down):
    depth = w_ada.shape[0]
    c_all = jnp.concatenate([c_prompt, c_sample], axis=0)
    yp, ys = x_prompt, x_sample
    st_p, st_s = [], []
    for l in range(depth):
        yp, ys, sp, ss = _layer(
            yp, ys, cache_k[l], cache_v[l], cache_logf[l], state_conv[l], c_all, w_ada[l], b_ada[l],
            norm1_g[l], w_in[l], b_f[l], q_norm_g[l], k_norm_g[l], conv_w[l], conv_b[l], conv_ln_g[l],
            conv_ln_b[l], w_out[l], norm2_g[l], w_gate[l], w_up[l], w_down[l])
        st_p.append(sp)
        st_s.append(ss)
    stack = lambda xs: xs[0][None] if len(xs) == 1 else jnp.stack(xs)
    outs_p = [stack([s[i] for s in st_p]) for i in range(4)]
    outs_s = [stack([s[i] for s in st_s]) for i in range(4)]
    return (yp, ys, *outs_p, *outs_s)
```

```python
import functools

import jax
import jax.numpy as jnp
from jax import lax
from jax.experimental import pallas as pl
from jax.experimental.pallas import tpu as pltpu

F32 = jnp.float32
BF16 = jnp.bfloat16

N_HEADS = 8
HEAD_DIM = 64
CONV_WIDTH = 31
EPS = 1e-6
NEG_INF = -1e30

LANES = 128
MXU_DIM = 256
VMEM_LIMIT_BYTES = 56 * 1024 * 1024

N_PARTS = 3
N_FEAT = N_PARTS * N_HEADS
HIST = 32
CONV_ROWS = 32

ROW_TILE = 512
Q_TILE = 256
ATTN_LAG = 2
DENOM_ROWS = 16
LOG2E = 1.4426950408889634
FF_CHUNK = 256


def _dot(a, b):
    return jnp.dot(a, b, preferred_element_type=F32)


def _dot_nt(a, b):
    return lax.dot_general(a, b, (((1,), (1,)), ((), ())), preferred_element_type=F32)


def _split3(x):
    hi = x.astype(BF16).astype(F32)
    r = x - hi
    mid = r.astype(BF16).astype(F32)
    lo = (r - mid).astype(BF16).astype(F32)
    return hi, mid, lo


def _log_sigmoid(x):
    return jnp.minimum(x, 0.0) - jnp.log1p(jnp.exp(-jnp.abs(x)))


def _adaln_rmsnorm(x, g, scale, shift):
    y = x * lax.rsqrt(jnp.mean(x * x, axis=-1, keepdims=True) + EPS)
    return (y * g) * (1.0 + scale) + shift


def _const_spec(shape):
    n = len(shape)
    return pl.BlockSpec(shape, lambda *_: (0,) * n, pipeline_mode=pl.Buffered(1))


def _modulation_kernel(c_ref, w_ref, b_ref, o_ref):
    c = c_ref[...]
    a = (c * jax.nn.sigmoid(c)).astype(BF16)
    o_ref[...] = _dot(a, w_ref[...].astype(BF16)) + b_ref[...]


def _modulation(c, w_ada, b_ada):
    nb, d = c.shape
    n = w_ada.shape[1]
    tn = 768
    return pl.pallas_call(
        _modulation_kernel,
        grid=(n // tn,),
        in_specs=[pl.BlockSpec((nb, d), lambda j: (0, 0)),
                  pl.BlockSpec((d, tn), lambda j: (0, j)),
                  pl.BlockSpec((1, tn), lambda j: (0, j))],
        out_specs=pl.BlockSpec((nb, tn), lambda j: (0, j)),
        out_shape=jax.ShapeDtypeStruct((nb, n), F32),
        compiler_params=pltpu.CompilerParams(dimension_semantics=("arbitrary",),
                                             vmem_limit_bytes=VMEM_LIMIT_BYTES),
        name="modulation",
    )(c, w_ada, b_ada.reshape(1, n))


def _qkvf_feature_major(hb, wqkvf_t, qg_col, kg_col, bf_col):
    d_attn = N_HEADS * HEAD_DIM
    zt = _dot_nt(wqkvf_t, hb)
    r = zt.shape[1]

    def head_rms(z, g_col):
        z3 = z.reshape(N_HEADS, HEAD_DIM, r)
        ms = jnp.mean(z3 * z3, axis=1, keepdims=True)
        return (z3 * lax.rsqrt(ms + EPS)).reshape(d_attn, r) * g_col

    q_t = head_rms(zt[0:d_attn], qg_col)
    k_t = head_rms(zt[d_attn:2 * d_attn], kg_col)
    v_t = zt[2 * d_attn:3 * d_attn]
    lf_t = _log_sigmoid(zt[3 * d_attn:3 * d_attn + N_HEADS] + bf_col)
    return q_t, k_t, v_t, lf_t


def _conv_ln_swish(load_rows, cw_ref, cb, ln_g, ln_b):
    acc = cb + cw_ref[0:1, :] * load_rows(0)
    for k in range(1, CONV_WIDTH):
        acc = acc + cw_ref[k:k + 1, :] * load_rows(k)
    mu = jnp.mean(acc, axis=-1, keepdims=True)
    cen = acc - mu
    var = jnp.mean(cen * cen, axis=-1, keepdims=True)
    y = cen * lax.rsqrt(var + EPS) * ln_g + ln_b
    return y * jax.nn.sigmoid(y)


def _inproj_prompt_kernel(x_ref, sh_ref, sc_ref, n1g_ref, wglu_ref, wqkvf_ref, bf_ref, qg_ref, kg_ref,
                          utri_ref, cw_ref, cb_ref, lng_ref, lnb_ref,
                          kt_ref, vt_ref, lft_ref, cst_ref, qt_ref, qft_ref, k_ref, kb_ref, co_ref,
                          ubuf_ref, carry_ref):
    t = pl.program_id(1)
    tm = x_ref.shape[1]
    d_conv = co_ref.shape[2]
    d_attn = N_HEADS * HEAD_DIM

    @pl.when(t == 0)
    def _():
        ubuf_ref[0:HIST, :] = jnp.zeros((HIST, d_conv), F32)
        carry_ref[...] = jnp.zeros(carry_ref.shape, F32)

    hb = _adaln_rmsnorm(x_ref[0], n1g_ref[...], sc_ref[0, 0], sh_ref[0, 0]).astype(BF16)

    zg = _dot(hb, wglu_ref[...])
    ubuf_ref[HIST:HIST + tm, :] = zg[:, :d_conv] * jax.nn.sigmoid(zg[:, d_conv:])

    q_t, k_t, v_t, lf_t = _qkvf_feature_major(hb, wqkvf_ref[...], qg_ref[...], kg_ref[...], bf_ref[...])
    kt_ref[0] = k_t
    vt_ref[0] = v_t
    lft_ref[0] = lf_t

    nblk = tm // MXU_DIM
    parts = jnp.concatenate(_split3(lf_t), axis=0)
    stacked = jnp.concatenate([parts[:, i * MXU_DIM:(i + 1) * MXU_DIM] for i in range(nblk)], axis=0)
    local = _dot(stacked.astype(BF16), utri_ref[...])
    carry = carry_ref[:, 0:1]
    cums = []
    for i in range(nblk):
        loc = local[i * N_FEAT:(i + 1) * N_FEAT]
        cums.append(loc + carry)
        carry = carry + loc[:, MXU_DIM - 1:MXU_DIM]
    carry_ref[...] = jnp.broadcast_to(carry, carry_ref.shape)
    cum_parts = jnp.concatenate(cums, axis=1)
    cum_t = (cum_parts[0:8] + cum_parts[8:16] + cum_parts[16:24]) * LOG2E

    c_hi, c_mid, c_lo = _split3(cum_t)
    ones = jnp.ones((N_FEAT, tm), F32)
    zeros = jnp.zeros((LANES - 2 * N_FEAT, tm), F32)
    kfeat_t = jnp.concatenate([ones, -c_hi, -c_mid, -c_lo, zeros], axis=0)
    qft_ref[0] = jnp.concatenate([c_hi, c_mid, c_lo, ones, zeros], axis=0).astype(BF16)
    qt_ref[0] = (q_t * (HEAD_DIM ** -0.5 * LOG2E)).astype(BF16)
    k_ref[0] = k_t.T.astype(BF16)
    kb_ref[0] = kfeat_t.T.astype(BF16)

    cb, ln_g, ln_b = cb_ref[...], lng_ref[...], lnb_ref[...]
    first = HIST - (CONV_WIDTH - 1)
    for c in range(tm // CONV_ROWS):
        base = c * CONV_ROWS + first
        y = _conv_ln_swish(lambda k: ubuf_ref[base + k:base + k + CONV_ROWS, :], cw_ref, cb, ln_g, ln_b)
        co_ref[0, c * CONV_ROWS:(c + 1) * CONV_ROWS, :] = y.astype(BF16)

    @pl.when(t == pl.num_programs(1) - 1)
    def _():
        cst_ref[0] = ubuf_ref[HIST + tm - (CONV_WIDTH - 1):HIST + tm, :]

    ubuf_ref[0:HIST, :] = ubuf_ref[tm:tm + HIST, :]


def _inproj_prompt(x, mod4, n1g, wglu, wqkvf_t, bf_col, qg_col, kg_col, utri, conv_w, conv_b, ln_g, ln_b):
    b, s, d = x.shape
    tm = ROW_TILE
    d_conv = conv_w.shape[1]
    d_attn = N_HEADS * HEAD_DIM
    mod_spec = lambda j: pl.BlockSpec((1, 1, 1, d), lambda i, t: (i, j, 0, 0))
    out_shape = (
        jax.ShapeDtypeStruct((b, d_attn, s), F32),
        jax.ShapeDtypeStruct((b, d_attn, s), F32),
        jax.ShapeDtypeStruct((b, N_HEADS, s), F32),
        jax.ShapeDtypeStruct((b, CONV_WIDTH - 1, d_conv), F32),
        jax.ShapeDtypeStruct((b, d_attn, s), BF16),
        jax.ShapeDtypeStruct((b, LANES, s), BF16),
        jax.ShapeDtypeStruct((b, s, d_attn), BF16),
        jax.ShapeDtypeStruct((b, s, LANES), BF16),
        jax.ShapeDtypeStruct((b, s, d_conv), BF16),
    )
    out_specs = (
        pl.BlockSpec((1, d_attn, tm), lambda i, t: (i, 0, t)),
        pl.BlockSpec((1, d_attn, tm), lambda i, t: (i, 0, t)),
        pl.BlockSpec((1, N_HEADS, tm), lambda i, t: (i, 0, t)),
        pl.BlockSpec((1, CONV_WIDTH - 1, d_conv), lambda i, t: (i, 0, 0)),
        pl.BlockSpec((1, d_attn, tm), lambda i, t: (i, 0, t)),
        pl.BlockSpec((1, LANES, tm), lambda i, t: (i, 0, t)),
        pl.BlockSpec((1, tm, d_attn), lambda i, t: (i, t, 0)),
        pl.BlockSpec((1, tm, LANES), lambda i, t: (i, t, 0)),
        pl.BlockSpec((1, tm, d_conv), lambda i, t: (i, t, 0)),
    )
    in_specs = [
        pl.BlockSpec((1, tm, d), lambda i, t: (i, t, 0)),
        mod_spec(0), mod_spec(1),
        _const_spec(n1g.shape), _const_spec(wglu.shape), _const_spec(wqkvf_t.shape),
        _const_spec(bf_col.shape), _const_spec(qg_col.shape), _const_spec(kg_col.shape),
        _const_spec(utri.shape), _const_spec(conv_w.shape), _const_spec(conv_b.shape),
        _const_spec(ln_g.shape), _const_spec(ln_b.shape),
    ]
    return pl.pallas_call(
        _inproj_prompt_kernel,
        grid=(b, s // tm),
        in_specs=in_specs,
        out_specs=out_specs,
        out_shape=out_shape,
        scratch_shapes=[pltpu.VMEM((tm + HIST, d_conv), F32), pltpu.VMEM((N_FEAT, LANES), F32)],
        compiler_params=pltpu.CompilerParams(dimension_semantics=("arbitrary", "arbitrary"),
                                             vmem_limit_bytes=VMEM_LIMIT_BYTES),
        name="inproj_prompt",
    )(x, mod4, mod4, n1g, wglu, wqkvf_t, bf_col, qg_col, kg_col, utri, conv_w, conv_b, ln_g, ln_b)


def _attn_prompt_kernel(qt_ref, qft_ref, k_ref, kb_ref, vt_ref, o_ref,
                        kp_ref, vp_ref, q2_ref, m_ref, acc_ref):
    qi = pl.program_id(1)
    tq = qt_ref.shape[2]
    n_pairs, nblk = vp_ref.shape[0], vp_ref.shape[1]

    @pl.when(qi == 0)
    def _():
        ones = jnp.ones((vp_ref.shape[2] - LANES, tq), BF16)
        for p in range(n_pairs):
            kp_ref[p, :, 0:LANES] = k_ref[0, :, p * LANES:(p + 1) * LANES]
            kp_ref[p, :, LANES:2 * LANES] = kb_ref[0]
            for i in range(nblk):
                vp_ref[p, i, 0:LANES, :] = vt_ref[0, p * LANES:(p + 1) * LANES, i * tq:(i + 1) * tq].astype(BF16)
                vp_ref[p, i, LANES:, :] = ones

    frow = lax.broadcasted_iota(jnp.int32, (LANES, tq), 0)
    feat = qft_ref[0].astype(F32)
    for p in range(n_pairs):
        qp = qt_ref[0, p * LANES:(p + 1) * LANES, :].astype(F32)
        cols = []
        for j in range(2):
            qm = jnp.where((frow >= j * HEAD_DIM) & (frow < (j + 1) * HEAD_DIM), qp, 0.0)
            fm = jnp.where((frow % N_HEADS) == 2 * p + j, feat, 0.0)
            cols.append(jnp.concatenate([qm, fm], axis=0))
        q2_ref[p] = jnp.concatenate(cols, axis=1).astype(BF16)

    m_ref[...] = jnp.full(m_ref.shape, NEG_INF, F32)
    acc_ref[...] = jnp.zeros(acc_ref.shape, F32)

    def scores(p, j):
        start = pl.multiple_of(j * tq, tq)
        return _dot(kp_ref[p, pl.ds(start, tq), :], q2_ref[p])

    def accumulate(p, j, masked, s):
        if masked:
            key = lax.broadcasted_iota(jnp.int32, s.shape, 0)
            qry = lax.broadcasted_iota(jnp.int32, s.shape, 1)
            qry = jnp.where(qry >= tq, qry - tq, qry)
            s = jnp.where(key <= qry, s, NEG_INF)
        m_old = m_ref[p]
        m_new = jnp.maximum(m_old, jnp.max(s, axis=0, keepdims=True))
        pr = jnp.exp2(s - m_new).astype(BF16)
        acc_ref[p] = jnp.exp2(m_old - m_new) * acc_ref[p] + _dot(vp_ref[p, j], pr)
        m_ref[p] = m_new

    def run_units(units):
        pending = {}
        for i in range(len(units) + ATTN_LAG):
            if i < len(units):
                p, j, _ = units[i]
                pending[i] = scores(p, j)
            if i >= ATTN_LAG:
                p, j, masked = units[i - ATTN_LAG]
                accumulate(p, j, masked, pending.pop(i - ATTN_LAG))

    def block(j, masked):
        return [(p, j, masked) for p in range(n_pairs)]

    def body(i, carry):
        run_units(block(2 * i, False) + block(2 * i + 1, False))
        return carry

    lax.fori_loop(0, qi // 2, body, 0)

    @pl.when(qi % 2 == 1)
    def _():
        run_units(block(qi - 1, False) + block(qi, True))

    @pl.when(qi % 2 == 0)
    def _():
        run_units(block(qi, True))

    for p in range(n_pairs):
        acc = acc_ref[p]
        out_t = acc[0:LANES] / acc[LANES:LANES + 1]
        pair_t = jnp.where(frow < HEAD_DIM, out_t[:, :tq], out_t[:, tq:])
        o_ref[0, :, p * LANES:(p + 1) * LANES] = pair_t.T.astype(BF16)


def _attn_prompt(qt, qft, k, kb, vt):
    b, d_attn, s = qt.shape
    tq = Q_TILE
    nblk = s // tq
    n_pairs = d_attn // LANES
    return pl.pallas_call(
        _attn_prompt_kernel,
        grid=(b, nblk),
        in_specs=[
            pl.BlockSpec((1, d_attn, tq), lambda i, j: (i, 0, j)),
            pl.BlockSpec((1, LANES, tq), lambda i, j: (i, 0, j)),
            pl.BlockSpec((1, s, d_attn), lambda i, j: (i, 0, 0)),
            pl.BlockSpec((1, s, LANES), lambda i, j: (i, 0, 0)),
            pl.BlockSpec((1, d_attn, s), lambda i, j: (i, 0, 0)),
        ],
        out_specs=pl.BlockSpec((1, tq, d_attn), lambda i, j: (i, j, 0)),
        out_shape=jax.ShapeDtypeStruct((b, s, d_attn), BF16),
        scratch_shapes=[
            pltpu.VMEM((n_pairs, s, 2 * LANES), BF16),
            pltpu.VMEM((n_pairs, nblk, LANES + DENOM_ROWS, tq), BF16),
            pltpu.VMEM((n_pairs, 2 * LANES, 2 * tq), BF16),
            pltpu.VMEM((n_pairs, 1, 2 * tq), F32),
            pltpu.VMEM((n_pairs, LANES + DENOM_ROWS, 2 * tq), F32),
        ],
        compiler_params=pltpu.CompilerParams(dimension_semantics=("arbitrary", "arbitrary"),
                                             vmem_limit_bytes=VMEM_LIMIT_BYTES),
        name="attn_prompt",
    )(qt, qft, k, kb, vt)


def _inproj_sample_kernel(x_ref, sh_ref, sc_ref, n1g_ref, wglu_ref, wqkvf_ref, bf_ref, qg_ref, kg_ref,
                          bdtri_ref, cw_ref, cb_ref, lng_ref, lnb_ref, st_ref,
                          k_ref, v_ref, lf_ref, nst_ref, q_ref, qb_ref, kb_ref, co_ref,
                          u_ref, hist_ref, cof_ref):
    nb, tt, d = x_ref.shape
    r = nb * tt
    d_conv = co_ref.shape[1]
    n_state = CONV_WIDTH - 1

    h = _adaln_rmsnorm(x_ref[...], n1g_ref[...], sc_ref[:, 0], sh_ref[:, 0])
    hb = h.reshape(r, d).astype(BF16)

    n_chunks = d_conv // LANES
    zg = _dot(hb, wglu_ref[...])
    u = zg[:, :d_conv] * jax.nn.sigmoid(zg[:, d_conv:])
    for c in range(n_chunks):
        u_ref[c] = u[:, c * LANES:(c + 1) * LANES]

    q_t, k_t, v_t, lf_t = _qkvf_feature_major(hb, wqkvf_ref[...], qg_ref[...], kg_ref[...], bf_ref[...])
    k_ref[...] = k_t.T
    v_ref[...] = v_t.T
    q_ref[...] = (q_t * (HEAD_DIM ** -0.5)).T.astype(BF16)
    lf = jnp.concatenate([lf_t, jnp.zeros((LANES - N_HEADS, r), F32)], axis=0).T
    lf_ref[...] = lf

    lane = lax.broadcasted_iota(jnp.int32, (r, LANES), 1)
    hi, mid, lo = _split3(lf)
    packed = hi + pltpu.roll(mid, N_HEADS, 1) + pltpu.roll(lo, 2 * N_HEADS, 1)
    c = _dot(bdtri_ref[...], packed.astype(BF16))
    cn = jnp.where(lane < N_HEADS,
                   c + pltpu.roll(c, LANES - N_HEADS, 1) + pltpu.roll(c, LANES - 2 * N_HEADS, 1), 0.0)
    hi, mid, lo = _split3(cn)
    p = hi + pltpu.roll(mid, N_HEADS, 1) + pltpu.roll(lo, 2 * N_HEADS, 1)
    qb_ref[...] = (p + jnp.where((lane >= N_FEAT) & (lane < 2 * N_FEAT), 1.0, 0.0)).astype(BF16)
    kb_ref[...] = (jnp.where(lane < N_FEAT, 1.0, 0.0) - pltpu.roll(p, N_FEAT, 1)).astype(BF16)

    hist_ref[0:n_state] = st_ref[...]
    for t in range(tt):
        for c in range(n_chunks):
            hist_ref[n_state + t, :, c * LANES:(c + 1) * LANES] = u_ref[c, pl.ds(t, nb, stride=tt), :]
    nst_ref[...] = hist_ref[tt:tt + n_state]
    cb, ln_g, ln_b = cb_ref[...], lng_ref[...], lnb_ref[...]
    for t in range(tt):
        y = _conv_ln_swish(lambda k: hist_ref[t + k], cw_ref, cb, ln_g, ln_b)
        for c in range(n_chunks):
            cof_ref[c, pl.ds(t, nb, stride=tt), :] = y[:, c * LANES:(c + 1) * LANES]
    co_ref[...] = jnp.concatenate([cof_ref[c] for c in range(n_chunks)], axis=1).astype(BF16)


def _inproj_sample(x, mod4, n1g, wglu, wqkvf_t, bf_col, qg_col, kg_col, bdtri, conv_w, conv_b, ln_g, ln_b,
                   state_t):
    nb, tt, d = x.shape
    r = nb * tt
    d_conv = conv_w.shape[1]
    d_attn = N_HEADS * HEAD_DIM
    n_state = CONV_WIDTH - 1
    mod_spec = lambda j: pl.BlockSpec((nb, 1, 1, d), lambda i: (0, j, 0, 0))
    full = lambda shape: pl.BlockSpec(shape, lambda i: (0,) * len(shape))
    out_shape = (
        jax.ShapeDtypeStruct((r, d_attn), F32),
        jax.ShapeDtypeStruct((r, d_attn), F32),
        jax.ShapeDtypeStruct((r, LANES), F32),
        jax.ShapeDtypeStruct((n_state, nb, d_conv), F32),
        jax.ShapeDtypeStruct((r, d_attn), BF16),
        jax.ShapeDtypeStruct((r, LANES), BF16),
        jax.ShapeDtypeStruct((r, LANES), BF16),
        jax.ShapeDtypeStruct((r, d_conv), BF16),
    )
    args = (x, mod4, mod4, n1g, wglu, wqkvf_t, bf_col, qg_col, kg_col, bdtri, conv_w, conv_b, ln_g, ln_b,
            state_t)
    in_specs = [full(x.shape), mod_spec(0), mod_spec(1)] + [full(a.shape) for a in args[3:]]
    return pl.pallas_call(
        _inproj_sample_kernel,
        grid=(1,),
        in_specs=in_specs,
        out_specs=tuple(full(o.shape) for o in out_shape),
        out_shape=out_shape,
        scratch_shapes=[pltpu.VMEM((d_conv // LANES, r, LANES), F32),
                        pltpu.VMEM((n_state + tt, nb, d_conv), F32),
                        pltpu.VMEM((d_conv // LANES, r, LANES), F32)],
        compiler_params=pltpu.CompilerParams(dimension_semantics=("arbitrary",),
                                             vmem_limit_bytes=VMEM_LIMIT_BYTES),
        name="inproj_sample",
    )(*args)


def _attn_sample_kernel(q_ref, qb_ref, kn_ref, vn_ref, kbn_ref, kt_ref, vt_ref, clft_ref, ltri_ref, o_ref):
    tt = q_ref.shape[1]
    d_attn = q_ref.shape[2]
    p_len = kt_ref.shape[2]
    nblk = p_len // MXU_DIM
    rows = N_HEADS * tt

    parts = jnp.concatenate(_split3(clft_ref[0]), axis=0)
    blocks = [parts[:, i * MXU_DIM:(i + 1) * MXU_DIM] for i in range(nblk)]
    local = _dot(jnp.concatenate(blocks, axis=0).astype(BF16), ltri_ref[...])
    off = jnp.zeros((N_FEAT, 1), F32)
    sufs = [None] * nblk
    for i in reversed(range(nblk)):
        loc = local[i * N_FEAT:(i + 1) * N_FEAT]
        sufs[i] = loc + off
        off = off + loc[:, 0:1] + blocks[i][:, 0:1]
    suf_parts = jnp.concatenate(sufs, axis=1)
    ck_rel = suf_parts[0:8] + suf_parts[8:16] + suf_parts[16:24]
    c_hi, c_mid, c_lo = _split3(ck_rel)
    kbt = jnp.concatenate([jnp.ones((N_FEAT, p_len), F32), c_hi, c_mid, c_lo,
                           jnp.zeros((LANES - 2 * N_FEAT, p_len), F32)], axis=0)
    k_all = jnp.concatenate([kt_ref[0].astype(BF16), kbt.astype(BF16)], axis=0)

    lane_q = lax.broadcasted_iota(jnp.int32, (tt, d_attn), 1)
    lane_b = lax.broadcasted_iota(jnp.int32, (tt, LANES), 1)
    q = q_ref[0].astype(F32)
    qb = qb_ref[0].astype(F32)
    stack = []
    for h in range(N_HEADS):
        qm = jnp.where((lane_q >= h * HEAD_DIM) & (lane_q < (h + 1) * HEAD_DIM), q, 0.0)
        bm = jnp.where((lane_b % N_HEADS) == h, qb, 0.0)
        stack.append(jnp.concatenate([qm, bm], axis=1))
    qs = jnp.concatenate(stack, axis=0).astype(BF16)

    s_c = _dot(qs, k_all)

    pad = jnp.zeros((LANES - tt, d_attn + LANES), F32)
    kn = jnp.concatenate([jnp.concatenate([kn_ref[0], kbn_ref[0].astype(F32)], axis=1), pad], axis=0)
    s_n = _dot_nt(qs, kn.astype(BF16))
    row = lax.broadcasted_iota(jnp.int32, s_n.shape, 0)
    col = lax.broadcasted_iota(jnp.int32, s_n.shape, 1)
    s_n = jnp.where(col <= (row % tt), s_n, NEG_INF)

    m = jnp.maximum(jnp.max(s_c, axis=-1, keepdims=True), jnp.max(s_n, axis=-1, keepdims=True))
    p_c = jnp.exp(s_c - m)
    p_n = jnp.exp(s_n - m)
    l = jnp.sum(p_c, axis=-1, keepdims=True) + jnp.sum(p_n, axis=-1, keepdims=True)
    vn = jnp.concatenate([vn_ref[0], jnp.zeros((LANES - tt, d_attn), F32)], axis=0).astype(BF16)
    o = _dot_nt(p_c.astype(BF16), vt_ref[0].astype(BF16)) + _dot(p_n.astype(BF16), vn)
    o = o / l

    out = jnp.zeros((tt, d_attn), F32)
    for h in range(N_HEADS):
        out = jnp.where((lane_q >= h * HEAD_DIM) & (lane_q < (h + 1) * HEAD_DIM), o[h * tt:(h + 1) * tt], out)
    o_ref[0] = out.astype(BF16)


def _attn_sample(q, qb, kn, vn, kbn, kt, vt, clft, ltri):
    nb, tt, d_attn = q.shape
    p_len = kt.shape[2]
    per_req = lambda shape: pl.BlockSpec((1,) + shape, lambda i: (i, 0, 0))
    return pl.pallas_call(
        _attn_sample_kernel,
        grid=(nb,),
        in_specs=[per_req((tt, d_attn)), per_req((tt, LANES)), per_req((tt, d_attn)), per_req((tt, d_attn)),
                  per_req((tt, LANES)), per_req((d_attn, p_len)), per_req((d_attn, p_len)),
                  per_req((N_HEADS, p_len)), _const_spec(ltri.shape)],
        out_specs=per_req((tt, d_attn)),
        out_shape=jax.ShapeDtypeStruct((nb, tt, d_attn), BF16),
        compiler_params=pltpu.CompilerParams(dimension_semantics=("arbitrary",),
                                             vmem_limit_bytes=VMEM_LIMIT_BYTES),
        name="attn_sample",
    )(q, qb, kn, vn, kbn, kt, vt, clft, ltri)


def _outproj_ffn_kernel(x_ref, co_ref, at_ref, g1_ref, sh2_ref, sc2_ref, g2_ref, n2g_ref,
                        wo_ref, wg_ref, wu_ref, wd_ref, y_ref, acc_ref):
    nb, tt, d = x_ref.shape
    r = nb * tt
    d_conv = co_ref.shape[2]
    co = co_ref[...].reshape(r, d_conv)
    at = at_ref[...].reshape(r, at_ref.shape[2])
    mix = _dot(co, wo_ref[0:d_conv, :]) + _dot(at, wo_ref[d_conv:, :])
    x1 = x_ref[...] + g1_ref[:, 0] * mix.reshape(nb, tt, d)
    hb = _adaln_rmsnorm(x1, n2g_ref[...], sc2_ref[:, 0], sh2_ref[:, 0]).reshape(r, d).astype(BF16)
    for c in range(wg_ref.shape[0]):
        g = _dot(hb, wg_ref[c])
        a = ((g * jax.nn.sigmoid(g)) * _dot(hb, wu_ref[c])).astype(BF16)
        part = _dot(a, wd_ref[c])
        if c == 0:
            acc_ref[...] = part
        else:
            acc_ref[...] += part
    y_ref[...] = x1 + g2_ref[:, 0] * acc_ref[...].reshape(nb, tt, d)


def _outproj_ffn(x, co, at, mod4, n2g, wo, wg3, wu3, wd3, bb, tt):
    nb, s, d = x.shape
    d_conv = co.shape[2]
    d_attn = at.shape[2]
    mod_spec = lambda j: pl.BlockSpec((bb, 1, 1, d), lambda i, t: (i, j, 0, 0))
    row_spec = lambda w: pl.BlockSpec((bb, tt, w), lambda i, t: (i, t, 0))
    return pl.pallas_call(
        _outproj_ffn_kernel,
        grid=(nb // bb, s // tt),
        in_specs=[row_spec(d), row_spec(d_conv), row_spec(d_attn),
                  mod_spec(2), mod_spec(3), mod_spec(4), mod_spec(5),
                  _const_spec(n2g.shape), _const_spec(wo.shape), _const_spec(wg3.shape),
                  _const_spec(wu3.shape), _const_spec(wd3.shape)],
        out_specs=row_spec(d),
        out_shape=jax.ShapeDtypeStruct((nb, s, d), F32),
        scratch_shapes=[pltpu.VMEM((bb * tt, d), F32)],
        compiler_params=pltpu.CompilerParams(dimension_semantics=("arbitrary", "arbitrary"),
                                             vmem_limit_bytes=VMEM_LIMIT_BYTES),
        name="outproj_ffn",
    )(x, co, at, mod4, mod4, mod4, mod4, n2g, wo, wg3, wu3, wd3)


def _tri(n, rel):
    i = lax.broadcasted_iota(jnp.int32, (n, n), 0)
    j = lax.broadcasted_iota(jnp.int32, (n, n), 1)
    return rel(i, j).astype(BF16)


def _layer(xp, xs, cache_k, cache_v, cache_logf, state_conv, c_all, w_ada, b_ada, norm1_g, w_in, b_f,
           q_norm_g, k_norm_g, conv_w, conv_b, conv_ln_g, conv_ln_b, w_out, norm2_g, w_gate, w_up, w_down):
    b, s, d = xp.shape
    nb, tt, _ = xs.shape
    d_conv = conv_w.shape[1]
    d_attn = N_HEADS * HEAD_DIM
    d_ff = w_gate.shape[1]
    p_len = cache_k.shape[1]

    mod4 = _modulation(c_all, w_ada, b_ada).reshape(b + nb, 6, 1, d)
    mod_p, mod_s = mod4[:b], mod4[b:]

    wglu = w_in[:, :2 * d_conv].astype(BF16)
    wqkvf_t = jnp.pad(w_in[:, 2 * d_conv:].T, ((0, LANES - N_HEADS), (0, 0))).astype(BF16)
    bf_col = b_f.reshape(N_HEADS, 1)
    qg_col = q_norm_g.reshape(d_attn, 1)
    kg_col = k_norm_g.reshape(d_attn, 1)
    row = lambda v: v.reshape(1, -1)
    n1g, n2g, cb, ln_g, ln_b = row(norm1_g), row(norm2_g), row(conv_b), row(conv_ln_g), row(conv_ln_b)
    wo = w_out.astype(BF16)
    n_ff = d_ff // FF_CHUNK
    wg3 = w_gate.astype(BF16).reshape(d, n_ff, FF_CHUNK).transpose(1, 0, 2)
    wu3 = w_up.astype(BF16).reshape(d, n_ff, FF_CHUNK).transpose(1, 0, 2)
    wd3 = w_down.astype(BF16).reshape(n_ff, FF_CHUNK, d)
    utri = _tri(MXU_DIM, lambda i, j: i <= j)
    ltri = _tri(MXU_DIM, lambda i, j: i > j)
    bdtri = _tri(nb * tt, lambda i, j: (i // tt == j // tt) & (j <= i))

    kt, vt, lft, cst, qt, qft, k, kb, co = _inproj_prompt(
        xp, mod_p, n1g, wglu, wqkvf_t, bf_col, qg_col, kg_col, utri, conv_w, cb, ln_g, ln_b)
    at = _attn_prompt(qt, qft, k, kb, vt)
    yp = _outproj_ffn(xp, co, at, mod_p, n2g, wo, wg3, wu3, wd3, 1, ROW_TILE)
    k_p = kt.reshape(b, N_HEADS, HEAD_DIM, s).transpose(0, 3, 1, 2)
    v_p = vt.reshape(b, N_HEADS, HEAD_DIM, s).transpose(0, 3, 1, 2)
    lf_p = lft.transpose(0, 2, 1)

    state_t = state_conv.transpose(1, 0, 2)
    k_s, v_s, lf_s, nst, q_s, qb_s, kb_s, co_s = _inproj_sample(
        xs, mod_s, n1g, wglu, wqkvf_t, bf_col, qg_col, kg_col, bdtri, conv_w, cb, ln_g, ln_b, state_t)
    r3 = lambda a: a.reshape(nb, tt, a.shape[-1])
    ckt = cache_k.transpose(0, 2, 3, 1).reshape(nb, d_attn, p_len)
    cvt = cache_v.transpose(0, 2, 3, 1).reshape(nb, d_attn, p_len)
    clft = cache_logf.transpose(0, 2, 1)
    at_s = _attn_sample(r3(q_s), r3(qb_s), r3(k_s), r3(v_s), r3(kb_s), ckt, cvt, clft, ltri)
    ys = _outproj_ffn(xs, r3(co_s), at_s, mod_s, n2g, wo, wg3, wu3, wd3, nb, tt)
    k_sn = k_s.reshape(nb, tt, N_HEADS, HEAD_DIM)
    v_sn = v_s.reshape(nb, tt, N_HEADS, HEAD_DIM)
    lf_sn = lf_s[:, :N_HEADS].reshape(nb, tt, N_HEADS)
    return yp, ys, (k_p, v_p, lf_p, cst), (k_sn, v_sn, lf_sn, nst.transpose(1, 0, 2))


def kernel(x_prompt, x_sample, cache_k, cache_v, cache_logf, state_conv, c_prompt, c_sample, w_ada, b_ada,
           norm1_g, w_in, b_f, q_norm_g, k_norm_g, conv_w, conv_b, conv_ln_g, conv_ln_b, w_out, norm2_g,
           w_gate, w_up, w_down):
    depth = w_ada.shape[0]
    c_all = jnp.concatenate([c_prompt, c_sample], axis=0)
    yp, ys = x_prompt, x_sample
    st_p, st_s = [], []
    for l in range(depth):
        yp, ys, sp, ss = _layer(
            yp, ys, cache_k[l], cache_v[l], cache_logf[l], state_conv[l], c_all, w_ada[l], b_ada[l],
            norm1_g[l], w_in[l], b_f[l], q_norm_g[l], k_norm_g[l], conv_w[l], conv_b[l], conv_ln_g[l],
            conv_ln_b[l], w_out[l], norm2_g[l], w_gate[l], w_up[l], w_down[l])
        st_p.append(sp)
        st_s.append(ss)
    stack = lambda xs: xs[0][None] if len(xs) == 1 else jnp.stack(xs)
    outs_p = [stack([s[i] for s in st_p]) for i in range(4)]
    outs_s = [stack([s[i] for s in st_s]) for i in range(4)]
    return (yp, ys, *outs_p, *outs_s)
```

```python
import functools

import jax
import jax.numpy as jnp
from jax import lax
from jax.experimental import pallas as pl
from jax.experimental.pallas import tpu as pltpu

F32 = jnp.float32
BF16 = jnp.bfloat16

N_HEADS = 8
HEAD_DIM = 64
CONV_WIDTH = 31
EPS = 1e-6
NEG_INF = -1e30

LANES = 128
SUBLANES = 8
MXU_DIM = 256
VMEM_LIMIT_BYTES = 56 * 1024 * 1024

N_PARTS = 3
N_FEAT = N_PARTS * N_HEADS
HIST = 32
CONV_ROWS = 32

ROW_TILE = 512
Q_TILE = 256
ATTN_LAG = 2
DENOM_ROWS = 16
LOG2E = 1.4426950408889634
FF_CHUNK = 256


def _dot(a, b):
    return jnp.dot(a, b, preferred_element_type=F32)


def _dot_nt(a, b):
    return lax.dot_general(a, b, (((1,), (1,)), ((), ())), preferred_element_type=F32)


def _split3(x):
    hi = x.astype(BF16).astype(F32)
    r = x - hi
    mid = r.astype(BF16).astype(F32)
    lo = (r - mid).astype(BF16).astype(F32)
    return hi, mid, lo


def _log_sigmoid(x):
    return jnp.minimum(x, 0.0) - jnp.log1p(jnp.exp(-jnp.abs(x)))


def _adaln_rmsnorm(x, g, scale, shift):
    y = x * lax.rsqrt(jnp.mean(x * x, axis=-1, keepdims=True) + EPS)
    return (y * g) * (1.0 + scale) + shift


def _const_spec(shape):
    n = len(shape)
    return pl.BlockSpec(shape, lambda *_: (0,) * n, pipeline_mode=pl.Buffered(1))


def _modulation_kernel(c_ref, w_ref, b_ref, o_ref):
    c = c_ref[...]
    a = (c * jax.nn.sigmoid(c)).astype(BF16)
    o_ref[...] = _dot(a, w_ref[...].astype(BF16)) + b_ref[...]


def _modulation(c, w_ada, b_ada):
    nb, d = c.shape
    n = w_ada.shape[1]
    tn = 768
    return pl.pallas_call(
        _modulation_kernel,
        grid=(n // tn,),
        in_specs=[pl.BlockSpec((nb, d), lambda j: (0, 0)),
                  pl.BlockSpec((d, tn), lambda j: (0, j)),
                  pl.BlockSpec((1, tn), lambda j: (0, j))],
        out_specs=pl.BlockSpec((nb, tn), lambda j: (0, j)),
        out_shape=jax.ShapeDtypeStruct((nb, n), F32),
        compiler_params=pltpu.CompilerParams(dimension_semantics=("arbitrary",),
                                             vmem_limit_bytes=VMEM_LIMIT_BYTES),
        name="modulation",
    )(c, w_ada, b_ada.reshape(1, n))


def _qkvf_feature_major(hb, wqkvf_t, qg_col, kg_col, bf_col):
    d_attn = N_HEADS * HEAD_DIM
    zt = _dot_nt(wqkvf_t, hb)
    r = zt.shape[1]

    def head_rms(z, g_col):
        z3 = z.reshape(N_HEADS, HEAD_DIM, r)
        ms = jnp.mean(z3 * z3, axis=1, keepdims=True)
        return (z3 * lax.rsqrt(ms + EPS)).reshape(d_attn, r) * g_col

    q_t = head_rms(zt[0:d_attn], qg_col)
    k_t = head_rms(zt[d_attn:2 * d_attn], kg_col)
    v_t = zt[2 * d_attn:3 * d_attn]
    lf_t = _log_sigmoid(zt[3 * d_attn:3 * d_attn + N_HEADS] + bf_col)
    return q_t, k_t, v_t, lf_t


def _conv_ln_swish(load_rows, cw_ref, cb, ln_g, ln_b):
    acc = cb + cw_ref[0:1, :] * load_rows(0)
    for k in range(1, CONV_WIDTH):
        acc = acc + cw_ref[k:k + 1, :] * load_rows(k)
    mu = jnp.mean(acc, axis=-1, keepdims=True)
    cen = acc - mu
    var = jnp.mean(cen * cen, axis=-1, keepdims=True)
    y = cen * lax.rsqrt(var + EPS) * ln_g + ln_b
    return y * jax.nn.sigmoid(y)


def _inproj_prompt_kernel(x_ref, sh_ref, sc_ref, n1g_ref, wglu_ref, wqkvf_ref, bf_ref, qg_ref, kg_ref,
                          utri_ref,
                          kt_ref, vt_ref, lft_ref, cst_ref, qt_ref, qft_ref, k_ref, kb_ref, u_ref,
                          carry_ref):
    t = pl.program_id(1)
    tm = x_ref.shape[1]
    d_conv = u_ref.shape[2]

    @pl.when(t == 0)
    def _():
        carry_ref[...] = jnp.zeros(carry_ref.shape, F32)

    hb = _adaln_rmsnorm(x_ref[0], n1g_ref[...], sc_ref[0, 0], sh_ref[0, 0]).astype(BF16)

    zg = _dot(hb, wglu_ref[...])
    u_ref[0] = zg[:, :d_conv] * jax.nn.sigmoid(zg[:, d_conv:])

    @pl.when(t == pl.num_programs(1) - 1)
    def _():
        cst_ref[0] = u_ref[0, tm - (CONV_WIDTH - 1):tm, :]

    q_t, k_t, v_t, lf_t = _qkvf_feature_major(hb, wqkvf_ref[...], qg_ref[...], kg_ref[...], bf_ref[...])
    kt_ref[0] = k_t
    vt_ref[0] = v_t
    lft_ref[0] = lf_t

    nblk = tm // MXU_DIM
    parts = jnp.concatenate(_split3(lf_t), axis=0)
    stacked = jnp.concatenate([parts[:, i * MXU_DIM:(i + 1) * MXU_DIM] for i in range(nblk)], axis=0)
    local = _dot(stacked.astype(BF16), utri_ref[...])
    carry = carry_ref[:, 0:1]
    cums = []
    for i in range(nblk):
        loc = local[i * N_FEAT:(i + 1) * N_FEAT]
        cums.append(loc + carry)
        carry = carry + loc[:, MXU_DIM - 1:MXU_DIM]
    carry_ref[...] = jnp.broadcast_to(carry, carry_ref.shape)
    cum_parts = jnp.concatenate(cums, axis=1)
    cum_t = (cum_parts[0:8] + cum_parts[8:16] + cum_parts[16:24]) * LOG2E

    c_hi, c_mid, c_lo = _split3(cum_t)
    ones = jnp.ones((N_FEAT, tm), F32)
    zeros = jnp.zeros((LANES - 2 * N_FEAT, tm), F32)
    kfeat_t = jnp.concatenate([ones, -c_hi, -c_mid, -c_lo, zeros], axis=0)
    qft_ref[0] = jnp.concatenate([c_hi, c_mid, c_lo, ones, zeros], axis=0).astype(BF16)
    qt_ref[0] = (q_t * (HEAD_DIM ** -0.5 * LOG2E)).astype(BF16)
    k_ref[0] = k_t.T.astype(BF16)
    kb_ref[0] = kfeat_t.T.astype(BF16)


def _inproj_prompt(x, mod4, n1g, wglu, wqkvf_t, bf_col, qg_col, kg_col, utri):
    b, s, d = x.shape
    tm = ROW_TILE
    d_conv = wglu.shape[1] // 2
    d_attn = N_HEADS * HEAD_DIM
    mod_spec = lambda j: pl.BlockSpec((1, 1, 1, d), lambda i, t: (i, j, 0, 0))
    out_shape = (
        jax.ShapeDtypeStruct((b, d_attn, s), F32),
        jax.ShapeDtypeStruct((b, d_attn, s), F32),
        jax.ShapeDtypeStruct((b, N_HEADS, s), F32),
        jax.ShapeDtypeStruct((b, CONV_WIDTH - 1, d_conv), F32),
        jax.ShapeDtypeStruct((b, d_attn, s), BF16),
        jax.ShapeDtypeStruct((b, LANES, s), BF16),
        jax.ShapeDtypeStruct((b, s, d_attn), BF16),
        jax.ShapeDtypeStruct((b, s, LANES), BF16),
        jax.ShapeDtypeStruct((b, s, d_conv), F32),
    )
    out_specs = (
        pl.BlockSpec((1, d_attn, tm), lambda i, t: (i, 0, t)),
        pl.BlockSpec((1, d_attn, tm), lambda i, t: (i, 0, t)),
        pl.BlockSpec((1, N_HEADS, tm), lambda i, t: (i, 0, t)),
        pl.BlockSpec((1, CONV_WIDTH - 1, d_conv), lambda i, t: (i, 0, 0)),
        pl.BlockSpec((1, d_attn, tm), lambda i, t: (i, 0, t)),
        pl.BlockSpec((1, LANES, tm), lambda i, t: (i, 0, t)),
        pl.BlockSpec((1, tm, d_attn), lambda i, t: (i, t, 0)),
        pl.BlockSpec((1, tm, LANES), lambda i, t: (i, t, 0)),
        pl.BlockSpec((1, tm, d_conv), lambda i, t: (i, t, 0)),
    )
    in_specs = [
        pl.BlockSpec((1, tm, d), lambda i, t: (i, t, 0)),
        mod_spec(0), mod_spec(1),
        _const_spec(n1g.shape), _const_spec(wglu.shape), _const_spec(wqkvf_t.shape),
        _const_spec(bf_col.shape), _const_spec(qg_col.shape), _const_spec(kg_col.shape),
        _const_spec(utri.shape),
    ]
    return pl.pallas_call(
        _inproj_prompt_kernel,
        grid=(b, s // tm),
        in_specs=in_specs,
        out_specs=out_specs,
        out_shape=out_shape,
        scratch_shapes=[pltpu.VMEM((N_FEAT, LANES), F32)],
        compiler_params=pltpu.CompilerParams(dimension_semantics=("arbitrary", "arbitrary"),
                                             vmem_limit_bytes=VMEM_LIMIT_BYTES),
        name="inproj_prompt",
    )(x, mod4, mod4, n1g, wglu, wqkvf_t, bf_col, qg_col, kg_col, utri)


def _attn_prompt_kernel(qt_ref, qft_ref, k_ref, kb_ref, vt_ref, o_ref,
                        kp_ref, vp_ref, q2_ref, m_ref, acc_ref):
    qi = pl.program_id(1)
    tq = qt_ref.shape[2]
    n_pairs, nblk = vp_ref.shape[0], vp_ref.shape[1]

    @pl.when(qi == 0)
    def _():
        ones = jnp.ones((vp_ref.shape[2] - LANES, tq), BF16)
        for p in range(n_pairs):
            kp_ref[p, :, 0:LANES] = k_ref[0, :, p * LANES:(p + 1) * LANES]
            kp_ref[p, :, LANES:2 * LANES] = kb_ref[0]
            for i in range(nblk):
                vp_ref[p, i, 0:LANES, :] = vt_ref[0, p * LANES:(p + 1) * LANES, i * tq:(i + 1) * tq].astype(BF16)
                vp_ref[p, i, LANES:, :] = ones

    frow = lax.broadcasted_iota(jnp.int32, (LANES, tq), 0)
    feat = qft_ref[0].astype(F32)
    for p in range(n_pairs):
        qp = qt_ref[0, p * LANES:(p + 1) * LANES, :].astype(F32)
        cols = []
        for j in range(2):
            qm = jnp.where((frow >= j * HEAD_DIM) & (frow < (j + 1) * HEAD_DIM), qp, 0.0)
            fm = jnp.where((frow % N_HEADS) == 2 * p + j, feat, 0.0)
            cols.append(jnp.concatenate([qm, fm], axis=0))
        q2_ref[p] = jnp.concatenate(cols, axis=1).astype(BF16)

    m_ref[...] = jnp.full(m_ref.shape, NEG_INF, F32)
    acc_ref[...] = jnp.zeros(acc_ref.shape, F32)

    def scores(p, j):
        start = pl.multiple_of(j * tq, tq)
        return _dot(kp_ref[p, pl.ds(start, tq), :], q2_ref[p])

    def accumulate(p, j, masked, s):
        if masked:
            key = lax.broadcasted_iota(jnp.int32, s.shape, 0)
            qry = lax.broadcasted_iota(jnp.int32, s.shape, 1)
            qry = jnp.where(qry >= tq, qry - tq, qry)
            s = jnp.where(key <= qry, s, NEG_INF)
        m_old = m_ref[p]
        m_new = jnp.maximum(m_old, jnp.max(s, axis=0, keepdims=True))
        pr = jnp.exp2(s - m_new).astype(BF16)
        acc_ref[p] = jnp.exp2(m_old - m_new) * acc_ref[p] + _dot(vp_ref[p, j], pr)
        m_ref[p] = m_new

    def run_units(units):
        pending = {}
        for i in range(len(units) + ATTN_LAG):
            if i < len(units):
                p, j, _ = units[i]
                pending[i] = scores(p, j)
            if i >= ATTN_LAG:
                p, j, masked = units[i - ATTN_LAG]
                accumulate(p, j, masked, pending.pop(i - ATTN_LAG))

    def block(j, masked):
        return [(p, j, masked) for p in range(n_pairs)]

    def body(i, carry):
        run_units(block(2 * i, False) + block(2 * i + 1, False))
        return carry

    lax.fori_loop(0, qi // 2, body, 0)

    @pl.when(qi % 2 == 1)
    def _():
        run_units(block(qi - 1, False) + block(qi, True))

    @pl.when(qi % 2 == 0)
    def _():
        run_units(block(qi, True))

    for p in range(n_pairs):
        acc = acc_ref[p]
        out_t = acc[0:LANES] / acc[LANES:LANES + 1]
        pair_t = jnp.where(frow < HEAD_DIM, out_t[:, :tq], out_t[:, tq:])
        o_ref[0, :, p * LANES:(p + 1) * LANES] = pair_t.T.astype(BF16)


def _attn_prompt(qt, qft, k, kb, vt):
    b, d_attn, s = qt.shape
    tq = Q_TILE
    nblk = s // tq
    n_pairs = d_attn // LANES
    return pl.pallas_call(
        _attn_prompt_kernel,
        grid=(b, nblk),
        in_specs=[
            pl.BlockSpec((1, d_attn, tq), lambda i, j: (i, 0, j)),
            pl.BlockSpec((1, LANES, tq), lambda i, j: (i, 0, j)),
            pl.BlockSpec((1, s, d_attn), lambda i, j: (i, 0, 0)),
            pl.BlockSpec((1, s, LANES), lambda i, j: (i, 0, 0)),
            pl.BlockSpec((1, d_attn, s), lambda i, j: (i, 0, 0)),
        ],
        out_specs=pl.BlockSpec((1, tq, d_attn), lambda i, j: (i, j, 0)),
        out_shape=jax.ShapeDtypeStruct((b, s, d_attn), BF16),
        scratch_shapes=[
            pltpu.VMEM((n_pairs, s, 2 * LANES), BF16),
            pltpu.VMEM((n_pairs, nblk, LANES + DENOM_ROWS, tq), BF16),
            pltpu.VMEM((n_pairs, 2 * LANES, 2 * tq), BF16),
            pltpu.VMEM((n_pairs, 1, 2 * tq), F32),
            pltpu.VMEM((n_pairs, LANES + DENOM_ROWS, 2 * tq), F32),
        ],
        compiler_params=pltpu.CompilerParams(dimension_semantics=("arbitrary", "arbitrary"),
                                             vmem_limit_bytes=VMEM_LIMIT_BYTES),
        name="attn_prompt",
    )(qt, qft, k, kb, vt)


def _inproj_sample_kernel(x_ref, sh_ref, sc_ref, n1g_ref, wglu_ref, wqkvf_ref, bf_ref, qg_ref, kg_ref,
                          bdtri_ref, cw_ref, cb_ref, lng_ref, lnb_ref, st_ref,
                          k_ref, v_ref, lf_ref, nst_ref, q_ref, qb_ref, kb_ref, co_ref,
                          u_ref, hist_ref, cof_ref):
    nb, tt, d = x_ref.shape
    r = nb * tt
    d_conv = co_ref.shape[1]
    n_state = CONV_WIDTH - 1

    h = _adaln_rmsnorm(x_ref[...], n1g_ref[...], sc_ref[:, 0], sh_ref[:, 0])
    hb = h.reshape(r, d).astype(BF16)

    n_chunks = d_conv // LANES
    zg = _dot(hb, wglu_ref[...])
    u = zg[:, :d_conv] * jax.nn.sigmoid(zg[:, d_conv:])
    for c in range(n_chunks):
        u_ref[c] = u[:, c * LANES:(c + 1) * LANES]

    q_t, k_t, v_t, lf_t = _qkvf_feature_major(hb, wqkvf_ref[...], qg_ref[...], kg_ref[...], bf_ref[...])
    k_ref[...] = k_t.T
    v_ref[...] = v_t.T
    q_ref[...] = (q_t * (HEAD_DIM ** -0.5)).T.astype(BF16)
    lf = jnp.concatenate([lf_t, jnp.zeros((LANES - N_HEADS, r), F32)], axis=0).T
    lf_ref[...] = lf

    lane = lax.broadcasted_iota(jnp.int32, (r, LANES), 1)
    hi, mid, lo = _split3(lf)
    packed = hi + pltpu.roll(mid, N_HEADS, 1) + pltpu.roll(lo, 2 * N_HEADS, 1)
    c = _dot(bdtri_ref[...], packed.astype(BF16))
    cn = jnp.where(lane < N_HEADS,
                   c + pltpu.roll(c, LANES - N_HEADS, 1) + pltpu.roll(c, LANES - 2 * N_HEADS, 1), 0.0)
    hi, mid, lo = _split3(cn)
    p = hi + pltpu.roll(mid, N_HEADS, 1) + pltpu.roll(lo, 2 * N_HEADS, 1)
    qb_ref[...] = (p + jnp.where((lane >= N_FEAT) & (lane < 2 * N_FEAT), 1.0, 0.0)).astype(BF16)
    kb_ref[...] = (jnp.where(lane < N_FEAT, 1.0, 0.0) - pltpu.roll(p, N_FEAT, 1)).astype(BF16)

    hist_ref[0:n_state] = st_ref[...]
    for t in range(tt):
        for c in range(n_chunks):
            hist_ref[n_state + t, :, c * LANES:(c + 1) * LANES] = u_ref[c, pl.ds(t, nb, stride=tt), :]
    nst_ref[...] = hist_ref[tt:tt + n_state]
    cb, ln_g, ln_b = cb_ref[...], lng_ref[...], lnb_ref[...]
    for t in range(tt):
        y = _conv_ln_swish(lambda k: hist_ref[t + k], cw_ref, cb, ln_g, ln_b)
        for c in range(n_chunks):
            cof_ref[c, pl.ds(t, nb, stride=tt), :] = y[:, c * LANES:(c + 1) * LANES]
    co_ref[...] = jnp.concatenate([cof_ref[c] for c in range(n_chunks)], axis=1).astype(BF16)


def _inproj_sample(x, mod4, n1g, wglu, wqkvf_t, bf_col, qg_col, kg_col, bdtri, conv_w, conv_b, ln_g, ln_b,
                   state_t):
    nb, tt, d = x.shape
    r = nb * tt
    d_conv = conv_w.shape[1]
    d_attn = N_HEADS * HEAD_DIM
    n_state = CONV_WIDTH - 1
    mod_spec = lambda j: pl.BlockSpec((nb, 1, 1, d), lambda i: (0, j, 0, 0))
    full = lambda shape: pl.BlockSpec(shape, lambda i: (0,) * len(shape))
    out_shape = (
        jax.ShapeDtypeStruct((r, d_attn), F32),
        jax.ShapeDtypeStruct((r, d_attn), F32),
        jax.ShapeDtypeStruct((r, LANES), F32),
        jax.ShapeDtypeStruct((n_state, nb, d_conv), F32),
        jax.ShapeDtypeStruct((r, d_attn), BF16),
        jax.ShapeDtypeStruct((r, LANES), BF16),
        jax.ShapeDtypeStruct((r, LANES), BF16),
        jax.ShapeDtypeStruct((r, d_conv), BF16),
    )
    args = (x, mod4, mod4, n1g, wglu, wqkvf_t, bf_col, qg_col, kg_col, bdtri, conv_w, conv_b, ln_g, ln_b,
            state_t)
    in_specs = [full(x.shape), mod_spec(0), mod_spec(1)] + [full(a.shape) for a in args[3:]]
    return pl.pallas_call(
        _inproj_sample_kernel,
        grid=(1,),
        in_specs=in_specs,
        out_specs=tuple(full(o.shape) for o in out_shape),
        out_shape=out_shape,
        scratch_shapes=[pltpu.VMEM((d_conv // LANES, r, LANES), F32),
                        pltpu.VMEM((n_state + tt, nb, d_conv), F32),
                        pltpu.VMEM((d_conv // LANES, r, LANES), F32)],
        compiler_params=pltpu.CompilerParams(dimension_semantics=("arbitrary",),
                                             vmem_limit_bytes=VMEM_LIMIT_BYTES),
        name="inproj_sample",
    )(*args)


def _attn_sample_kernel(q_ref, qb_ref, kn_ref, vn_ref, kbn_ref, kt_ref, vt_ref, clft_ref, ltri_ref, o_ref):
    tt = q_ref.shape[1]
    d_attn = q_ref.shape[2]
    p_len = kt_ref.shape[2]
    nblk = p_len // MXU_DIM
    rows = N_HEADS * tt

    parts = jnp.concatenate(_split3(clft_ref[0]), axis=0)
    blocks = [parts[:, i * MXU_DIM:(i + 1) * MXU_DIM] for i in range(nblk)]
    local = _dot(jnp.concatenate(blocks, axis=0).astype(BF16), ltri_ref[...])
    off = jnp.zeros((N_FEAT, 1), F32)
    sufs = [None] * nblk
    for i in reversed(range(nblk)):
        loc = local[i * N_FEAT:(i + 1) * N_FEAT]
        sufs[i] = loc + off
        off = off + loc[:, 0:1] + blocks[i][:, 0:1]
    suf_parts = jnp.concatenate(sufs, axis=1)
    ck_rel = suf_parts[0:8] + suf_parts[8:16] + suf_parts[16:24]
    c_hi, c_mid, c_lo = _split3(ck_rel)
    kbt = jnp.concatenate([jnp.ones((N_FEAT, p_len), F32), c_hi, c_mid, c_lo,
                           jnp.zeros((LANES - 2 * N_FEAT, p_len), F32)], axis=0)
    k_all = jnp.concatenate([kt_ref[0].astype(BF16), kbt.astype(BF16)], axis=0)

    lane_q = lax.broadcasted_iota(jnp.int32, (tt, d_attn), 1)
    lane_b = lax.broadcasted_iota(jnp.int32, (tt, LANES), 1)
    q = q_ref[0].astype(F32)
    qb = qb_ref[0].astype(F32)
    stack = []
    for h in range(N_HEADS):
        qm = jnp.where((lane_q >= h * HEAD_DIM) & (lane_q < (h + 1) * HEAD_DIM), q, 0.0)
        bm = jnp.where((lane_b % N_HEADS) == h, qb, 0.0)
        stack.append(jnp.concatenate([qm, bm], axis=1))
    qs = jnp.concatenate(stack, axis=0).astype(BF16)

    s_c = _dot(qs, k_all)

    pad = jnp.zeros((LANES - tt, d_attn + LANES), F32)
    kn = jnp.concatenate([jnp.concatenate([kn_ref[0], kbn_ref[0].astype(F32)], axis=1), pad], axis=0)
    s_n = _dot_nt(qs, kn.astype(BF16))
    row = lax.broadcasted_iota(jnp.int32, s_n.shape, 0)
    col = lax.broadcasted_iota(jnp.int32, s_n.shape, 1)
    s_n = jnp.where(col <= (row % tt), s_n, NEG_INF)

    m = jnp.maximum(jnp.max(s_c, axis=-1, keepdims=True), jnp.max(s_n, axis=-1, keepdims=True))
    p_c = jnp.exp(s_c - m)
    p_n = jnp.exp(s_n - m)
    l = jnp.sum(p_c, axis=-1, keepdims=True) + jnp.sum(p_n, axis=-1, keepdims=True)
    vn = jnp.concatenate([vn_ref[0], jnp.zeros((LANES - tt, d_attn), F32)], axis=0).astype(BF16)
    o = _dot_nt(p_c.astype(BF16), vt_ref[0].astype(BF16)) + _dot(p_n.astype(BF16), vn)
    o = o / l

    out = jnp.zeros((tt, d_attn), F32)
    for h in range(N_HEADS):
        out = jnp.where((lane_q >= h * HEAD_DIM) & (lane_q < (h + 1) * HEAD_DIM), o[h * tt:(h + 1) * tt], out)
    o_ref[0] = out.astype(BF16)


def _attn_sample(q, qb, kn, vn, kbn, kt, vt, clft, ltri):
    nb, tt, d_attn = q.shape
    p_len = kt.shape[2]
    per_req = lambda shape: pl.BlockSpec((1,) + shape, lambda i: (i, 0, 0))
    return pl.pallas_call(
        _attn_sample_kernel,
        grid=(nb,),
        in_specs=[per_req((tt, d_attn)), per_req((tt, LANES)), per_req((tt, d_attn)), per_req((tt, d_attn)),
                  per_req((tt, LANES)), per_req((d_attn, p_len)), per_req((d_attn, p_len)),
                  per_req((N_HEADS, p_len)), _const_spec(ltri.shape)],
        out_specs=per_req((tt, d_attn)),
        out_shape=jax.ShapeDtypeStruct((nb, tt, d_attn), BF16),
        compiler_params=pltpu.CompilerParams(dimension_semantics=("arbitrary",),
                                             vmem_limit_bytes=VMEM_LIMIT_BYTES),
        name="attn_sample",
    )(q, qb, kn, vn, kbn, kt, vt, clft, ltri)


def _outproj_ffn_body(x_ref, co, at, g1_ref, sh2_ref, sc2_ref, g2_ref, n2g_ref,
                      wo_ref, wg_ref, wu_ref, wd_ref, y_ref, acc_ref, vpu_slot=None):
    nb, tt, d = x_ref.shape
    r = nb * tt
    d_conv = co.shape[1]
    mix = _dot(co, wo_ref[0:d_conv, :]) + _dot(at, wo_ref[d_conv:, :])
    x1 = x_ref[...] + g1_ref[:, 0] * mix.reshape(nb, tt, d)
    hb = _adaln_rmsnorm(x1, n2g_ref[...], sc2_ref[:, 0], sh2_ref[:, 0]).reshape(r, d).astype(BF16)
    lead = x1.reshape(r, d)[0:SUBLANES, 0:d_conv]
    for c in range(wg_ref.shape[0]):
        zero = vpu_slot(c, lead) if vpu_slot is not None else None
        g = _dot(hb, wg_ref[c])
        a = ((g * jax.nn.sigmoid(g)) * _dot(hb, wu_ref[c])).astype(BF16)
        part = _dot(a, wd_ref[c])
        if c == 0:
            acc_ref[...] = part
        else:
            acc_ref[...] += part
        if zero is not None:
            acc_ref[0:SUBLANES, 0:LANES] += zero
        lead = part[0:SUBLANES, 0:d_conv]
    y_ref[...] = x1 + g2_ref[:, 0] * acc_ref[...].reshape(nb, tt, d)


def _outproj_ffn_kernel(x_ref, co_ref, at_ref, g1_ref, sh2_ref, sc2_ref, g2_ref, n2g_ref,
                        wo_ref, wg_ref, wu_ref, wd_ref, y_ref, acc_ref):
    nb, tt, _ = x_ref.shape
    co = co_ref[...].reshape(nb * tt, co_ref.shape[2])
    at = at_ref[...].reshape(nb * tt, at_ref.shape[2])
    _outproj_ffn_body(x_ref, co, at, g1_ref, sh2_ref, sc2_ref, g2_ref, n2g_ref,
                      wo_ref, wg_ref, wu_ref, wd_ref, y_ref, acc_ref)


def _exact_zero(v):
    return jnp.minimum(jnp.abs(v), 0.0)


def _outproj_ffn_conv_kernel(x_ref, at_ref, un_ref, u0_ref, g1_ref, sh2_ref, sc2_ref, g2_ref, n2g_ref,
                             wo_ref, wg_ref, wu_ref, wd_ref, cw_ref, cb_ref, lng_ref, lnb_ref,
                             y_ref, acc_ref, shift_ref, co_ref, *, tiles_per_seq):
    g = pl.program_id(0)
    tm = x_ref.shape[1]
    d_conv = co_ref.shape[1]
    span = tm + HIST - SUBLANES
    n_conv = tm // CONV_ROWS
    n_ff = wg_ref.shape[0]
    cb, ln_g, ln_b = cb_ref[...], lng_ref[...], lnb_ref[...]

    def stage(u, hist):
        shift_ref[0, 0:HIST, :] = hist
        shift_ref[0, HIST:HIST + tm, :] = u
        for r in range(1, SUBLANES):
            shift_ref[r, 0:span, :] = shift_ref[0, r:r + span, :]

    def conv_chunk(c, bias):
        def tap(k):
            off = k + HIST - (CONV_WIDTH - 1)
            row = c * CONV_ROWS + off - off % SUBLANES
            return shift_ref[off % SUBLANES, row:row + CONV_ROWS, :]
        y = _conv_ln_swish(tap, cw_ref, bias, ln_g, ln_b)
        co_ref[c * CONV_ROWS:(c + 1) * CONV_ROWS, :] = y.astype(BF16)
        return y

    @pl.when(g == 0)
    def _():
        stage(u0_ref[0], jnp.zeros((HIST, d_conv), F32))
        for c in range(n_conv):
            conv_chunk(c, cb)

    def conv_slot(c, lead):
        if c == 0:
            tail = shift_ref[0, tm:tm + HIST, :]
            starts_seq = (g + 1) % tiles_per_seq == 0
            stage(un_ref[0], jnp.where(starts_seq, 0.0, tail))
        bias = jnp.concatenate([cb + _exact_zero(lead)] * (CONV_ROWS // SUBLANES), axis=0)
        zero = jnp.zeros((SUBLANES, LANES), F32)
        for i in range(c * n_conv // n_ff, (c + 1) * n_conv // n_ff):
            zero = zero + _exact_zero(conv_chunk(i, bias)[0:SUBLANES, 0:LANES])
        return zero

    co = co_ref[...]
    _outproj_ffn_body(x_ref, co, at_ref[0], g1_ref, sh2_ref, sc2_ref, g2_ref, n2g_ref,
                      wo_ref, wg_ref, wu_ref, wd_ref, y_ref, acc_ref, vpu_slot=conv_slot)


def _outproj_ffn_conv(x, at, u, mod4, n2g, wo, wg3, wu3, wd3, conv_w, conv_b, ln_g, ln_b):
    b, s, d = x.shape
    tm = ROW_TILE
    d_conv = u.shape[2]
    d_attn = at.shape[2]
    tps = s // tm
    n_tiles = b * tps
    tile = lambda g: (g // tps, g % tps, 0)
    next_tile = lambda g: tile(jnp.minimum(g + 1, n_tiles - 1))
    mod_spec = lambda j: pl.BlockSpec((1, 1, 1, d), lambda g: (g // tps, j, 0, 0))
    consts = (n2g, wo, wg3, wu3, wd3, conv_w, conv_b, ln_g, ln_b)
    return pl.pallas_call(
        functools.partial(_outproj_ffn_conv_kernel, tiles_per_seq=tps),
        grid=(n_tiles,),
        in_specs=[pl.BlockSpec((1, tm, d), tile), pl.BlockSpec((1, tm, d_attn), tile),
                  pl.BlockSpec((1, tm, d_conv), next_tile),
                  pl.BlockSpec((1, tm, d_conv), lambda g: (0, 0, 0), pipeline_mode=pl.Buffered(1)),
                  mod_spec(2), mod_spec(3), mod_spec(4), mod_spec(5)] + [_const_spec(c.shape) for c in consts],
        out_specs=pl.BlockSpec((1, tm, d), tile),
        out_shape=jax.ShapeDtypeStruct((b, s, d), F32),
        scratch_shapes=[pltpu.VMEM((tm, d), F32),
                        pltpu.VMEM((SUBLANES, tm + HIST, d_conv), F32),
                        pltpu.VMEM((tm, d_conv), BF16)],
        compiler_params=pltpu.CompilerParams(dimension_semantics=("arbitrary",),
                                             vmem_limit_bytes=VMEM_LIMIT_BYTES),
        name="outproj_ffn_conv",
    )(x, at, u, u, mod4, mod4, mod4, mod4, *consts)


def _outproj_ffn(x, co, at, mod4, n2g, wo, wg3, wu3, wd3, bb, tt):
    nb, s, d = x.shape
    d_conv = co.shape[2]
    d_attn = at.shape[2]
    mod_spec = lambda j: pl.BlockSpec((bb, 1, 1, d), lambda i, t: (i, j, 0, 0))
    row_spec = lambda w: pl.BlockSpec((bb, tt, w), lambda i, t: (i, t, 0))
    return pl.pallas_call(
        _outproj_ffn_kernel,
        grid=(nb // bb, s // tt),
        in_specs=[row_spec(d), row_spec(d_conv), row_spec(d_attn),
                  mod_spec(2), mod_spec(3), mod_spec(4), mod_spec(5),
                  _const_spec(n2g.shape), _const_spec(wo.shape), _const_spec(wg3.shape),
                  _const_spec(wu3.shape), _const_spec(wd3.shape)],
        out_specs=row_spec(d),
        out_shape=jax.ShapeDtypeStruct((nb, s, d), F32),
        scratch_shapes=[pltpu.VMEM((bb * tt, d), F32)],
        compiler_params=pltpu.CompilerParams(dimension_semantics=("arbitrary", "arbitrary"),
                                             vmem_limit_bytes=VMEM_LIMIT_BYTES),
        name="outproj_ffn",
    )(x, co, at, mod4, mod4, mod4, mod4, n2g, wo, wg3, wu3, wd3)


def _tri(n, rel):
    i = lax.broadcasted_iota(jnp.int32, (n, n), 0)
    j = lax.broadcasted_iota(jnp.int32, (n, n), 1)
    return rel(i, j).astype(BF16)


def _layer(xp, xs, cache_k, cache_v, cache_logf, state_conv, c_all, w_ada, b_ada, norm1_g, w_in, b_f,
           q_norm_g, k_norm_g, conv_w, conv_b, conv_ln_g, conv_ln_b, w_out, norm2_g, w_gate, w_up, w_down):
    b, s, d = xp.shape
    nb, tt, _ = xs.shape
    d_conv = conv_w.shape[1]
    d_attn = N_HEADS * HEAD_DIM
    d_ff = w_gate.shape[1]
    p_len = cache_k.shape[1]

    mod4 = _modulation(c_all, w_ada, b_ada).reshape(b + nb, 6, 1, d)
    mod_p, mod_s = mod4[:b], mod4[b:]

    wglu = w_in[:, :2 * d_conv].astype(BF16)
    wqkvf_t = jnp.pad(w_in[:, 2 * d_conv:].T, ((0, LANES - N_HEADS), (0, 0))).astype(BF16)
    bf_col = b_f.reshape(N_HEADS, 1)
    qg_col = q_norm_g.reshape(d_attn, 1)
    kg_col = k_norm_g.reshape(d_attn, 1)
    row = lambda v: v.reshape(1, -1)
    n1g, n2g, cb, ln_g, ln_b = row(norm1_g), row(norm2_g), row(conv_b), row(conv_ln_g), row(conv_ln_b)
    wo = w_out.astype(BF16)
    n_ff = d_ff // FF_CHUNK
    wg3 = w_gate.astype(BF16).reshape(d, n_ff, FF_CHUNK).transpose(1, 0, 2)
    wu3 = w_up.astype(BF16).reshape(d, n_ff, FF_CHUNK).transpose(1, 0, 2)
    wd3 = w_down.astype(BF16).reshape(n_ff, FF_CHUNK, d)
    utri = _tri(MXU_DIM, lambda i, j: i <= j)
    ltri = _tri(MXU_DIM, lambda i, j: i > j)
    bdtri = _tri(nb * tt, lambda i, j: (i // tt == j // tt) & (j <= i))

    kt, vt, lft, cst, qt, qft, k, kb, u = _inproj_prompt(
        xp, mod_p, n1g, wglu, wqkvf_t, bf_col, qg_col, kg_col, utri)
    at = _attn_prompt(qt, qft, k, kb, vt)
    yp = _outproj_ffn_conv(xp, at, u, mod_p, n2g, wo, wg3, wu3, wd3, conv_w, cb, ln_g, ln_b)
    k_p = kt.reshape(b, N_HEADS, HEAD_DIM, s).transpose(0, 3, 1, 2)
    v_p = vt.reshape(b, N_HEADS, HEAD_DIM, s).transpose(0, 3, 1, 2)
    lf_p = lft.transpose(0, 2, 1)

    state_t = state_conv.transpose(1, 0, 2)
    k_s, v_s, lf_s, nst, q_s, qb_s, kb_s, co_s = _inproj_sample(
        xs, mod_s, n1g, wglu, wqkvf_t, bf_col, qg_col, kg_col, bdtri, conv_w, cb, ln_g, ln_b, state_t)
    r3 = lambda a: a.reshape(nb, tt, a.shape[-1])
    ckt = cache_k.transpose(0, 2, 3, 1).reshape(nb, d_attn, p_len)
    cvt = cache_v.transpose(0, 2, 3, 1).reshape(nb, d_attn, p_len)
    clft = cache_logf.transpose(0, 2, 1)
    at_s = _attn_sample(r3(q_s), r3(qb_s), r3(k_s), r3(v_s), r3(kb_s), ckt, cvt, clft, ltri)
    ys = _outproj_ffn(xs, r3(co_s), at_s, mod_s, n2g, wo, wg3, wu3, wd3, nb, tt)
    k_sn = k_s.reshape(nb, tt, N_HEADS, HEAD_DIM)
    v_sn = v_s.reshape(nb, tt, N_HEADS, HEAD_DIM)
    lf_sn = lf_s[:, :N_HEADS].reshape(nb, tt, N_HEADS)
    return yp, ys, (k_p, v_p, lf_p, cst), (k_sn, v_sn, lf_sn, nst.transpose(1, 0, 2))


def kernel(x_prompt, x_sample, cache_k, cache_v, cache_logf, state_conv, c_prompt, c_sample, w_ada, b_ada,
           norm1_g, w_in, b_f, q_norm_g, k_norm_g, conv_w, conv_b, conv_ln_g, conv_ln_b, w_out, norm2_g,
           w_gate, w_up, w_down):
    depth = w_ada.shape[0]
    c_all = jnp.concatenate([c_prompt, c_sample], axis=0)
    yp, ys = x_prompt, x_sample
    st_p, st_s = [], []
    for l in range(depth):
        yp, ys, sp, ss = _layer(
            yp, ys, cache_k[l], cache_v[l], cache_logf[l], state_conv[l], c_all, w_ada[l], b_ada[l],
            norm1_g[l], w_in[l], b_f[l], q_norm_g[l], k_norm_g[l], conv_w[l], conv_b[l], conv_ln_g[l],
            conv_ln_b[l], w_out[l], norm2_g[l], w_gate[l], w_up[l], w_down[l])
        st_p.append(sp)
        st_s.append(ss)
    stack = lambda xs: xs[0][None] if len(xs) == 1 else jnp.stack(xs)
    outs_p = [stack([s[i] for s in st_p]) for i in range(4)]
    outs_s = [stack([s[i] for s in st_s]) for i in range(4)]
    return (yp, ys, *outs_p, *outs_s)
```

```python
import functools

import jax
import jax.numpy as jnp
from jax import lax
from jax.experimental import pallas as pl
from jax.experimental.pallas import tpu as pltpu

F32 = jnp.float32
BF16 = jnp.bfloat16

N_HEADS = 8
HEAD_DIM = 64
CONV_WIDTH = 31
EPS = 1e-6
NEG_INF = -1e30

LANES = 128
SUBLANES = 8
MXU_DIM = 256
VMEM_LIMIT_BYTES = 56 * 1024 * 1024

N_PARTS = 3
N_FEAT = N_PARTS * N_HEADS
HIST = 32
CONV_ROWS = 32

ROW_TILE = 512
Q_TILE = 256
ATTN_LAG = 2
DENOM_ROWS = 16
LOG2E = 1.4426950408889634
FF_CHUNK = 256


def _dot(a, b):
    return jnp.dot(a, b, preferred_element_type=F32)


def _dot_nt(a, b):
    return lax.dot_general(a, b, (((1,), (1,)), ((), ())), preferred_element_type=F32)


def _split3(x):
    hi = x.astype(BF16).astype(F32)
    r = x - hi
    mid = r.astype(BF16).astype(F32)
    lo = (r - mid).astype(BF16).astype(F32)
    return hi, mid, lo


def _log_sigmoid(x):
    return jnp.minimum(x, 0.0) - jnp.log1p(jnp.exp(-jnp.abs(x)))


def _adaln_rmsnorm(x, g, scale, shift):
    y = x * lax.rsqrt(jnp.mean(x * x, axis=-1, keepdims=True) + EPS)
    return (y * g) * (1.0 + scale) + shift


def _const_spec(shape):
    n = len(shape)
    return pl.BlockSpec(shape, lambda *_: (0,) * n, pipeline_mode=pl.Buffered(1))


def _modulation_kernel(c_ref, w_ref, b_ref, o_ref):
    c = c_ref[...]
    a = (c * jax.nn.sigmoid(c)).astype(BF16)
    o_ref[...] = _dot(a, w_ref[...].astype(BF16)) + b_ref[...]


def _modulation(c, w_ada, b_ada):
    nb, d = c.shape
    n = w_ada.shape[1]
    tn = 768
    return pl.pallas_call(
        _modulation_kernel,
        grid=(n // tn,),
        in_specs=[pl.BlockSpec((nb, d), lambda j: (0, 0)),
                  pl.BlockSpec((d, tn), lambda j: (0, j)),
                  pl.BlockSpec((1, tn), lambda j: (0, j))],
        out_specs=pl.BlockSpec((nb, tn), lambda j: (0, j)),
        out_shape=jax.ShapeDtypeStruct((nb, n), F32),
        compiler_params=pltpu.CompilerParams(dimension_semantics=("arbitrary",),
                                             vmem_limit_bytes=VMEM_LIMIT_BYTES),
        name="modulation",
    )(c, w_ada, b_ada.reshape(1, n))


def _qkvf_feature_major(hb, wqkvf_t, qg_col, kg_col, bf_col):
    d_attn = N_HEADS * HEAD_DIM
    zt = _dot_nt(wqkvf_t, hb)
    r = zt.shape[1]

    def head_rms(z, g_col):
        z3 = z.reshape(N_HEADS, HEAD_DIM, r)
        ms = jnp.mean(z3 * z3, axis=1, keepdims=True)
        return (z3 * lax.rsqrt(ms + EPS)).reshape(d_attn, r) * g_col

    q_t = head_rms(zt[0:d_attn], qg_col)
    k_t = head_rms(zt[d_attn:2 * d_attn], kg_col)
    v_t = zt[2 * d_attn:3 * d_attn]
    lf_t = _log_sigmoid(zt[3 * d_attn:3 * d_attn + N_HEADS] + bf_col)
    return q_t, k_t, v_t, lf_t


def _conv_ln_swish(load_rows, cw_ref, cb, ln_g, ln_b):
    acc = cb + cw_ref[0:1, :] * load_rows(0)
    for k in range(1, CONV_WIDTH):
        acc = acc + cw_ref[k:k + 1, :] * load_rows(k)
    mu = jnp.mean(acc, axis=-1, keepdims=True)
    cen = acc - mu
    var = jnp.mean(cen * cen, axis=-1, keepdims=True)
    y = cen * lax.rsqrt(var + EPS) * ln_g + ln_b
    return y * jax.nn.sigmoid(y)


def _inproj_prompt_kernel(x_ref, sh_ref, sc_ref, n1g_ref, wglu_ref, wqkvf_ref, bf_ref, qg_ref, kg_ref,
                          utri_ref,
                          kt_ref, vt_ref, lft_ref, cst_ref, qt_ref, qft_ref, k_ref, kb_ref, u_ref,
                          carry_ref):
    t = pl.program_id(1)
    tm = x_ref.shape[1]
    d_conv = u_ref.shape[2]

    @pl.when(t == 0)
    def _():
        carry_ref[...] = jnp.zeros(carry_ref.shape, F32)

    hb = _adaln_rmsnorm(x_ref[0], n1g_ref[...], sc_ref[0, 0], sh_ref[0, 0]).astype(BF16)

    q_t, k_t, v_t, lf_t = _qkvf_feature_major(hb, wqkvf_ref[...], qg_ref[...], kg_ref[...], bf_ref[...])
    kt_ref[0] = k_t
    vt_ref[0] = v_t
    lft_ref[0] = lf_t

    zg = _dot(hb, wglu_ref[...])
    u_ref[0] = zg[:, :d_conv] * jax.nn.sigmoid(zg[:, d_conv:])

    nblk = tm // MXU_DIM
    parts = jnp.concatenate(_split3(lf_t), axis=0)
    stacked = jnp.concatenate([parts[:, i * MXU_DIM:(i + 1) * MXU_DIM] for i in range(nblk)], axis=0)
    local = _dot(stacked.astype(BF16), utri_ref[...])
    carry = carry_ref[:, 0:1]
    cums = []
    for i in range(nblk):
        loc = local[i * N_FEAT:(i + 1) * N_FEAT]
        cums.append(loc + carry)
        carry = carry + loc[:, MXU_DIM - 1:MXU_DIM]
    carry_ref[...] = jnp.broadcast_to(carry, carry_ref.shape)
    cum_parts = jnp.concatenate(cums, axis=1)
    cum_t = (cum_parts[0:8] + cum_parts[8:16] + cum_parts[16:24]) * LOG2E

    c_hi, c_mid, c_lo = _split3(cum_t)
    ones = jnp.ones((N_FEAT, tm), F32)
    zeros = jnp.zeros((LANES - 2 * N_FEAT, tm), F32)
    kfeat_t = jnp.concatenate([ones, -c_hi, -c_mid, -c_lo, zeros], axis=0)
    qft_ref[0] = jnp.concatenate([c_hi, c_mid, c_lo, ones, zeros], axis=0).astype(BF16)
    qt_ref[0] = (q_t * (HEAD_DIM ** -0.5 * LOG2E)).astype(BF16)
    k_ref[0] = k_t.T.astype(BF16)
    kb_ref[0] = kfeat_t.T.astype(BF16)

    @pl.when(t == pl.num_programs(1) - 1)
    def _():
        cst_ref[0] = u_ref[0, tm - (CONV_WIDTH - 1):tm, :]


def _inproj_prompt(x, mod4, n1g, wglu, wqkvf_t, bf_col, qg_col, kg_col, utri):
    b, s, d = x.shape
    tm = ROW_TILE
    d_conv = wglu.shape[1] // 2
    d_attn = N_HEADS * HEAD_DIM
    mod_spec = lambda j: pl.BlockSpec((1, 1, 1, d), lambda i, t: (i, j, 0, 0))
    out_shape = (
        jax.ShapeDtypeStruct((b, d_attn, s), F32),
        jax.ShapeDtypeStruct((b, d_attn, s), F32),
        jax.ShapeDtypeStruct((b, N_HEADS, s), F32),
        jax.ShapeDtypeStruct((b, CONV_WIDTH - 1, d_conv), F32),
        jax.ShapeDtypeStruct((b, d_attn, s), BF16),
        jax.ShapeDtypeStruct((b, LANES, s), BF16),
        jax.ShapeDtypeStruct((b, s, d_attn), BF16),
        jax.ShapeDtypeStruct((b, s, LANES), BF16),
        jax.ShapeDtypeStruct((b, s, d_conv), F32),
    )
    out_specs = (
        pl.BlockSpec((1, d_attn, tm), lambda i, t: (i, 0, t)),
        pl.BlockSpec((1, d_attn, tm), lambda i, t: (i, 0, t)),
        pl.BlockSpec((1, N_HEADS, tm), lambda i, t: (i, 0, t)),
        pl.BlockSpec((1, CONV_WIDTH - 1, d_conv), lambda i, t: (i, 0, 0)),
        pl.BlockSpec((1, d_attn, tm), lambda i, t: (i, 0, t)),
        pl.BlockSpec((1, LANES, tm), lambda i, t: (i, 0, t)),
        pl.BlockSpec((1, tm, d_attn), lambda i, t: (i, t, 0)),
        pl.BlockSpec((1, tm, LANES), lambda i, t: (i, t, 0)),
        pl.BlockSpec((1, tm, d_conv), lambda i, t: (i, t, 0)),
    )
    in_specs = [
        pl.BlockSpec((1, tm, d), lambda i, t: (i, t, 0)),
        mod_spec(0), mod_spec(1),
        _const_spec(n1g.shape), _const_spec(wglu.shape), _const_spec(wqkvf_t.shape),
        _const_spec(bf_col.shape), _const_spec(qg_col.shape), _const_spec(kg_col.shape),
        _const_spec(utri.shape),
    ]
    return pl.pallas_call(
        _inproj_prompt_kernel,
        grid=(b, s // tm),
        in_specs=in_specs,
        out_specs=out_specs,
        out_shape=out_shape,
        scratch_shapes=[pltpu.VMEM((N_FEAT, LANES), F32)],
        compiler_params=pltpu.CompilerParams(dimension_semantics=("arbitrary", "arbitrary"),
                                             vmem_limit_bytes=VMEM_LIMIT_BYTES),
        name="inproj_prompt",
    )(x, mod4, mod4, n1g, wglu, wqkvf_t, bf_col, qg_col, kg_col, utri)


def _attn_prompt_kernel(qt_ref, qft_ref, k_ref, kb_ref, vt_ref, o_ref,
                        kp_ref, vp_ref, q2_ref, m_ref, acc_ref, sc_ref):
    qi = pl.program_id(1)
    tq = qt_ref.shape[2]
    n_pairs, nblk = vp_ref.shape[0], vp_ref.shape[1]

    @pl.when(qi == 0)
    def _():
        ones = jnp.ones((vp_ref.shape[2] - LANES, tq), BF16)
        for p in range(n_pairs):
            kp_ref[p, :, 0:LANES] = k_ref[0, :, p * LANES:(p + 1) * LANES]
            kp_ref[p, :, LANES:2 * LANES] = kb_ref[0]
            for i in range(nblk):
                vp_ref[p, i, 0:LANES, :] = vt_ref[0, p * LANES:(p + 1) * LANES, i * tq:(i + 1) * tq].astype(BF16)
                vp_ref[p, i, LANES:, :] = ones

    frow = lax.broadcasted_iota(jnp.int32, (LANES, tq), 0)
    feat = qft_ref[0].astype(F32)
    for p in range(n_pairs):
        qp = qt_ref[0, p * LANES:(p + 1) * LANES, :].astype(F32)
        cols = []
        for j in range(2):
            qm = jnp.where((frow >= j * HEAD_DIM) & (frow < (j + 1) * HEAD_DIM), qp, 0.0)
            fm = jnp.where((frow % N_HEADS) == 2 * p + j, feat, 0.0)
            cols.append(jnp.concatenate([qm, fm], axis=0))
        q2_ref[p] = jnp.concatenate(cols, axis=1).astype(BF16)

    m_ref[...] = jnp.full(m_ref.shape, NEG_INF, F32)
    acc_ref[...] = jnp.zeros(acc_ref.shape, F32)

    def scores(p, j):
        start = pl.multiple_of(j * tq, tq)
        return _dot(kp_ref[p, pl.ds(start, tq), :], q2_ref[p])

    def accumulate(p, j, masked, s):
        if masked:
            key = lax.broadcasted_iota(jnp.int32, s.shape, 0)
            qry = lax.broadcasted_iota(jnp.int32, s.shape, 1)
            qry = jnp.where(qry >= tq, qry - tq, qry)
            s = jnp.where(key <= qry, s, NEG_INF)
        m_old = m_ref[p]
        m_new = jnp.maximum(m_old, jnp.max(s, axis=0, keepdims=True))
        pr = jnp.exp2(s - m_new).astype(BF16)
        acc_ref[p] = jnp.exp2(m_old - m_new) * acc_ref[p] + _dot(vp_ref[p, j], pr)
        m_ref[p] = m_new

    def run_units(units, next_block):
        pending = {}
        for i in range(len(units) + ATTN_LAG):
            if i < ATTN_LAG:
                pending[i] = sc_ref[i]
            elif i < len(units):
                p, j, _ = units[i]
                pending[i] = scores(p, j)
            elif next_block is not None:
                sc_ref[i - len(units)] = scores(i - len(units), next_block)
            if i >= ATTN_LAG:
                p, j, masked = units[i - ATTN_LAG]
                accumulate(p, j, masked, pending.pop(i - ATTN_LAG))

    def block(j, masked):
        return [(p, j, masked) for p in range(n_pairs)]

    for p in range(ATTN_LAG):
        sc_ref[p] = scores(p, 0)

    def body(i, carry):
        run_units(block(2 * i, False) + block(2 * i + 1, False), 2 * i + 2)
        return carry

    lax.fori_loop(0, qi // 2, body, 0)

    @pl.when(qi % 2 == 1)
    def _():
        run_units(block(qi - 1, False) + block(qi, True), None)

    @pl.when(qi % 2 == 0)
    def _():
        run_units(block(qi, True), None)

    for p in range(n_pairs):
        acc = acc_ref[p]
        out_t = acc[0:LANES] / acc[LANES:LANES + 1]
        pair_t = jnp.where(frow < HEAD_DIM, out_t[:, :tq], out_t[:, tq:])
        o_ref[0, :, p * LANES:(p + 1) * LANES] = pair_t.T.astype(BF16)


def _attn_prompt(qt, qft, k, kb, vt):
    b, d_attn, s = qt.shape
    tq = Q_TILE
    nblk = s // tq
    n_pairs = d_attn // LANES
    return pl.pallas_call(
        _attn_prompt_kernel,
        grid=(b, nblk),
        in_specs=[
            pl.BlockSpec((1, d_attn, tq), lambda i, j: (i, 0, j)),
            pl.BlockSpec((1, LANES, tq), lambda i, j: (i, 0, j)),
            pl.BlockSpec((1, s, d_attn), lambda i, j: (i, 0, 0)),
            pl.BlockSpec((1, s, LANES), lambda i, j: (i, 0, 0)),
            pl.BlockSpec((1, d_attn, s), lambda i, j: (i, 0, 0)),
        ],
        out_specs=pl.BlockSpec((1, tq, d_attn), lambda i, j: (i, j, 0)),
        out_shape=jax.ShapeDtypeStruct((b, s, d_attn), BF16),
        scratch_shapes=[
            pltpu.VMEM((n_pairs, s, 2 * LANES), BF16),
            pltpu.VMEM((n_pairs, nblk, LANES + DENOM_ROWS, tq), BF16),
            pltpu.VMEM((n_pairs, 2 * LANES, 2 * tq), BF16),
            pltpu.VMEM((n_pairs, 1, 2 * tq), F32),
            pltpu.VMEM((n_pairs, LANES + DENOM_ROWS, 2 * tq), F32),
            pltpu.VMEM((ATTN_LAG, tq, 2 * tq), F32),
        ],
        compiler_params=pltpu.CompilerParams(dimension_semantics=("arbitrary", "arbitrary"),
                                             vmem_limit_bytes=VMEM_LIMIT_BYTES),
        name="attn_prompt",
    )(qt, qft, k, kb, vt)


def _inproj_sample_kernel(x_ref, sh_ref, sc_ref, n1g_ref, wglu_ref, wqkvf_ref, bf_ref, qg_ref, kg_ref,
                          bdtri_ref, cw_ref, cb_ref, lng_ref, lnb_ref, st_ref,
                          k_ref, v_ref, lf_ref, nst_ref, q_ref, qb_ref, kb_ref, co_ref,
                          u_ref, hist_ref, cof_ref):
    nb, tt, d = x_ref.shape
    r = nb * tt
    d_conv = co_ref.shape[1]
    n_state = CONV_WIDTH - 1

    h = _adaln_rmsnorm(x_ref[...], n1g_ref[...], sc_ref[:, 0], sh_ref[:, 0])
    hb = h.reshape(r, d).astype(BF16)

    n_chunks = d_conv // LANES
    zg = _dot(hb, wglu_ref[...])
    u = zg[:, :d_conv] * jax.nn.sigmoid(zg[:, d_conv:])
    for c in range(n_chunks):
        u_ref[c] = u[:, c * LANES:(c + 1) * LANES]

    q_t, k_t, v_t, lf_t = _qkvf_feature_major(hb, wqkvf_ref[...], qg_ref[...], kg_ref[...], bf_ref[...])
    k_ref[...] = k_t.T
    v_ref[...] = v_t.T
    q_ref[...] = (q_t * (HEAD_DIM ** -0.5)).T.astype(BF16)
    lf = jnp.concatenate([lf_t, jnp.zeros((LANES - N_HEADS, r), F32)], axis=0).T
    lf_ref[...] = lf

    lane = lax.broadcasted_iota(jnp.int32, (r, LANES), 1)
    hi, mid, lo = _split3(lf)
    packed = hi + pltpu.roll(mid, N_HEADS, 1) + pltpu.roll(lo, 2 * N_HEADS, 1)
    c = _dot(bdtri_ref[...], packed.astype(BF16))
    cn = jnp.where(lane < N_HEADS,
                   c + pltpu.roll(c, LANES - N_HEADS, 1) + pltpu.roll(c, LANES - 2 * N_HEADS, 1), 0.0)
    hi, mid, lo = _split3(cn)
    p = hi + pltpu.roll(mid, N_HEADS, 1) + pltpu.roll(lo, 2 * N_HEADS, 1)
    qb_ref[...] = (p + jnp.where((lane >= N_FEAT) & (lane < 2 * N_FEAT), 1.0, 0.0)).astype(BF16)
    kb_ref[...] = (jnp.where(lane < N_FEAT, 1.0, 0.0) - pltpu.roll(p, N_FEAT, 1)).astype(BF16)

    hist_ref[0:n_state] = st_ref[...]
    for t in range(tt):
        for c in range(n_chunks):
            hist_ref[n_state + t, :, c * LANES:(c + 1) * LANES] = u_ref[c, pl.ds(t, nb, stride=tt), :]
    nst_ref[...] = hist_ref[tt:tt + n_state]
    cb, ln_g, ln_b = cb_ref[...], lng_ref[...], lnb_ref[...]
    for t in range(tt):
        y = _conv_ln_swish(lambda k: hist_ref[t + k], cw_ref, cb, ln_g, ln_b)
        for c in range(n_chunks):
            cof_ref[c, pl.ds(t, nb, stride=tt), :] = y[:, c * LANES:(c + 1) * LANES]
    co_ref[...] = jnp.concatenate([cof_ref[c] for c in range(n_chunks)], axis=1).astype(BF16)


def _inproj_sample(x, mod4, n1g, wglu, wqkvf_t, bf_col, qg_col, kg_col, bdtri, conv_w, conv_b, ln_g, ln_b,
                   state_t):
    nb, tt, d = x.shape
    r = nb * tt
    d_conv = conv_w.shape[1]
    d_attn = N_HEADS * HEAD_DIM
    n_state = CONV_WIDTH - 1
    mod_spec = lambda j: pl.BlockSpec((nb, 1, 1, d), lambda i: (0, j, 0, 0))
    full = lambda shape: pl.BlockSpec(shape, lambda i: (0,) * len(shape))
    out_shape = (
        jax.ShapeDtypeStruct((r, d_attn), F32),
        jax.ShapeDtypeStruct((r, d_attn), F32),
        jax.ShapeDtypeStruct((r, LANES), F32),
        jax.ShapeDtypeStruct((n_state, nb, d_conv), F32),
        jax.ShapeDtypeStruct((r, d_attn), BF16),
        jax.ShapeDtypeStruct((r, LANES), BF16),
        jax.ShapeDtypeStruct((r, LANES), BF16),
        jax.ShapeDtypeStruct((r, d_conv), BF16),
    )
    args = (x, mod4, mod4, n1g, wglu, wqkvf_t, bf_col, qg_col, kg_col, bdtri, conv_w, conv_b, ln_g, ln_b,
            state_t)
    in_specs = [full(x.shape), mod_spec(0), mod_spec(1)] + [full(a.shape) for a in args[3:]]
    return pl.pallas_call(
        _inproj_sample_kernel,
        grid=(1,),
        in_specs=in_specs,
        out_specs=tuple(full(o.shape) for o in out_shape),
        out_shape=out_shape,
        scratch_shapes=[pltpu.VMEM((d_conv // LANES, r, LANES), F32),
                        pltpu.VMEM((n_state + tt, nb, d_conv), F32),
                        pltpu.VMEM((d_conv // LANES, r, LANES), F32)],
        compiler_params=pltpu.CompilerParams(dimension_semantics=("arbitrary",),
                                             vmem_limit_bytes=VMEM_LIMIT_BYTES),
        name="inproj_sample",
    )(*args)


def _attn_sample_kernel(q_ref, qb_ref, kn_ref, vn_ref, kbn_ref, kt_ref, vt_ref, clft_ref, ltri_ref, o_ref):
    tt = q_ref.shape[1]
    d_attn = q_ref.shape[2]
    p_len = kt_ref.shape[2]
    nblk = p_len // MXU_DIM
    rows = N_HEADS * tt

    parts = jnp.concatenate(_split3(clft_ref[0]), axis=0)
    blocks = [parts[:, i * MXU_DIM:(i + 1) * MXU_DIM] for i in range(nblk)]
    local = _dot(jnp.concatenate(blocks, axis=0).astype(BF16), ltri_ref[...])
    off = jnp.zeros((N_FEAT, 1), F32)
    sufs = [None] * nblk
    for i in reversed(range(nblk)):
        loc = local[i * N_FEAT:(i + 1) * N_FEAT]
        sufs[i] = loc + off
        off = off + loc[:, 0:1] + blocks[i][:, 0:1]
    suf_parts = jnp.concatenate(sufs, axis=1)
    ck_rel = suf_parts[0:8] + suf_parts[8:16] + suf_parts[16:24]
    c_hi, c_mid, c_lo = _split3(ck_rel)
    kbt = jnp.concatenate([jnp.ones((N_FEAT, p_len), F32), c_hi, c_mid, c_lo,
                           jnp.zeros((LANES - 2 * N_FEAT, p_len), F32)], axis=0)
    k_all = jnp.concatenate([kt_ref[0].astype(BF16), kbt.astype(BF16)], axis=0)

    lane_q = lax.broadcasted_iota(jnp.int32, (tt, d_attn), 1)
    lane_b = lax.broadcasted_iota(jnp.int32, (tt, LANES), 1)
    q = q_ref[0].astype(F32)
    qb = qb_ref[0].astype(F32)
    stack = []
    for h in range(N_HEADS):
        qm = jnp.where((lane_q >= h * HEAD_DIM) & (lane_q < (h + 1) * HEAD_DIM), q, 0.0)
        bm = jnp.where((lane_b % N_HEADS) == h, qb, 0.0)
        stack.append(jnp.concatenate([qm, bm], axis=1))
    qs = jnp.concatenate(stack, axis=0).astype(BF16)

    s_c = _dot(qs, k_all)

    pad = jnp.zeros((LANES - tt, d_attn + LANES), F32)
    kn = jnp.concatenate([jnp.concatenate([kn_ref[0], kbn_ref[0].astype(F32)], axis=1), pad], axis=0)
    s_n = _dot_nt(qs, kn.astype(BF16))
    row = lax.broadcasted_iota(jnp.int32, s_n.shape, 0)
    col = lax.broadcasted_iota(jnp.int32, s_n.shape, 1)
    s_n = jnp.where(col <= (row % tt), s_n, NEG_INF)

    m = jnp.maximum(jnp.max(s_c, axis=-1, keepdims=True), jnp.max(s_n, axis=-1, keepdims=True))
    p_c = jnp.exp(s_c - m)
    p_n = jnp.exp(s_n - m)
    l = jnp.sum(p_c, axis=-1, keepdims=True) + jnp.sum(p_n, axis=-1, keepdims=True)
    vn = jnp.concatenate([vn_ref[0], jnp.zeros((LANES - tt, d_attn), F32)], axis=0).astype(BF16)
    o = _dot_nt(p_c.astype(BF16), vt_ref[0].astype(BF16)) + _dot(p_n.astype(BF16), vn)
    o = o / l

    out = jnp.zeros((tt, d_attn), F32)
    for h in range(N_HEADS):
        out = jnp.where((lane_q >= h * HEAD_DIM) & (lane_q < (h + 1) * HEAD_DIM), o[h * tt:(h + 1) * tt], out)
    o_ref[0] = out.astype(BF16)


def _attn_sample(q, qb, kn, vn, kbn, kt, vt, clft, ltri):
    nb, tt, d_attn = q.shape
    p_len = kt.shape[2]
    per_req = lambda shape: pl.BlockSpec((1,) + shape, lambda i: (i, 0, 0))
    return pl.pallas_call(
        _attn_sample_kernel,
        grid=(nb,),
        in_specs=[per_req((tt, d_attn)), per_req((tt, LANES)), per_req((tt, d_attn)), per_req((tt, d_attn)),
                  per_req((tt, LANES)), per_req((d_attn, p_len)), per_req((d_attn, p_len)),
                  per_req((N_HEADS, p_len)), _const_spec(ltri.shape)],
        out_specs=per_req((tt, d_attn)),
        out_shape=jax.ShapeDtypeStruct((nb, tt, d_attn), BF16),
        compiler_params=pltpu.CompilerParams(dimension_semantics=("arbitrary",),
                                             vmem_limit_bytes=VMEM_LIMIT_BYTES),
        name="attn_sample",
    )(q, qb, kn, vn, kbn, kt, vt, clft, ltri)


def _outproj_ffn_body(x_ref, co, at, g1_ref, sh2_ref, sc2_ref, g2_ref, n2g_ref,
                      wo_ref, wg_ref, wu_ref, wd_ref, y_ref, acc_ref, vpu_slot=None):
    nb, tt, d = x_ref.shape
    r = nb * tt
    d_conv = co.shape[1]
    mix = _dot(co, wo_ref[0:d_conv, :]) + _dot(at, wo_ref[d_conv:, :])
    x1 = x_ref[...] + g1_ref[:, 0] * mix.reshape(nb, tt, d)
    hb = _adaln_rmsnorm(x1, n2g_ref[...], sc2_ref[:, 0], sh2_ref[:, 0]).reshape(r, d).astype(BF16)
    lead = x1.reshape(r, d)[0:SUBLANES, 0:d_conv]
    for c in range(wg_ref.shape[1] // FF_CHUNK):
        cols = slice(c * FF_CHUNK, (c + 1) * FF_CHUNK)
        zero = vpu_slot(c, lead) if vpu_slot is not None else None
        g = _dot(hb, wg_ref[:, cols])
        a = ((g * jax.nn.sigmoid(g)) * _dot(hb, wu_ref[:, cols])).astype(BF16)
        part = _dot(a, wd_ref[cols, :])
        if c == 0:
            acc_ref[...] = part
        else:
            acc_ref[...] += part
        if zero is not None:
            acc_ref[0:SUBLANES, 0:LANES] += zero
        lead = part[0:SUBLANES, 0:d_conv]
    y_ref[...] = x1 + g2_ref[:, 0] * acc_ref[...].reshape(nb, tt, d)


def _outproj_ffn_kernel(x_ref, co_ref, at_ref, g1_ref, sh2_ref, sc2_ref, g2_ref, n2g_ref,
                        wo_ref, wg_ref, wu_ref, wd_ref, y_ref, acc_ref):
    nb, tt, _ = x_ref.shape
    co = co_ref[...].reshape(nb * tt, co_ref.shape[2])
    at = at_ref[...].reshape(nb * tt, at_ref.shape[2])
    _outproj_ffn_body(x_ref, co, at, g1_ref, sh2_ref, sc2_ref, g2_ref, n2g_ref,
                      wo_ref, wg_ref, wu_ref, wd_ref, y_ref, acc_ref)


def _exact_zero(v):
    return jnp.minimum(jnp.abs(v), 0.0)


def _outproj_ffn_conv_kernel(x_ref, at_ref, un_ref, u0_ref, g1_ref, sh2_ref, sc2_ref, g2_ref, n2g_ref,
                             wo_ref, wg_ref, wu_ref, wd_ref, cw_ref, cb_ref, lng_ref, lnb_ref,
                             y_ref, acc_ref, shift_ref, co_ref, *, tiles_per_seq):
    g = pl.program_id(0)
    tm = x_ref.shape[1]
    d_conv = co_ref.shape[1]
    span = tm + HIST - SUBLANES
    n_conv = tm // CONV_ROWS
    n_ff = wg_ref.shape[1] // FF_CHUNK
    cb, ln_g, ln_b = cb_ref[...], lng_ref[...], lnb_ref[...]

    def stage(u, hist):
        shift_ref[0, 0:HIST, :] = hist
        shift_ref[0, HIST:HIST + tm, :] = u
        for r in range(1, SUBLANES):
            shift_ref[r, 0:span, :] = shift_ref[0, r:r + span, :]

    def conv_chunk(c, bias):
        def tap(k):
            off = k + HIST - (CONV_WIDTH - 1)
            row = c * CONV_ROWS + off - off % SUBLANES
            return shift_ref[off % SUBLANES, row:row + CONV_ROWS, :]
        y = _conv_ln_swish(tap, cw_ref, bias, ln_g, ln_b)
        co_ref[c * CONV_ROWS:(c + 1) * CONV_ROWS, :] = y.astype(BF16)
        return y

    @pl.when(g == 0)
    def _():
        stage(u0_ref[0], jnp.zeros((HIST, d_conv), F32))
        for c in range(n_conv):
            conv_chunk(c, cb)

    def conv_slot(c, lead):
        if c == 0:
            tail = shift_ref[0, tm:tm + HIST, :]
            starts_seq = (g + 1) % tiles_per_seq == 0
            stage(un_ref[0], jnp.where(starts_seq, 0.0, tail))
        bias = jnp.concatenate([cb + _exact_zero(lead)] * (CONV_ROWS // SUBLANES), axis=0)
        zero = jnp.zeros((SUBLANES, LANES), F32)
        for i in range(c * n_conv // n_ff, (c + 1) * n_conv // n_ff):
            zero = zero + _exact_zero(conv_chunk(i, bias)[0:SUBLANES, 0:LANES])
        return zero

    co = co_ref[...]
    _outproj_ffn_body(x_ref, co, at_ref[0], g1_ref, sh2_ref, sc2_ref, g2_ref, n2g_ref,
                      wo_ref, wg_ref, wu_ref, wd_ref, y_ref, acc_ref, vpu_slot=conv_slot)


def _outproj_ffn_conv(x, at, u, mod4, n2g, wo, wg, wu, wd, conv_w, conv_b, ln_g, ln_b):
    b, s, d = x.shape
    tm = ROW_TILE
    d_conv = u.shape[2]
    d_attn = at.shape[2]
    tps = s // tm
    n_tiles = b * tps
    tile = lambda g: (g // tps, g % tps, 0)
    next_tile = lambda g: tile(jnp.minimum(g + 1, n_tiles - 1))
    mod_spec = lambda j: pl.BlockSpec((1, 1, 1, d), lambda g: (g // tps, j, 0, 0))
    consts = (n2g, wo, wg, wu, wd, conv_w, conv_b, ln_g, ln_b)
    return pl.pallas_call(
        functools.partial(_outproj_ffn_conv_kernel, tiles_per_seq=tps),
        grid=(n_tiles,),
        in_specs=[pl.BlockSpec((1, tm, d), tile), pl.BlockSpec((1, tm, d_attn), tile),
                  pl.BlockSpec((1, tm, d_conv), next_tile),
                  pl.BlockSpec((1, tm, d_conv), lambda g: (0, 0, 0), pipeline_mode=pl.Buffered(1)),
                  mod_spec(2), mod_spec(3), mod_spec(4), mod_spec(5)] + [_const_spec(c.shape) for c in consts],
        out_specs=pl.BlockSpec((1, tm, d), tile),
        out_shape=jax.ShapeDtypeStruct((b, s, d), F32),
        scratch_shapes=[pltpu.VMEM((tm, d), F32),
                        pltpu.VMEM((SUBLANES, tm + HIST, d_conv), F32),
                        pltpu.VMEM((tm, d_conv), BF16)],
        compiler_params=pltpu.CompilerParams(dimension_semantics=("arbitrary",),
                                             vmem_limit_bytes=VMEM_LIMIT_BYTES),
        name="outproj_ffn_conv",
    )(x, at, u, u, mod4, mod4, mod4, mod4, *consts)


def _outproj_ffn(x, co, at, mod4, n2g, wo, wg, wu, wd, bb, tt):
    nb, s, d = x.shape
    d_conv = co.shape[2]
    d_attn = at.shape[2]
    mod_spec = lambda j: pl.BlockSpec((bb, 1, 1, d), lambda i, t: (i, j, 0, 0))
    row_spec = lambda w: pl.BlockSpec((bb, tt, w), lambda i, t: (i, t, 0))
    return pl.pallas_call(
        _outproj_ffn_kernel,
        grid=(nb // bb, s // tt),
        in_specs=[row_spec(d), row_spec(d_conv), row_spec(d_attn),
                  mod_spec(2), mod_spec(3), mod_spec(4), mod_spec(5),
                  _const_spec(n2g.shape), _const_spec(wo.shape), _const_spec(wg.shape),
                  _const_spec(wu.shape), _const_spec(wd.shape)],
        out_specs=row_spec(d),
        out_shape=jax.ShapeDtypeStruct((nb, s, d), F32),
        scratch_shapes=[pltpu.VMEM((bb * tt, d), F32)],
        compiler_params=pltpu.CompilerParams(dimension_semantics=("arbitrary", "arbitrary"),
                                             vmem_limit_bytes=VMEM_LIMIT_BYTES),
        name="outproj_ffn",
    )(x, co, at, mod4, mod4, mod4, mod4, n2g, wo, wg, wu, wd)


def _tri(n, rel):
    i = lax.broadcasted_iota(jnp.int32, (n, n), 0)
    j = lax.broadcasted_iota(jnp.int32, (n, n), 1)
    return rel(i, j).astype(BF16)


def _layer(xp, xs, cache_k, cache_v, cache_logf, state_conv, c_all, w_ada, b_ada, norm1_g, w_in, b_f,
           q_norm_g, k_norm_g, conv_w, conv_b, conv_ln_g, conv_ln_b, w_out, norm2_g, w_gate, w_up, w_down):
    b, s, d = xp.shape
    nb, tt, _ = xs.shape
    d_conv = conv_w.shape[1]
    d_attn = N_HEADS * HEAD_DIM
    d_ff = w_gate.shape[1]
    p_len = cache_k.shape[1]

    mod4 = _modulation(c_all, w_ada, b_ada).reshape(b + nb, 6, 1, d)
    mod_p, mod_s = mod4[:b], mod4[b:]

    wglu = w_in[:, :2 * d_conv].astype(BF16)
    wqkvf_t = jnp.pad(w_in[:, 2 * d_conv:].T, ((0, LANES - N_HEADS), (0, 0))).astype(BF16)
    bf_col = b_f.reshape(N_HEADS, 1)
    qg_col = q_norm_g.reshape(d_attn, 1)
    kg_col = k_norm_g.reshape(d_attn, 1)
    row = lambda v: v.reshape(1, -1)
    n1g, n2g, cb, ln_g, ln_b = row(norm1_g), row(norm2_g), row(conv_b), row(conv_ln_g), row(conv_ln_b)
    wo = w_out.astype(BF16)
    assert d_ff % FF_CHUNK == 0
    wg, wu, wd = w_gate.astype(BF16), w_up.astype(BF16), w_down.astype(BF16)
    utri = _tri(MXU_DIM, lambda i, j: i <= j)
    ltri = _tri(MXU_DIM, lambda i, j: i > j)
    bdtri = _tri(nb * tt, lambda i, j: (i // tt == j // tt) & (j <= i))

    kt, vt, lft, cst, qt, qft, k, kb, u = _inproj_prompt(
        xp, mod_p, n1g, wglu, wqkvf_t, bf_col, qg_col, kg_col, utri)
    at = _attn_prompt(qt, qft, k, kb, vt)
    yp = _outproj_ffn_conv(xp, at, u, mod_p, n2g, wo, wg, wu, wd, conv_w, cb, ln_g, ln_b)
    k_p = kt.reshape(b, N_HEADS, HEAD_DIM, s).transpose(0, 3, 1, 2)
    v_p = vt.reshape(b, N_HEADS, HEAD_DIM, s).transpose(0, 3, 1, 2)
    lf_p = lft.transpose(0, 2, 1)

    state_t = state_conv.transpose(1, 0, 2)
    k_s, v_s, lf_s, nst, q_s, qb_s, kb_s, co_s = _inproj_sample(
        xs, mod_s, n1g, wglu, wqkvf_t, bf_col, qg_col, kg_col, bdtri, conv_w, cb, ln_g, ln_b, state_t)
    r3 = lambda a: a.reshape(nb, tt, a.shape[-1])
    ckt = cache_k.transpose(0, 2, 3, 1).reshape(nb, d_attn, p_len)
    cvt = cache_v.transpose(0, 2, 3, 1).reshape(nb, d_attn, p_len)
    clft = cache_logf.transpose(0, 2, 1)
    at_s = _attn_sample(r3(q_s), r3(qb_s), r3(k_s), r3(v_s), r3(kb_s), ckt, cvt, clft, ltri)
    ys = _outproj_ffn(xs, r3(co_s), at_s, mod_s, n2g, wo, wg, wu, wd, nb, tt)
    k_sn = k_s.reshape(nb, tt, N_HEADS, HEAD_DIM)
    v_sn = v_s.reshape(nb, tt, N_HEADS, HEAD_DIM)
    lf_sn = lf_s[:, :N_HEADS].reshape(nb, tt, N_HEADS)
    return yp, ys, (k_p, v_p, lf_p, cst), (k_sn, v_sn, lf_sn, nst.transpose(1, 0, 2))


def kernel(x_prompt, x_sample, cache_k, cache_v, cache_logf, state_conv, c_prompt, c_sample, w_ada, b_ada,
           norm1_g, w_in, b_f, q_norm_g, k_norm_g, conv_w, conv_b, conv_ln_g, conv_ln_b, w_out, norm2_g,
           w_gate, w_up, w_down):
    depth = w_ada.shape[0]
    c_all = jnp.concatenate([c_prompt, c_sample], axis=0)
    yp, ys = x_prompt, x_sample
    st_p, st_s = [], []
    for l in range(depth):
        yp, ys, sp, ss = _layer(
            yp, ys, cache_k[l], cache_v[l], cache_logf[l], state_conv[l], c_all, w_ada[l], b_ada[l],
            norm1_g[l], w_in[l], b_f[l], q_norm_g[l], k_norm_g[l], conv_w[l], conv_b[l], conv_ln_g[l],
            conv_ln_b[l], w_out[l], norm2_g[l], w_gate[l], w_up[l], w_down[l])
        st_p.append(sp)
        st_s.append(ss)
    stack = lambda xs: xs[0][None] if len(xs) == 1 else jnp.stack(xs)
    outs_p = [stack([s[i] for s in st_p]) for i in range(4)]
    outs_s = [stack([s[i] for s in st_s]) for i in range(4)]
    return (yp, ys, *outs_p, *outs_s)
```

```python
import functools

import jax
import jax.numpy as jnp
from jax import lax
from jax.experimental import pallas as pl
from jax.experimental.pallas import tpu as pltpu

F32 = jnp.float32
BF16 = jnp.bfloat16

N_HEADS = 8
HEAD_DIM = 64
CONV_WIDTH = 31
EPS = 1e-6
NEG_INF = -1e30

LANES = 128
SUBLANES = 8
MXU_DIM = 256
VMEM_LIMIT_BYTES = 56 * 1024 * 1024

N_PARTS = 3
N_FEAT = N_PARTS * N_HEADS
HIST = 32
CONV_ROWS = 32

ROW_TILE = 512
Q_TILE = 256
ATTN_LAG = 2
DENOM_ROWS = 16
LOG2E = 1.4426950408889634
FF_CHUNK = 256


def _dot(a, b):
    return jnp.dot(a, b, preferred_element_type=F32)


def _dot_nt(a, b):
    return lax.dot_general(a, b, (((1,), (1,)), ((), ())), preferred_element_type=F32)


def _split3(x):
    hi = x.astype(BF16).astype(F32)
    r = x - hi
    mid = r.astype(BF16).astype(F32)
    lo = (r - mid).astype(BF16).astype(F32)
    return hi, mid, lo


def _log_sigmoid(x):
    return jnp.minimum(x, 0.0) - jnp.log1p(jnp.exp(-jnp.abs(x)))


def _adaln_rmsnorm(x, g, scale, shift):
    y = x * lax.rsqrt(jnp.mean(x * x, axis=-1, keepdims=True) + EPS)
    return (y * g) * (1.0 + scale) + shift


def _const_spec(shape):
    n = len(shape)
    return pl.BlockSpec(shape, lambda *_: (0,) * n, pipeline_mode=pl.Buffered(1))


def _modulation_kernel(c_ref, w_ref, b_ref, o_ref):
    c = c_ref[...]
    a = (c * jax.nn.sigmoid(c)).astype(BF16)
    o_ref[...] = _dot(a, w_ref[...].astype(BF16)) + b_ref[...]


def _modulation(c, w_ada, b_ada):
    nb, d = c.shape
    n = w_ada.shape[1]
    tn = 768
    return pl.pallas_call(
        _modulation_kernel,
        grid=(n // tn,),
        in_specs=[pl.BlockSpec((nb, d), lambda j: (0, 0)),
                  pl.BlockSpec((d, tn), lambda j: (0, j)),
                  pl.BlockSpec((1, tn), lambda j: (0, j))],
        out_specs=pl.BlockSpec((nb, tn), lambda j: (0, j)),
        out_shape=jax.ShapeDtypeStruct((nb, n), F32),
        compiler_params=pltpu.CompilerParams(dimension_semantics=("arbitrary",),
                                             vmem_limit_bytes=VMEM_LIMIT_BYTES),
        name="modulation",
    )(c, w_ada, b_ada.reshape(1, n))


def _qkvf_feature_major(hb, wqkvf_t, qg_col, kg_col, bf_col):
    d_attn = N_HEADS * HEAD_DIM
    zt = _dot_nt(wqkvf_t, hb)
    r = zt.shape[1]

    def head_rms(z, g_col):
        z3 = z.reshape(N_HEADS, HEAD_DIM, r)
        ms = jnp.mean(z3 * z3, axis=1, keepdims=True)
        return (z3 * lax.rsqrt(ms + EPS)).reshape(d_attn, r) * g_col

    q_t = head_rms(zt[0:d_attn], qg_col)
    k_t = head_rms(zt[d_attn:2 * d_attn], kg_col)
    v_t = zt[2 * d_attn:3 * d_attn]
    lf_t = _log_sigmoid(zt[3 * d_attn:3 * d_attn + N_HEADS] + bf_col)
    return q_t, k_t, v_t, lf_t


def _conv_ln_swish(load_rows, weight, cb, ln_g, ln_b):
    acc = cb + weight(0) * load_rows(0)
    for k in range(1, CONV_WIDTH):
        acc = acc + weight(k) * load_rows(k)
    mu = jnp.mean(acc, axis=-1, keepdims=True)
    cen = acc - mu
    var = jnp.mean(cen * cen, axis=-1, keepdims=True)
    y = cen * lax.rsqrt(var + EPS) * ln_g + ln_b
    return y * jax.nn.sigmoid(y)


def _inproj_prompt_kernel(x_ref, sh_ref, sc_ref, n1g_ref, wglu_ref, wqkvf_ref, bf_ref, qg_ref, kg_ref,
                          utri_ref,
                          kt_ref, vt_ref, lft_ref, cst_ref, qt_ref, qft_ref, k_ref, kb_ref, u_ref,
                          carry_ref):
    t = pl.program_id(1)
    tm = x_ref.shape[1]
    d_conv = u_ref.shape[2]

    @pl.when(t == 0)
    def _():
        carry_ref[...] = jnp.zeros(carry_ref.shape, F32)

    hb = _adaln_rmsnorm(x_ref[0], n1g_ref[...], sc_ref[0, 0], sh_ref[0, 0]).astype(BF16)

    q_t, k_t, v_t, lf_t = _qkvf_feature_major(hb, wqkvf_ref[...], qg_ref[...], kg_ref[...], bf_ref[...])
    kt_ref[0] = k_t
    vt_ref[0] = v_t
    lft_ref[0] = lf_t

    zg = _dot(hb, wglu_ref[...])
    u_ref[0] = zg[:, :d_conv] * jax.nn.sigmoid(zg[:, d_conv:])

    nblk = tm // MXU_DIM
    parts = jnp.concatenate(_split3(lf_t), axis=0)
    stacked = jnp.concatenate([parts[:, i * MXU_DIM:(i + 1) * MXU_DIM] for i in range(nblk)], axis=0)
    local = _dot(stacked.astype(BF16), utri_ref[...])
    carry = carry_ref[:, 0:1]
    cums = []
    for i in range(nblk):
        loc = local[i * N_FEAT:(i + 1) * N_FEAT]
        cums.append(loc + carry)
        carry = carry + loc[:, MXU_DIM - 1:MXU_DIM]
    carry_ref[...] = jnp.broadcast_to(carry, carry_ref.shape)
    cum_parts = jnp.concatenate(cums, axis=1)
    cum_t = (cum_parts[0:8] + cum_parts[8:16] + cum_parts[16:24]) * LOG2E

    c_hi, c_mid, c_lo = _split3(cum_t)
    ones = jnp.ones((N_FEAT, tm), F32)
    zeros = jnp.zeros((LANES - 2 * N_FEAT, tm), F32)
    kfeat_t = jnp.concatenate([ones, -c_hi, -c_mid, -c_lo, zeros], axis=0)
    qft_ref[0] = jnp.concatenate([c_hi, c_mid, c_lo, ones, zeros], axis=0).astype(BF16)
    qt_ref[0] = (q_t * (HEAD_DIM ** -0.5 * LOG2E)).astype(BF16)
    k_ref[0] = k_t.T.astype(BF16)
    kb_ref[0] = kfeat_t.T.astype(BF16)

    @pl.when(t == pl.num_programs(1) - 1)
    def _():
        cst_ref[0] = u_ref[0, tm - (CONV_WIDTH - 1):tm, :]


def _inproj_prompt(x, mod4, n1g, wglu, wqkvf_t, bf_col, qg_col, kg_col, utri):
    b, s, d = x.shape
    tm = ROW_TILE
    d_conv = wglu.shape[1] // 2
    d_attn = N_HEADS * HEAD_DIM
    mod_spec = lambda j: pl.BlockSpec((1, 1, 1, d), lambda i, t: (i, j, 0, 0))
    out_shape = (
        jax.ShapeDtypeStruct((b, d_attn, s), F32),
        jax.ShapeDtypeStruct((b, d_attn, s), F32),
        jax.ShapeDtypeStruct((b, N_HEADS, s), F32),
        jax.ShapeDtypeStruct((b, CONV_WIDTH - 1, d_conv), F32),
        jax.ShapeDtypeStruct((b, d_attn, s), BF16),
        jax.ShapeDtypeStruct((b, LANES, s), BF16),
        jax.ShapeDtypeStruct((b, s, d_attn), BF16),
        jax.ShapeDtypeStruct((b, s, LANES), BF16),
        jax.ShapeDtypeStruct((b, s, d_conv), F32),
    )
    out_specs = (
        pl.BlockSpec((1, d_attn, tm), lambda i, t: (i, 0, t)),
        pl.BlockSpec((1, d_attn, tm), lambda i, t: (i, 0, t)),
        pl.BlockSpec((1, N_HEADS, tm), lambda i, t: (i, 0, t)),
        pl.BlockSpec((1, CONV_WIDTH - 1, d_conv), lambda i, t: (i, 0, 0)),
        pl.BlockSpec((1, d_attn, tm), lambda i, t: (i, 0, t)),
        pl.BlockSpec((1, LANES, tm), lambda i, t: (i, 0, t)),
        pl.BlockSpec((1, tm, d_attn), lambda i, t: (i, t, 0)),
        pl.BlockSpec((1, tm, LANES), lambda i, t: (i, t, 0)),
        pl.BlockSpec((1, tm, d_conv), lambda i, t: (i, t, 0)),
    )
    in_specs = [
        pl.BlockSpec((1, tm, d), lambda i, t: (i, t, 0)),
        mod_spec(0), mod_spec(1),
        _const_spec(n1g.shape), _const_spec(wglu.shape), _const_spec(wqkvf_t.shape),
        _const_spec(bf_col.shape), _const_spec(qg_col.shape), _const_spec(kg_col.shape),
        _const_spec(utri.shape),
    ]
    return pl.pallas_call(
        _inproj_prompt_kernel,
        grid=(b, s // tm),
        in_specs=in_specs,
        out_specs=out_specs,
        out_shape=out_shape,
        scratch_shapes=[pltpu.VMEM((N_FEAT, LANES), F32)],
        compiler_params=pltpu.CompilerParams(dimension_semantics=("arbitrary", "arbitrary"),
                                             vmem_limit_bytes=VMEM_LIMIT_BYTES),
        name="inproj_prompt",
    )(x, mod4, mod4, n1g, wglu, wqkvf_t, bf_col, qg_col, kg_col, utri)


def _attn_prompt_kernel(qt_ref, qft_ref, k_ref, kb_ref, vt_ref, o_ref,
                        kp_ref, vp_ref, q2_ref, m_ref, acc_ref, sc_ref):
    qi = pl.program_id(1)
    tq = qt_ref.shape[2]
    n_pairs, nblk = vp_ref.shape[0], vp_ref.shape[1]

    @pl.when(qi == 0)
    def _():
        ones = jnp.ones((vp_ref.shape[2] - LANES, tq), BF16)
        for p in range(n_pairs):
            kp_ref[p, :, 0:LANES] = k_ref[0, :, p * LANES:(p + 1) * LANES]
            kp_ref[p, :, LANES:2 * LANES] = kb_ref[0]
            for i in range(nblk):
                vp_ref[p, i, 0:LANES, :] = vt_ref[0, p * LANES:(p + 1) * LANES, i * tq:(i + 1) * tq].astype(BF16)
                vp_ref[p, i, LANES:, :] = ones

    frow = lax.broadcasted_iota(jnp.int32, (LANES, tq), 0)
    feat = qft_ref[0].astype(F32)
    for p in range(n_pairs):
        qp = qt_ref[0, p * LANES:(p + 1) * LANES, :].astype(F32)
        cols = []
        for j in range(2):
            qm = jnp.where((frow >= j * HEAD_DIM) & (frow < (j + 1) * HEAD_DIM), qp, 0.0)
            fm = jnp.where((frow % N_HEADS) == 2 * p + j, feat, 0.0)
            cols.append(jnp.concatenate([qm, fm], axis=0))
        q2_ref[p] = jnp.concatenate(cols, axis=1).astype(BF16)

    m_ref[...] = jnp.full(m_ref.shape, NEG_INF, F32)
    acc_ref[...] = jnp.zeros(acc_ref.shape, F32)

    def scores(p, j):
        start = pl.multiple_of(j * tq, tq)
        return _dot(kp_ref[p, pl.ds(start, tq), :], q2_ref[p])

    def accumulate(p, j, masked, s):
        if masked:
            key = lax.broadcasted_iota(jnp.int32, s.shape, 0)
            qry = lax.broadcasted_iota(jnp.int32, s.shape, 1)
            qry = jnp.where(qry >= tq, qry - tq, qry)
            s = jnp.where(key <= qry, s, NEG_INF)
        m_old = m_ref[p]
        m_new = jnp.maximum(m_old, jnp.max(s, axis=0, keepdims=True))
        pr = jnp.exp2(s - m_new).astype(BF16)
        acc_ref[p] = jnp.exp2(m_old - m_new) * acc_ref[p] + _dot(vp_ref[p, j], pr)
        m_ref[p] = m_new

    def run_units(units, next_block):
        pending = {}
        for i in range(len(units) + ATTN_LAG):
            if i < ATTN_LAG:
                pending[i] = sc_ref[i]
            elif i < len(units):
                p, j, _ = units[i]
                pending[i] = scores(p, j)
            elif next_block is not None:
                sc_ref[i - len(units)] = scores(i - len(units), next_block)
            if i >= ATTN_LAG:
                p, j, masked = units[i - ATTN_LAG]
                accumulate(p, j, masked, pending.pop(i - ATTN_LAG))

    def block(j, masked):
        return [(p, j, masked) for p in range(n_pairs)]

    for p in range(ATTN_LAG):
        sc_ref[p] = scores(p, 0)

    def body(i, carry):
        run_units(block(2 * i, False) + block(2 * i + 1, False), 2 * i + 2)
        return carry

    lax.fori_loop(0, qi // 2, body, 0)

    @pl.when(qi % 2 == 1)
    def _():
        run_units(block(qi - 1, False) + block(qi, True), None)

    @pl.when(qi % 2 == 0)
    def _():
        run_units(block(qi, True), None)

    for p in range(n_pairs):
        acc = acc_ref[p]
        out_t = acc[0:LANES] / acc[LANES:LANES + 1]
        pair_t = jnp.where(frow < HEAD_DIM, out_t[:, :tq], out_t[:, tq:])
        o_ref[0, :, p * LANES:(p + 1) * LANES] = pair_t.T.astype(BF16)


def _attn_prompt(qt, qft, k, kb, vt):
    b, d_attn, s = qt.shape
    tq = Q_TILE
    nblk = s // tq
    n_pairs = d_attn // LANES
    return pl.pallas_call(
        _attn_prompt_kernel,
        grid=(b, nblk),
        in_specs=[
            pl.BlockSpec((1, d_attn, tq), lambda i, j: (i, 0, j)),
            pl.BlockSpec((1, LANES, tq), lambda i, j: (i, 0, j)),
            pl.BlockSpec((1, s, d_attn), lambda i, j: (i, 0, 0)),
            pl.BlockSpec((1, s, LANES), lambda i, j: (i, 0, 0)),
            pl.BlockSpec((1, d_attn, s), lambda i, j: (i, 0, 0)),
        ],
        out_specs=pl.BlockSpec((1, tq, d_attn), lambda i, j: (i, j, 0)),
        out_shape=jax.ShapeDtypeStruct((b, s, d_attn), BF16),
        scratch_shapes=[
            pltpu.VMEM((n_pairs, s, 2 * LANES), BF16),
            pltpu.VMEM((n_pairs, nblk, LANES + DENOM_ROWS, tq), BF16),
            pltpu.VMEM((n_pairs, 2 * LANES, 2 * tq), BF16),
            pltpu.VMEM((n_pairs, 1, 2 * tq), F32),
            pltpu.VMEM((n_pairs, LANES + DENOM_ROWS, 2 * tq), F32),
            pltpu.VMEM((ATTN_LAG, tq, 2 * tq), F32),
        ],
        compiler_params=pltpu.CompilerParams(dimension_semantics=("arbitrary", "arbitrary"),
                                             vmem_limit_bytes=VMEM_LIMIT_BYTES),
        name="attn_prompt",
    )(qt, qft, k, kb, vt)


def _inproj_sample_kernel(x_ref, sh_ref, sc_ref, n1g_ref, wglu_ref, wqkvf_ref, bf_ref, qg_ref, kg_ref,
                          bdtri_ref, cw_ref, cb_ref, lng_ref, lnb_ref, st_ref,
                          k_ref, v_ref, lf_ref, nst_ref, q_ref, qb_ref, kb_ref, co_ref,
                          u_ref, hist_ref, cof_ref):
    nb, tt, d = x_ref.shape
    r = nb * tt
    d_conv = co_ref.shape[1]
    n_state = CONV_WIDTH - 1

    h = _adaln_rmsnorm(x_ref[...], n1g_ref[...], sc_ref[:, 0], sh_ref[:, 0])
    hb = h.reshape(r, d).astype(BF16)

    n_chunks = d_conv // LANES
    zg = _dot(hb, wglu_ref[...])
    u = zg[:, :d_conv] * jax.nn.sigmoid(zg[:, d_conv:])
    for c in range(n_chunks):
        u_ref[c] = u[:, c * LANES:(c + 1) * LANES]

    q_t, k_t, v_t, lf_t = _qkvf_feature_major(hb, wqkvf_ref[...], qg_ref[...], kg_ref[...], bf_ref[...])
    k_ref[...] = k_t.T
    v_ref[...] = v_t.T
    q_ref[...] = (q_t * (HEAD_DIM ** -0.5)).T.astype(BF16)
    lf = jnp.concatenate([lf_t, jnp.zeros((LANES - N_HEADS, r), F32)], axis=0).T
    lf_ref[...] = lf

    lane = lax.broadcasted_iota(jnp.int32, (r, LANES), 1)
    hi, mid, lo = _split3(lf)
    packed = hi + pltpu.roll(mid, N_HEADS, 1) + pltpu.roll(lo, 2 * N_HEADS, 1)
    c = _dot(bdtri_ref[...], packed.astype(BF16))
    cn = jnp.where(lane < N_HEADS,
                   c + pltpu.roll(c, LANES - N_HEADS, 1) + pltpu.roll(c, LANES - 2 * N_HEADS, 1), 0.0)
    hi, mid, lo = _split3(cn)
    p = hi + pltpu.roll(mid, N_HEADS, 1) + pltpu.roll(lo, 2 * N_HEADS, 1)
    qb_ref[...] = (p + jnp.where((lane >= N_FEAT) & (lane < 2 * N_FEAT), 1.0, 0.0)).astype(BF16)
    kb_ref[...] = (jnp.where(lane < N_FEAT, 1.0, 0.0) - pltpu.roll(p, N_FEAT, 1)).astype(BF16)

    hist_ref[0:n_state] = st_ref[...]
    for t in range(tt):
        for c in range(n_chunks):
            hist_ref[n_state + t, :, c * LANES:(c + 1) * LANES] = u_ref[c, pl.ds(t, nb, stride=tt), :]
    nst_ref[...] = hist_ref[tt:tt + n_state]
    cb, ln_g, ln_b = cb_ref[...], lng_ref[...], lnb_ref[...]
    for t in range(tt):
        y = _conv_ln_swish(lambda k: hist_ref[t + k], lambda k: cw_ref[k:k + 1, :], cb, ln_g, ln_b)
        for c in range(n_chunks):
            cof_ref[c, pl.ds(t, nb, stride=tt), :] = y[:, c * LANES:(c + 1) * LANES]
    co_ref[...] = jnp.concatenate([cof_ref[c] for c in range(n_chunks)], axis=1).astype(BF16)


def _inproj_sample(x, mod4, n1g, wglu, wqkvf_t, bf_col, qg_col, kg_col, bdtri, conv_w, conv_b, ln_g, ln_b,
                   state_t):
    nb, tt, d = x.shape
    r = nb * tt
    d_conv = conv_w.shape[1]
    d_attn = N_HEADS * HEAD_DIM
    n_state = CONV_WIDTH - 1
    mod_spec = lambda j: pl.BlockSpec((nb, 1, 1, d), lambda i: (0, j, 0, 0))
    full = lambda shape: pl.BlockSpec(shape, lambda i: (0,) * len(shape))
    out_shape = (
        jax.ShapeDtypeStruct((r, d_attn), F32),
        jax.ShapeDtypeStruct((r, d_attn), F32),
        jax.ShapeDtypeStruct((r, LANES), F32),
        jax.ShapeDtypeStruct((n_state, nb, d_conv), F32),
        jax.ShapeDtypeStruct((r, d_attn), BF16),
        jax.ShapeDtypeStruct((r, LANES), BF16),
        jax.ShapeDtypeStruct((r, LANES), BF16),
        jax.ShapeDtypeStruct((r, d_conv), BF16),
    )
    args = (x, mod4, mod4, n1g, wglu, wqkvf_t, bf_col, qg_col, kg_col, bdtri, conv_w, conv_b, ln_g, ln_b,
            state_t)
    in_specs = [full(x.shape), mod_spec(0), mod_spec(1)] + [full(a.shape) for a in args[3:]]
    return pl.pallas_call(
        _inproj_sample_kernel,
        grid=(1,),
        in_specs=in_specs,
        out_specs=tuple(full(o.shape) for o in out_shape),
        out_shape=out_shape,
        scratch_shapes=[pltpu.VMEM((d_conv // LANES, r, LANES), F32),
                        pltpu.VMEM((n_state + tt, nb, d_conv), F32),
                        pltpu.VMEM((d_conv // LANES, r, LANES), F32)],
        compiler_params=pltpu.CompilerParams(dimension_semantics=("arbitrary",),
                                             vmem_limit_bytes=VMEM_LIMIT_BYTES),
        name="inproj_sample",
    )(*args)


def _attn_sample_kernel(q_ref, qb_ref, kn_ref, vn_ref, kbn_ref, kt_ref, vt_ref, clft_ref, ltri_ref, o_ref):
    tt = q_ref.shape[1]
    d_attn = q_ref.shape[2]
    p_len = kt_ref.shape[2]
    nblk = p_len // MXU_DIM
    rows = N_HEADS * tt

    parts = jnp.concatenate(_split3(clft_ref[0]), axis=0)
    blocks = [parts[:, i * MXU_DIM:(i + 1) * MXU_DIM] for i in range(nblk)]
    local = _dot(jnp.concatenate(blocks, axis=0).astype(BF16), ltri_ref[...])
    off = jnp.zeros((N_FEAT, 1), F32)
    sufs = [None] * nblk
    for i in reversed(range(nblk)):
        loc = local[i * N_FEAT:(i + 1) * N_FEAT]
        sufs[i] = loc + off
        off = off + loc[:, 0:1] + blocks[i][:, 0:1]
    suf_parts = jnp.concatenate(sufs, axis=1)
    ck_rel = suf_parts[0:8] + suf_parts[8:16] + suf_parts[16:24]
    c_hi, c_mid, c_lo = _split3(ck_rel)
    kbt = jnp.concatenate([jnp.ones((N_FEAT, p_len), F32), c_hi, c_mid, c_lo,
                           jnp.zeros((LANES - 2 * N_FEAT, p_len), F32)], axis=0)
    k_all = jnp.concatenate([kt_ref[0].astype(BF16), kbt.astype(BF16)], axis=0)

    lane_q = lax.broadcasted_iota(jnp.int32, (tt, d_attn), 1)
    lane_b = lax.broadcasted_iota(jnp.int32, (tt, LANES), 1)
    q = q_ref[0].astype(F32)
    qb = qb_ref[0].astype(F32)
    stack = []
    for h in range(N_HEADS):
        qm = jnp.where((lane_q >= h * HEAD_DIM) & (lane_q < (h + 1) * HEAD_DIM), q, 0.0)
        bm = jnp.where((lane_b % N_HEADS) == h, qb, 0.0)
        stack.append(jnp.concatenate([qm, bm], axis=1))
    qs = jnp.concatenate(stack, axis=0).astype(BF16)

    s_c = _dot(qs, k_all)

    pad = jnp.zeros((LANES - tt, d_attn + LANES), F32)
    kn = jnp.concatenate([jnp.concatenate([kn_ref[0], kbn_ref[0].astype(F32)], axis=1), pad], axis=0)
    s_n = _dot_nt(qs, kn.astype(BF16))
    row = lax.broadcasted_iota(jnp.int32, s_n.shape, 0)
    col = lax.broadcasted_iota(jnp.int32, s_n.shape, 1)
    s_n = jnp.where(col <= (row % tt), s_n, NEG_INF)

    m = jnp.maximum(jnp.max(s_c, axis=-1, keepdims=True), jnp.max(s_n, axis=-1, keepdims=True))
    p_c = jnp.exp(s_c - m)
    p_n = jnp.exp(s_n - m)
    l = jnp.sum(p_c, axis=-1, keepdims=True) + jnp.sum(p_n, axis=-1, keepdims=True)
    vn = jnp.concatenate([vn_ref[0], jnp.zeros((LANES - tt, d_attn), F32)], axis=0).astype(BF16)
    o = _dot_nt(p_c.astype(BF16), vt_ref[0].astype(BF16)) + _dot(p_n.astype(BF16), vn)
    o = o / l

    out = jnp.zeros((tt, d_attn), F32)
    for h in range(N_HEADS):
        out = jnp.where((lane_q >= h * HEAD_DIM) & (lane_q < (h + 1) * HEAD_DIM), o[h * tt:(h + 1) * tt], out)
    o_ref[0] = out.astype(BF16)


def _attn_sample(q, qb, kn, vn, kbn, kt, vt, clft, ltri):
    nb, tt, d_attn = q.shape
    p_len = kt.shape[2]
    per_req = lambda shape: pl.BlockSpec((1,) + shape, lambda i: (i, 0, 0))
    return pl.pallas_call(
        _attn_sample_kernel,
        grid=(nb,),
        in_specs=[per_req((tt, d_attn)), per_req((tt, LANES)), per_req((tt, d_attn)), per_req((tt, d_attn)),
                  per_req((tt, LANES)), per_req((d_attn, p_len)), per_req((d_attn, p_len)),
                  per_req((N_HEADS, p_len)), _const_spec(ltri.shape)],
        out_specs=per_req((tt, d_attn)),
        out_shape=jax.ShapeDtypeStruct((nb, tt, d_attn), BF16),
        compiler_params=pltpu.CompilerParams(dimension_semantics=("arbitrary",),
                                             vmem_limit_bytes=VMEM_LIMIT_BYTES),
        name="attn_sample",
    )(q, qb, kn, vn, kbn, kt, vt, clft, ltri)


def _outproj_ffn_body(x_ref, co, at, g1_ref, sh2_ref, sc2_ref, g2_ref, n2g_ref,
                      wo_ref, wg_ref, wu_ref, wd_ref, y_ref, acc_ref, vpu_slot=None):
    nb, tt, d = x_ref.shape
    r = nb * tt
    d_conv = co.shape[1]
    mix = _dot(co, wo_ref[0:d_conv, :]) + _dot(at, wo_ref[d_conv:, :])
    x1 = x_ref[...] + g1_ref[:, 0] * mix.reshape(nb, tt, d)
    hb = _adaln_rmsnorm(x1, n2g_ref[...], sc2_ref[:, 0], sh2_ref[:, 0]).reshape(r, d).astype(BF16)
    lead = x1.reshape(r, d)[0:SUBLANES, 0:d_conv]
    n_ff = wg_ref.shape[1] // FF_CHUNK
    chunk = lambda c: slice(c * FF_CHUNK, (c + 1) * FF_CHUNK)
    for c in range(n_ff):
        zero = vpu_slot(c, lead) if vpu_slot is not None else None
        g, u = _dot(hb, wg_ref[:, chunk(c)]), _dot(hb, wu_ref[:, chunk(c)])
        if zero is not None:
            zero = jnp.concatenate([zero] * (FF_CHUNK // LANES), axis=1)
            g = jnp.concatenate([g[:SUBLANES] + zero, g[SUBLANES:]], axis=0)
        a = ((g * jax.nn.sigmoid(g)) * u).astype(BF16)
        part = _dot(a, wd_ref[chunk(c), :])
        if c == 0:
            acc_ref[...] = part
        else:
            acc_ref[...] += part
        lead = part[0:SUBLANES, 0:d_conv]
    y_ref[...] = x1 + g2_ref[:, 0] * acc_ref[...].reshape(nb, tt, d)


def _outproj_ffn_kernel(x_ref, co_ref, at_ref, g1_ref, sh2_ref, sc2_ref, g2_ref, n2g_ref,
                        wo_ref, wg_ref, wu_ref, wd_ref, y_ref, acc_ref):
    nb, tt, _ = x_ref.shape
    co = co_ref[...].reshape(nb * tt, co_ref.shape[2])
    at = at_ref[...].reshape(nb * tt, at_ref.shape[2])
    _outproj_ffn_body(x_ref, co, at, g1_ref, sh2_ref, sc2_ref, g2_ref, n2g_ref,
                      wo_ref, wg_ref, wu_ref, wd_ref, y_ref, acc_ref)


def _exact_zero(v):
    return jnp.minimum(jnp.abs(v), 0.0)


def _outproj_ffn_conv_kernel(x_ref, at_ref, un_ref, u0_ref, g1_ref, sh2_ref, sc2_ref, g2_ref, n2g_ref,
                             wo_ref, wg_ref, wu_ref, wd_ref, cw_ref, cb_ref, lng_ref, lnb_ref,
                             y_ref, acc_ref, shift_ref, co_ref, wb_ref, *, tiles_per_seq):
    g = pl.program_id(0)
    tm = x_ref.shape[1]
    d_conv = co_ref.shape[1]
    span = tm + HIST - SUBLANES
    n_conv = tm // CONV_ROWS
    n_ff = wg_ref.shape[1] // FF_CHUNK
    cb, ln_g, ln_b = cb_ref[...], lng_ref[...], lnb_ref[...]

    def stage(u, hist):
        shift_ref[0, 0:HIST, :] = hist
        shift_ref[0, HIST:HIST + tm, :] = u
        for r in range(1, SUBLANES):
            shift_ref[r, 0:span, :] = shift_ref[0, r:r + span, :]

    def conv_chunk(c, bias):
        def tap(k):
            off = k + HIST - (CONV_WIDTH - 1)
            row = c * CONV_ROWS + off - off % SUBLANES
            return shift_ref[off % SUBLANES, row:row + CONV_ROWS, :]
        weight = lambda k: jnp.concatenate([wb_ref[k]] * (CONV_ROWS // SUBLANES), axis=0)
        y = _conv_ln_swish(tap, weight, bias, ln_g, ln_b)
        co_ref[c * CONV_ROWS:(c + 1) * CONV_ROWS, :] = y.astype(BF16)
        return y

    @pl.when(g == 0)
    def _():
        for k in range(CONV_WIDTH):
            wb_ref[k] = jnp.broadcast_to(cw_ref[k:k + 1, :], (SUBLANES, d_conv))
        stage(u0_ref[0], jnp.zeros((HIST, d_conv), F32))
        for c in range(n_conv):
            conv_chunk(c, cb)

    def conv_slot(c, lead):
        if c == 0:
            tail = shift_ref[0, tm:tm + HIST, :]
            starts_seq = (g + 1) % tiles_per_seq == 0
            stage(un_ref[0], jnp.where(starts_seq, 0.0, tail))
        bias = jnp.concatenate([cb + _exact_zero(lead)] * (CONV_ROWS // SUBLANES), axis=0)
        zero = jnp.zeros((SUBLANES, LANES), F32)
        for i in range(c * n_conv // n_ff, (c + 1) * n_conv // n_ff):
            zero = zero + _exact_zero(conv_chunk(i, bias)[0:SUBLANES, 0:LANES])
        return zero

    co = co_ref[...]
    _outproj_ffn_body(x_ref, co, at_ref[0], g1_ref, sh2_ref, sc2_ref, g2_ref, n2g_ref,
                      wo_ref, wg_ref, wu_ref, wd_ref, y_ref, acc_ref, vpu_slot=conv_slot)


def _outproj_ffn_conv(x, at, u, mod4, n2g, wo, wg, wu, wd, conv_w, conv_b, ln_g, ln_b):
    b, s, d = x.shape
    tm = ROW_TILE
    d_conv = u.shape[2]
    d_attn = at.shape[2]
    tps = s // tm
    n_tiles = b * tps
    tile = lambda g: (g // tps, g % tps, 0)
    next_tile = lambda g: tile(jnp.minimum(g + 1, n_tiles - 1))
    mod_spec = lambda j: pl.BlockSpec((1, 1, 1, d), lambda g: (g // tps, j, 0, 0))
    consts = (n2g, wo, wg, wu, wd, conv_w, conv_b, ln_g, ln_b)
    return pl.pallas_call(
        functools.partial(_outproj_ffn_conv_kernel, tiles_per_seq=tps),
        grid=(n_tiles,),
        in_specs=[pl.BlockSpec((1, tm, d), tile), pl.BlockSpec((1, tm, d_attn), tile),
                  pl.BlockSpec((1, tm, d_conv), next_tile),
                  pl.BlockSpec((1, tm, d_conv), lambda g: (0, 0, 0), pipeline_mode=pl.Buffered(1)),
                  mod_spec(2), mod_spec(3), mod_spec(4), mod_spec(5)] + [_const_spec(c.shape) for c in consts],
        out_specs=pl.BlockSpec((1, tm, d), tile),
        out_shape=jax.ShapeDtypeStruct((b, s, d), F32),
        scratch_shapes=[pltpu.VMEM((tm, d), F32),
                        pltpu.VMEM((SUBLANES, tm + HIST, d_conv), F32),
                        pltpu.VMEM((tm, d_conv), BF16),
                        pltpu.VMEM((CONV_WIDTH, SUBLANES, d_conv), F32)],
        compiler_params=pltpu.CompilerParams(dimension_semantics=("arbitrary",),
                                             vmem_limit_bytes=VMEM_LIMIT_BYTES),
        name="outproj_ffn_conv",
    )(x, at, u, u, mod4, mod4, mod4, mod4, *consts)


def _outproj_ffn(x, co, at, mod4, n2g, wo, wg, wu, wd, bb, tt):
    nb, s, d = x.shape
    d_conv = co.shape[2]
    d_attn = at.shape[2]
    mod_spec = lambda j: pl.BlockSpec((bb, 1, 1, d), lambda i, t: (i, j, 0, 0))
    row_spec = lambda w: pl.BlockSpec((bb, tt, w), lambda i, t: (i, t, 0))
    return pl.pallas_call(
        _outproj_ffn_kernel,
        grid=(nb // bb, s // tt),
        in_specs=[row_spec(d), row_spec(d_conv), row_spec(d_attn),
                  mod_spec(2), mod_spec(3), mod_spec(4), mod_spec(5),
                  _const_spec(n2g.shape), _const_spec(wo.shape), _const_spec(wg.shape),
                  _const_spec(wu.shape), _const_spec(wd.shape)],
        out_specs=row_spec(d),
        out_shape=jax.ShapeDtypeStruct((nb, s, d), F32),
        scratch_shapes=[pltpu.VMEM((bb * tt, d), F32)],
        compiler_params=pltpu.CompilerParams(dimension_semantics=("arbitrary", "arbitrary"),
                                             vmem_limit_bytes=VMEM_LIMIT_BYTES),
        name="outproj_ffn",
    )(x, co, at, mod4, mod4, mod4, mod4, n2g, wo, wg, wu, wd)


def _tri(n, rel):
    i = lax.broadcasted_iota(jnp.int32, (n, n), 0)
    j = lax.broadcasted_iota(jnp.int32, (n, n), 1)
    return rel(i, j).astype(BF16)


def _layer(xp, xs, cache_k, cache_v, cache_logf, state_conv, c_all, w_ada, b_ada, norm1_g, w_in, b_f,
           q_norm_g, k_norm_g, conv_w, conv_b, conv_ln_g, conv_ln_b, w_out, norm2_g, w_gate, w_up, w_down):
    b, s, d = xp.shape
    nb, tt, _ = xs.shape
    d_conv = conv_w.shape[1]
    d_attn = N_HEADS * HEAD_DIM
    d_ff = w_gate.shape[1]
    p_len = cache_k.shape[1]

    mod4 = _modulation(c_all, w_ada, b_ada).reshape(b + nb, 6, 1, d)
    mod_p, mod_s = mod4[:b], mod4[b:]

    wglu = w_in[:, :2 * d_conv].astype(BF16)
    wqkvf_t = jnp.pad(w_in[:, 2 * d_conv:].T, ((0, LANES - N_HEADS), (0, 0))).astype(BF16)
    bf_col = b_f.reshape(N_HEADS, 1)
    qg_col = q_norm_g.reshape(d_attn, 1)
    kg_col = k_norm_g.reshape(d_attn, 1)
    row = lambda v: v.reshape(1, -1)
    n1g, n2g, cb, ln_g, ln_b = row(norm1_g), row(norm2_g), row(conv_b), row(conv_ln_g), row(conv_ln_b)
    wo = w_out.astype(BF16)
    assert d_ff % FF_CHUNK == 0
    wg, wu, wd = w_gate.astype(BF16), w_up.astype(BF16), w_down.astype(BF16)
    utri = _tri(MXU_DIM, lambda i, j: i <= j)
    ltri = _tri(MXU_DIM, lambda i, j: i > j)
    bdtri = _tri(nb * tt, lambda i, j: (i // tt == j // tt) & (j <= i))

    kt, vt, lft, cst, qt, qft, k, kb, u = _inproj_prompt(
        xp, mod_p, n1g, wglu, wqkvf_t, bf_col, qg_col, kg_col, utri)
    at = _attn_prompt(qt, qft, k, kb, vt)
    yp = _outproj_ffn_conv(xp, at, u, mod_p, n2g, wo, wg, wu, wd, conv_w, cb, ln_g, ln_b)
    k_p = kt.reshape(b, N_HEADS, HEAD_DIM, s).transpose(0, 3, 1, 2)
    v_p = vt.reshape(b, N_HEADS, HEAD_DIM, s).transpose(0, 3, 1, 2)
    lf_p = lft.transpose(0, 2, 1)

    state_t = state_conv.transpose(1, 0, 2)
    k_s, v_s, lf_s, nst, q_s, qb_s, kb_s, co_s = _inproj_sample(
        xs, mod_s, n1g, wglu, wqkvf_t, bf_col, qg_col, kg_col, bdtri, conv_w, cb, ln_g, ln_b, state_t)
    r3 = lambda a: a.reshape(nb, tt, a.shape[-1])
    ckt = cache_k.transpose(0, 2, 3, 1).reshape(nb, d_attn, p_len)
    cvt = cache_v.transpose(0, 2, 3, 1).reshape(nb, d_attn, p_len)
    clft = cache_logf.transpose(0, 2, 1)
    at_s = _attn_sample(r3(q_s), r3(qb_s), r3(k_s), r3(v_s), r3(kb_s), ckt, cvt, clft, ltri)
    ys = _outproj_ffn(xs, r3(co_s), at_s, mod_s, n2g, wo, wg, wu, wd, nb, tt)
    k_sn = k_s.reshape(nb, tt, N_HEADS, HEAD_DIM)
    v_sn = v_s.reshape(nb, tt, N_HEADS, HEAD_DIM)
    lf_sn = lf_s[:, :N_HEADS].reshape(nb, tt, N_HEADS)
    return yp, ys, (k_p, v_p, lf_p, cst), (k_sn, v_sn, lf_sn, nst.transpose(1, 0, 2))


def kernel(x_prompt, x_sample, cache_k, cache_v, cache_logf, state_conv, c_prompt, c_sample, w_ada, b_ada,
           norm1_g, w_in, b_f, q_norm_g, k_norm_g, conv_w, conv_b, conv_ln_g, conv_ln_b, w_out, norm2_g,
           w_gate, w_up, w_down):
    depth = w_ada.shape[0]
    c_all = jnp.concatenate([c_prompt, c_sample], axis=0)
    yp, ys = x_prompt, x_sample
    st_p, st_s = [], []
    for l in range(depth):
        yp, ys, sp, ss = _layer(
            yp, ys, cache_k[l], cache_v[l], cache_logf[l], state_conv[l], c_all, w_ada[l], b_ada[l],
            norm1_g[l], w_in[l], b_f[l], q_norm_g[l], k_norm_g[l], conv_w[l], conv_b[l], conv_ln_g[l],
            conv_ln_b[l], w_out[l], norm2_g[l], w_gate[l], w_up[l], w_down[l])
        st_p.append(sp)
        st_s.append(ss)
    stack = lambda xs: xs[0][None] if len(xs) == 1 else jnp.stack(xs)
    outs_p = [stack([s[i] for s in st_p]) for i in range(4)]
    outs_s = [stack([s[i] for s in st_s]) for i in range(4)]
    return (yp, ys, *outs_p, *outs_s)
```

```python
import functools

import jax
import jax.numpy as jnp
from jax import lax
from jax.experimental import pallas as pl
from jax.experimental.pallas import tpu as pltpu

F32 = jnp.float32
BF16 = jnp.bfloat16

N_HEADS = 8
HEAD_DIM = 64
CONV_WIDTH = 31
EPS = 1e-6
NEG_INF = -1e30

LANES = 128
SUBLANES = 8
MXU_DIM = 256
VMEM_LIMIT_BYTES = 56 * 1024 * 1024

N_PARTS = 3
N_FEAT = N_PARTS * N_HEADS
HIST = 32
CONV_ROWS = 32

ROW_TILE = 512
Q_TILE = 256
ATTN_LAG = 2
DENOM_ROWS = 16
LOG2E = 1.4426950408889634
FF_CHUNK = 256
SAMPLE_REQS_PER_STEP = 2


def _dot(a, b):
    return jnp.dot(a, b, preferred_element_type=F32)


def _dot_nt(a, b):
    return lax.dot_general(a, b, (((1,), (1,)), ((), ())), preferred_element_type=F32)


def _split3(x):
    hi = x.astype(BF16).astype(F32)
    r = x - hi
    mid = r.astype(BF16).astype(F32)
    lo = (r - mid).astype(BF16).astype(F32)
    return hi, mid, lo


def _log_sigmoid(x):
    return jnp.minimum(x, 0.0) - jnp.log1p(jnp.exp(-jnp.abs(x)))


def _adaln_rmsnorm(x, g, scale, shift):
    y = x * lax.rsqrt(jnp.mean(x * x, axis=-1, keepdims=True) + EPS)
    return (y * g) * (1.0 + scale) + shift


def _const_spec(shape):
    n = len(shape)
    return pl.BlockSpec(shape, lambda *_: (0,) * n, pipeline_mode=pl.Buffered(1))


def _modulation_kernel(c_ref, w_ref, b_ref, o_ref):
    c = c_ref[...]
    a = (c * jax.nn.sigmoid(c)).astype(BF16)
    o_ref[...] = _dot(a, w_ref[...].astype(BF16)) + b_ref[...]


def _modulation(c, w_ada, b_ada):
    nb, d = c.shape
    n = w_ada.shape[1]
    tn = 768
    return pl.pallas_call(
        _modulation_kernel,
        grid=(n // tn,),
        in_specs=[pl.BlockSpec((nb, d), lambda j: (0, 0)),
                  pl.BlockSpec((d, tn), lambda j: (0, j)),
                  pl.BlockSpec((1, tn), lambda j: (0, j))],
        out_specs=pl.BlockSpec((nb, tn), lambda j: (0, j)),
        out_shape=jax.ShapeDtypeStruct((nb, n), F32),
        compiler_params=pltpu.CompilerParams(dimension_semantics=("arbitrary",),
                                             vmem_limit_bytes=VMEM_LIMIT_BYTES),
        name="modulation",
    )(c, w_ada, b_ada.reshape(1, n))


def _qkvf_feature_major(hb, wqkvf_t, qg_col, kg_col, bf_col):
    d_attn = N_HEADS * HEAD_DIM
    zt = _dot_nt(wqkvf_t, hb)
    r = zt.shape[1]

    def head_rms(z, g_col):
        z3 = z.reshape(N_HEADS, HEAD_DIM, r)
        ms = jnp.mean(z3 * z3, axis=1, keepdims=True)
        return (z3 * lax.rsqrt(ms + EPS)).reshape(d_attn, r) * g_col

    q_t = head_rms(zt[0:d_attn], qg_col)
    k_t = head_rms(zt[d_attn:2 * d_attn], kg_col)
    v_t = zt[2 * d_attn:3 * d_attn]
    lf_t = _log_sigmoid(zt[3 * d_attn:3 * d_attn + N_HEADS] + bf_col)
    return q_t, k_t, v_t, lf_t


def _conv_ln_swish(load_rows, weight, cb, ln_g, ln_b):
    acc = cb + weight(0) * load_rows(0)
    for k in range(1, CONV_WIDTH):
        acc = acc + weight(k) * load_rows(k)
    mu = jnp.mean(acc, axis=-1, keepdims=True)
    cen = acc - mu
    var = jnp.mean(cen * cen, axis=-1, keepdims=True)
    y = cen * lax.rsqrt(var + EPS) * ln_g + ln_b
    return y * jax.nn.sigmoid(y)


def _inproj_prompt_kernel(x_ref, sh_ref, sc_ref, n1g_ref, wglu_ref, wqkvf_ref, bf_ref, qg_ref, kg_ref,
                          utri_ref,
                          kt_ref, vt_ref, lft_ref, cst_ref, qt_ref, qft_ref, k_ref, kb_ref, u_ref,
                          carry_ref):
    t = pl.program_id(1)
    tm = x_ref.shape[1]
    d_conv = u_ref.shape[2]

    @pl.when(t == 0)
    def _():
        carry_ref[...] = jnp.zeros(carry_ref.shape, F32)

    hb = _adaln_rmsnorm(x_ref[0], n1g_ref[...], sc_ref[0, 0], sh_ref[0, 0]).astype(BF16)

    q_t, k_t, v_t, lf_t = _qkvf_feature_major(hb, wqkvf_ref[...], qg_ref[...], kg_ref[...], bf_ref[...])
    kt_ref[0] = k_t
    vt_ref[0] = v_t
    lft_ref[0] = lf_t

    zg = _dot(hb, wglu_ref[...])
    u_ref[0] = zg[:, :d_conv] * jax.nn.sigmoid(zg[:, d_conv:])

    nblk = tm // MXU_DIM
    parts = jnp.concatenate(_split3(lf_t), axis=0)
    stacked = jnp.concatenate([parts[:, i * MXU_DIM:(i + 1) * MXU_DIM] for i in range(nblk)], axis=0)
    local = _dot(stacked.astype(BF16), utri_ref[...])
    carry = carry_ref[:, 0:1]
    cums = []
    for i in range(nblk):
        loc = local[i * N_FEAT:(i + 1) * N_FEAT]
        cums.append(loc + carry)
        carry = carry + loc[:, MXU_DIM - 1:MXU_DIM]
    carry_ref[...] = jnp.broadcast_to(carry, carry_ref.shape)
    cum_parts = jnp.concatenate(cums, axis=1)
    cum_t = (cum_parts[0:8] + cum_parts[8:16] + cum_parts[16:24]) * LOG2E

    c_hi, c_mid, c_lo = _split3(cum_t)
    ones = jnp.ones((N_FEAT, tm), F32)
    zeros = jnp.zeros((LANES - 2 * N_FEAT, tm), F32)
    kfeat_t = jnp.concatenate([ones, -c_hi, -c_mid, -c_lo, zeros], axis=0)
    qft_ref[0] = jnp.concatenate([c_hi, c_mid, c_lo, ones, zeros], axis=0).astype(BF16)
    qt_ref[0] = (q_t * (HEAD_DIM ** -0.5 * LOG2E)).astype(BF16)
    k_ref[0] = k_t.T.astype(BF16)
    kb_ref[0] = kfeat_t.T.astype(BF16)

    @pl.when(t == pl.num_programs(1) - 1)
    def _():
        cst_ref[0] = u_ref[0, tm - (CONV_WIDTH - 1):tm, :]


def _inproj_prompt(x, mod4, n1g, wglu, wqkvf_t, bf_col, qg_col, kg_col, utri):
    b, s, d = x.shape
    tm = ROW_TILE
    d_conv = wglu.shape[1] // 2
    d_attn = N_HEADS * HEAD_DIM
    mod_spec = lambda j: pl.BlockSpec((1, 1, 1, d), lambda i, t: (i, j, 0, 0))
    out_shape = (
        jax.ShapeDtypeStruct((b, d_attn, s), F32),
        jax.ShapeDtypeStruct((b, d_attn, s), F32),
        jax.ShapeDtypeStruct((b, N_HEADS, s), F32),
        jax.ShapeDtypeStruct((b, CONV_WIDTH - 1, d_conv), F32),
        jax.ShapeDtypeStruct((b, d_attn, s), BF16),
        jax.ShapeDtypeStruct((b, LANES, s), BF16),
        jax.ShapeDtypeStruct((b, s, d_attn), BF16),
        jax.ShapeDtypeStruct((b, s, LANES), BF16),
        jax.ShapeDtypeStruct((b, s, d_conv), F32),
    )
    out_specs = (
        pl.BlockSpec((1, d_attn, tm), lambda i, t: (i, 0, t)),
        pl.BlockSpec((1, d_attn, tm), lambda i, t: (i, 0, t)),
        pl.BlockSpec((1, N_HEADS, tm), lambda i, t: (i, 0, t)),
        pl.BlockSpec((1, CONV_WIDTH - 1, d_conv), lambda i, t: (i, 0, 0)),
        pl.BlockSpec((1, d_attn, tm), lambda i, t: (i, 0, t)),
        pl.BlockSpec((1, LANES, tm), lambda i, t: (i, 0, t)),
        pl.BlockSpec((1, tm, d_attn), lambda i, t: (i, t, 0)),
        pl.BlockSpec((1, tm, LANES), lambda i, t: (i, t, 0)),
        pl.BlockSpec((1, tm, d_conv), lambda i, t: (i, t, 0)),
    )
    in_specs = [
        pl.BlockSpec((1, tm, d), lambda i, t: (i, t, 0)),
        mod_spec(0), mod_spec(1),
        _const_spec(n1g.shape), _const_spec(wglu.shape), _const_spec(wqkvf_t.shape),
        _const_spec(bf_col.shape), _const_spec(qg_col.shape), _const_spec(kg_col.shape),
        _const_spec(utri.shape),
    ]
    return pl.pallas_call(
        _inproj_prompt_kernel,
        grid=(b, s // tm),
        in_specs=in_specs,
        out_specs=out_specs,
        out_shape=out_shape,
        scratch_shapes=[pltpu.VMEM((N_FEAT, LANES), F32)],
        compiler_params=pltpu.CompilerParams(dimension_semantics=("arbitrary", "arbitrary"),
                                             vmem_limit_bytes=VMEM_LIMIT_BYTES),
        name="inproj_prompt",
    )(x, mod4, mod4, n1g, wglu, wqkvf_t, bf_col, qg_col, kg_col, utri)


def _attn_prompt_kernel(qt_ref, qft_ref, k_ref, kb_ref, vt_ref, o_ref,
                        kp_ref, vp_ref, q2_ref, m_ref, acc_ref, sc_ref):
    qi = pl.program_id(1)
    tq = qt_ref.shape[2]
    n_pairs, nblk = vp_ref.shape[0], vp_ref.shape[1]

    @pl.when(qi == 0)
    def _():
        ones = jnp.ones((vp_ref.shape[2] - LANES, tq), BF16)
        for p in range(n_pairs):
            kp_ref[p, :, 0:LANES] = k_ref[0, :, p * LANES:(p + 1) * LANES]
            kp_ref[p, :, LANES:2 * LANES] = kb_ref[0]
            for i in range(nblk):
                vp_ref[p, i, 0:LANES, :] = vt_ref[0, p * LANES:(p + 1) * LANES, i * tq:(i + 1) * tq].astype(BF16)
                vp_ref[p, i, LANES:, :] = ones

    frow = lax.broadcasted_iota(jnp.int32, (LANES, tq), 0)
    feat = qft_ref[0].astype(F32)
    for p in range(n_pairs):
        qp = qt_ref[0, p * LANES:(p + 1) * LANES, :].astype(F32)
        cols = []
        for j in range(2):
            qm = jnp.where((frow >= j * HEAD_DIM) & (frow < (j + 1) * HEAD_DIM), qp, 0.0)
            fm = jnp.where((frow % N_HEADS) == 2 * p + j, feat, 0.0)
            cols.append(jnp.concatenate([qm, fm], axis=0))
        q2_ref[p] = jnp.concatenate(cols, axis=1).astype(BF16)

    m_ref[...] = jnp.full(m_ref.shape, NEG_INF, F32)
    acc_ref[...] = jnp.zeros(acc_ref.shape, F32)

    def scores(p, j):
        start = pl.multiple_of(j * tq, tq)
        return _dot(kp_ref[p, pl.ds(start, tq), :], q2_ref[p])

    def accumulate(p, j, masked, s):
        if masked:
            key = lax.broadcasted_iota(jnp.int32, s.shape, 0)
            qry = lax.broadcasted_iota(jnp.int32, s.shape, 1)
            qry = jnp.where(qry >= tq, qry - tq, qry)
            s = jnp.where(key <= qry, s, NEG_INF)
        m_old = m_ref[p]
        m_new = jnp.maximum(m_old, jnp.max(s, axis=0, keepdims=True))
        pr = jnp.exp2(s - m_new).astype(BF16)
        acc_ref[p] = jnp.exp2(m_old - m_new) * acc_ref[p] + _dot(vp_ref[p, j], pr)
        m_ref[p] = m_new

    def run_units(units, next_block):
        pending = {}
        for i in range(len(units) + ATTN_LAG):
            if i < ATTN_LAG:
                pending[i] = sc_ref[i]
            elif i < len(units):
                p, j, _ = units[i]
                pending[i] = scores(p, j)
            elif next_block is not None:
                sc_ref[i - len(units)] = scores(i - len(units), next_block)
            if i >= ATTN_LAG:
                p, j, masked = units[i - ATTN_LAG]
                accumulate(p, j, masked, pending.pop(i - ATTN_LAG))

    def block(j, masked):
        return [(p, j, masked) for p in range(n_pairs)]

    for p in range(ATTN_LAG):
        sc_ref[p] = scores(p, 0)

    def body(i, carry):
        run_units(block(2 * i, False) + block(2 * i + 1, False), 2 * i + 2)
        return carry

    lax.fori_loop(0, qi // 2, body, 0)

    @pl.when(qi % 2 == 1)
    def _():
        run_units(block(qi - 1, False) + block(qi, True), None)

    @pl.when(qi % 2 == 0)
    def _():
        run_units(block(qi, True), None)

    for p in range(n_pairs):
        acc = acc_ref[p]
        out_t = acc[0:LANES] / acc[LANES:LANES + 1]
        pair_t = jnp.where(frow < HEAD_DIM, out_t[:, :tq], out_t[:, tq:])
        o_ref[0, :, p * LANES:(p + 1) * LANES] = pair_t.T.astype(BF16)


def _attn_prompt(qt, qft, k, kb, vt):
    b, d_attn, s = qt.shape
    tq = Q_TILE
    nblk = s // tq
    n_pairs = d_attn // LANES
    return pl.pallas_call(
        _attn_prompt_kernel,
        grid=(b, nblk),
        in_specs=[
            pl.BlockSpec((1, d_attn, tq), lambda i, j: (i, 0, j)),
            pl.BlockSpec((1, LANES, tq), lambda i, j: (i, 0, j)),
            pl.BlockSpec((1, s, d_attn), lambda i, j: (i, 0, 0)),
            pl.BlockSpec((1, s, LANES), lambda i, j: (i, 0, 0)),
            pl.BlockSpec((1, d_attn, s), lambda i, j: (i, 0, 0)),
        ],
        out_specs=pl.BlockSpec((1, tq, d_attn), lambda i, j: (i, j, 0)),
        out_shape=jax.ShapeDtypeStruct((b, s, d_attn), BF16),
        scratch_shapes=[
            pltpu.VMEM((n_pairs, s, 2 * LANES), BF16),
            pltpu.VMEM((n_pairs, nblk, LANES + DENOM_ROWS, tq), BF16),
            pltpu.VMEM((n_pairs, 2 * LANES, 2 * tq), BF16),
            pltpu.VMEM((n_pairs, 1, 2 * tq), F32),
            pltpu.VMEM((n_pairs, LANES + DENOM_ROWS, 2 * tq), F32),
            pltpu.VMEM((ATTN_LAG, tq, 2 * tq), F32),
        ],
        compiler_params=pltpu.CompilerParams(dimension_semantics=("arbitrary", "arbitrary"),
                                             vmem_limit_bytes=VMEM_LIMIT_BYTES),
        name="attn_prompt",
    )(qt, qft, k, kb, vt)


def _inproj_sample_kernel(x_ref, sh_ref, sc_ref, n1g_ref, wglu_ref, wqkvf_ref, bf_ref, qg_ref, kg_ref,
                          bdtri_ref, cw_ref, cb_ref, lng_ref, lnb_ref, st_ref,
                          k_ref, v_ref, lf_ref, nst_ref, q_ref, qb_ref, kb_ref, co_ref,
                          u_ref, hist_ref, cof_ref):
    nb, tt, d = x_ref.shape
    r = nb * tt
    d_conv = co_ref.shape[1]
    n_state = CONV_WIDTH - 1

    h = _adaln_rmsnorm(x_ref[...], n1g_ref[...], sc_ref[:, 0], sh_ref[:, 0])
    hb = h.reshape(r, d).astype(BF16)

    n_chunks = d_conv // LANES
    zg = _dot(hb, wglu_ref[...])
    u = zg[:, :d_conv] * jax.nn.sigmoid(zg[:, d_conv:])
    for c in range(n_chunks):
        u_ref[c] = u[:, c * LANES:(c + 1) * LANES]

    q_t, k_t, v_t, lf_t = _qkvf_feature_major(hb, wqkvf_ref[...], qg_ref[...], kg_ref[...], bf_ref[...])
    k_ref[...] = k_t.T
    v_ref[...] = v_t.T
    q_ref[...] = (q_t * (HEAD_DIM ** -0.5)).T.astype(BF16)
    lf = jnp.concatenate([lf_t, jnp.zeros((LANES - N_HEADS, r), F32)], axis=0).T
    lf_ref[...] = lf

    lane = lax.broadcasted_iota(jnp.int32, (r, LANES), 1)
    hi, mid, lo = _split3(lf)
    packed = hi + pltpu.roll(mid, N_HEADS, 1) + pltpu.roll(lo, 2 * N_HEADS, 1)
    c = _dot(bdtri_ref[...], packed.astype(BF16))
    cn = jnp.where(lane < N_HEADS,
                   c + pltpu.roll(c, LANES - N_HEADS, 1) + pltpu.roll(c, LANES - 2 * N_HEADS, 1), 0.0)
    hi, mid, lo = _split3(cn)
    p = hi + pltpu.roll(mid, N_HEADS, 1) + pltpu.roll(lo, 2 * N_HEADS, 1)
    qb_ref[...] = (p + jnp.where((lane >= N_FEAT) & (lane < 2 * N_FEAT), 1.0, 0.0)).astype(BF16)
    kb_ref[...] = (jnp.where(lane < N_FEAT, 1.0, 0.0) - pltpu.roll(p, N_FEAT, 1)).astype(BF16)

    hist_ref[0:n_state] = st_ref[...]
    for t in range(tt):
        for c in range(n_chunks):
            hist_ref[n_state + t, :, c * LANES:(c + 1) * LANES] = u_ref[c, pl.ds(t, nb, stride=tt), :]
    nst_ref[...] = hist_ref[tt:tt + n_state]
    cb, ln_g, ln_b = cb_ref[...], lng_ref[...], lnb_ref[...]
    for t in range(tt):
        y = _conv_ln_swish(lambda k: hist_ref[t + k], lambda k: cw_ref[k:k + 1, :], cb, ln_g, ln_b)
        for c in range(n_chunks):
            cof_ref[c, pl.ds(t, nb, stride=tt), :] = y[:, c * LANES:(c + 1) * LANES]
    co_ref[...] = jnp.concatenate([cof_ref[c] for c in range(n_chunks)], axis=1).astype(BF16)


def _inproj_sample(x, mod4, n1g, wglu, wqkvf_t, bf_col, qg_col, kg_col, bdtri, conv_w, conv_b, ln_g, ln_b,
                   state_t):
    nb, tt, d = x.shape
    r = nb * tt
    d_conv = conv_w.shape[1]
    d_attn = N_HEADS * HEAD_DIM
    n_state = CONV_WIDTH - 1
    mod_spec = lambda j: pl.BlockSpec((nb, 1, 1, d), lambda i: (0, j, 0, 0))
    full = lambda shape: pl.BlockSpec(shape, lambda i: (0,) * len(shape))
    out_shape = (
        jax.ShapeDtypeStruct((r, d_attn), F32),
        jax.ShapeDtypeStruct((r, d_attn), F32),
        jax.ShapeDtypeStruct((r, LANES), F32),
        jax.ShapeDtypeStruct((n_state, nb, d_conv), F32),
        jax.ShapeDtypeStruct((r, d_attn), BF16),
        jax.ShapeDtypeStruct((r, LANES), BF16),
        jax.ShapeDtypeStruct((r, LANES), BF16),
        jax.ShapeDtypeStruct((r, d_conv), BF16),
    )
    args = (x, mod4, mod4, n1g, wglu, wqkvf_t, bf_col, qg_col, kg_col, bdtri, conv_w, conv_b, ln_g, ln_b,
            state_t)
    in_specs = [full(x.shape), mod_spec(0), mod_spec(1)] + [full(a.shape) for a in args[3:]]
    return pl.pallas_call(
        _inproj_sample_kernel,
        grid=(1,),
        in_specs=in_specs,
        out_specs=tuple(full(o.shape) for o in out_shape),
        out_shape=out_shape,
        scratch_shapes=[pltpu.VMEM((d_conv // LANES, r, LANES), F32),
                        pltpu.VMEM((n_state + tt, nb, d_conv), F32),
                        pltpu.VMEM((d_conv // LANES, r, LANES), F32)],
        compiler_params=pltpu.CompilerParams(dimension_semantics=("arbitrary",),
                                             vmem_limit_bytes=VMEM_LIMIT_BYTES),
        name="inproj_sample",
    )(*args)


def _attn_sample_kernel(*refs):
    for i in range(refs[0].shape[0]):
        _attn_sample_request(i, *refs)


def _attn_sample_request(i, q_ref, qb_ref, kn_ref, vn_ref, kbn_ref, kt_ref, vt_ref, clft_ref, ltri_ref, o_ref):
    tt = q_ref.shape[1]
    d_attn = q_ref.shape[2]
    p_len = kt_ref.shape[2]
    nblk = p_len // MXU_DIM
    rows = N_HEADS * tt

    parts = jnp.concatenate(_split3(clft_ref[i]), axis=0)
    blocks = [parts[:, blk * MXU_DIM:(blk + 1) * MXU_DIM] for blk in range(nblk)]
    local = _dot(jnp.concatenate(blocks, axis=0).astype(BF16), ltri_ref[...])
    off = jnp.zeros((N_FEAT, 1), F32)
    sufs = [None] * nblk
    for blk in reversed(range(nblk)):
        loc = local[blk * N_FEAT:(blk + 1) * N_FEAT]
        sufs[blk] = loc + off
        off = off + loc[:, 0:1] + blocks[blk][:, 0:1]
    suf_parts = jnp.concatenate(sufs, axis=1)
    ck_rel = suf_parts[0:8] + suf_parts[8:16] + suf_parts[16:24]
    c_hi, c_mid, c_lo = _split3(ck_rel)
    kbt = jnp.concatenate([jnp.ones((N_FEAT, p_len), F32), c_hi, c_mid, c_lo,
                           jnp.zeros((LANES - 2 * N_FEAT, p_len), F32)], axis=0)
    k_all = jnp.concatenate([kt_ref[i].astype(BF16), kbt.astype(BF16)], axis=0)

    lane_q = lax.broadcasted_iota(jnp.int32, (tt, d_attn), 1)
    lane_b = lax.broadcasted_iota(jnp.int32, (tt, LANES), 1)
    q = q_ref[i].astype(F32)
    qb = qb_ref[i].astype(F32)
    stack = []
    for h in range(N_HEADS):
        qm = jnp.where((lane_q >= h * HEAD_DIM) & (lane_q < (h + 1) * HEAD_DIM), q, 0.0)
        bm = jnp.where((lane_b % N_HEADS) == h, qb, 0.0)
        stack.append(jnp.concatenate([qm, bm], axis=1))
    qs = jnp.concatenate(stack, axis=0).astype(BF16)

    s_c = _dot(qs, k_all)

    pad = jnp.zeros((LANES - tt, d_attn + LANES), F32)
    kn = jnp.concatenate([jnp.concatenate([kn_ref[i], kbn_ref[i].astype(F32)], axis=1), pad], axis=0)
    s_n = _dot_nt(qs, kn.astype(BF16))
    row = lax.broadcasted_iota(jnp.int32, s_n.shape, 0)
    col = lax.broadcasted_iota(jnp.int32, s_n.shape, 1)
    s_n = jnp.where(col <= (row % tt), s_n, NEG_INF)

    m = jnp.maximum(jnp.max(s_c, axis=-1, keepdims=True), jnp.max(s_n, axis=-1, keepdims=True))
    p_c = jnp.exp(s_c - m)
    p_n = jnp.exp(s_n - m)
    l = jnp.sum(p_c, axis=-1, keepdims=True) + jnp.sum(p_n, axis=-1, keepdims=True)
    vn = jnp.concatenate([vn_ref[i], jnp.zeros((LANES - tt, d_attn), F32)], axis=0).astype(BF16)
    o = _dot_nt(p_c.astype(BF16), vt_ref[i].astype(BF16)) + _dot(p_n.astype(BF16), vn)
    o = o / l

    out = jnp.zeros((tt, d_attn), F32)
    for h in range(N_HEADS):
        out = jnp.where((lane_q >= h * HEAD_DIM) & (lane_q < (h + 1) * HEAD_DIM), o[h * tt:(h + 1) * tt], out)
    o_ref[i] = out.astype(BF16)


def _attn_sample(q, qb, kn, vn, kbn, kt, vt, clft, ltri):
    nb, tt, d_attn = q.shape
    p_len = kt.shape[2]
    per_req = lambda shape: pl.BlockSpec((SAMPLE_REQS_PER_STEP,) + shape, lambda i: (i, 0, 0))
    return pl.pallas_call(
        _attn_sample_kernel,
        grid=(nb // SAMPLE_REQS_PER_STEP,),
        in_specs=[per_req((tt, d_attn)), per_req((tt, LANES)), per_req((tt, d_attn)), per_req((tt, d_attn)),
                  per_req((tt, LANES)), per_req((d_attn, p_len)), per_req((d_attn, p_len)),
                  per_req((N_HEADS, p_len)), _const_spec(ltri.shape)],
        out_specs=per_req((tt, d_attn)),
        out_shape=jax.ShapeDtypeStruct((nb, tt, d_attn), BF16),
        compiler_params=pltpu.CompilerParams(dimension_semantics=("arbitrary",),
                                             vmem_limit_bytes=VMEM_LIMIT_BYTES),
        name="attn_sample",
    )(q, qb, kn, vn, kbn, kt, vt, clft, ltri)


def _outproj_ffn_body(x_ref, co, at, g1_ref, sh2_ref, sc2_ref, g2_ref, n2g_ref,
                      wo_ref, wg_ref, wu_ref, wd_ref, y_ref, acc_ref, vpu_slot=None):
    nb, tt, d = x_ref.shape
    r = nb * tt
    d_conv = co.shape[1]
    mix = _dot(co, wo_ref[0:d_conv, :]) + _dot(at, wo_ref[d_conv:, :])
    x1 = x_ref[...] + g1_ref[:, 0] * mix.reshape(nb, tt, d)
    hb = _adaln_rmsnorm(x1, n2g_ref[...], sc2_ref[:, 0], sh2_ref[:, 0]).reshape(r, d).astype(BF16)
    lead = x1.reshape(r, d)[0:SUBLANES, 0:d_conv]
    n_ff = wg_ref.shape[1] // FF_CHUNK
    chunk = lambda c: slice(c * FF_CHUNK, (c + 1) * FF_CHUNK)
    for c in range(n_ff):
        zero = vpu_slot(c, lead) if vpu_slot is not None else None
        g, u = _dot(hb, wg_ref[:, chunk(c)]), _dot(hb, wu_ref[:, chunk(c)])
        if zero is not None:
            zero = jnp.concatenate([zero] * (FF_CHUNK // LANES), axis=1)
            g = jnp.concatenate([g[:SUBLANES] + zero, g[SUBLANES:]], axis=0)
        a = ((g * jax.nn.sigmoid(g)) * u).astype(BF16)
        part = _dot(a, wd_ref[chunk(c), :])
        if c == 0:
            acc_ref[...] = part
        else:
            acc_ref[...] += part
        lead = part[0:SUBLANES, 0:d_conv]
    y_ref[...] = x1 + g2_ref[:, 0] * acc_ref[...].reshape(nb, tt, d)


def _outproj_ffn_kernel(x_ref, co_ref, at_ref, g1_ref, sh2_ref, sc2_ref, g2_ref, n2g_ref,
                        wo_ref, wg_ref, wu_ref, wd_ref, y_ref, acc_ref):
    nb, tt, _ = x_ref.shape
    co = co_ref[...].reshape(nb * tt, co_ref.shape[2])
    at = at_ref[...].reshape(nb * tt, at_ref.shape[2])
    _outproj_ffn_body(x_ref, co, at, g1_ref, sh2_ref, sc2_ref, g2_ref, n2g_ref,
                      wo_ref, wg_ref, wu_ref, wd_ref, y_ref, acc_ref)


def _exact_zero(v):
    return jnp.minimum(jnp.abs(v), 0.0)


def _outproj_ffn_conv_kernel(x_ref, at_ref, un_ref, u0_ref, g1_ref, sh2_ref, sc2_ref, g2_ref, n2g_ref,
                             wo_ref, wg_ref, wu_ref, wd_ref, cw_ref, cb_ref, lng_ref, lnb_ref,
                             y_ref, acc_ref, shift_ref, co_ref, wb_ref, *, tiles_per_seq):
    g = pl.program_id(0)
    tm = x_ref.shape[1]
    d_conv = co_ref.shape[1]
    span = tm + HIST - SUBLANES
    n_conv = tm // CONV_ROWS
    n_ff = wg_ref.shape[1] // FF_CHUNK
    cb, ln_g, ln_b = cb_ref[...], lng_ref[...], lnb_ref[...]

    def stage(u, hist):
        shift_ref[0, 0:HIST, :] = hist
        shift_ref[0, HIST:HIST + tm, :] = u
        for r in range(1, SUBLANES):
            shift_ref[r, 0:span, :] = shift_ref[0, r:r + span, :]

    def conv_chunk(c, bias):
        def tap(k):
            off = k + HIST - (CONV_WIDTH - 1)
            row = c * CONV_ROWS + off - off % SUBLANES
            return shift_ref[off % SUBLANES, row:row + CONV_ROWS, :]
        weight = lambda k: jnp.concatenate([wb_ref[k]] * (CONV_ROWS // SUBLANES), axis=0)
        y = _conv_ln_swish(tap, weight, bias, ln_g, ln_b)
        co_ref[c * CONV_ROWS:(c + 1) * CONV_ROWS, :] = y.astype(BF16)
        return y

    @pl.when(g == 0)
    def _():
        for k in range(CONV_WIDTH):
            wb_ref[k] = jnp.broadcast_to(cw_ref[k:k + 1, :], (SUBLANES, d_conv))
        stage(u0_ref[0], jnp.zeros((HIST, d_conv), F32))
        for c in range(n_conv):
            conv_chunk(c, cb)

    def conv_slot(c, lead):
        if c == 0:
            tail = shift_ref[0, tm:tm + HIST, :]
            starts_seq = (g + 1) % tiles_per_seq == 0
            stage(un_ref[0], jnp.where(starts_seq, 0.0, tail))
        bias = jnp.concatenate([cb + _exact_zero(lead)] * (CONV_ROWS // SUBLANES), axis=0)
        zero = jnp.zeros((SUBLANES, LANES), F32)
        for i in range(c * n_conv // n_ff, (c + 1) * n_conv // n_ff):
            zero = zero + _exact_zero(conv_chunk(i, bias)[0:SUBLANES, 0:LANES])
        return zero

    co = co_ref[...]
    _outproj_ffn_body(x_ref, co, at_ref[0], g1_ref, sh2_ref, sc2_ref, g2_ref, n2g_ref,
                      wo_ref, wg_ref, wu_ref, wd_ref, y_ref, acc_ref, vpu_slot=conv_slot)


def _outproj_ffn_conv(x, at, u, mod4, n2g, wo, wg, wu, wd, conv_w, conv_b, ln_g, ln_b):
    b, s, d = x.shape
    tm = ROW_TILE
    d_conv = u.shape[2]
    d_attn = at.shape[2]
    tps = s // tm
    n_tiles = b * tps
    tile = lambda g: (g // tps, g % tps, 0)
    next_tile = lambda g: tile(jnp.minimum(g + 1, n_tiles - 1))
    mod_spec = lambda j: pl.BlockSpec((1, 1, 1, d), lambda g: (g // tps, j, 0, 0))
    consts = (n2g, wo, wg, wu, wd, conv_w, conv_b, ln_g, ln_b)
    return pl.pallas_call(
        functools.partial(_outproj_ffn_conv_kernel, tiles_per_seq=tps),
        grid=(n_tiles,),
        in_specs=[pl.BlockSpec((1, tm, d), tile), pl.BlockSpec((1, tm, d_attn), tile),
                  pl.BlockSpec((1, tm, d_conv), next_tile),
                  pl.BlockSpec((1, tm, d_conv), lambda g: (0, 0, 0), pipeline_mode=pl.Buffered(1)),
                  mod_spec(2), mod_spec(3), mod_spec(4), mod_spec(5)] + [_const_spec(c.shape) for c in consts],
        out_specs=pl.BlockSpec((1, tm, d), tile),
        out_shape=jax.ShapeDtypeStruct((b, s, d), F32),
        scratch_shapes=[pltpu.VMEM((tm, d), F32),
                        pltpu.VMEM((SUBLANES, tm + HIST, d_conv), F32),
                        pltpu.VMEM((tm, d_conv), BF16),
                        pltpu.VMEM((CONV_WIDTH, SUBLANES, d_conv), F32)],
        compiler_params=pltpu.CompilerParams(dimension_semantics=("arbitrary",),
                                             vmem_limit_bytes=VMEM_LIMIT_BYTES),
        name="outproj_ffn_conv",
    )(x, at, u, u, mod4, mod4, mod4, mod4, *consts)


def _outproj_ffn(x, co, at, mod4, n2g, wo, wg, wu, wd, bb, tt):
    nb, s, d = x.shape
    d_conv = co.shape[2]
    d_attn = at.shape[2]
    mod_spec = lambda j: pl.BlockSpec((bb, 1, 1, d), lambda i, t: (i, j, 0, 0))
    row_spec = lambda w: pl.BlockSpec((bb, tt, w), lambda i, t: (i, t, 0))
    return pl.pallas_call(
        _outproj_ffn_kernel,
        grid=(nb // bb, s // tt),
        in_specs=[row_spec(d), row_spec(d_conv), row_spec(d_attn),
                  mod_spec(2), mod_spec(3), mod_spec(4), mod_spec(5),
                  _const_spec(n2g.shape), _const_spec(wo.shape), _const_spec(wg.shape),
                  _const_spec(wu.shape), _const_spec(wd.shape)],
        out_specs=row_spec(d),
        out_shape=jax.ShapeDtypeStruct((nb, s, d), F32),
        scratch_shapes=[pltpu.VMEM((bb * tt, d), F32)],
        compiler_params=pltpu.CompilerParams(dimension_semantics=("arbitrary", "arbitrary"),
                                             vmem_limit_bytes=VMEM_LIMIT_BYTES),
        name="outproj_ffn",
    )(x, co, at, mod4, mod4, mod4, mod4, n2g, wo, wg, wu, wd)


def _tri(n, rel):
    i = lax.broadcasted_iota(jnp.int32, (n, n), 0)
    j = lax.broadcasted_iota(jnp.int32, (n, n), 1)
    return rel(i, j).astype(BF16)


def _layer(xp, xs, cache_k, cache_v, cache_logf, state_conv, c_all, w_ada, b_ada, norm1_g, w_in, b_f,
           q_norm_g, k_norm_g, conv_w, conv_b, conv_ln_g, conv_ln_b, w_out, norm2_g, w_gate, w_up, w_down):
    b, s, d = xp.shape
    nb, tt, _ = xs.shape
    d_conv = conv_w.shape[1]
    d_attn = N_HEADS * HEAD_DIM
    d_ff = w_gate.shape[1]
    p_len = cache_k.shape[1]

    mod4 = _modulation(c_all, w_ada, b_ada).reshape(b + nb, 6, 1, d)
    mod_p, mod_s = mod4[:b], mod4[b:]

    wglu = w_in[:, :2 * d_conv].astype(BF16)
    wqkvf_t = jnp.pad(w_in[:, 2 * d_conv:].T, ((0, LANES - N_HEADS), (0, 0))).astype(BF16)
    bf_col = b_f.reshape(N_HEADS, 1)
    qg_col = q_norm_g.reshape(d_attn, 1)
    kg_col = k_norm_g.reshape(d_attn, 1)
    row = lambda v: v.reshape(1, -1)
    n1g, n2g, cb, ln_g, ln_b = row(norm1_g), row(norm2_g), row(conv_b), row(conv_ln_g), row(conv_ln_b)
    wo = w_out.astype(BF16)
    assert d_ff % FF_CHUNK == 0
    wg, wu, wd = w_gate.astype(BF16), w_up.astype(BF16), w_down.astype(BF16)
    utri = _tri(MXU_DIM, lambda i, j: i <= j)
    ltri = _tri(MXU_DIM, lambda i, j: i > j)
    bdtri = _tri(nb * tt, lambda i, j: (i // tt == j // tt) & (j <= i))

    kt, vt, lft, cst, qt, qft, k, kb, u = _inproj_prompt(
        xp, mod_p, n1g, wglu, wqkvf_t, bf_col, qg_col, kg_col, utri)
    at = _attn_prompt(qt, qft, k, kb, vt)
    yp = _outproj_ffn_conv(xp, at, u, mod_p, n2g, wo, wg, wu, wd, conv_w, cb, ln_g, ln_b)
    k_p = kt.reshape(b, N_HEADS, HEAD_DIM, s).transpose(0, 3, 1, 2)
    v_p = vt.reshape(b, N_HEADS, HEAD_DIM, s).transpose(0, 3, 1, 2)
    lf_p = lft.transpose(0, 2, 1)

    state_t = state_conv.transpose(1, 0, 2)
    k_s, v_s, lf_s, nst, q_s, qb_s, kb_s, co_s = _inproj_sample(
        xs, mod_s, n1g, wglu, wqkvf_t, bf_col, qg_col, kg_col, bdtri, conv_w, cb, ln_g, ln_b, state_t)
    r3 = lambda a: a.reshape(nb, tt, a.shape[-1])
    ckt = cache_k.transpose(0, 2, 3, 1).reshape(nb, d_attn, p_len)
    cvt = cache_v.transpose(0, 2, 3, 1).reshape(nb, d_attn, p_len)
    clft = cache_logf.transpose(0, 2, 1)
    at_s = _attn_sample(r3(q_s), r3(qb_s), r3(k_s), r3(v_s), r3(kb_s), ckt, cvt, clft, ltri)
    ys = _outproj_ffn(xs, r3(co_s), at_s, mod_s, n2g, wo, wg, wu, wd, nb, tt)
    k_sn = k_s.reshape(nb, tt, N_HEADS, HEAD_DIM)
    v_sn = v_s.reshape(nb, tt, N_HEADS, HEAD_DIM)
    lf_sn = lf_s[:, :N_HEADS].reshape(nb, tt, N_HEADS)
    return yp, ys, (k_p, v_p, lf_p, cst), (k_sn, v_sn, lf_sn, nst.transpose(1, 0, 2))


def kernel(x_prompt, x_sample, cache_k, cache_v, cache_logf, state_conv, c_prompt, c_sample, w_ada, b_ada,
           norm1_g, w_in, b_f, q_norm_g, k_norm_g, conv_w, conv_b, conv_ln_g, conv_ln_b, w_out, norm2_g,
           w_gate, w_up, w_down):
    depth = w_ada.shape[0]
    c_all = jnp.concatenate([c_prompt, c_sample], axis=0)
    yp, ys = x_prompt, x_sample
    st_p, st_s = [], []
    for l in range(depth):
        yp, ys, sp, ss = _layer(
            yp, ys, cache_k[l], cache_v[l], cache_logf[l], state_conv[l], c_all, w_ada[l], b_ada[l],
            norm1_g[l], w_in[l], b_f[l], q_norm_g[l], k_norm_g[l], conv_w[l], conv_b[l], conv_ln_g[l],
            conv_ln_b[l], w_out[l], norm2_g[l], w_gate[l], w_up[l], w_down[l])
        st_p.append(sp)
        st_s.append(ss)
    stack = lambda xs: xs[0][None] if len(xs) == 1 else jnp.stack(xs)
    outs_p = [stack([s[i] for s in st_p]) for i in range(4)]
    outs_s = [stack([s[i] for s in st_s]) for i in range(4)]
    return (yp, ys, *outs_p, *outs_s)
```

```python
import functools

import jax
import jax.numpy as jnp
from jax import lax
from jax.experimental import pallas as pl
from jax.experimental.pallas import tpu as pltpu

F32 = jnp.float32
BF16 = jnp.bfloat16

N_HEADS = 8
HEAD_DIM = 64
CONV_WIDTH = 31
EPS = 1e-6
NEG_INF = -1e30

LANES = 128
SUBLANES = 8
MXU_DIM = 256
VMEM_LIMIT_BYTES = 56 * 1024 * 1024

N_PARTS = 3
N_FEAT = N_PARTS * N_HEADS
HIST = 32
CONV_ROWS = 32

ROW_TILE = 512
Q_TILE = 256
ATTN_LAG = 2
DENOM_ROWS = 16
LOG2E = 1.4426950408889634
FF_CHUNK = 256
SAMPLE_REQS_PER_STEP = 2


def _dot(a, b):
    return jnp.dot(a, b, preferred_element_type=F32)


def _dot_nt(a, b):
    return lax.dot_general(a, b, (((1,), (1,)), ((), ())), preferred_element_type=F32)


def _split3(x):
    hi = x.astype(BF16).astype(F32)
    r = x - hi
    mid = r.astype(BF16).astype(F32)
    lo = (r - mid).astype(BF16).astype(F32)
    return hi, mid, lo


def _log_sigmoid(x):
    return jnp.minimum(x, 0.0) - jnp.log1p(jnp.exp(-jnp.abs(x)))


def _adaln_rmsnorm(x, g, scale, shift):
    y = x * lax.rsqrt(jnp.mean(x * x, axis=-1, keepdims=True) + EPS)
    return (y * g) * (1.0 + scale) + shift


def _const_spec(shape):
    n = len(shape)
    return pl.BlockSpec(shape, lambda *_: (0,) * n, pipeline_mode=pl.Buffered(1))


def _modulation_kernel(c_ref, w_ref, b_ref, o_ref):
    c = c_ref[...]
    a = (c * jax.nn.sigmoid(c)).astype(BF16)
    o_ref[...] = _dot(a, w_ref[...].astype(BF16)) + b_ref[...]


def _modulation(c, w_ada, b_ada):
    nb, d = c.shape
    n = w_ada.shape[1]
    tn = 1536
    return pl.pallas_call(
        _modulation_kernel,
        grid=(n // tn,),
        in_specs=[pl.BlockSpec((nb, d), lambda j: (0, 0)),
                  pl.BlockSpec((d, tn), lambda j: (0, j)),
                  pl.BlockSpec((1, tn), lambda j: (0, j))],
        out_specs=pl.BlockSpec((nb, tn), lambda j: (0, j)),
        out_shape=jax.ShapeDtypeStruct((nb, n), F32),
        compiler_params=pltpu.CompilerParams(dimension_semantics=("arbitrary",),
                                             vmem_limit_bytes=VMEM_LIMIT_BYTES),
        name="modulation",
    )(c, w_ada, b_ada.reshape(1, n))


def _qkvf_feature_major(hb, wqkvf_t, qg_col, kg_col, bf_col):
    d_attn = N_HEADS * HEAD_DIM
    zt = _dot_nt(wqkvf_t, hb)
    r = zt.shape[1]

    def head_rms(z, g_col):
        z3 = z.reshape(N_HEADS, HEAD_DIM, r)
        ms = jnp.mean(z3 * z3, axis=1, keepdims=True)
        return (z3 * lax.rsqrt(ms + EPS)).reshape(d_attn, r) * g_col

    q_t = head_rms(zt[0:d_attn], qg_col)
    k_t = head_rms(zt[d_attn:2 * d_attn], kg_col)
    v_t = zt[2 * d_attn:3 * d_attn]
    lf_t = _log_sigmoid(zt[3 * d_attn:3 * d_attn + N_HEADS] + bf_col)
    return q_t, k_t, v_t, lf_t


def _conv_ln_swish(load_rows, weight, cb, ln_g, ln_b):
    acc = cb + weight(0) * load_rows(0)
    for k in range(1, CONV_WIDTH):
        acc = acc + weight(k) * load_rows(k)
    mu = jnp.mean(acc, axis=-1, keepdims=True)
    cen = acc - mu
    var = jnp.mean(cen * cen, axis=-1, keepdims=True)
    y = cen * lax.rsqrt(var + EPS) * ln_g + ln_b
    return y * jax.nn.sigmoid(y)


def _inproj_prompt_kernel(x_ref, sh_ref, sc_ref, n1g_ref, wglu_ref, wqkvf_ref, bf_ref, qg_ref, kg_ref,
                          utri_ref,
                          kt_ref, vt_ref, lft_ref, cst_ref, qt_ref, qft_ref, k_ref, kb_ref, u_ref,
                          carry_ref):
    t = pl.program_id(1)
    tm = x_ref.shape[1]
    d_conv = u_ref.shape[2]

    @pl.when(t == 0)
    def _():
        carry_ref[...] = jnp.zeros(carry_ref.shape, F32)

    hb = _adaln_rmsnorm(x_ref[0], n1g_ref[...], sc_ref[0, 0], sh_ref[0, 0]).astype(BF16)

    q_t, k_t, v_t, lf_t = _qkvf_feature_major(hb, wqkvf_ref[...], qg_ref[...], kg_ref[...], bf_ref[...])
    kt_ref[0] = k_t
    vt_ref[0] = v_t
    lft_ref[0] = lf_t

    zg = _dot(hb, wglu_ref[...])
    u_ref[0] = zg[:, :d_conv] * jax.nn.sigmoid(zg[:, d_conv:])

    nblk = tm // MXU_DIM
    parts = jnp.concatenate(_split3(lf_t), axis=0)
    stacked = jnp.concatenate([parts[:, i * MXU_DIM:(i + 1) * MXU_DIM] for i in range(nblk)], axis=0)
    local = _dot(stacked.astype(BF16), utri_ref[...])
    carry = carry_ref[:, 0:1]
    cums = []
    for i in range(nblk):
        loc = local[i * N_FEAT:(i + 1) * N_FEAT]
        cums.append(loc + carry)
        carry = carry + loc[:, MXU_DIM - 1:MXU_DIM]
    carry_ref[...] = jnp.broadcast_to(carry, carry_ref.shape)
    cum_parts = jnp.concatenate(cums, axis=1)
    cum_t = (cum_parts[0:8] + cum_parts[8:16] + cum_parts[16:24]) * LOG2E

    c_hi, c_mid, c_lo = _split3(cum_t)
    ones = jnp.ones((N_FEAT, tm), F32)
    zeros = jnp.zeros((LANES - 2 * N_FEAT, tm), F32)
    kfeat_t = jnp.concatenate([ones, -c_hi, -c_mid, -c_lo, zeros], axis=0)
    qft_ref[0] = jnp.concatenate([c_hi, c_mid, c_lo, ones, zeros], axis=0).astype(BF16)
    qt_ref[0] = (q_t * (HEAD_DIM ** -0.5 * LOG2E)).astype(BF16)
    k_ref[0] = k_t.T.astype(BF16)
    kb_ref[0] = kfeat_t.T.astype(BF16)

    @pl.when(t == pl.num_programs(1) - 1)
    def _():
        cst_ref[0] = u_ref[0, tm - (CONV_WIDTH - 1):tm, :]


def _inproj_prompt(x, mod4, mod_row0, n1g, wglu, wqkvf_t, bf_col, qg_col, kg_col, utri):
    b, s, d = x.shape
    tm = ROW_TILE
    d_conv = wglu.shape[1] // 2
    d_attn = N_HEADS * HEAD_DIM
    mod_spec = lambda j: pl.BlockSpec((1, 1, 1, d), lambda i, t: (mod_row0 + i, j, 0, 0))
    out_shape = (
        jax.ShapeDtypeStruct((b, d_attn, s), F32),
        jax.ShapeDtypeStruct((b, d_attn, s), F32),
        jax.ShapeDtypeStruct((b, N_HEADS, s), F32),
        jax.ShapeDtypeStruct((b, CONV_WIDTH - 1, d_conv), F32),
        jax.ShapeDtypeStruct((b, d_attn, s), BF16),
        jax.ShapeDtypeStruct((b, LANES, s), BF16),
        jax.ShapeDtypeStruct((b, s, d_attn), BF16),
        jax.ShapeDtypeStruct((b, s, LANES), BF16),
        jax.ShapeDtypeStruct((b, s, d_conv), F32),
    )
    out_specs = (
        pl.BlockSpec((1, d_attn, tm), lambda i, t: (i, 0, t)),
        pl.BlockSpec((1, d_attn, tm), lambda i, t: (i, 0, t)),
        pl.BlockSpec((1, N_HEADS, tm), lambda i, t: (i, 0, t)),
        pl.BlockSpec((1, CONV_WIDTH - 1, d_conv), lambda i, t: (i, 0, 0)),
        pl.BlockSpec((1, d_attn, tm), lambda i, t: (i, 0, t)),
        pl.BlockSpec((1, LANES, tm), lambda i, t: (i, 0, t)),
        pl.BlockSpec((1, tm, d_attn), lambda i, t: (i, t, 0)),
        pl.BlockSpec((1, tm, LANES), lambda i, t: (i, t, 0)),
        pl.BlockSpec((1, tm, d_conv), lambda i, t: (i, t, 0)),
    )
    in_specs = [
        pl.BlockSpec((1, tm, d), lambda i, t: (i, t, 0)),
        mod_spec(0), mod_spec(1),
        _const_spec(n1g.shape), _const_spec(wglu.shape), _const_spec(wqkvf_t.shape),
        _const_spec(bf_col.shape), _const_spec(qg_col.shape), _const_spec(kg_col.shape),
        _const_spec(utri.shape),
    ]
    return pl.pallas_call(
        _inproj_prompt_kernel,
        grid=(b, s // tm),
        in_specs=in_specs,
        out_specs=out_specs,
        out_shape=out_shape,
        scratch_shapes=[pltpu.VMEM((N_FEAT, LANES), F32)],
        compiler_params=pltpu.CompilerParams(dimension_semantics=("arbitrary", "arbitrary"),
                                             vmem_limit_bytes=VMEM_LIMIT_BYTES),
        name="inproj_prompt",
    )(x, mod4, mod4, n1g, wglu, wqkvf_t, bf_col, qg_col, kg_col, utri)


def _attn_prompt_kernel(qt_ref, qft_ref, k_ref, kb_ref, vt_ref, o_ref,
                        kp_ref, vp_ref, q2_ref, m_ref, acc_ref, sc_ref):
    qi = pl.program_id(1)
    tq = qt_ref.shape[2]
    n_pairs, nblk = vp_ref.shape[0], vp_ref.shape[1]

    @pl.when(qi == 0)
    def _():
        ones = jnp.ones((vp_ref.shape[2] - LANES, tq), BF16)
        for p in range(n_pairs):
            kp_ref[p, :, 0:LANES] = k_ref[0, :, p * LANES:(p + 1) * LANES]
            kp_ref[p, :, LANES:2 * LANES] = kb_ref[0]
            for i in range(nblk):
                vp_ref[p, i, 0:LANES, :] = vt_ref[0, p * LANES:(p + 1) * LANES, i * tq:(i + 1) * tq].astype(BF16)
                vp_ref[p, i, LANES:, :] = ones

    frow = lax.broadcasted_iota(jnp.int32, (LANES, tq), 0)
    feat = qft_ref[0].astype(F32)
    for p in range(n_pairs):
        qp = qt_ref[0, p * LANES:(p + 1) * LANES, :].astype(F32)
        cols = []
        for j in range(2):
            qm = jnp.where((frow >= j * HEAD_DIM) & (frow < (j + 1) * HEAD_DIM), qp, 0.0)
            fm = jnp.where((frow % N_HEADS) == 2 * p + j, feat, 0.0)
            cols.append(jnp.concatenate([qm, fm], axis=0))
        q2_ref[p] = jnp.concatenate(cols, axis=1).astype(BF16)

    m_ref[...] = jnp.full(m_ref.shape, NEG_INF, F32)
    acc_ref[...] = jnp.zeros(acc_ref.shape, F32)

    def scores(p, j):
        start = pl.multiple_of(j * tq, tq)
        return _dot(kp_ref[p, pl.ds(start, tq), :], q2_ref[p])

    def accumulate(p, j, masked, s):
        if masked:
            key = lax.broadcasted_iota(jnp.int32, s.shape, 0)
            qry = lax.broadcasted_iota(jnp.int32, s.shape, 1)
            qry = jnp.where(qry >= tq, qry - tq, qry)
            s = jnp.where(key <= qry, s, NEG_INF)
        m_old = m_ref[p]
        m_new = jnp.maximum(m_old, jnp.max(s, axis=0, keepdims=True))
        pr = jnp.exp2(s - m_new).astype(BF16)
        acc_ref[p] = jnp.exp2(m_old - m_new) * acc_ref[p] + _dot(vp_ref[p, j], pr)
        m_ref[p] = m_new

    def run_units(units, next_block):
        pending = {}
        for i in range(len(units) + ATTN_LAG):
            if i < ATTN_LAG:
                pending[i] = sc_ref[i]
            elif i < len(units):
                p, j, _ = units[i]
                pending[i] = scores(p, j)
            elif next_block is not None:
                sc_ref[i - len(units)] = scores(i - len(units), next_block)
            if i >= ATTN_LAG:
                p, j, masked = units[i - ATTN_LAG]
                accumulate(p, j, masked, pending.pop(i - ATTN_LAG))

    def block(j, masked):
        return [(p, j, masked) for p in range(n_pairs)]

    for p in range(ATTN_LAG):
        sc_ref[p] = scores(p, 0)

    def body(i, carry):
        run_units(block(2 * i, False) + block(2 * i + 1, False), 2 * i + 2)
        return carry

    lax.fori_loop(0, qi // 2, body, 0)

    @pl.when(qi % 2 == 1)
    def _():
        run_units(block(qi - 1, False) + block(qi, True), None)

    @pl.when(qi % 2 == 0)
    def _():
        run_units(block(qi, True), None)

    for p in range(n_pairs):
        acc = acc_ref[p]
        out_t = acc[0:LANES] / acc[LANES:LANES + 1]
        pair_t = jnp.where(frow < HEAD_DIM, out_t[:, :tq], out_t[:, tq:])
        o_ref[0, :, p * LANES:(p + 1) * LANES] = pair_t.T.astype(BF16)


def _attn_prompt(qt, qft, k, kb, vt):
    b, d_attn, s = qt.shape
    tq = Q_TILE
    nblk = s // tq
    n_pairs = d_attn // LANES
    return pl.pallas_call(
        _attn_prompt_kernel,
        grid=(b, nblk),
        in_specs=[
            pl.BlockSpec((1, d_attn, tq), lambda i, j: (i, 0, j)),
            pl.BlockSpec((1, LANES, tq), lambda i, j: (i, 0, j)),
            pl.BlockSpec((1, s, d_attn), lambda i, j: (i, 0, 0)),
            pl.BlockSpec((1, s, LANES), lambda i, j: (i, 0, 0)),
            pl.BlockSpec((1, d_attn, s), lambda i, j: (i, 0, 0)),
        ],
        out_specs=pl.BlockSpec((1, tq, d_attn), lambda i, j: (i, j, 0)),
        out_shape=jax.ShapeDtypeStruct((b, s, d_attn), BF16),
        scratch_shapes=[
            pltpu.VMEM((n_pairs, s, 2 * LANES), BF16),
            pltpu.VMEM((n_pairs, nblk, LANES + DENOM_ROWS, tq), BF16),
            pltpu.VMEM((n_pairs, 2 * LANES, 2 * tq), BF16),
            pltpu.VMEM((n_pairs, 1, 2 * tq), F32),
            pltpu.VMEM((n_pairs, LANES + DENOM_ROWS, 2 * tq), F32),
            pltpu.VMEM((ATTN_LAG, tq, 2 * tq), F32),
        ],
        compiler_params=pltpu.CompilerParams(dimension_semantics=("arbitrary", "arbitrary"),
                                             vmem_limit_bytes=VMEM_LIMIT_BYTES),
        name="attn_prompt",
    )(qt, qft, k, kb, vt)


def _inproj_sample_kernel(x_ref, sh_ref, sc_ref, n1g_ref, wglu_ref, wqkvf_ref, bf_ref, qg_ref, kg_ref,
                          bdtri_ref, cw_ref, cb_ref, lng_ref, lnb_ref, st_ref,
                          k_ref, v_ref, lf_ref, nst_ref, q_ref, qb_ref, kb_ref, co_ref,
                          u_ref, hist_ref, cof_ref):
    nb, tt, d = x_ref.shape
    r = nb * tt
    d_conv = co_ref.shape[1]
    n_state = CONV_WIDTH - 1

    h = _adaln_rmsnorm(x_ref[...], n1g_ref[...], sc_ref[:, 0], sh_ref[:, 0])
    hb = h.reshape(r, d).astype(BF16)

    n_chunks = d_conv // LANES
    zg = _dot(hb, wglu_ref[...])
    u = zg[:, :d_conv] * jax.nn.sigmoid(zg[:, d_conv:])
    for c in range(n_chunks):
        u_ref[c] = u[:, c * LANES:(c + 1) * LANES]

    q_t, k_t, v_t, lf_t = _qkvf_feature_major(hb, wqkvf_ref[...], qg_ref[...], kg_ref[...], bf_ref[...])
    k_ref[...] = k_t.T
    v_ref[...] = v_t.T
    q_ref[...] = (q_t * (HEAD_DIM ** -0.5)).T.astype(BF16)
    lf = jnp.concatenate([lf_t, jnp.zeros((LANES - N_HEADS, r), F32)], axis=0).T
    lf_ref[...] = lf

    lane = lax.broadcasted_iota(jnp.int32, (r, LANES), 1)
    hi, mid, lo = _split3(lf)
    packed = hi + pltpu.roll(mid, N_HEADS, 1) + pltpu.roll(lo, 2 * N_HEADS, 1)
    c = _dot(bdtri_ref[...], packed.astype(BF16))
    cn = jnp.where(lane < N_HEADS,
                   c + pltpu.roll(c, LANES - N_HEADS, 1) + pltpu.roll(c, LANES - 2 * N_HEADS, 1), 0.0)
    hi, mid, lo = _split3(cn)
    p = hi + pltpu.roll(mid, N_HEADS, 1) + pltpu.roll(lo, 2 * N_HEADS, 1)
    qb_ref[...] = (p + jnp.where((lane >= N_FEAT) & (lane < 2 * N_FEAT), 1.0, 0.0)).astype(BF16)
    kb_ref[...] = (jnp.where(lane < N_FEAT, 1.0, 0.0) - pltpu.roll(p, N_FEAT, 1)).astype(BF16)

    hist_ref[0:n_state] = st_ref[...]
    for t in range(tt):
        for c in range(n_chunks):
            hist_ref[n_state + t, :, c * LANES:(c + 1) * LANES] = u_ref[c, pl.ds(t, nb, stride=tt), :]
    nst_ref[...] = hist_ref[tt:tt + n_state]
    cb, ln_g, ln_b = cb_ref[...], lng_ref[...], lnb_ref[...]
    for t in range(tt):
        y = _conv_ln_swish(lambda k: hist_ref[t + k], lambda k: cw_ref[k:k + 1, :], cb, ln_g, ln_b)
        for c in range(n_chunks):
            cof_ref[c, pl.ds(t, nb, stride=tt), :] = y[:, c * LANES:(c + 1) * LANES]
    co_ref[...] = jnp.concatenate([cof_ref[c] for c in range(n_chunks)], axis=1).astype(BF16)


def _inproj_sample(x, mod4, n1g, wglu, wqkvf_t, bf_col, qg_col, kg_col, bdtri, conv_w, conv_b, ln_g, ln_b,
                   state_t):
    nb, tt, d = x.shape
    r = nb * tt
    d_conv = conv_w.shape[1]
    d_attn = N_HEADS * HEAD_DIM
    n_state = CONV_WIDTH - 1
    mod_spec = lambda j: pl.BlockSpec((nb, 1, 1, d), lambda i: (0, j, 0, 0))
    full = lambda shape: pl.BlockSpec(shape, lambda i: (0,) * len(shape))
    out_shape = (
        jax.ShapeDtypeStruct((r, d_attn), F32),
        jax.ShapeDtypeStruct((r, d_attn), F32),
        jax.ShapeDtypeStruct((r, LANES), F32),
        jax.ShapeDtypeStruct((n_state, nb, d_conv), F32),
        jax.ShapeDtypeStruct((r, d_attn), BF16),
        jax.ShapeDtypeStruct((r, LANES), BF16),
        jax.ShapeDtypeStruct((r, LANES), BF16),
        jax.ShapeDtypeStruct((r, d_conv), BF16),
    )
    args = (x, mod4, mod4, n1g, wglu, wqkvf_t, bf_col, qg_col, kg_col, bdtri, conv_w, conv_b, ln_g, ln_b,
            state_t)
    in_specs = [full(x.shape), mod_spec(0), mod_spec(1)] + [full(a.shape) for a in args[3:]]
    return pl.pallas_call(
        _inproj_sample_kernel,
        grid=(1,),
        in_specs=in_specs,
        out_specs=tuple(full(o.shape) for o in out_shape),
        out_shape=out_shape,
        scratch_shapes=[pltpu.VMEM((d_conv // LANES, r, LANES), F32),
                        pltpu.VMEM((n_state + tt, nb, d_conv), F32),
                        pltpu.VMEM((d_conv // LANES, r, LANES), F32)],
        compiler_params=pltpu.CompilerParams(dimension_semantics=("arbitrary",),
                                             vmem_limit_bytes=VMEM_LIMIT_BYTES),
        name="inproj_sample",
    )(*args)


def _attn_sample_kernel(*refs):
    for i in range(refs[0].shape[0]):
        _attn_sample_request(i, *refs)


def _attn_sample_request(i, q_ref, qb_ref, kn_ref, vn_ref, kbn_ref, kt_ref, vt_ref, clft_ref, ltri_ref, o_ref):
    tt = q_ref.shape[1]
    d_attn = q_ref.shape[2]
    p_len = kt_ref.shape[2]
    nblk = p_len // MXU_DIM
    rows = N_HEADS * tt

    parts = jnp.concatenate(_split3(clft_ref[i]), axis=0)
    blocks = [parts[:, blk * MXU_DIM:(blk + 1) * MXU_DIM] for blk in range(nblk)]
    local = _dot(jnp.concatenate(blocks, axis=0).astype(BF16), ltri_ref[...])
    off = jnp.zeros((N_FEAT, 1), F32)
    sufs = [None] * nblk
    for blk in reversed(range(nblk)):
        loc = local[blk * N_FEAT:(blk + 1) * N_FEAT]
        sufs[blk] = loc + off
        off = off + loc[:, 0:1] + blocks[blk][:, 0:1]
    suf_parts = jnp.concatenate(sufs, axis=1)
    ck_rel = suf_parts[0:8] + suf_parts[8:16] + suf_parts[16:24]
    c_hi, c_mid, c_lo = _split3(ck_rel)
    kbt = jnp.concatenate([jnp.ones((N_FEAT, p_len), F32), c_hi, c_mid, c_lo,
                           jnp.zeros((LANES - 2 * N_FEAT, p_len), F32)], axis=0)
    k_all = jnp.concatenate([kt_ref[i].astype(BF16), kbt.astype(BF16)], axis=0)

    lane_q = lax.broadcasted_iota(jnp.int32, (tt, d_attn), 1)
    lane_b = lax.broadcasted_iota(jnp.int32, (tt, LANES), 1)
    q = q_ref[i].astype(F32)
    qb = qb_ref[i].astype(F32)
    stack = []
    for h in range(N_HEADS):
        qm = jnp.where((lane_q >= h * HEAD_DIM) & (lane_q < (h + 1) * HEAD_DIM), q, 0.0)
        bm = jnp.where((lane_b % N_HEADS) == h, qb, 0.0)
        stack.append(jnp.concatenate([qm, bm], axis=1))
    qs = jnp.concatenate(stack, axis=0).astype(BF16)

    s_c = _dot(qs, k_all)

    pad = jnp.zeros((LANES - tt, d_attn + LANES), F32)
    kn = jnp.concatenate([jnp.concatenate([kn_ref[i], kbn_ref[i].astype(F32)], axis=1), pad], axis=0)
    s_n = _dot_nt(qs, kn.astype(BF16))
    row = lax.broadcasted_iota(jnp.int32, s_n.shape, 0)
    col = lax.broadcasted_iota(jnp.int32, s_n.shape, 1)
    s_n = jnp.where(col <= (row % tt), s_n, NEG_INF)

    m = jnp.maximum(jnp.max(s_c, axis=-1, keepdims=True), jnp.max(s_n, axis=-1, keepdims=True))
    p_c = jnp.exp(s_c - m)
    p_n = jnp.exp(s_n - m)
    l = jnp.sum(p_c, axis=-1, keepdims=True) + jnp.sum(p_n, axis=-1, keepdims=True)
    vn = jnp.concatenate([vn_ref[i], jnp.zeros((LANES - tt, d_attn), F32)], axis=0).astype(BF16)
    o = _dot_nt(p_c.astype(BF16), vt_ref[i].astype(BF16)) + _dot(p_n.astype(BF16), vn)
    o = o / l

    out = jnp.zeros((tt, d_attn), F32)
    for h in range(N_HEADS):
        out = jnp.where((lane_q >= h * HEAD_DIM) & (lane_q < (h + 1) * HEAD_DIM), o[h * tt:(h + 1) * tt], out)
    o_ref[i] = out.astype(BF16)


def _attn_sample(q, qb, kn, vn, kbn, kt, vt, clft, ltri):
    nb, tt, d_attn = q.shape
    p_len = kt.shape[2]
    per_req = lambda shape: pl.BlockSpec((SAMPLE_REQS_PER_STEP,) + shape, lambda i: (i, 0, 0))
    return pl.pallas_call(
        _attn_sample_kernel,
        grid=(nb // SAMPLE_REQS_PER_STEP,),
        in_specs=[per_req((tt, d_attn)), per_req((tt, LANES)), per_req((tt, d_attn)), per_req((tt, d_attn)),
                  per_req((tt, LANES)), per_req((d_attn, p_len)), per_req((d_attn, p_len)),
                  per_req((N_HEADS, p_len)), _const_spec(ltri.shape)],
        out_specs=per_req((tt, d_attn)),
        out_shape=jax.ShapeDtypeStruct((nb, tt, d_attn), BF16),
        compiler_params=pltpu.CompilerParams(dimension_semantics=("arbitrary",),
                                             vmem_limit_bytes=VMEM_LIMIT_BYTES),
        name="attn_sample",
    )(q, qb, kn, vn, kbn, kt, vt, clft, ltri)


def _outproj_ffn_body(x_ref, co, at, g1_ref, sh2_ref, sc2_ref, g2_ref, n2g_ref,
                      wo_ref, wg_ref, wu_ref, wd_ref, y_ref, acc_ref, vpu_slot=None):
    nb, tt, d = x_ref.shape
    r = nb * tt
    d_conv = co.shape[1]
    mix = _dot(co, wo_ref[0:d_conv, :]) + _dot(at, wo_ref[d_conv:, :])
    x1 = x_ref[...] + g1_ref[:, 0] * mix.reshape(nb, tt, d)
    hb = _adaln_rmsnorm(x1, n2g_ref[...], sc2_ref[:, 0], sh2_ref[:, 0]).reshape(r, d).astype(BF16)
    lead = x1.reshape(r, d)[0:SUBLANES, 0:d_conv]
    n_ff = wg_ref.shape[1] // FF_CHUNK
    chunk = lambda c: slice(c * FF_CHUNK, (c + 1) * FF_CHUNK)
    for c in range(n_ff):
        zero = vpu_slot(c, lead) if vpu_slot is not None else None
        g, u = _dot(hb, wg_ref[:, chunk(c)]), _dot(hb, wu_ref[:, chunk(c)])
        if zero is not None:
            zero = jnp.concatenate([zero] * (FF_CHUNK // LANES), axis=1)
            g = jnp.concatenate([g[:SUBLANES] + zero, g[SUBLANES:]], axis=0)
        a = ((g * jax.nn.sigmoid(g)) * u).astype(BF16)
        part = _dot(a, wd_ref[chunk(c), :])
        if c == 0:
            acc_ref[...] = part
        else:
            acc_ref[...] += part
        lead = part[0:SUBLANES, 0:d_conv]
    y_ref[...] = x1 + g2_ref[:, 0] * acc_ref[...].reshape(nb, tt, d)


def _outproj_ffn_kernel(x_ref, co_ref, at_ref, g1_ref, sh2_ref, sc2_ref, g2_ref, n2g_ref,
                        wo_ref, wg_ref, wu_ref, wd_ref, y_ref, acc_ref):
    nb, tt, _ = x_ref.shape
    co = co_ref[...].reshape(nb * tt, co_ref.shape[2])
    at = at_ref[...].reshape(nb * tt, at_ref.shape[2])
    _outproj_ffn_body(x_ref, co, at, g1_ref, sh2_ref, sc2_ref, g2_ref, n2g_ref,
                      wo_ref, wg_ref, wu_ref, wd_ref, y_ref, acc_ref)


def _exact_zero(v):
    return jnp.minimum(jnp.abs(v), 0.0)


def _outproj_ffn_conv_kernel(x_ref, at_ref, un_ref, u0_ref, g1_ref, sh2_ref, sc2_ref, g2_ref, n2g_ref,
                             wo_ref, wg_ref, wu_ref, wd_ref, cw_ref, cb_ref, lng_ref, lnb_ref,
                             y_ref, acc_ref, shift_ref, co_ref, wb_ref, *, tiles_per_seq):
    g = pl.program_id(0)
    tm = x_ref.shape[1]
    d_conv = co_ref.shape[1]
    span = tm + HIST - SUBLANES
    n_conv = tm // CONV_ROWS
    n_ff = wg_ref.shape[1] // FF_CHUNK
    cb, ln_g, ln_b = cb_ref[...], lng_ref[...], lnb_ref[...]

    def stage(u, hist):
        shift_ref[0, 0:HIST, :] = hist
        shift_ref[0, HIST:HIST + tm, :] = u
        for r in range(1, SUBLANES):
            shift_ref[r, 0:span, :] = shift_ref[0, r:r + span, :]

    def conv_chunk(c, bias):
        def tap(k):
            off = k + HIST - (CONV_WIDTH - 1)
            row = c * CONV_ROWS + off - off % SUBLANES
            return shift_ref[off % SUBLANES, row:row + CONV_ROWS, :]
        weight = lambda k: jnp.concatenate([wb_ref[k]] * (CONV_ROWS // SUBLANES), axis=0)
        y = _conv_ln_swish(tap, weight, bias, ln_g, ln_b)
        co_ref[c * CONV_ROWS:(c + 1) * CONV_ROWS, :] = y.astype(BF16)
        return y

    @pl.when(g == 0)
    def _():
        for k in range(CONV_WIDTH):
            wb_ref[k] = jnp.broadcast_to(cw_ref[k:k + 1, :], (SUBLANES, d_conv))
        stage(u0_ref[0], jnp.zeros((HIST, d_conv), F32))
        for c in range(n_conv):
            conv_chunk(c, cb)

    def conv_slot(c, lead):
        if c == 0:
            tail = shift_ref[0, tm:tm + HIST, :]
            starts_seq = (g + 1) % tiles_per_seq == 0
            stage(un_ref[0], jnp.where(starts_seq, 0.0, tail))
        bias = jnp.concatenate([cb + _exact_zero(lead)] * (CONV_ROWS // SUBLANES), axis=0)
        zero = jnp.zeros((SUBLANES, LANES), F32)
        for i in range(c * n_conv // n_ff, (c + 1) * n_conv // n_ff):
            zero = zero + _exact_zero(conv_chunk(i, bias)[0:SUBLANES, 0:LANES])
        return zero

    co = co_ref[...]
    _outproj_ffn_body(x_ref, co, at_ref[0], g1_ref, sh2_ref, sc2_ref, g2_ref, n2g_ref,
                      wo_ref, wg_ref, wu_ref, wd_ref, y_ref, acc_ref, vpu_slot=conv_slot)


def _outproj_ffn_conv(x, at, u, mod4, mod_row0, n2g, wo, wg, wu, wd, conv_w, conv_b, ln_g, ln_b):
    b, s, d = x.shape
    tm = ROW_TILE
    d_conv = u.shape[2]
    d_attn = at.shape[2]
    tps = s // tm
    n_tiles = b * tps
    tile = lambda g: (g // tps, g % tps, 0)
    next_tile = lambda g: tile(jnp.minimum(g + 1, n_tiles - 1))
    mod_spec = lambda j: pl.BlockSpec((1, 1, 1, d), lambda g: (mod_row0 + g // tps, j, 0, 0))
    consts = (n2g, wo, wg, wu, wd, conv_w, conv_b, ln_g, ln_b)
    return pl.pallas_call(
        functools.partial(_outproj_ffn_conv_kernel, tiles_per_seq=tps),
        grid=(n_tiles,),
        in_specs=[pl.BlockSpec((1, tm, d), tile), pl.BlockSpec((1, tm, d_attn), tile),
                  pl.BlockSpec((1, tm, d_conv), next_tile),
                  pl.BlockSpec((1, tm, d_conv), lambda g: (0, 0, 0), pipeline_mode=pl.Buffered(1)),
                  mod_spec(2), mod_spec(3), mod_spec(4), mod_spec(5)] + [_const_spec(c.shape) for c in consts],
        out_specs=pl.BlockSpec((1, tm, d), tile),
        out_shape=jax.ShapeDtypeStruct((b, s, d), F32),
        scratch_shapes=[pltpu.VMEM((tm, d), F32),
                        pltpu.VMEM((SUBLANES, tm + HIST, d_conv), F32),
                        pltpu.VMEM((tm, d_conv), BF16),
                        pltpu.VMEM((CONV_WIDTH, SUBLANES, d_conv), F32)],
        compiler_params=pltpu.CompilerParams(dimension_semantics=("arbitrary",),
                                             vmem_limit_bytes=VMEM_LIMIT_BYTES),
        name="outproj_ffn_conv",
    )(x, at, u, u, mod4, mod4, mod4, mod4, *consts)


def _outproj_ffn(x, co, at, mod4, n2g, wo, wg, wu, wd, bb, tt):
    nb, s, d = x.shape
    d_conv = co.shape[2]
    d_attn = at.shape[2]
    mod_spec = lambda j: pl.BlockSpec((bb, 1, 1, d), lambda i, t: (i, j, 0, 0))
    row_spec = lambda w: pl.BlockSpec((bb, tt, w), lambda i, t: (i, t, 0))
    return pl.pallas_call(
        _outproj_ffn_kernel,
        grid=(nb // bb, s // tt),
        in_specs=[row_spec(d), row_spec(d_conv), row_spec(d_attn),
                  mod_spec(2), mod_spec(3), mod_spec(4), mod_spec(5),
                  _const_spec(n2g.shape), _const_spec(wo.shape), _const_spec(wg.shape),
                  _const_spec(wu.shape), _const_spec(wd.shape)],
        out_specs=row_spec(d),
        out_shape=jax.ShapeDtypeStruct((nb, s, d), F32),
        scratch_shapes=[pltpu.VMEM((bb * tt, d), F32)],
        compiler_params=pltpu.CompilerParams(dimension_semantics=("arbitrary", "arbitrary"),
                                             vmem_limit_bytes=VMEM_LIMIT_BYTES),
        name="outproj_ffn",
    )(x, co, at, mod4, mod4, mod4, mod4, n2g, wo, wg, wu, wd)


def _tri(n, rel):
    i = lax.broadcasted_iota(jnp.int32, (n, n), 0)
    j = lax.broadcasted_iota(jnp.int32, (n, n), 1)
    return rel(i, j).astype(BF16)


def _layer(xp, xs, cache_k, cache_v, cache_logf, state_conv, c_all, w_ada, b_ada, norm1_g, w_in, b_f,
           q_norm_g, k_norm_g, conv_w, conv_b, conv_ln_g, conv_ln_b, w_out, norm2_g, w_gate, w_up, w_down):
    b, s, d = xp.shape
    nb, tt, _ = xs.shape
    d_conv = conv_w.shape[1]
    d_attn = N_HEADS * HEAD_DIM
    d_ff = w_gate.shape[1]
    p_len = cache_k.shape[1]

    mod4 = _modulation(c_all, w_ada, b_ada).reshape(nb + b, 6, 1, d)

    wglu = w_in[:, :2 * d_conv].astype(BF16)
    wqkvf_t = jnp.pad(w_in[:, 2 * d_conv:].T, ((0, 2 * SUBLANES - N_HEADS), (0, 0))).astype(BF16)
    bf_col = b_f.reshape(N_HEADS, 1)
    qg_col = q_norm_g.reshape(d_attn, 1)
    kg_col = k_norm_g.reshape(d_attn, 1)
    row = lambda v: v.reshape(1, -1)
    n1g, n2g, cb, ln_g, ln_b = row(norm1_g), row(norm2_g), row(conv_b), row(conv_ln_g), row(conv_ln_b)
    wo = w_out.astype(BF16)
    assert d_ff % FF_CHUNK == 0
    wg, wu, wd = w_gate.astype(BF16), w_up.astype(BF16), w_down.astype(BF16)
    utri = _tri(MXU_DIM, lambda i, j: i <= j)
    ltri = _tri(MXU_DIM, lambda i, j: i > j)
    bdtri = _tri(nb * tt, lambda i, j: (i // tt == j // tt) & (j <= i))

    kt, vt, lft, cst, qt, qft, k, kb, u = _inproj_prompt(
        xp, mod4, nb, n1g, wglu, wqkvf_t, bf_col, qg_col, kg_col, utri)
    at = _attn_prompt(qt, qft, k, kb, vt)
    yp = _outproj_ffn_conv(xp, at, u, mod4, nb, n2g, wo, wg, wu, wd, conv_w, cb, ln_g, ln_b)
    k_p = kt.reshape(b, N_HEADS, HEAD_DIM, s).transpose(0, 3, 1, 2)
    v_p = vt.reshape(b, N_HEADS, HEAD_DIM, s).transpose(0, 3, 1, 2)
    lf_p = lft.transpose(0, 2, 1)

    state_t = state_conv.transpose(1, 0, 2)
    k_s, v_s, lf_s, nst, q_s, qb_s, kb_s, co_s = _inproj_sample(
        xs, mod4, n1g, wglu, wqkvf_t, bf_col, qg_col, kg_col, bdtri, conv_w, cb, ln_g, ln_b, state_t)
    r3 = lambda a: a.reshape(nb, tt, a.shape[-1])
    ckt = cache_k.transpose(0, 2, 3, 1).reshape(nb, d_attn, p_len)
    cvt = cache_v.transpose(0, 2, 3, 1).reshape(nb, d_attn, p_len)
    clft = cache_logf.transpose(0, 2, 1)
    at_s = _attn_sample(r3(q_s), r3(qb_s), r3(k_s), r3(v_s), r3(kb_s), ckt, cvt, clft, ltri)
    ys = _outproj_ffn(xs, r3(co_s), at_s, mod4, n2g, wo, wg, wu, wd, nb, tt)
    k_sn = k_s.reshape(nb, tt, N_HEADS, HEAD_DIM)
    v_sn = v_s.reshape(nb, tt, N_HEADS, HEAD_DIM)
    lf_sn = lf_s[:, :N_HEADS].reshape(nb, tt, N_HEADS)
    return yp, ys, (k_p, v_p, lf_p, cst), (k_sn, v_sn, lf_sn, nst.transpose(1, 0, 2))


def kernel(x_prompt, x_sample, cache_k, cache_v, cache_logf, state_conv, c_prompt, c_sample, w_ada, b_ada,
           norm1_g, w_in, b_f, q_norm_g, k_norm_g, conv_w, conv_b, conv_ln_g, conv_ln_b, w_out, norm2_g,
           w_gate, w_up, w_down):
    depth = w_ada.shape[0]
    c_all = jnp.concatenate([c_sample, c_prompt], axis=0)
    yp, ys = x_prompt, x_sample
    st_p, st_s = [], []
    for l in range(depth):
        yp, ys, sp, ss = _layer(
            yp, ys, cache_k[l], cache_v[l], cache_logf[l], state_conv[l], c_all, w_ada[l], b_ada[l],
            norm1_g[l], w_in[l], b_f[l], q_norm_g[l], k_norm_g[l], conv_w[l], conv_b[l], conv_ln_g[l],
            conv_ln_b[l], w_out[l], norm2_g[l], w_gate[l], w_up[l], w_down[l])
        st_p.append(sp)
        st_s.append(ss)
    stack = lambda xs: xs[0][None] if len(xs) == 1 else jnp.stack(xs)
    outs_p = [stack([s[i] for s in st_p]) for i in range(4)]
    outs_s = [stack([s[i] for s in st_s]) for i in range(4)]
    return (yp, ys, *outs_p, *outs_s)
```

```python
import functools

import jax
import jax.numpy as jnp
from jax import lax
from jax.experimental import pallas as pl
from jax.experimental.pallas import tpu as pltpu

F32 = jnp.float32
BF16 = jnp.bfloat16

N_HEADS = 8
HEAD_DIM = 64
CONV_WIDTH = 31
EPS = 1e-6
NEG_INF = -1e30

LANES = 128
SUBLANES = 8
MXU_DIM = 256
VMEM_LIMIT_BYTES = 56 * 1024 * 1024

N_PARTS = 3
N_FEAT = N_PARTS * N_HEADS
HIST = 32
CONV_ROWS = 32

ROW_TILE = 512
Q_TILE = 256
ATTN_LAG = 2
DENOM_ROWS = 16
LOG2E = 1.4426950408889634
FF_CHUNK = 256
SAMPLE_REQS_PER_STEP = 2


def _dot(a, b):
    return jnp.dot(a, b, preferred_element_type=F32)


def _dot_nt(a, b):
    return lax.dot_general(a, b, (((1,), (1,)), ((), ())), preferred_element_type=F32)


def _split3(x):
    hi = x.astype(BF16).astype(F32)
    r = x - hi
    mid = r.astype(BF16).astype(F32)
    lo = (r - mid).astype(BF16).astype(F32)
    return hi, mid, lo


def _log_sigmoid(x):
    return jnp.minimum(x, 0.0) - jnp.log1p(jnp.exp(-jnp.abs(x)))


def _adaln_rmsnorm(x, g, scale, shift):
    y = x * lax.rsqrt(jnp.mean(x * x, axis=-1, keepdims=True) + EPS)
    return (y * g) * (1.0 + scale) + shift


def _const_spec(shape):
    n = len(shape)
    return pl.BlockSpec(shape, lambda *_: (0,) * n, pipeline_mode=pl.Buffered(1))


def _modulation_kernel(c_ref, w_ref, b_ref, o_ref):
    c = c_ref[...]
    a = (c * jax.nn.sigmoid(c)).astype(BF16)
    o_ref[...] = _dot(a, w_ref[...].astype(BF16)) + b_ref[...]


def _modulation(c, w_ada, b_ada):
    nb, d = c.shape
    n = w_ada.shape[1]
    tn = 1536
    return pl.pallas_call(
        _modulation_kernel,
        grid=(n // tn,),
        in_specs=[pl.BlockSpec((nb, d), lambda j: (0, 0)),
                  pl.BlockSpec((d, tn), lambda j: (0, j)),
                  pl.BlockSpec((1, tn), lambda j: (0, j))],
        out_specs=pl.BlockSpec((nb, tn), lambda j: (0, j)),
        out_shape=jax.ShapeDtypeStruct((nb, n), F32),
        compiler_params=pltpu.CompilerParams(dimension_semantics=("arbitrary",),
                                             vmem_limit_bytes=VMEM_LIMIT_BYTES),
        name="modulation",
    )(c, w_ada, b_ada.reshape(1, n))


def _qkvf_feature_major(hb, wqkvf_t, qg_col, kg_col, bf_col):
    d_attn = N_HEADS * HEAD_DIM
    zt = _dot_nt(wqkvf_t, hb)
    r = zt.shape[1]

    def head_rms(z, g_col):
        z3 = z.reshape(N_HEADS, HEAD_DIM, r)
        ms = jnp.mean(z3 * z3, axis=1, keepdims=True)
        return (z3 * lax.rsqrt(ms + EPS)).reshape(d_attn, r) * g_col

    q_t = head_rms(zt[0:d_attn], qg_col)
    k_t = head_rms(zt[d_attn:2 * d_attn], kg_col)
    v_t = zt[2 * d_attn:3 * d_attn]
    lf_t = _log_sigmoid(zt[3 * d_attn:3 * d_attn + N_HEADS] + bf_col)
    return q_t, k_t, v_t, lf_t


def _conv_ln_swish(load_rows, weight, cb, ln_g, ln_b):
    acc = cb + weight(0) * load_rows(0)
    for k in range(1, CONV_WIDTH):
        acc = acc + weight(k) * load_rows(k)
    mu = jnp.mean(acc, axis=-1, keepdims=True)
    cen = acc - mu
    var = jnp.mean(cen * cen, axis=-1, keepdims=True)
    y = cen * lax.rsqrt(var + EPS) * ln_g + ln_b
    return y * jax.nn.sigmoid(y)


def _inproj_prompt_kernel(x_ref, sh_ref, sc_ref, n1g_ref, wglu_ref, wqkvf_ref, bf_ref, qg_ref, kg_ref,
                          utri_ref,
                          kt_ref, vt_ref, lft_ref, cst_ref, qt_ref, qft_ref, k_ref, kb_ref, u_ref,
                          carry_ref):
    t = pl.program_id(1)
    tm = x_ref.shape[1]
    d_conv = u_ref.shape[2]

    @pl.when(t == 0)
    def _():
        carry_ref[...] = jnp.zeros(carry_ref.shape, F32)

    hb = _adaln_rmsnorm(x_ref[0], n1g_ref[...], sc_ref[0, 0], sh_ref[0, 0]).astype(BF16)

    q_t, k_t, v_t, lf_t = _qkvf_feature_major(hb, wqkvf_ref[...], qg_ref[...], kg_ref[...], bf_ref[...])
    kt_ref[0] = k_t
    vt_ref[0] = v_t
    lft_ref[0] = lf_t

    zg = _dot(hb, wglu_ref[...])
    u_ref[0] = zg[:, :d_conv] * jax.nn.sigmoid(zg[:, d_conv:])

    nblk = tm // MXU_DIM
    parts = jnp.concatenate(_split3(lf_t), axis=0)
    stacked = jnp.concatenate([parts[:, i * MXU_DIM:(i + 1) * MXU_DIM] for i in range(nblk)], axis=0)
    local = _dot(stacked.astype(BF16), utri_ref[...])
    carry = carry_ref[:, 0:1]
    cums = []
    for i in range(nblk):
        loc = local[i * N_FEAT:(i + 1) * N_FEAT]
        cums.append(loc + carry)
        carry = carry + loc[:, MXU_DIM - 1:MXU_DIM]
    carry_ref[...] = jnp.broadcast_to(carry, carry_ref.shape)
    cum_parts = jnp.concatenate(cums, axis=1)
    cum_t = (cum_parts[0:8] + cum_parts[8:16] + cum_parts[16:24]) * LOG2E

    c_hi, c_mid, c_lo = _split3(cum_t)
    ones = jnp.ones((N_FEAT, tm), F32)
    zeros = jnp.zeros((LANES - 2 * N_FEAT, tm), F32)
    kfeat_t = jnp.concatenate([ones, -c_hi, -c_mid, -c_lo, zeros], axis=0)
    qft_ref[0] = jnp.concatenate([c_hi, c_mid, c_lo, ones, zeros], axis=0).astype(BF16)
    qt_ref[0] = (q_t * (HEAD_DIM ** -0.5 * LOG2E)).astype(BF16)
    k_ref[0] = k_t.T.astype(BF16)
    kb_ref[0] = kfeat_t.T.astype(BF16)

    @pl.when(t == pl.num_programs(1) - 1)
    def _():
        cst_ref[0] = u_ref[0, tm - (CONV_WIDTH - 1):tm, :]


def _inproj_prompt(x, mod4, mod_row0, n1g, wglu, wqkvf_t, bf_col, qg_col, kg_col, utri):
    b, s, d = x.shape
    tm = ROW_TILE
    d_conv = wglu.shape[1] // 2
    d_attn = N_HEADS * HEAD_DIM
    mod_spec = lambda j: pl.BlockSpec((1, 1, 1, d), lambda i, t: (mod_row0 + i, j, 0, 0))
    out_shape = (
        jax.ShapeDtypeStruct((b, d_attn, s), F32),
        jax.ShapeDtypeStruct((b, d_attn, s), F32),
        jax.ShapeDtypeStruct((b, N_HEADS, s), F32),
        jax.ShapeDtypeStruct((b, CONV_WIDTH - 1, d_conv), F32),
        jax.ShapeDtypeStruct((b, d_attn, s), BF16),
        jax.ShapeDtypeStruct((b, LANES, s), BF16),
        jax.ShapeDtypeStruct((b, s, d_attn), BF16),
        jax.ShapeDtypeStruct((b, s, LANES), BF16),
        jax.ShapeDtypeStruct((b, s, d_conv), F32),
    )
    out_specs = (
        pl.BlockSpec((1, d_attn, tm), lambda i, t: (i, 0, t)),
        pl.BlockSpec((1, d_attn, tm), lambda i, t: (i, 0, t)),
        pl.BlockSpec((1, N_HEADS, tm), lambda i, t: (i, 0, t)),
        pl.BlockSpec((1, CONV_WIDTH - 1, d_conv), lambda i, t: (i, 0, 0)),
        pl.BlockSpec((1, d_attn, tm), lambda i, t: (i, 0, t)),
        pl.BlockSpec((1, LANES, tm), lambda i, t: (i, 0, t)),
        pl.BlockSpec((1, tm, d_attn), lambda i, t: (i, t, 0)),
        pl.BlockSpec((1, tm, LANES), lambda i, t: (i, t, 0)),
        pl.BlockSpec((1, tm, d_conv), lambda i, t: (i, t, 0)),
    )
    in_specs = [
        pl.BlockSpec((1, tm, d), lambda i, t: (i, t, 0)),
        mod_spec(0), mod_spec(1),
        _const_spec(n1g.shape), _const_spec(wglu.shape), _const_spec(wqkvf_t.shape),
        _const_spec(bf_col.shape), _const_spec(qg_col.shape), _const_spec(kg_col.shape),
        _const_spec(utri.shape),
    ]
    return pl.pallas_call(
        _inproj_prompt_kernel,
        grid=(b, s // tm),
        in_specs=in_specs,
        out_specs=out_specs,
        out_shape=out_shape,
        scratch_shapes=[pltpu.VMEM((N_FEAT, LANES), F32)],
        compiler_params=pltpu.CompilerParams(dimension_semantics=("arbitrary", "arbitrary"),
                                             vmem_limit_bytes=VMEM_LIMIT_BYTES),
        name="inproj_prompt",
    )(x, mod4, mod4, n1g, wglu, wqkvf_t, bf_col, qg_col, kg_col, utri)


def _attn_prompt_kernel(qt_ref, qft_ref, k_ref, kb_ref, vt_ref, o_ref,
                        kp_ref, vp_ref, q2_ref, m_ref, acc_ref, sc_ref):
    qi = pl.program_id(1)
    tq = qt_ref.shape[2]
    n_pairs, nblk = vp_ref.shape[0], vp_ref.shape[1]

    @pl.when(qi == 0)
    def _():
        ones = jnp.ones((vp_ref.shape[2] - LANES, tq), BF16)
        for p in range(n_pairs):
            kp_ref[p, :, 0:LANES] = k_ref[0, :, p * LANES:(p + 1) * LANES]
            kp_ref[p, :, LANES:2 * LANES] = kb_ref[0]
            for i in range(nblk):
                vp_ref[p, i, 0:LANES, :] = vt_ref[0, p * LANES:(p + 1) * LANES, i * tq:(i + 1) * tq].astype(BF16)
                vp_ref[p, i, LANES:, :] = ones

    frow = lax.broadcasted_iota(jnp.int32, (LANES, tq), 0)
    feat = qft_ref[0].astype(F32)
    for p in range(n_pairs):
        qp = qt_ref[0, p * LANES:(p + 1) * LANES, :].astype(F32)
        cols = []
        for j in range(2):
            qm = jnp.where((frow >= j * HEAD_DIM) & (frow < (j + 1) * HEAD_DIM), qp, 0.0)
            fm = jnp.where((frow % N_HEADS) == 2 * p + j, feat, 0.0)
            cols.append(jnp.concatenate([qm, fm], axis=0))
        q2_ref[p] = jnp.concatenate(cols, axis=1).astype(BF16)

    m_ref[...] = jnp.full(m_ref.shape, NEG_INF, F32)
    acc_ref[...] = jnp.zeros(acc_ref.shape, F32)

    def scores(p, j):
        start = pl.multiple_of(j * tq, tq)
        return _dot(kp_ref[p, pl.ds(start, tq), :], q2_ref[p])

    def accumulate(p, j, masked, s):
        if masked:
            key = lax.broadcasted_iota(jnp.int32, s.shape, 0)
            qry = lax.broadcasted_iota(jnp.int32, s.shape, 1)
            qry = jnp.where(qry >= tq, qry - tq, qry)
            s = jnp.where(key <= qry, s, NEG_INF)
        m_old = m_ref[p]
        m_new = jnp.maximum(m_old, jnp.max(s, axis=0, keepdims=True))
        pr = jnp.exp2(s - m_new).astype(BF16)
        acc_ref[p] = jnp.exp2(m_old - m_new) * acc_ref[p] + _dot(vp_ref[p, j], pr)
        m_ref[p] = m_new

    def run_units(units, next_block):
        pending = {}
        for i in range(len(units) + ATTN_LAG):
            if i < ATTN_LAG:
                pending[i] = sc_ref[i]
            elif i < len(units):
                p, j, _ = units[i]
                pending[i] = scores(p, j)
            elif next_block is not None:
                sc_ref[i - len(units)] = scores(i - len(units), next_block)
            if i >= ATTN_LAG:
                p, j, masked = units[i - ATTN_LAG]
                accumulate(p, j, masked, pending.pop(i - ATTN_LAG))

    def block(j, masked):
        return [(p, j, masked) for p in range(n_pairs)]

    for p in range(ATTN_LAG):
        sc_ref[p] = scores(p, 0)

    def body(i, carry):
        run_units(block(2 * i, False) + block(2 * i + 1, False), 2 * i + 2)
        return carry

    lax.fori_loop(0, qi // 2, body, 0)

    @pl.when(qi % 2 == 1)
    def _():
        run_units(block(qi - 1, False) + block(qi, True), None)

    @pl.when(qi % 2 == 0)
    def _():
        run_units(block(qi, True), None)

    for p in range(n_pairs):
        acc = acc_ref[p]
        out_t = acc[0:LANES] / acc[LANES:LANES + 1]
        pair_t = jnp.where(frow < HEAD_DIM, out_t[:, :tq], out_t[:, tq:])
        o_ref[0, :, p * LANES:(p + 1) * LANES] = pair_t.T.astype(BF16)


def _attn_prompt(qt, qft, k, kb, vt):
    b, d_attn, s = qt.shape
    tq = Q_TILE
    nblk = s // tq
    n_pairs = d_attn // LANES
    return pl.pallas_call(
        _attn_prompt_kernel,
        grid=(b, nblk),
        in_specs=[
            pl.BlockSpec((1, d_attn, tq), lambda i, j: (i, 0, j)),
            pl.BlockSpec((1, LANES, tq), lambda i, j: (i, 0, j)),
            pl.BlockSpec((1, s, d_attn), lambda i, j: (i, 0, 0)),
            pl.BlockSpec((1, s, LANES), lambda i, j: (i, 0, 0)),
            pl.BlockSpec((1, d_attn, s), lambda i, j: (i, 0, 0)),
        ],
        out_specs=pl.BlockSpec((1, tq, d_attn), lambda i, j: (i, j, 0)),
        out_shape=jax.ShapeDtypeStruct((b, s, d_attn), BF16),
        scratch_shapes=[
            pltpu.VMEM((n_pairs, s, 2 * LANES), BF16),
            pltpu.VMEM((n_pairs, nblk, LANES + DENOM_ROWS, tq), BF16),
            pltpu.VMEM((n_pairs, 2 * LANES, 2 * tq), BF16),
            pltpu.VMEM((n_pairs, 1, 2 * tq), F32),
            pltpu.VMEM((n_pairs, LANES + DENOM_ROWS, 2 * tq), F32),
            pltpu.VMEM((ATTN_LAG, tq, 2 * tq), F32),
        ],
        compiler_params=pltpu.CompilerParams(dimension_semantics=("arbitrary", "arbitrary"),
                                             vmem_limit_bytes=VMEM_LIMIT_BYTES),
        name="attn_prompt",
    )(qt, qft, k, kb, vt)


def _inproj_sample_kernel(x_ref, sh_ref, sc_ref, n1g_ref, wglu_ref, wqkvf_ref, bf_ref, qg_ref, kg_ref,
                          bdtri_ref, cw_ref, cb_ref, lng_ref, lnb_ref, st_ref,
                          k_ref, v_ref, lf_ref, nst_ref, q_ref, qb_ref, kb_ref, co_ref,
                          u_ref, hist_ref, cof_ref):
    nb, tt, d = x_ref.shape
    r = nb * tt
    d_conv = co_ref.shape[1]
    n_state = CONV_WIDTH - 1

    h = _adaln_rmsnorm(x_ref[...], n1g_ref[...], sc_ref[:, 0], sh_ref[:, 0])
    hb = h.reshape(r, d).astype(BF16)

    n_chunks = d_conv // LANES
    zg = _dot(hb, wglu_ref[...])
    u = zg[:, :d_conv] * jax.nn.sigmoid(zg[:, d_conv:])
    for c in range(n_chunks):
        u_ref[c] = u[:, c * LANES:(c + 1) * LANES]

    q_t, k_t, v_t, lf_t = _qkvf_feature_major(hb, wqkvf_ref[...], qg_ref[...], kg_ref[...], bf_ref[...])
    k_ref[...] = k_t.T
    v_ref[...] = v_t.T
    q_ref[...] = (q_t * (HEAD_DIM ** -0.5)).T.astype(BF16)
    lf = jnp.concatenate([lf_t, jnp.zeros((LANES - N_HEADS, r), F32)], axis=0).T
    lf_ref[...] = lf

    lane = lax.broadcasted_iota(jnp.int32, (r, LANES), 1)
    hi, mid, lo = _split3(lf)
    packed = hi + pltpu.roll(mid, N_HEADS, 1) + pltpu.roll(lo, 2 * N_HEADS, 1)
    c = _dot(bdtri_ref[...], packed.astype(BF16))
    cn = jnp.where(lane < N_HEADS,
                   c + pltpu.roll(c, LANES - N_HEADS, 1) + pltpu.roll(c, LANES - 2 * N_HEADS, 1), 0.0)
    hi, mid, lo = _split3(cn)
    p = hi + pltpu.roll(mid, N_HEADS, 1) + pltpu.roll(lo, 2 * N_HEADS, 1)
    qb_ref[...] = (p + jnp.where((lane >= N_FEAT) & (lane < 2 * N_FEAT), 1.0, 0.0)).astype(BF16)
    kb_ref[...] = (jnp.where(lane < N_FEAT, 1.0, 0.0) - pltpu.roll(p, N_FEAT, 1)).astype(BF16)

    hist_ref[0:n_state] = st_ref[...]
    for t in range(tt):
        for c in range(n_chunks):
            hist_ref[n_state + t, :, c * LANES:(c + 1) * LANES] = u_ref[c, pl.ds(t, nb, stride=tt), :]
    nst_ref[...] = hist_ref[tt:tt + n_state]
    cb, ln_g, ln_b = cb_ref[...], lng_ref[...], lnb_ref[...]
    for t in range(tt):
        y = _conv_ln_swish(lambda k: hist_ref[t + k], lambda k: cw_ref[k:k + 1, :], cb, ln_g, ln_b)
        for c in range(n_chunks):
            cof_ref[c, pl.ds(t, nb, stride=tt), :] = y[:, c * LANES:(c + 1) * LANES]
    co_ref[...] = jnp.concatenate([cof_ref[c] for c in range(n_chunks)], axis=1).astype(BF16)


def _inproj_sample(x, mod4, n1g, wglu, wqkvf_t, bf_col, qg_col, kg_col, bdtri, conv_w, conv_b, ln_g, ln_b,
                   state_t):
    nb, tt, d = x.shape
    r = nb * tt
    d_conv = conv_w.shape[1]
    d_attn = N_HEADS * HEAD_DIM
    n_state = CONV_WIDTH - 1
    mod_spec = lambda j: pl.BlockSpec((nb, 1, 1, d), lambda i: (0, j, 0, 0))
    full = lambda shape: pl.BlockSpec(shape, lambda i: (0,) * len(shape))
    out_shape = (
        jax.ShapeDtypeStruct((r, d_attn), F32),
        jax.ShapeDtypeStruct((r, d_attn), F32),
        jax.ShapeDtypeStruct((r, LANES), F32),
        jax.ShapeDtypeStruct((n_state, nb, d_conv), F32),
        jax.ShapeDtypeStruct((r, d_attn), BF16),
        jax.ShapeDtypeStruct((r, LANES), BF16),
        jax.ShapeDtypeStruct((r, LANES), BF16),
        jax.ShapeDtypeStruct((r, d_conv), BF16),
    )
    args = (x, mod4, mod4, n1g, wglu, wqkvf_t, bf_col, qg_col, kg_col, bdtri, conv_w, conv_b, ln_g, ln_b,
            state_t)
    in_specs = [full(x.shape), mod_spec(0), mod_spec(1)] + [full(a.shape) for a in args[3:]]
    return pl.pallas_call(
        _inproj_sample_kernel,
        grid=(1,),
        in_specs=in_specs,
        out_specs=tuple(full(o.shape) for o in out_shape),
        out_shape=out_shape,
        scratch_shapes=[pltpu.VMEM((d_conv // LANES, r, LANES), F32),
                        pltpu.VMEM((n_state + tt, nb, d_conv), F32),
                        pltpu.VMEM((d_conv // LANES, r, LANES), F32)],
        compiler_params=pltpu.CompilerParams(dimension_semantics=("arbitrary",),
                                             vmem_limit_bytes=VMEM_LIMIT_BYTES),
        name="inproj_sample",
    )(*args)


def _attn_sample_kernel(*refs):
    for i in range(refs[0].shape[0]):
        _attn_sample_request(i, *refs)


def _attn_sample_request(i, q_ref, qb_ref, kn_ref, vn_ref, kbn_ref, kt_ref, vt_ref, clft_ref, ltri_ref, o_ref):
    tt = q_ref.shape[1]
    d_attn = q_ref.shape[2]
    p_len = kt_ref.shape[2]
    nblk = p_len // MXU_DIM
    rows = N_HEADS * tt

    parts = jnp.concatenate(_split3(clft_ref[i]), axis=0)
    blocks = [parts[:, blk * MXU_DIM:(blk + 1) * MXU_DIM] for blk in range(nblk)]
    local = _dot(jnp.concatenate(blocks, axis=0).astype(BF16), ltri_ref[...])
    off = jnp.zeros((N_FEAT, 1), F32)
    sufs = [None] * nblk
    for blk in reversed(range(nblk)):
        loc = local[blk * N_FEAT:(blk + 1) * N_FEAT]
        sufs[blk] = loc + off
        off = off + loc[:, 0:1] + blocks[blk][:, 0:1]
    suf_parts = jnp.concatenate(sufs, axis=1)
    ck_rel = suf_parts[0:8] + suf_parts[8:16] + suf_parts[16:24]
    c_hi, c_mid, c_lo = _split3(ck_rel)
    kbt = jnp.concatenate([jnp.ones((N_FEAT, p_len), F32), c_hi, c_mid, c_lo,
                           jnp.zeros((LANES - 2 * N_FEAT, p_len), F32)], axis=0)
    k_all = jnp.concatenate([kt_ref[i].astype(BF16), kbt.astype(BF16)], axis=0)

    lane_q = lax.broadcasted_iota(jnp.int32, (tt, d_attn), 1)
    lane_b = lax.broadcasted_iota(jnp.int32, (tt, LANES), 1)
    q = q_ref[i].astype(F32)
    qb = qb_ref[i].astype(F32)
    stack = []
    for h in range(N_HEADS):
        qm = jnp.where((lane_q >= h * HEAD_DIM) & (lane_q < (h + 1) * HEAD_DIM), q, 0.0)
        bm = jnp.where((lane_b % N_HEADS) == h, qb, 0.0)
        stack.append(jnp.concatenate([qm, bm], axis=1))
    qs = jnp.concatenate(stack, axis=0).astype(BF16)

    s_c = _dot(qs, k_all)

    pad = jnp.zeros((LANES - tt, d_attn + LANES), F32)
    kn = jnp.concatenate([jnp.concatenate([kn_ref[i], kbn_ref[i].astype(F32)], axis=1), pad], axis=0)
    s_n = _dot_nt(qs, kn.astype(BF16))
    row = lax.broadcasted_iota(jnp.int32, s_n.shape, 0)
    col = lax.broadcasted_iota(jnp.int32, s_n.shape, 1)
    s_n = jnp.where(col <= (row % tt), s_n, NEG_INF)

    m = jnp.maximum(jnp.max(s_c, axis=-1, keepdims=True), jnp.max(s_n, axis=-1, keepdims=True))
    p_c = jnp.exp(s_c - m)
    p_n = jnp.exp(s_n - m)
    l = jnp.sum(p_c, axis=-1, keepdims=True) + jnp.sum(p_n, axis=-1, keepdims=True)
    vn = jnp.concatenate([vn_ref[i], jnp.zeros((LANES - tt, d_attn), F32)], axis=0).astype(BF16)
    o = _dot_nt(p_c.astype(BF16), vt_ref[i].astype(BF16)) + _dot(p_n.astype(BF16), vn)
    o = o / l

    out = jnp.zeros((tt, d_attn), F32)
    for h in range(N_HEADS):
        out = jnp.where((lane_q >= h * HEAD_DIM) & (lane_q < (h + 1) * HEAD_DIM), o[h * tt:(h + 1) * tt], out)
    o_ref[i] = out.astype(BF16)


def _attn_sample(q, qb, kn, vn, kbn, kt, vt, clft, ltri):
    nb, tt, d_attn = q.shape
    p_len = kt.shape[2]
    per_req = lambda shape: pl.BlockSpec((SAMPLE_REQS_PER_STEP,) + shape, lambda i: (i, 0, 0))
    return pl.pallas_call(
        _attn_sample_kernel,
        grid=(nb // SAMPLE_REQS_PER_STEP,),
        in_specs=[per_req((tt, d_attn)), per_req((tt, LANES)), per_req((tt, d_attn)), per_req((tt, d_attn)),
                  per_req((tt, LANES)), per_req((d_attn, p_len)), per_req((d_attn, p_len)),
                  per_req((N_HEADS, p_len)), _const_spec(ltri.shape)],
        out_specs=per_req((tt, d_attn)),
        out_shape=jax.ShapeDtypeStruct((nb, tt, d_attn), BF16),
        compiler_params=pltpu.CompilerParams(dimension_semantics=("arbitrary",),
                                             vmem_limit_bytes=VMEM_LIMIT_BYTES),
        name="attn_sample",
    )(q, qb, kn, vn, kbn, kt, vt, clft, ltri)


def _outproj_ffn_body(x_ref, co, at, g1_ref, sh2_ref, sc2_ref, g2_ref, n2g_ref,
                      wo_ref, wg_ref, wu_ref, wd_ref, y_ref, a_ref, vpu_slot=None):
    nb, tt, d = x_ref.shape
    r = nb * tt
    d_conv = co.shape[1]
    mix = _dot(co, wo_ref[0:d_conv, :]) + _dot(at, wo_ref[d_conv:, :])
    x1 = x_ref[...] + g1_ref[:, 0] * mix.reshape(nb, tt, d)
    hb = _adaln_rmsnorm(x1, n2g_ref[...], sc2_ref[:, 0], sh2_ref[:, 0]).reshape(r, d).astype(BF16)
    lead = x1.reshape(r, d)[0:SUBLANES, 0:d_conv]
    n_ff = wg_ref.shape[1] // FF_CHUNK
    chunk = lambda c: slice(c * FF_CHUNK, (c + 1) * FF_CHUNK)
    for c in range(n_ff):
        zero = vpu_slot(c, lead) if vpu_slot is not None else None
        g, u = _dot(hb, wg_ref[:, chunk(c)]), _dot(hb, wu_ref[:, chunk(c)])
        if zero is not None:
            zero = jnp.concatenate([zero] * (FF_CHUNK // LANES), axis=1)
            g = jnp.concatenate([g[:SUBLANES] + zero, g[SUBLANES:]], axis=0)
        a = (g * jax.nn.sigmoid(g)) * u
        a_ref[:, chunk(c)] = a.astype(BF16)
        lead = jnp.concatenate([a[0:SUBLANES]] * (d_conv // FF_CHUNK), axis=1)
    ffn = _dot(a_ref[...], wd_ref[...])
    y_ref[...] = x1 + g2_ref[:, 0] * ffn.reshape(nb, tt, d)


def _outproj_ffn_kernel(x_ref, co_ref, at_ref, g1_ref, sh2_ref, sc2_ref, g2_ref, n2g_ref,
                        wo_ref, wg_ref, wu_ref, wd_ref, y_ref, a_ref):
    nb, tt, _ = x_ref.shape
    co = co_ref[...].reshape(nb * tt, co_ref.shape[2])
    at = at_ref[...].reshape(nb * tt, at_ref.shape[2])
    _outproj_ffn_body(x_ref, co, at, g1_ref, sh2_ref, sc2_ref, g2_ref, n2g_ref,
                      wo_ref, wg_ref, wu_ref, wd_ref, y_ref, a_ref)


def _exact_zero(v):
    return jnp.minimum(jnp.abs(v), 0.0)


def _outproj_ffn_conv_kernel(x_ref, at_ref, un_ref, u0_ref, g1_ref, sh2_ref, sc2_ref, g2_ref, n2g_ref,
                             wo_ref, wg_ref, wu_ref, wd_ref, cw_ref, cb_ref, lng_ref, lnb_ref,
                             y_ref, a_ref, shift_ref, co_ref, wb_ref, *, tiles_per_seq):
    g = pl.program_id(0)
    tm = x_ref.shape[1]
    d_conv = co_ref.shape[1]
    span = tm + HIST - SUBLANES
    n_conv = tm // CONV_ROWS
    n_ff = wg_ref.shape[1] // FF_CHUNK
    cb, ln_g, ln_b = cb_ref[...], lng_ref[...], lnb_ref[...]

    def stage(u, hist):
        shift_ref[0, 0:HIST, :] = hist
        shift_ref[0, HIST:HIST + tm, :] = u
        for r in range(1, SUBLANES):
            shift_ref[r, 0:span, :] = shift_ref[0, r:r + span, :]

    def conv_chunk(c, bias):
        def tap(k):
            off = k + HIST - (CONV_WIDTH - 1)
            row = c * CONV_ROWS + off - off % SUBLANES
            return shift_ref[off % SUBLANES, row:row + CONV_ROWS, :]
        weight = lambda k: jnp.concatenate([wb_ref[k]] * (CONV_ROWS // SUBLANES), axis=0)
        y = _conv_ln_swish(tap, weight, bias, ln_g, ln_b)
        co_ref[c * CONV_ROWS:(c + 1) * CONV_ROWS, :] = y.astype(BF16)
        return y

    @pl.when(g == 0)
    def _():
        for k in range(CONV_WIDTH):
            wb_ref[k] = jnp.broadcast_to(cw_ref[k:k + 1, :], (SUBLANES, d_conv))
        stage(u0_ref[0], jnp.zeros((HIST, d_conv), F32))
        for c in range(n_conv):
            conv_chunk(c, cb)

    def conv_slot(c, lead):
        if c == 0:
            tail = shift_ref[0, tm:tm + HIST, :]
            starts_seq = (g + 1) % tiles_per_seq == 0
            stage(un_ref[0], jnp.where(starts_seq, 0.0, tail))
        bias = jnp.concatenate([cb + _exact_zero(lead)] * (CONV_ROWS // SUBLANES), axis=0)
        zero = jnp.zeros((SUBLANES, LANES), F32)
        for i in range(c * n_conv // n_ff, (c + 1) * n_conv // n_ff):
            zero = zero + _exact_zero(conv_chunk(i, bias)[0:SUBLANES, 0:LANES])
        return zero

    co = co_ref[...]
    _outproj_ffn_body(x_ref, co, at_ref[0], g1_ref, sh2_ref, sc2_ref, g2_ref, n2g_ref,
                      wo_ref, wg_ref, wu_ref, wd_ref, y_ref, a_ref, vpu_slot=conv_slot)


def _outproj_ffn_conv(x, at, u, mod4, mod_row0, n2g, wo, wg, wu, wd, conv_w, conv_b, ln_g, ln_b):
    b, s, d = x.shape
    tm = ROW_TILE
    d_conv = u.shape[2]
    d_attn = at.shape[2]
    tps = s // tm
    n_tiles = b * tps
    tile = lambda g: (g // tps, g % tps, 0)
    next_tile = lambda g: tile(jnp.minimum(g + 1, n_tiles - 1))
    mod_spec = lambda j: pl.BlockSpec((1, 1, 1, d), lambda g: (mod_row0 + g // tps, j, 0, 0))
    consts = (n2g, wo, wg, wu, wd, conv_w, conv_b, ln_g, ln_b)
    return pl.pallas_call(
        functools.partial(_outproj_ffn_conv_kernel, tiles_per_seq=tps),
        grid=(n_tiles,),
        in_specs=[pl.BlockSpec((1, tm, d), tile), pl.BlockSpec((1, tm, d_attn), tile),
                  pl.BlockSpec((1, tm, d_conv), next_tile),
                  pl.BlockSpec((1, tm, d_conv), lambda g: (0, 0, 0), pipeline_mode=pl.Buffered(1)),
                  mod_spec(2), mod_spec(3), mod_spec(4), mod_spec(5)] + [_const_spec(c.shape) for c in consts],
        out_specs=pl.BlockSpec((1, tm, d), tile),
        out_shape=jax.ShapeDtypeStruct((b, s, d), F32),
        scratch_shapes=[pltpu.VMEM((tm, wg.shape[1]), BF16),
                        pltpu.VMEM((SUBLANES, tm + HIST, d_conv), F32),
                        pltpu.VMEM((tm, d_conv), BF16),
                        pltpu.VMEM((CONV_WIDTH, SUBLANES, d_conv), F32)],
        compiler_params=pltpu.CompilerParams(dimension_semantics=("arbitrary",),
                                             vmem_limit_bytes=VMEM_LIMIT_BYTES),
        name="outproj_ffn_conv",
    )(x, at, u, u, mod4, mod4, mod4, mod4, *consts)


def _outproj_ffn(x, co, at, mod4, n2g, wo, wg, wu, wd, bb, tt):
    nb, s, d = x.shape
    d_conv = co.shape[2]
    d_attn = at.shape[2]
    mod_spec = lambda j: pl.BlockSpec((bb, 1, 1, d), lambda i, t: (i, j, 0, 0))
    row_spec = lambda w: pl.BlockSpec((bb, tt, w), lambda i, t: (i, t, 0))
    return pl.pallas_call(
        _outproj_ffn_kernel,
        grid=(nb // bb, s // tt),
        in_specs=[row_spec(d), row_spec(d_conv), row_spec(d_attn),
                  mod_spec(2), mod_spec(3), mod_spec(4), mod_spec(5),
                  _const_spec(n2g.shape), _const_spec(wo.shape), _const_spec(wg.shape),
                  _const_spec(wu.shape), _const_spec(wd.shape)],
        out_specs=row_spec(d),
        out_shape=jax.ShapeDtypeStruct((nb, s, d), F32),
        scratch_shapes=[pltpu.VMEM((bb * tt, wg.shape[1]), BF16)],
        compiler_params=pltpu.CompilerParams(dimension_semantics=("arbitrary", "arbitrary"),
                                             vmem_limit_bytes=VMEM_LIMIT_BYTES),
        name="outproj_ffn",
    )(x, co, at, mod4, mod4, mod4, mod4, n2g, wo, wg, wu, wd)


def _tri(n, rel):
    i = lax.broadcasted_iota(jnp.int32, (n, n), 0)
    j = lax.broadcasted_iota(jnp.int32, (n, n), 1)
    return rel(i, j).astype(BF16)


def _layer(xp, xs, cache_k, cache_v, cache_logf, state_conv, c_all, w_ada, b_ada, norm1_g, w_in, b_f,
           q_norm_g, k_norm_g, conv_w, conv_b, conv_ln_g, conv_ln_b, w_out, norm2_g, w_gate, w_up, w_down):
    b, s, d = xp.shape
    nb, tt, _ = xs.shape
    d_conv = conv_w.shape[1]
    d_attn = N_HEADS * HEAD_DIM
    d_ff = w_gate.shape[1]
    p_len = cache_k.shape[1]

    mod4 = _modulation(c_all, w_ada, b_ada).reshape(nb + b, 6, 1, d)

    wglu = w_in[:, :2 * d_conv].astype(BF16)
    wqkvf_t = jnp.pad(w_in[:, 2 * d_conv:].T, ((0, 2 * SUBLANES - N_HEADS), (0, 0))).astype(BF16)
    bf_col = b_f.reshape(N_HEADS, 1)
    qg_col = q_norm_g.reshape(d_attn, 1)
    kg_col = k_norm_g.reshape(d_attn, 1)
    row = lambda v: v.reshape(1, -1)
    n1g, n2g, cb, ln_g, ln_b = row(norm1_g), row(norm2_g), row(conv_b), row(conv_ln_g), row(conv_ln_b)
    wo = w_out.astype(BF16)
    assert d_ff % FF_CHUNK == 0
    wg, wu, wd = w_gate.astype(BF16), w_up.astype(BF16), w_down.astype(BF16)
    utri = _tri(MXU_DIM, lambda i, j: i <= j)
    ltri = _tri(MXU_DIM, lambda i, j: i > j)
    bdtri = _tri(nb * tt, lambda i, j: (i // tt == j // tt) & (j <= i))

    kt, vt, lft, cst, qt, qft, k, kb, u = _inproj_prompt(
        xp, mod4, nb, n1g, wglu, wqkvf_t, bf_col, qg_col, kg_col, utri)
    at = _attn_prompt(qt, qft, k, kb, vt)
    yp = _outproj_ffn_conv(xp, at, u, mod4, nb, n2g, wo, wg, wu, wd, conv_w, cb, ln_g, ln_b)
    k_p = kt.reshape(b, N_HEADS, HEAD_DIM, s).transpose(0, 3, 1, 2)
    v_p = vt.reshape(b, N_HEADS, HEAD_DIM, s).transpose(0, 3, 1, 2)
    lf_p = lft.transpose(0, 2, 1)

    state_t = state_conv.transpose(1, 0, 2)
    k_s, v_s, lf_s, nst, q_s, qb_s, kb_s, co_s = _inproj_sample(
        xs, mod4, n1g, wglu, wqkvf_t, bf_col, qg_col, kg_col, bdtri, conv_w, cb, ln_g, ln_b, state_t)
    r3 = lambda a: a.reshape(nb, tt, a.shape[-1])
    ckt = cache_k.transpose(0, 2, 3, 1).reshape(nb, d_attn, p_len)
    cvt = cache_v.transpose(0, 2, 3, 1).reshape(nb, d_attn, p_len)
    clft = cache_logf.transpose(0, 2, 1)
    at_s = _attn_sample(r3(q_s), r3(qb_s), r3(k_s), r3(v_s), r3(kb_s), ckt, cvt, clft, ltri)
    ys = _outproj_ffn(xs, r3(co_s), at_s, mod4, n2g, wo, wg, wu, wd, nb, tt)
    k_sn = k_s.reshape(nb, tt, N_HEADS, HEAD_DIM)
    v_sn = v_s.reshape(nb, tt, N_HEADS, HEAD_DIM)
    lf_sn = lf_s[:, :N_HEADS].reshape(nb, tt, N_HEADS)
    return yp, ys, (k_p, v_p, lf_p, cst), (k_sn, v_sn, lf_sn, nst.transpose(1, 0, 2))


def kernel(x_prompt, x_sample, cache_k, cache_v, cache_logf, state_conv, c_prompt, c_sample, w_ada, b_ada,
           norm1_g, w_in, b_f, q_norm_g, k_norm_g, conv_w, conv_b, conv_ln_g, conv_ln_b, w_out, norm2_g,
           w_gate, w_up, w_down):
    depth = w_ada.shape[0]
    c_all = jnp.concatenate([c_sample, c_prompt], axis=0)
    yp, ys = x_prompt, x_sample
    st_p, st_s = [], []
    for l in range(depth):
        yp, ys, sp, ss = _layer(
            yp, ys, cache_k[l], cache_v[l], cache_logf[l], state_conv[l], c_all, w_ada[l], b_ada[l],
            norm1_g[l], w_in[l], b_f[l], q_norm_g[l], k_norm_g[l], conv_w[l], conv_b[l], conv_ln_g[l],
            conv_ln_b[l], w_out[l], norm2_g[l], w_gate[l], w_up[l], w_down[l])
        st_p.append(sp)
        st_s.append(ss)
    stack = lambda xs: xs[0][None] if len(xs) == 1 else jnp.stack(xs)
    outs_p = [stack([s[i] for s in st_p]) for i in range(4)]
    outs_s = [stack([s[i] for s in st_s]) for i in range(4)]
    return (yp, ys, *outs_p, *outs_s)
```

```python
import functools

import jax
import jax.numpy as jnp
from jax import lax
from jax.experimental import pallas as pl
from jax.experimental.pallas import tpu as pltpu

F32 = jnp.float32
BF16 = jnp.bfloat16

N_HEADS = 8
HEAD_DIM = 64
CONV_WIDTH = 31
EPS = 1e-6
NEG_INF = -1e30

LANES = 128
SUBLANES = 8
MXU_DIM = 256
VMEM_LIMIT_BYTES = 56 * 1024 * 1024

N_PARTS = 3
N_FEAT = N_PARTS * N_HEADS
HIST = 32
CONV_ROWS = 32

ROW_TILE = 512
Q_TILE = 256
ATTN_LAG = 2
DENOM_ROWS = 16
LOG2E = 1.4426950408889634
FF_CHUNK = 256
SAMPLE_REQS_PER_STEP = 4
N_MOD = 6
MOD_TILE = 1536


def _dot(a, b):
    return jnp.dot(a, b, preferred_element_type=F32)


def _dot_nt(a, b):
    return lax.dot_general(a, b, (((1,), (1,)), ((), ())), preferred_element_type=F32)


def _split3(x):
    hi = x.astype(BF16).astype(F32)
    r = x - hi
    mid = r.astype(BF16).astype(F32)
    lo = (r - mid).astype(BF16).astype(F32)
    return hi, mid, lo


def _log_sigmoid(x):
    return jnp.minimum(x, 0.0) - jnp.log1p(jnp.exp(-jnp.abs(x)))


def _adaln_rmsnorm(x, g, scale, shift):
    y = x * lax.rsqrt(jnp.mean(x * x, axis=-1, keepdims=True) + EPS)
    return (y * g) * (1.0 + scale) + shift


def _const_spec(shape):
    n = len(shape)
    return pl.BlockSpec(shape, lambda *_: (0,) * n, pipeline_mode=pl.Buffered(1))


def _modulation_kernel(c_ref, w_ref, b_ref, o_ref):
    c = c_ref[...]
    a = (c * jax.nn.sigmoid(c)).astype(BF16)
    o_ref[...] = _dot(a, w_ref[...].astype(BF16)) + b_ref[...]


def _modulation(c, w_ada, b_ada):
    nb, d = c.shape
    n = w_ada.shape[1]
    tn = MOD_TILE
    return pl.pallas_call(
        _modulation_kernel,
        grid=(n // tn,),
        in_specs=[pl.BlockSpec((nb, d), lambda j: (0, 0)),
                  pl.BlockSpec((d, tn), lambda j: (0, j)),
                  pl.BlockSpec((1, tn), lambda j: (0, j))],
        out_specs=pl.BlockSpec((nb, tn), lambda j: (0, j)),
        out_shape=jax.ShapeDtypeStruct((nb, n), F32),
        compiler_params=pltpu.CompilerParams(dimension_semantics=("arbitrary",),
                                             vmem_limit_bytes=VMEM_LIMIT_BYTES),
        name="modulation",
    )(c, w_ada, b_ada.reshape(1, n))


def _qkvf_feature_major(hb, wqkvf_t, qg_col, kg_col, bf_col):
    d_attn = N_HEADS * HEAD_DIM
    zt = _dot_nt(wqkvf_t, hb)
    r = zt.shape[1]

    def head_rms(z, g_col):
        z3 = z.reshape(N_HEADS, HEAD_DIM, r)
        ms = jnp.mean(z3 * z3, axis=1, keepdims=True)
        return (z3 * lax.rsqrt(ms + EPS)).reshape(d_attn, r) * g_col

    q_t = head_rms(zt[0:d_attn], qg_col)
    k_t = head_rms(zt[d_attn:2 * d_attn], kg_col)
    v_t = zt[2 * d_attn:3 * d_attn]
    lf_t = _log_sigmoid(zt[3 * d_attn:3 * d_attn + N_HEADS] + bf_col)
    return q_t, k_t, v_t, lf_t


def _conv_ln_swish(load_rows, weight, cb, ln_g, ln_b):
    acc = cb + weight(0) * load_rows(0)
    for k in range(1, CONV_WIDTH):
        acc = acc + weight(k) * load_rows(k)
    mu = jnp.mean(acc, axis=-1, keepdims=True)
    cen = acc - mu
    var = jnp.mean(cen * cen, axis=-1, keepdims=True)
    y = cen * lax.rsqrt(var + EPS) * ln_g + ln_b
    return y * jax.nn.sigmoid(y)


def _inproj_prompt_kernel(x_ref, sh_ref, sc_ref, n1g_ref, wglu_ref, wqkvf_ref, bf_ref, qg_ref, kg_ref,
                          utri_ref,
                          kt_ref, vt_ref, lft_ref, cst_ref, qt_ref, qft_ref, k_ref, kb_ref, u_ref,
                          carry_ref):
    t = pl.program_id(1)
    tm = x_ref.shape[1]
    d_conv = u_ref.shape[2]

    @pl.when(t == 0)
    def _():
        carry_ref[...] = jnp.zeros(carry_ref.shape, F32)

    hb = _adaln_rmsnorm(x_ref[0], n1g_ref[...], sc_ref[0, 0], sh_ref[0, 0]).astype(BF16)

    q_t, k_t, v_t, lf_t = _qkvf_feature_major(hb, wqkvf_ref[...], qg_ref[...], kg_ref[...], bf_ref[...])
    kt_ref[0] = k_t
    vt_ref[0] = v_t
    lft_ref[0] = lf_t

    zg = _dot(hb, wglu_ref[...])
    u_ref[0] = zg[:, :d_conv] * jax.nn.sigmoid(zg[:, d_conv:])

    nblk = tm // MXU_DIM
    parts = jnp.concatenate(_split3(lf_t), axis=0)
    stacked = jnp.concatenate([parts[:, i * MXU_DIM:(i + 1) * MXU_DIM] for i in range(nblk)], axis=0)
    local = _dot(stacked.astype(BF16), utri_ref[...])
    carry = carry_ref[:, 0:1]
    cums = []
    for i in range(nblk):
        loc = local[i * N_FEAT:(i + 1) * N_FEAT]
        cums.append(loc + carry)
        carry = carry + loc[:, MXU_DIM - 1:MXU_DIM]
    carry_ref[...] = jnp.broadcast_to(carry, carry_ref.shape)
    cum_parts = jnp.concatenate(cums, axis=1)
    cum_t = (cum_parts[0:8] + cum_parts[8:16] + cum_parts[16:24]) * LOG2E

    c_hi, c_mid, c_lo = _split3(cum_t)
    ones = jnp.ones((N_FEAT, tm), F32)
    zeros = jnp.zeros((LANES - 2 * N_FEAT, tm), F32)
    kfeat_t = jnp.concatenate([ones, -c_hi, -c_mid, -c_lo, zeros], axis=0)
    qft_ref[0] = jnp.concatenate([c_hi, c_mid, c_lo, ones, zeros], axis=0).astype(BF16)
    qt_ref[0] = (q_t * (HEAD_DIM ** -0.5 * LOG2E)).astype(BF16)
    k_ref[0] = k_t.T.astype(BF16)
    kb_ref[0] = kfeat_t.T.astype(BF16)

    @pl.when(t == pl.num_programs(1) - 1)
    def _():
        cst_ref[0] = u_ref[0, tm - (CONV_WIDTH - 1):tm, :]


def _inproj_prompt(x, mod4, mod_row0, n1g, wglu, wqkvf_t, bf_col, qg_col, kg_col, utri):
    b, s, d = x.shape
    tm = ROW_TILE
    d_conv = wglu.shape[1] // 2
    d_attn = N_HEADS * HEAD_DIM
    mod_spec = lambda j: pl.BlockSpec((1, 1, 1, d), lambda i, t: (mod_row0 + i, j, 0, 0))
    out_shape = (
        jax.ShapeDtypeStruct((b, d_attn, s), F32),
        jax.ShapeDtypeStruct((b, d_attn, s), F32),
        jax.ShapeDtypeStruct((b, N_HEADS, s), F32),
        jax.ShapeDtypeStruct((b, CONV_WIDTH - 1, d_conv), F32),
        jax.ShapeDtypeStruct((b, d_attn, s), BF16),
        jax.ShapeDtypeStruct((b, LANES, s), BF16),
        jax.ShapeDtypeStruct((b, s, d_attn), BF16),
        jax.ShapeDtypeStruct((b, s, LANES), BF16),
        jax.ShapeDtypeStruct((b, s, d_conv), F32),
    )
    out_specs = (
        pl.BlockSpec((1, d_attn, tm), lambda i, t: (i, 0, t)),
        pl.BlockSpec((1, d_attn, tm), lambda i, t: (i, 0, t)),
        pl.BlockSpec((1, N_HEADS, tm), lambda i, t: (i, 0, t)),
        pl.BlockSpec((1, CONV_WIDTH - 1, d_conv), lambda i, t: (i, 0, 0)),
        pl.BlockSpec((1, d_attn, tm), lambda i, t: (i, 0, t)),
        pl.BlockSpec((1, LANES, tm), lambda i, t: (i, 0, t)),
        pl.BlockSpec((1, tm, d_attn), lambda i, t: (i, t, 0)),
        pl.BlockSpec((1, tm, LANES), lambda i, t: (i, t, 0)),
        pl.BlockSpec((1, tm, d_conv), lambda i, t: (i, t, 0)),
    )
    in_specs = [
        pl.BlockSpec((1, tm, d), lambda i, t: (i, t, 0)),
        mod_spec(0), mod_spec(1),
        _const_spec(n1g.shape), _const_spec(wglu.shape), _const_spec(wqkvf_t.shape),
        _const_spec(bf_col.shape), _const_spec(qg_col.shape), _const_spec(kg_col.shape),
        _const_spec(utri.shape),
    ]
    return pl.pallas_call(
        _inproj_prompt_kernel,
        grid=(b, s // tm),
        in_specs=in_specs,
        out_specs=out_specs,
        out_shape=out_shape,
        scratch_shapes=[pltpu.VMEM((N_FEAT, LANES), F32)],
        compiler_params=pltpu.CompilerParams(dimension_semantics=("arbitrary", "arbitrary"),
                                             vmem_limit_bytes=VMEM_LIMIT_BYTES),
        name="inproj_prompt",
    )(x, mod4, mod4, n1g, wglu, wqkvf_t, bf_col, qg_col, kg_col, utri)


def _attn_prompt_kernel(qt_ref, qft_ref, k_ref, kb_ref, vt_ref, o_ref,
                        kp_ref, vp_ref, q2_ref, m_ref, acc_ref, sc_ref):
    qi = pl.program_id(1)
    tq = qt_ref.shape[2]
    n_pairs, nblk = vp_ref.shape[0], vp_ref.shape[1]

    @pl.when(qi == 0)
    def _():
        ones = jnp.ones((vp_ref.shape[2] - LANES, tq), BF16)
        for p in range(n_pairs):
            kp_ref[p, :, 0:LANES] = k_ref[0, :, p * LANES:(p + 1) * LANES]
            kp_ref[p, :, LANES:2 * LANES] = kb_ref[0]
            for i in range(nblk):
                vp_ref[p, i, 0:LANES, :] = vt_ref[0, p * LANES:(p + 1) * LANES, i * tq:(i + 1) * tq].astype(BF16)
                vp_ref[p, i, LANES:, :] = ones

    frow = lax.broadcasted_iota(jnp.int32, (LANES, tq), 0)
    feat = qft_ref[0].astype(F32)
    for p in range(n_pairs):
        qp = qt_ref[0, p * LANES:(p + 1) * LANES, :].astype(F32)
        cols = []
        for j in range(2):
            qm = jnp.where((frow >= j * HEAD_DIM) & (frow < (j + 1) * HEAD_DIM), qp, 0.0)
            fm = jnp.where((frow % N_HEADS) == 2 * p + j, feat, 0.0)
            cols.append(jnp.concatenate([qm, fm], axis=0))
        q2_ref[p] = jnp.concatenate(cols, axis=1).astype(BF16)

    m_ref[...] = jnp.full(m_ref.shape, NEG_INF, F32)
    acc_ref[...] = jnp.zeros(acc_ref.shape, F32)

    def scores(p, j):
        start = pl.multiple_of(j * tq, tq)
        return _dot(kp_ref[p, pl.ds(start, tq), :], q2_ref[p])

    def accumulate(p, j, masked, s):
        if masked:
            key = lax.broadcasted_iota(jnp.int32, s.shape, 0)
            qry = lax.broadcasted_iota(jnp.int32, s.shape, 1)
            qry = jnp.where(qry >= tq, qry - tq, qry)
            s = jnp.where(key <= qry, s, NEG_INF)
        m_old = m_ref[p]
        m_new = jnp.maximum(m_old, jnp.max(s, axis=0, keepdims=True))
        pr = jnp.exp2(s - m_new).astype(BF16)
        acc_ref[p] = jnp.exp2(m_old - m_new) * acc_ref[p] + _dot(vp_ref[p, j], pr)
        m_ref[p] = m_new

    def run_units(units, next_block):
        pending = {}
        for i in range(len(units) + ATTN_LAG):
            if i < ATTN_LAG:
                pending[i] = sc_ref[i]
            elif i < len(units):
                p, j, _ = units[i]
                pending[i] = scores(p, j)
            elif next_block is not None:
                sc_ref[i - len(units)] = scores(i - len(units), next_block)
            if i >= ATTN_LAG:
                p, j, masked = units[i - ATTN_LAG]
                accumulate(p, j, masked, pending.pop(i - ATTN_LAG))

    def block(j, masked):
        return [(p, j, masked) for p in range(n_pairs)]

    for p in range(ATTN_LAG):
        sc_ref[p] = scores(p, 0)

    def body(i, carry):
        run_units(block(2 * i, False) + block(2 * i + 1, False), 2 * i + 2)
        return carry

    lax.fori_loop(0, qi // 2, body, 0)

    @pl.when(qi % 2 == 1)
    def _():
        run_units(block(qi - 1, False) + block(qi, True), None)

    @pl.when(qi % 2 == 0)
    def _():
        run_units(block(qi, True), None)

    for p in range(n_pairs):
        acc = acc_ref[p]
        out_t = acc[0:LANES] / acc[LANES:LANES + 1]
        pair_t = jnp.where(frow < HEAD_DIM, out_t[:, :tq], out_t[:, tq:])
        o_ref[0, :, p * LANES:(p + 1) * LANES] = pair_t.T.astype(BF16)


def _attn_prompt(qt, qft, k, kb, vt):
    b, d_attn, s = qt.shape
    tq = Q_TILE
    nblk = s // tq
    n_pairs = d_attn // LANES
    return pl.pallas_call(
        _attn_prompt_kernel,
        grid=(b, nblk),
        in_specs=[
            pl.BlockSpec((1, d_attn, tq), lambda i, j: (i, 0, j)),
            pl.BlockSpec((1, LANES, tq), lambda i, j: (i, 0, j)),
            pl.BlockSpec((1, s, d_attn), lambda i, j: (i, 0, 0)),
            pl.BlockSpec((1, s, LANES), lambda i, j: (i, 0, 0)),
            pl.BlockSpec((1, d_attn, s), lambda i, j: (i, 0, 0)),
        ],
        out_specs=pl.BlockSpec((1, tq, d_attn), lambda i, j: (i, j, 0)),
        out_shape=jax.ShapeDtypeStruct((b, s, d_attn), BF16),
        scratch_shapes=[
            pltpu.VMEM((n_pairs, s, 2 * LANES), BF16),
            pltpu.VMEM((n_pairs, nblk, LANES + DENOM_ROWS, tq), BF16),
            pltpu.VMEM((n_pairs, 2 * LANES, 2 * tq), BF16),
            pltpu.VMEM((n_pairs, 1, 2 * tq), F32),
            pltpu.VMEM((n_pairs, LANES + DENOM_ROWS, 2 * tq), F32),
            pltpu.VMEM((ATTN_LAG, tq, 2 * tq), F32),
        ],
        compiler_params=pltpu.CompilerParams(dimension_semantics=("arbitrary", "arbitrary"),
                                             vmem_limit_bytes=VMEM_LIMIT_BYTES),
        name="attn_prompt",
    )(qt, qft, k, kb, vt)


def _inproj_sample_kernel(x_ref, sh_ref, sc_ref, n1g_ref, wglu_ref, wqkvf_ref, bf_ref, qg_ref, kg_ref,
                          bdtri_ref, cw_ref, cb_ref, lng_ref, lnb_ref, st_ref,
                          k_ref, v_ref, lf_ref, nst_ref, q_ref, qb_ref, kb_ref, co_ref,
                          u_ref, hist_ref, cof_ref):
    nb, tt, d = x_ref.shape
    r = nb * tt
    d_conv = co_ref.shape[1]
    n_state = CONV_WIDTH - 1

    h = _adaln_rmsnorm(x_ref[...], n1g_ref[...], sc_ref[:, 0], sh_ref[:, 0])
    hb = h.reshape(r, d).astype(BF16)

    n_chunks = d_conv // LANES
    zg = _dot(hb, wglu_ref[...])
    u = zg[:, :d_conv] * jax.nn.sigmoid(zg[:, d_conv:])
    for c in range(n_chunks):
        u_ref[c] = u[:, c * LANES:(c + 1) * LANES]

    q_t, k_t, v_t, lf_t = _qkvf_feature_major(hb, wqkvf_ref[...], qg_ref[...], kg_ref[...], bf_ref[...])
    k_ref[...] = k_t.T
    v_ref[...] = v_t.T
    q_ref[...] = (q_t * (HEAD_DIM ** -0.5)).T.astype(BF16)
    lf = jnp.concatenate([lf_t, jnp.zeros((LANES - N_HEADS, r), F32)], axis=0).T
    lf_ref[...] = lf

    lane = lax.broadcasted_iota(jnp.int32, (r, LANES), 1)
    hi, mid, lo = _split3(lf)
    packed = hi + pltpu.roll(mid, N_HEADS, 1) + pltpu.roll(lo, 2 * N_HEADS, 1)
    c = _dot(bdtri_ref[...], packed.astype(BF16))
    cn = jnp.where(lane < N_HEADS,
                   c + pltpu.roll(c, LANES - N_HEADS, 1) + pltpu.roll(c, LANES - 2 * N_HEADS, 1), 0.0)
    hi, mid, lo = _split3(cn)
    p = hi + pltpu.roll(mid, N_HEADS, 1) + pltpu.roll(lo, 2 * N_HEADS, 1)
    qb_ref[...] = (p + jnp.where((lane >= N_FEAT) & (lane < 2 * N_FEAT), 1.0, 0.0)).astype(BF16)
    kb_ref[...] = (jnp.where(lane < N_FEAT, 1.0, 0.0) - pltpu.roll(p, N_FEAT, 1)).astype(BF16)

    hist_ref[0:n_state] = st_ref[...]
    for t in range(tt):
        for c in range(n_chunks):
            hist_ref[n_state + t, :, c * LANES:(c + 1) * LANES] = u_ref[c, pl.ds(t, nb, stride=tt), :]
    nst_ref[...] = hist_ref[tt:tt + n_state]
    cb, ln_g, ln_b = cb_ref[...], lng_ref[...], lnb_ref[...]
    for t in range(tt):
        y = _conv_ln_swish(lambda k: hist_ref[t + k], lambda k: cw_ref[k:k + 1, :], cb, ln_g, ln_b)
        for c in range(n_chunks):
            cof_ref[c, pl.ds(t, nb, stride=tt), :] = y[:, c * LANES:(c + 1) * LANES]
    co_ref[...] = jnp.concatenate([cof_ref[c] for c in range(n_chunks)], axis=1).astype(BF16)


def _inproj_sample(x, mod4, n1g, wglu, wqkvf_t, bf_col, qg_col, kg_col, bdtri, conv_w, conv_b, ln_g, ln_b,
                   state_t):
    nb, tt, d = x.shape
    r = nb * tt
    d_conv = conv_w.shape[1]
    d_attn = N_HEADS * HEAD_DIM
    n_state = CONV_WIDTH - 1
    mod_spec = lambda j: pl.BlockSpec((nb, 1, 1, d), lambda i: (0, j, 0, 0))
    full = lambda shape: pl.BlockSpec(shape, lambda i: (0,) * len(shape))
    out_shape = (
        jax.ShapeDtypeStruct((r, d_attn), F32),
        jax.ShapeDtypeStruct((r, d_attn), F32),
        jax.ShapeDtypeStruct((r, LANES), F32),
        jax.ShapeDtypeStruct((n_state, nb, d_conv), F32),
        jax.ShapeDtypeStruct((r, d_attn), BF16),
        jax.ShapeDtypeStruct((r, LANES), BF16),
        jax.ShapeDtypeStruct((r, LANES), BF16),
        jax.ShapeDtypeStruct((r, d_conv), BF16),
    )
    args = (x, mod4, mod4, n1g, wglu, wqkvf_t, bf_col, qg_col, kg_col, bdtri, conv_w, conv_b, ln_g, ln_b,
            state_t)
    in_specs = [full(x.shape), mod_spec(0), mod_spec(1)] + [full(a.shape) for a in args[3:]]
    return pl.pallas_call(
        _inproj_sample_kernel,
        grid=(1,),
        in_specs=in_specs,
        out_specs=tuple(full(o.shape) for o in out_shape),
        out_shape=out_shape,
        scratch_shapes=[pltpu.VMEM((d_conv // LANES, r, LANES), F32),
                        pltpu.VMEM((n_state + tt, nb, d_conv), F32),
                        pltpu.VMEM((d_conv // LANES, r, LANES), F32)],
        compiler_params=pltpu.CompilerParams(dimension_semantics=("arbitrary",),
                                             vmem_limit_bytes=VMEM_LIMIT_BYTES),
        name="inproj_sample",
    )(*args)


def _attn_sample_kernel(*refs):
    for i in range(refs[0].shape[0]):
        _attn_sample_request(i, *refs)


def _attn_sample_request(i, q_ref, qb_ref, kn_ref, vn_ref, kbn_ref, kt_ref, vt_ref, clft_ref, ltri_ref, o_ref):
    tt = q_ref.shape[1]
    d_attn = q_ref.shape[2]
    p_len = kt_ref.shape[2]
    nblk = p_len // MXU_DIM
    rows = N_HEADS * tt

    parts = jnp.concatenate(_split3(clft_ref[i]), axis=0)
    blocks = [parts[:, blk * MXU_DIM:(blk + 1) * MXU_DIM] for blk in range(nblk)]
    local = _dot(jnp.concatenate(blocks, axis=0).astype(BF16), ltri_ref[...])
    off = jnp.zeros((N_FEAT, 1), F32)
    sufs = [None] * nblk
    for blk in reversed(range(nblk)):
        loc = local[blk * N_FEAT:(blk + 1) * N_FEAT]
        sufs[blk] = loc + off
        off = off + loc[:, 0:1] + blocks[blk][:, 0:1]
    suf_parts = jnp.concatenate(sufs, axis=1)
    ck_rel = suf_parts[0:8] + suf_parts[8:16] + suf_parts[16:24]
    c_hi, c_mid, c_lo = _split3(ck_rel)
    kbt = jnp.concatenate([jnp.ones((N_FEAT, p_len), F32), c_hi, c_mid, c_lo,
                           jnp.zeros((LANES - 2 * N_FEAT, p_len), F32)], axis=0)
    k_all = jnp.concatenate([kt_ref[i].astype(BF16), kbt.astype(BF16)], axis=0)

    lane_q = lax.broadcasted_iota(jnp.int32, (tt, d_attn), 1)
    lane_b = lax.broadcasted_iota(jnp.int32, (tt, LANES), 1)
    q = q_ref[i].astype(F32)
    qb = qb_ref[i].astype(F32)
    stack = []
    for h in range(N_HEADS):
        qm = jnp.where((lane_q >= h * HEAD_DIM) & (lane_q < (h + 1) * HEAD_DIM), q, 0.0)
        bm = jnp.where((lane_b % N_HEADS) == h, qb, 0.0)
        stack.append(jnp.concatenate([qm, bm], axis=1))
    qs = jnp.concatenate(stack, axis=0).astype(BF16)

    s_c = _dot(qs, k_all)

    pad = jnp.zeros((LANES - tt, d_attn + LANES), F32)
    kn = jnp.concatenate([jnp.concatenate([kn_ref[i], kbn_ref[i].astype(F32)], axis=1), pad], axis=0)
    s_n = _dot_nt(qs, kn.astype(BF16))
    row = lax.broadcasted_iota(jnp.int32, s_n.shape, 0)
    col = lax.broadcasted_iota(jnp.int32, s_n.shape, 1)
    s_n = jnp.where(col <= (row % tt), s_n, NEG_INF)

    m = jnp.maximum(jnp.max(s_c, axis=-1, keepdims=True), jnp.max(s_n, axis=-1, keepdims=True))
    p_c = jnp.exp(s_c - m)
    p_n = jnp.exp(s_n - m)
    l = jnp.sum(p_c, axis=-1, keepdims=True) + jnp.sum(p_n, axis=-1, keepdims=True)
    vn = jnp.concatenate([vn_ref[i], jnp.zeros((LANES - tt, d_attn), F32)], axis=0).astype(BF16)
    o = _dot_nt(p_c.astype(BF16), vt_ref[i].astype(BF16)) + _dot(p_n.astype(BF16), vn)
    o = o / l

    out = jnp.zeros((tt, d_attn), F32)
    for h in range(N_HEADS):
        out = jnp.where((lane_q >= h * HEAD_DIM) & (lane_q < (h + 1) * HEAD_DIM), o[h * tt:(h + 1) * tt], out)
    o_ref[i] = out.astype(BF16)


def _attn_sample(q, qb, kn, vn, kbn, kt, vt, clft, ltri):
    nb, tt, d_attn = q.shape
    p_len = kt.shape[2]
    per_req = lambda shape: pl.BlockSpec((SAMPLE_REQS_PER_STEP,) + shape, lambda i: (i, 0, 0))
    return pl.pallas_call(
        _attn_sample_kernel,
        grid=(nb // SAMPLE_REQS_PER_STEP,),
        in_specs=[per_req((tt, d_attn)), per_req((tt, LANES)), per_req((tt, d_attn)), per_req((tt, d_attn)),
                  per_req((tt, LANES)), per_req((d_attn, p_len)), per_req((d_attn, p_len)),
                  per_req((N_HEADS, p_len)), _const_spec(ltri.shape)],
        out_specs=per_req((tt, d_attn)),
        out_shape=jax.ShapeDtypeStruct((nb, tt, d_attn), BF16),
        compiler_params=pltpu.CompilerParams(dimension_semantics=("arbitrary",),
                                             vmem_limit_bytes=VMEM_LIMIT_BYTES),
        name="attn_sample",
    )(q, qb, kn, vn, kbn, kt, vt, clft, ltri)


def _outproj_ffn_body(x_ref, co, at, g1_ref, sh2_ref, sc2_ref, g2_ref, n2g_ref,
                      wo_ref, wg_ref, wu_ref, wd_ref, y_ref, a_ref, vpu_slot=None):
    nb, tt, d = x_ref.shape
    r = nb * tt
    d_conv = co.shape[1]
    mix = _dot(co, wo_ref[0:d_conv, :]) + _dot(at, wo_ref[d_conv:, :])
    x1 = x_ref[...] + g1_ref[:, 0] * mix.reshape(nb, tt, d)
    hb = _adaln_rmsnorm(x1, n2g_ref[...], sc2_ref[:, 0], sh2_ref[:, 0]).reshape(r, d).astype(BF16)
    lead = x1.reshape(r, d)[0:SUBLANES, 0:d_conv]
    n_ff = wg_ref.shape[1] // FF_CHUNK
    chunk = lambda c: slice(c * FF_CHUNK, (c + 1) * FF_CHUNK)
    for c in range(n_ff):
        zero = vpu_slot(c, lead) if vpu_slot is not None else None
        g, u = _dot(hb, wg_ref[:, chunk(c)]), _dot(hb, wu_ref[:, chunk(c)])
        if zero is not None:
            zero = jnp.concatenate([zero] * (FF_CHUNK // LANES), axis=1)
            g = jnp.concatenate([g[:SUBLANES] + zero, g[SUBLANES:]], axis=0)
        a = (g * jax.nn.sigmoid(g)) * u
        a_ref[:, chunk(c)] = a.astype(BF16)
        lead = jnp.concatenate([a[0:SUBLANES]] * (d_conv // FF_CHUNK), axis=1)
    ffn = _dot(a_ref[...], wd_ref[...])
    y_ref[...] = x1 + g2_ref[:, 0] * ffn.reshape(nb, tt, d)


def _outproj_ffn_kernel(x_ref, co_ref, at_ref, g1_ref, sh2_ref, sc2_ref, g2_ref, n2g_ref,
                        wo_ref, wg_ref, wu_ref, wd_ref, y_ref, a_ref):
    nb, tt, _ = x_ref.shape
    co = co_ref[...].reshape(nb * tt, co_ref.shape[2])
    at = at_ref[...].reshape(nb * tt, at_ref.shape[2])
    _outproj_ffn_body(x_ref, co, at, g1_ref, sh2_ref, sc2_ref, g2_ref, n2g_ref,
                      wo_ref, wg_ref, wu_ref, wd_ref, y_ref, a_ref)


def _exact_zero(v):
    return jnp.minimum(jnp.abs(v), 0.0)


def _outproj_ffn_conv_kernel(x_ref, at_ref, un_ref, u0_ref, g1_ref, sh2_ref, sc2_ref, g2_ref, n2g_ref,
                             wo_ref, wg_ref, wu_ref, wd_ref, cw_ref, cb_ref, lng_ref, lnb_ref,
                             y_ref, a_ref, shift_ref, co_ref, wb_ref, *, tiles_per_seq):
    g = pl.program_id(0)
    tm = x_ref.shape[1]
    d_conv = co_ref.shape[1]
    span = tm + HIST - SUBLANES
    n_conv = tm // CONV_ROWS
    n_ff = wg_ref.shape[1] // FF_CHUNK
    cb, ln_g, ln_b = cb_ref[...], lng_ref[...], lnb_ref[...]

    def stage(u, hist):
        shift_ref[0, 0:HIST, :] = hist
        shift_ref[0, HIST:HIST + tm, :] = u
        for r in range(1, SUBLANES):
            shift_ref[r, 0:span, :] = shift_ref[0, r:r + span, :]

    def conv_chunk(c, bias):
        def tap(k):
            off = k + HIST - (CONV_WIDTH - 1)
            row = c * CONV_ROWS + off - off % SUBLANES
            return shift_ref[off % SUBLANES, row:row + CONV_ROWS, :]
        weight = lambda k: jnp.concatenate([wb_ref[k]] * (CONV_ROWS // SUBLANES), axis=0)
        y = _conv_ln_swish(tap, weight, bias, ln_g, ln_b)
        co_ref[c * CONV_ROWS:(c + 1) * CONV_ROWS, :] = y.astype(BF16)
        return y

    @pl.when(g == 0)
    def _():
        for k in range(CONV_WIDTH):
            wb_ref[k] = jnp.broadcast_to(cw_ref[k:k + 1, :], (SUBLANES, d_conv))
        stage(u0_ref[0], jnp.zeros((HIST, d_conv), F32))
        for c in range(n_conv):
            conv_chunk(c, cb)

    def conv_slot(c, lead):
        if c == 0:
            tail = shift_ref[0, tm:tm + HIST, :]
            starts_seq = (g + 1) % tiles_per_seq == 0
            stage(un_ref[0], jnp.where(starts_seq, 0.0, tail))
        bias = jnp.concatenate([cb + _exact_zero(lead)] * (CONV_ROWS // SUBLANES), axis=0)
        zero = jnp.zeros((SUBLANES, LANES), F32)
        for i in range(c * n_conv // n_ff, (c + 1) * n_conv // n_ff):
            zero = zero + _exact_zero(conv_chunk(i, bias)[0:SUBLANES, 0:LANES])
        return zero

    co = co_ref[...]
    _outproj_ffn_body(x_ref, co, at_ref[0], g1_ref, sh2_ref, sc2_ref, g2_ref, n2g_ref,
                      wo_ref, wg_ref, wu_ref, wd_ref, y_ref, a_ref, vpu_slot=conv_slot)


def _outproj_ffn_conv(x, at, u, mod4, mod_row0, n2g, wo, wg, wu, wd, conv_w, conv_b, ln_g, ln_b):
    b, s, d = x.shape
    tm = ROW_TILE
    d_conv = u.shape[2]
    d_attn = at.shape[2]
    tps = s // tm
    n_tiles = b * tps
    tile = lambda g: (g // tps, g % tps, 0)
    next_tile = lambda g: tile(jnp.minimum(g + 1, n_tiles - 1))
    mod_spec = lambda j: pl.BlockSpec((1, 1, 1, d), lambda g: (mod_row0 + g // tps, j, 0, 0))
    consts = (n2g, wo, wg, wu, wd, conv_w, conv_b, ln_g, ln_b)
    return pl.pallas_call(
        functools.partial(_outproj_ffn_conv_kernel, tiles_per_seq=tps),
        grid=(n_tiles,),
        in_specs=[pl.BlockSpec((1, tm, d), tile), pl.BlockSpec((1, tm, d_attn), tile),
                  pl.BlockSpec((1, tm, d_conv), next_tile),
                  pl.BlockSpec((1, tm, d_conv), lambda g: (0, 0, 0), pipeline_mode=pl.Buffered(1)),
                  mod_spec(2), mod_spec(3), mod_spec(4), mod_spec(5)] + [_const_spec(c.shape) for c in consts],
        out_specs=pl.BlockSpec((1, tm, d), tile),
        out_shape=jax.ShapeDtypeStruct((b, s, d), F32),
        scratch_shapes=[pltpu.VMEM((tm, wg.shape[1]), BF16),
                        pltpu.VMEM((SUBLANES, tm + HIST, d_conv), F32),
                        pltpu.VMEM((tm, d_conv), BF16),
                        pltpu.VMEM((CONV_WIDTH, SUBLANES, d_conv), F32)],
        compiler_params=pltpu.CompilerParams(dimension_semantics=("arbitrary",),
                                             vmem_limit_bytes=VMEM_LIMIT_BYTES),
        name="outproj_ffn_conv",
    )(x, at, u, u, mod4, mod4, mod4, mod4, *consts)


def _outproj_ffn(x, co, at, mod4, n2g, wo, wg, wu, wd, bb, tt):
    nb, s, d = x.shape
    d_conv = co.shape[2]
    d_attn = at.shape[2]
    mod_spec = lambda j: pl.BlockSpec((bb, 1, 1, d), lambda i, t: (i, j, 0, 0))
    row_spec = lambda w: pl.BlockSpec((bb, tt, w), lambda i, t: (i, t, 0))
    return pl.pallas_call(
        _outproj_ffn_kernel,
        grid=(nb // bb, s // tt),
        in_specs=[row_spec(d), row_spec(d_conv), row_spec(d_attn),
                  mod_spec(2), mod_spec(3), mod_spec(4), mod_spec(5),
                  _const_spec(n2g.shape), _const_spec(wo.shape), _const_spec(wg.shape),
                  _const_spec(wu.shape), _const_spec(wd.shape)],
        out_specs=row_spec(d),
        out_shape=jax.ShapeDtypeStruct((nb, s, d), F32),
        scratch_shapes=[pltpu.VMEM((bb * tt, wg.shape[1]), BF16)],
        compiler_params=pltpu.CompilerParams(dimension_semantics=("arbitrary", "arbitrary"),
                                             vmem_limit_bytes=VMEM_LIMIT_BYTES),
        name="outproj_ffn",
    )(x, co, at, mod4, mod4, mod4, mod4, n2g, wo, wg, wu, wd)


def _tri(n, rel):
    i = lax.broadcasted_iota(jnp.int32, (n, n), 0)
    j = lax.broadcasted_iota(jnp.int32, (n, n), 1)
    return rel(i, j).astype(BF16)


def _layer(xp, xs, cache_k, cache_v, cache_logf, state_conv, c_all, w_ada, b_ada, norm1_g, w_in, b_f,
           q_norm_g, k_norm_g, conv_w, conv_b, conv_ln_g, conv_ln_b, w_out, norm2_g, w_gate, w_up, w_down):
    b, s, d = xp.shape
    nb, tt, _ = xs.shape
    d_conv = conv_w.shape[1]
    d_attn = N_HEADS * HEAD_DIM
    d_ff = w_gate.shape[1]
    p_len = cache_k.shape[1]

    mod4 = _modulation(c_all, w_ada, b_ada).reshape(nb + b, N_MOD, 1, d)

    wglu = w_in[:, :2 * d_conv].astype(BF16)
    wqkvf_t = jnp.pad(w_in[:, 2 * d_conv:].T, ((0, 2 * SUBLANES - N_HEADS), (0, 0))).astype(BF16)
    bf_col = b_f.reshape(N_HEADS, 1)
    qg_col = q_norm_g.reshape(d_attn, 1)
    kg_col = k_norm_g.reshape(d_attn, 1)
    row = lambda v: v.reshape(1, -1)
    n1g, n2g, cb, ln_g, ln_b = row(norm1_g), row(norm2_g), row(conv_b), row(conv_ln_g), row(conv_ln_b)
    wo = w_out.astype(BF16)
    assert d_ff % FF_CHUNK == 0
    wg, wu, wd = w_gate.astype(BF16), w_up.astype(BF16), w_down.astype(BF16)
    utri = _tri(MXU_DIM, lambda i, j: i <= j)
    ltri = _tri(MXU_DIM, lambda i, j: i > j)
    bdtri = _tri(nb * tt, lambda i, j: (i // tt == j // tt) & (j <= i))

    kt, vt, lft, cst, qt, qft, k, kb, u = _inproj_prompt(
        xp, mod4, nb, n1g, wglu, wqkvf_t, bf_col, qg_col, kg_col, utri)
    at = _attn_prompt(qt, qft, k, kb, vt)
    yp = _outproj_ffn_conv(xp, at, u, mod4, nb, n2g, wo, wg, wu, wd, conv_w, cb, ln_g, ln_b)
    k_p = kt.reshape(b, N_HEADS, HEAD_DIM, s).transpose(0, 3, 1, 2)
    v_p = vt.reshape(b, N_HEADS, HEAD_DIM, s).transpose(0, 3, 1, 2)
    lf_p = lft.transpose(0, 2, 1)

    state_t = state_conv.transpose(1, 0, 2)
    k_s, v_s, lf_s, nst, q_s, qb_s, kb_s, co_s = _inproj_sample(
        xs, mod4, n1g, wglu, wqkvf_t, bf_col, qg_col, kg_col, bdtri, conv_w, cb, ln_g, ln_b, state_t)
    r3 = lambda a: a.reshape(nb, tt, a.shape[-1])
    ckt = cache_k.transpose(0, 2, 3, 1).reshape(nb, d_attn, p_len)
    cvt = cache_v.transpose(0, 2, 3, 1).reshape(nb, d_attn, p_len)
    clft = cache_logf.transpose(0, 2, 1)
    at_s = _attn_sample(r3(q_s), r3(qb_s), r3(k_s), r3(v_s), r3(kb_s), ckt, cvt, clft, ltri)
    ys = _outproj_ffn(xs, r3(co_s), at_s, mod4, n2g, wo, wg, wu, wd, nb, tt)
    k_sn = k_s.reshape(nb, tt, N_HEADS, HEAD_DIM)
    v_sn = v_s.reshape(nb, tt, N_HEADS, HEAD_DIM)
    lf_sn = lf_s[:, :N_HEADS].reshape(nb, tt, N_HEADS)
    return yp, ys, (k_p, v_p, lf_p, cst), (k_sn, v_sn, lf_sn, nst.transpose(1, 0, 2))


def kernel(x_prompt, x_sample, cache_k, cache_v, cache_logf, state_conv, c_prompt, c_sample, w_ada, b_ada,
           norm1_g, w_in, b_f, q_norm_g, k_norm_g, conv_w, conv_b, conv_ln_g, conv_ln_b, w_out, norm2_g,
           w_gate, w_up, w_down):
    depth = w_ada.shape[0]
    c_all = jnp.concatenate([c_sample, c_prompt], axis=0)
    yp, ys = x_prompt, x_sample
    st_p, st_s = [], []
    for l in range(depth):
        yp, ys, sp, ss = _layer(
            yp, ys, cache_k[l], cache_v[l], cache_logf[l], state_conv[l], c_all, w_ada[l], b_ada[l],
            norm1_g[l], w_in[l], b_f[l], q_norm_g[l], k_norm_g[l], conv_w[l], conv_b[l], conv_ln_g[l],
            conv_ln_b[l], w_out[l], norm2_g[l], w_gate[l], w_up[l], w_down[l])
        st_p.append(sp)
        st_s.append(ss)
    stack = lambda xs: xs[0][None] if len(xs) == 1 else jnp.stack(xs)
    outs_p = [stack([s[i] for s in st_p]) for i in range(4)]
    outs_s = [stack([s[i] for s in st_s]) for i in range(4)]
    return (yp, ys, *outs_p, *outs_s)
```

```python
import functools

import jax
import jax.numpy as jnp
from jax import lax
from jax.experimental import pallas as pl
from jax.experimental.pallas import tpu as pltpu

F32 = jnp.float32
BF16 = jnp.bfloat16

N_HEADS = 8
HEAD_DIM = 64
CONV_WIDTH = 31
EPS = 1e-6
NEG_INF = -1e30

LANES = 128
SUBLANES = 8
MXU_DIM = 256
VMEM_LIMIT_BYTES = 56 * 1024 * 1024

N_PARTS = 3
N_FEAT = N_PARTS * N_HEADS
HIST = 32
CONV_ROWS = 32

ROW_TILE = 512
INPROJ_TILE = 1024
Q_TILE = 256
ATTN_LAG = 2
DENOM_ROWS = 16
LOG2E = 1.4426950408889634
FF_CHUNK = 256
SAMPLE_REQS_PER_STEP = 4
N_MOD = 6
MOD_TILE = 1536


def _dot(a, b):
    return jnp.dot(a, b, preferred_element_type=F32)


def _dot_nt(a, b):
    return lax.dot_general(a, b, (((1,), (1,)), ((), ())), preferred_element_type=F32)


def _split3(x):
    hi = x.astype(BF16).astype(F32)
    r = x - hi
    mid = r.astype(BF16).astype(F32)
    lo = (r - mid).astype(BF16).astype(F32)
    return hi, mid, lo


def _log_sigmoid(x):
    return jnp.minimum(x, 0.0) - jnp.log1p(jnp.exp(-jnp.abs(x)))


def _adaln_rmsnorm(x, g, scale, shift):
    y = x * lax.rsqrt(jnp.mean(x * x, axis=-1, keepdims=True) + EPS)
    return (y * g) * (1.0 + scale) + shift


def _const_spec(shape):
    n = len(shape)
    return pl.BlockSpec(shape, lambda *_: (0,) * n, pipeline_mode=pl.Buffered(1))


def _modulation_kernel(c_ref, w_ref, b_ref, o_ref):
    c = c_ref[...]
    a = (c * jax.nn.sigmoid(c)).astype(BF16)
    o_ref[...] = _dot(a, w_ref[...].astype(BF16)) + b_ref[...]


def _modulation(c, w_ada, b_ada):
    nb, d = c.shape
    n = w_ada.shape[1]
    tn = MOD_TILE
    return pl.pallas_call(
        _modulation_kernel,
        grid=(n // tn,),
        in_specs=[pl.BlockSpec((nb, d), lambda j: (0, 0)),
                  pl.BlockSpec((d, tn), lambda j: (0, j)),
                  pl.BlockSpec((1, tn), lambda j: (0, j))],
        out_specs=pl.BlockSpec((nb, tn), lambda j: (0, j)),
        out_shape=jax.ShapeDtypeStruct((nb, n), F32),
        compiler_params=pltpu.CompilerParams(dimension_semantics=("arbitrary",),
                                             vmem_limit_bytes=VMEM_LIMIT_BYTES),
        name="modulation",
    )(c, w_ada, b_ada.reshape(1, n))


def _qkvf_feature_major(hb, wqkvf_t, qg_col, kg_col, bf_col):
    d_attn = N_HEADS * HEAD_DIM
    zt = _dot_nt(wqkvf_t, hb)
    r = zt.shape[1]

    def head_rms(z, g_col):
        z3 = z.reshape(N_HEADS, HEAD_DIM, r)
        ms = jnp.mean(z3 * z3, axis=1, keepdims=True)
        return (z3 * lax.rsqrt(ms + EPS)).reshape(d_attn, r) * g_col

    q_t = head_rms(zt[0:d_attn], qg_col)
    k_t = head_rms(zt[d_attn:2 * d_attn], kg_col)
    v_t = zt[2 * d_attn:3 * d_attn]
    lf_t = _log_sigmoid(zt[3 * d_attn:3 * d_attn + N_HEADS] + bf_col)
    return q_t, k_t, v_t, lf_t


def _conv_ln_swish(load_rows, weight, cb, ln_g, ln_b):
    acc = cb + weight(0) * load_rows(0)
    for k in range(1, CONV_WIDTH):
        acc = acc + weight(k) * load_rows(k)
    mu = jnp.mean(acc, axis=-1, keepdims=True)
    cen = acc - mu
    var = jnp.mean(cen * cen, axis=-1, keepdims=True)
    y = cen * lax.rsqrt(var + EPS) * ln_g + ln_b
    return y * jax.nn.sigmoid(y)


def _inproj_prompt_kernel(x_ref, sh_ref, sc_ref, n1g_ref, wglu_ref, wqkvf_ref, bf_ref, qg_ref, kg_ref,
                          utri_ref,
                          kt_ref, vt_ref, lft_ref, cst_ref, qt_ref, qft_ref, k_ref, kb_ref, u_ref,
                          carry_ref):
    t = pl.program_id(1)
    tm = x_ref.shape[1]
    d_conv = u_ref.shape[2]

    @pl.when(t == 0)
    def _():
        carry_ref[...] = jnp.zeros(carry_ref.shape, F32)

    carry = carry_ref[:, 0:1]
    for h in range(tm // ROW_TILE):
        rows = slice(h * ROW_TILE, (h + 1) * ROW_TILE)
        hb = _adaln_rmsnorm(x_ref[0, rows, :], n1g_ref[...], sc_ref[0, 0], sh_ref[0, 0]).astype(BF16)

        q_t, k_t, v_t, lf_t = _qkvf_feature_major(hb, wqkvf_ref[...], qg_ref[...], kg_ref[...], bf_ref[...])
        kt_ref[0, :, rows] = k_t
        vt_ref[0, :, rows] = v_t
        lft_ref[0, :, rows] = lf_t

        zg = _dot(hb, wglu_ref[...])
        u_ref[0, rows, :] = zg[:, :d_conv] * jax.nn.sigmoid(zg[:, d_conv:])

        nblk = ROW_TILE // MXU_DIM
        parts = jnp.concatenate(_split3(lf_t), axis=0)
        stacked = jnp.concatenate([parts[:, i * MXU_DIM:(i + 1) * MXU_DIM] for i in range(nblk)], axis=0)
        local = _dot(stacked.astype(BF16), utri_ref[...])
        cums = []
        for i in range(nblk):
            loc = local[i * N_FEAT:(i + 1) * N_FEAT]
            cums.append(loc + carry)
            carry = carry + loc[:, MXU_DIM - 1:MXU_DIM]
        cum_parts = jnp.concatenate(cums, axis=1)
        cum_t = (cum_parts[0:8] + cum_parts[8:16] + cum_parts[16:24]) * LOG2E

        c_hi, c_mid, c_lo = _split3(cum_t)
        ones = jnp.ones((N_FEAT, ROW_TILE), F32)
        zeros = jnp.zeros((LANES - 2 * N_FEAT, ROW_TILE), F32)
        kfeat_t = jnp.concatenate([ones, -c_hi, -c_mid, -c_lo, zeros], axis=0)
        qft_ref[0, :, rows] = jnp.concatenate([c_hi, c_mid, c_lo, ones, zeros], axis=0).astype(BF16)
        qt_ref[0, :, rows] = (q_t * (HEAD_DIM ** -0.5 * LOG2E)).astype(BF16)
        k_ref[0, rows, :] = k_t.T.astype(BF16)
        kb_ref[0, rows, :] = kfeat_t.T.astype(BF16)
    carry_ref[...] = jnp.broadcast_to(carry, carry_ref.shape)

    @pl.when(t == pl.num_programs(1) - 1)
    def _():
        cst_ref[0] = u_ref[0, tm - (CONV_WIDTH - 1):tm, :]


def _inproj_prompt(x, mod4, mod_row0, n1g, wglu, wqkvf_t, bf_col, qg_col, kg_col, utri):
    b, s, d = x.shape
    tm = INPROJ_TILE
    d_conv = wglu.shape[1] // 2
    d_attn = N_HEADS * HEAD_DIM
    mod_spec = lambda j: pl.BlockSpec((1, 1, 1, d), lambda i, t: (mod_row0 + i, j, 0, 0))
    out_shape = (
        jax.ShapeDtypeStruct((b, d_attn, s), F32),
        jax.ShapeDtypeStruct((b, d_attn, s), F32),
        jax.ShapeDtypeStruct((b, N_HEADS, s), F32),
        jax.ShapeDtypeStruct((b, CONV_WIDTH - 1, d_conv), F32),
        jax.ShapeDtypeStruct((b, d_attn, s), BF16),
        jax.ShapeDtypeStruct((b, LANES, s), BF16),
        jax.ShapeDtypeStruct((b, s, d_attn), BF16),
        jax.ShapeDtypeStruct((b, s, LANES), BF16),
        jax.ShapeDtypeStruct((b, s, d_conv), F32),
    )
    out_specs = (
        pl.BlockSpec((1, d_attn, tm), lambda i, t: (i, 0, t)),
        pl.BlockSpec((1, d_attn, tm), lambda i, t: (i, 0, t)),
        pl.BlockSpec((1, N_HEADS, tm), lambda i, t: (i, 0, t)),
        pl.BlockSpec((1, CONV_WIDTH - 1, d_conv), lambda i, t: (i, 0, 0)),
        pl.BlockSpec((1, d_attn, tm), lambda i, t: (i, 0, t)),
        pl.BlockSpec((1, LANES, tm), lambda i, t: (i, 0, t)),
        pl.BlockSpec((1, tm, d_attn), lambda i, t: (i, t, 0)),
        pl.BlockSpec((1, tm, LANES), lambda i, t: (i, t, 0)),
        pl.BlockSpec((1, tm, d_conv), lambda i, t: (i, t, 0)),
    )
    in_specs = [
        pl.BlockSpec((1, tm, d), lambda i, t: (i, t, 0)),
        mod_spec(0), mod_spec(1),
        _const_spec(n1g.shape), _const_spec(wglu.shape), _const_spec(wqkvf_t.shape),
        _const_spec(bf_col.shape), _const_spec(qg_col.shape), _const_spec(kg_col.shape),
        _const_spec(utri.shape),
    ]
    return pl.pallas_call(
        _inproj_prompt_kernel,
        grid=(b, s // tm),
        in_specs=in_specs,
        out_specs=out_specs,
        out_shape=out_shape,
        scratch_shapes=[pltpu.VMEM((N_FEAT, LANES), F32)],
        compiler_params=pltpu.CompilerParams(dimension_semantics=("arbitrary", "arbitrary"),
                                             vmem_limit_bytes=VMEM_LIMIT_BYTES),
        name="inproj_prompt",
    )(x, mod4, mod4, n1g, wglu, wqkvf_t, bf_col, qg_col, kg_col, utri)


def _attn_prompt_kernel(qt_ref, qft_ref, k_ref, kb_ref, vt_ref, o_ref,
                        kp_ref, vp_ref, q2_ref, m_ref, acc_ref, sc_ref):
    qi = pl.program_id(1)
    tq = qt_ref.shape[2]
    n_pairs, nblk = vp_ref.shape[0], vp_ref.shape[1]

    @pl.when(qi == 0)
    def _():
        ones = jnp.ones((vp_ref.shape[2] - LANES, tq), BF16)
        for p in range(n_pairs):
            kp_ref[p, :, 0:LANES] = k_ref[0, :, p * LANES:(p + 1) * LANES]
            kp_ref[p, :, LANES:2 * LANES] = kb_ref[0]
            for i in range(nblk):
                vp_ref[p, i, 0:LANES, :] = vt_ref[0, p * LANES:(p + 1) * LANES, i * tq:(i + 1) * tq].astype(BF16)
                vp_ref[p, i, LANES:, :] = ones

    frow = lax.broadcasted_iota(jnp.int32, (LANES, tq), 0)
    feat = qft_ref[0].astype(F32)
    for p in range(n_pairs):
        qp = qt_ref[0, p * LANES:(p + 1) * LANES, :].astype(F32)
        cols = []
        for j in range(2):
            qm = jnp.where((frow >= j * HEAD_DIM) & (frow < (j + 1) * HEAD_DIM), qp, 0.0)
            fm = jnp.where((frow % N_HEADS) == 2 * p + j, feat, 0.0)
            cols.append(jnp.concatenate([qm, fm], axis=0))
        q2_ref[p] = jnp.concatenate(cols, axis=1).astype(BF16)

    m_ref[...] = jnp.full(m_ref.shape, NEG_INF, F32)
    acc_ref[...] = jnp.zeros(acc_ref.shape, F32)

    def scores(p, j):
        start = pl.multiple_of(j * tq, tq)
        return _dot(kp_ref[p, pl.ds(start, tq), :], q2_ref[p])

    def accumulate(p, j, masked, s):
        if masked:
            key = lax.broadcasted_iota(jnp.int32, s.shape, 0)
            qry = lax.broadcasted_iota(jnp.int32, s.shape, 1)
            qry = jnp.where(qry >= tq, qry - tq, qry)
            s = jnp.where(key <= qry, s, NEG_INF)
        m_old = m_ref[p]
        m_new = jnp.maximum(m_old, jnp.max(s, axis=0, keepdims=True))
        pr = jnp.exp2(s - m_new).astype(BF16)
        acc_ref[p] = jnp.exp2(m_old - m_new) * acc_ref[p] + _dot(vp_ref[p, j], pr)
        m_ref[p] = m_new

    def run_units(units, next_block):
        pending = {}
        for i in range(len(units) + ATTN_LAG):
            if i < ATTN_LAG:
                pending[i] = sc_ref[i]
            elif i < len(units):
                p, j, _ = units[i]
                pending[i] = scores(p, j)
            elif next_block is not None:
                sc_ref[i - len(units)] = scores(i - len(units), next_block)
            if i >= ATTN_LAG:
                p, j, masked = units[i - ATTN_LAG]
                accumulate(p, j, masked, pending.pop(i - ATTN_LAG))

    def block(j, masked):
        return [(p, j, masked) for p in range(n_pairs)]

    for p in range(ATTN_LAG):
        sc_ref[p] = scores(p, 0)

    def body(i, carry):
        run_units(block(2 * i, False) + block(2 * i + 1, False), 2 * i + 2)
        return carry

    lax.fori_loop(0, qi // 2, body, 0)

    @pl.when(qi % 2 == 1)
    def _():
        run_units(block(qi - 1, False) + block(qi, True), None)

    @pl.when(qi % 2 == 0)
    def _():
        run_units(block(qi, True), None)

    for p in range(n_pairs):
        acc = acc_ref[p]
        out_t = acc[0:LANES] / acc[LANES:LANES + 1]
        pair_t = jnp.where(frow < HEAD_DIM, out_t[:, :tq], out_t[:, tq:])
        o_ref[0, :, p * LANES:(p + 1) * LANES] = pair_t.T.astype(BF16)


def _attn_prompt(qt, qft, k, kb, vt):
    b, d_attn, s = qt.shape
    tq = Q_TILE
    nblk = s // tq
    n_pairs = d_attn // LANES
    return pl.pallas_call(
        _attn_prompt_kernel,
        grid=(b, nblk),
        in_specs=[
            pl.BlockSpec((1, d_attn, tq), lambda i, j: (i, 0, j)),
            pl.BlockSpec((1, LANES, tq), lambda i, j: (i, 0, j)),
            pl.BlockSpec((1, s, d_attn), lambda i, j: (i, 0, 0)),
            pl.BlockSpec((1, s, LANES), lambda i, j: (i, 0, 0)),
            pl.BlockSpec((1, d_attn, s), lambda i, j: (i, 0, 0)),
        ],
        out_specs=pl.BlockSpec((1, tq, d_attn), lambda i, j: (i, j, 0)),
        out_shape=jax.ShapeDtypeStruct((b, s, d_attn), BF16),
        scratch_shapes=[
            pltpu.VMEM((n_pairs, s, 2 * LANES), BF16),
            pltpu.VMEM((n_pairs, nblk, LANES + DENOM_ROWS, tq), BF16),
            pltpu.VMEM((n_pairs, 2 * LANES, 2 * tq), BF16),
            pltpu.VMEM((n_pairs, 1, 2 * tq), F32),
            pltpu.VMEM((n_pairs, LANES + DENOM_ROWS, 2 * tq), F32),
            pltpu.VMEM((ATTN_LAG, tq, 2 * tq), F32),
        ],
        compiler_params=pltpu.CompilerParams(dimension_semantics=("arbitrary", "arbitrary"),
                                             vmem_limit_bytes=VMEM_LIMIT_BYTES),
        name="attn_prompt",
    )(qt, qft, k, kb, vt)


def _inproj_sample_kernel(x_ref, sh_ref, sc_ref, n1g_ref, wglu_ref, wqkvf_ref, bf_ref, qg_ref, kg_ref,
                          bdtri_ref, cw_ref, cb_ref, lng_ref, lnb_ref, st_ref,
                          k_ref, v_ref, lf_ref, nst_ref, q_ref, qb_ref, kb_ref, co_ref,
                          u_ref, hist_ref, cof_ref):
    nb, tt, d = x_ref.shape
    r = nb * tt
    d_conv = co_ref.shape[1]
    n_state = CONV_WIDTH - 1

    h = _adaln_rmsnorm(x_ref[...], n1g_ref[...], sc_ref[:, 0], sh_ref[:, 0])
    hb = h.reshape(r, d).astype(BF16)

    n_chunks = d_conv // LANES
    zg = _dot(hb, wglu_ref[...])
    u = zg[:, :d_conv] * jax.nn.sigmoid(zg[:, d_conv:])
    for c in range(n_chunks):
        u_ref[c] = u[:, c * LANES:(c + 1) * LANES]

    q_t, k_t, v_t, lf_t = _qkvf_feature_major(hb, wqkvf_ref[...], qg_ref[...], kg_ref[...], bf_ref[...])
    k_ref[...] = k_t.T
    v_ref[...] = v_t.T
    q_ref[...] = (q_t * (HEAD_DIM ** -0.5)).T.astype(BF16)
    lf = jnp.concatenate([lf_t, jnp.zeros((LANES - N_HEADS, r), F32)], axis=0).T
    lf_ref[...] = lf

    lane = lax.broadcasted_iota(jnp.int32, (r, LANES), 1)
    hi, mid, lo = _split3(lf)
    packed = hi + pltpu.roll(mid, N_HEADS, 1) + pltpu.roll(lo, 2 * N_HEADS, 1)
    c = _dot(bdtri_ref[...], packed.astype(BF16))
    cn = jnp.where(lane < N_HEADS,
                   c + pltpu.roll(c, LANES - N_HEADS, 1) + pltpu.roll(c, LANES - 2 * N_HEADS, 1), 0.0)
    hi, mid, lo = _split3(cn)
    p = hi + pltpu.roll(mid, N_HEADS, 1) + pltpu.roll(lo, 2 * N_HEADS, 1)
    qb_ref[...] = (p + jnp.where((lane >= N_FEAT) & (lane < 2 * N_FEAT), 1.0, 0.0)).astype(BF16)
    kb_ref[...] = (jnp.where(lane < N_FEAT, 1.0, 0.0) - pltpu.roll(p, N_FEAT, 1)).astype(BF16)

    hist_ref[0:n_state] = st_ref[...]
    for t in range(tt):
        for c in range(n_chunks):
            hist_ref[n_state + t, :, c * LANES:(c + 1) * LANES] = u_ref[c, pl.ds(t, nb, stride=tt), :]
    nst_ref[...] = hist_ref[tt:tt + n_state]
    cb, ln_g, ln_b = cb_ref[...], lng_ref[...], lnb_ref[...]
    for t in range(tt):
        y = _conv_ln_swish(lambda k: hist_ref[t + k], lambda k: cw_ref[k:k + 1, :], cb, ln_g, ln_b)
        for c in range(n_chunks):
            cof_ref[c, pl.ds(t, nb, stride=tt), :] = y[:, c * LANES:(c + 1) * LANES]
    co_ref[...] = jnp.concatenate([cof_ref[c] for c in range(n_chunks)], axis=1).astype(BF16)


def _inproj_sample(x, mod4, n1g, wglu, wqkvf_t, bf_col, qg_col, kg_col, bdtri, conv_w, conv_b, ln_g, ln_b,
                   state_t):
    nb, tt, d = x.shape
    r = nb * tt
    d_conv = conv_w.shape[1]
    d_attn = N_HEADS * HEAD_DIM
    n_state = CONV_WIDTH - 1
    mod_spec = lambda j: pl.BlockSpec((nb, 1, 1, d), lambda i: (0, j, 0, 0))
    full = lambda shape: pl.BlockSpec(shape, lambda i: (0,) * len(shape))
    out_shape = (
        jax.ShapeDtypeStruct((r, d_attn), F32),
        jax.ShapeDtypeStruct((r, d_attn), F32),
        jax.ShapeDtypeStruct((r, LANES), F32),
        jax.ShapeDtypeStruct((n_state, nb, d_conv), F32),
        jax.ShapeDtypeStruct((r, d_attn), BF16),
        jax.ShapeDtypeStruct((r, LANES), BF16),
        jax.ShapeDtypeStruct((r, LANES), BF16),
        jax.ShapeDtypeStruct((r, d_conv), BF16),
    )
    args = (x, mod4, mod4, n1g, wglu, wqkvf_t, bf_col, qg_col, kg_col, bdtri, conv_w, conv_b, ln_g, ln_b,
            state_t)
    in_specs = [full(x.shape), mod_spec(0), mod_spec(1)] + [full(a.shape) for a in args[3:]]
    return pl.pallas_call(
        _inproj_sample_kernel,
        grid=(1,),
        in_specs=in_specs,
        out_specs=tuple(full(o.shape) for o in out_shape),
        out_shape=out_shape,
        scratch_shapes=[pltpu.VMEM((d_conv // LANES, r, LANES), F32),
                        pltpu.VMEM((n_state + tt, nb, d_conv), F32),
                        pltpu.VMEM((d_conv // LANES, r, LANES), F32)],
        compiler_params=pltpu.CompilerParams(dimension_semantics=("arbitrary",),
                                             vmem_limit_bytes=VMEM_LIMIT_BYTES),
        name="inproj_sample",
    )(*args)


def _attn_sample_kernel(*refs):
    for i in range(refs[0].shape[0]):
        _attn_sample_request(i, *refs)


def _attn_sample_request(i, q_ref, qb_ref, kn_ref, vn_ref, kbn_ref, kt_ref, vt_ref, clft_ref, ltri_ref, o_ref):
    tt = q_ref.shape[1]
    d_attn = q_ref.shape[2]
    p_len = kt_ref.shape[2]
    nblk = p_len // MXU_DIM
    rows = N_HEADS * tt

    parts = jnp.concatenate(_split3(clft_ref[i]), axis=0)
    blocks = [parts[:, blk * MXU_DIM:(blk + 1) * MXU_DIM] for blk in range(nblk)]
    local = _dot(jnp.concatenate(blocks, axis=0).astype(BF16), ltri_ref[...])
    off = jnp.zeros((N_FEAT, 1), F32)
    sufs = [None] * nblk
    for blk in reversed(range(nblk)):
        loc = local[blk * N_FEAT:(blk + 1) * N_FEAT]
        sufs[blk] = loc + off
        off = off + loc[:, 0:1] + blocks[blk][:, 0:1]
    suf_parts = jnp.concatenate(sufs, axis=1)
    ck_rel = suf_parts[0:8] + suf_parts[8:16] + suf_parts[16:24]
    c_hi, c_mid, c_lo = _split3(ck_rel)
    kbt = jnp.concatenate([jnp.ones((N_FEAT, p_len), F32), c_hi, c_mid, c_lo,
                           jnp.zeros((LANES - 2 * N_FEAT, p_len), F32)], axis=0)
    k_all = jnp.concatenate([kt_ref[i].astype(BF16), kbt.astype(BF16)], axis=0)

    lane_q = lax.broadcasted_iota(jnp.int32, (tt, d_attn), 1)
    lane_b = lax.broadcasted_iota(jnp.int32, (tt, LANES), 1)
    q = q_ref[i].astype(F32)
    qb = qb_ref[i].astype(F32)
    stack = []
    for h in range(N_HEADS):
        qm = jnp.where((lane_q >= h * HEAD_DIM) & (lane_q < (h + 1) * HEAD_DIM), q, 0.0)
        bm = jnp.where((lane_b % N_HEADS) == h, qb, 0.0)
        stack.append(jnp.concatenate([qm, bm], axis=1))
    qs = jnp.concatenate(stack, axis=0).astype(BF16)

    s_c = _dot(qs, k_all)

    pad = jnp.zeros((LANES - tt, d_attn + LANES), F32)
    kn = jnp.concatenate([jnp.concatenate([kn_ref[i], kbn_ref[i].astype(F32)], axis=1), pad], axis=0)
    s_n = _dot_nt(qs, kn.astype(BF16))
    row = lax.broadcasted_iota(jnp.int32, s_n.shape, 0)
    col = lax.broadcasted_iota(jnp.int32, s_n.shape, 1)
    s_n = jnp.where(col <= (row % tt), s_n, NEG_INF)

    m = jnp.maximum(jnp.max(s_c, axis=-1, keepdims=True), jnp.max(s_n, axis=-1, keepdims=True))
    p_c = jnp.exp(s_c - m)
    p_n = jnp.exp(s_n - m)
    l = jnp.sum(p_c, axis=-1, keepdims=True) + jnp.sum(p_n, axis=-1, keepdims=True)
    vn = jnp.concatenate([vn_ref[i], jnp.zeros((LANES - tt, d_attn), F32)], axis=0).astype(BF16)
    o = _dot_nt(p_c.astype(BF16), vt_ref[i].astype(BF16)) + _dot(p_n.astype(BF16), vn)
    o = o / l

    out = jnp.zeros((tt, d_attn), F32)
    for h in range(N_HEADS):
        out = jnp.where((lane_q >= h * HEAD_DIM) & (lane_q < (h + 1) * HEAD_DIM), o[h * tt:(h + 1) * tt], out)
    o_ref[i] = out.astype(BF16)


def _attn_sample(q, qb, kn, vn, kbn, kt, vt, clft, ltri):
    nb, tt, d_attn = q.shape
    p_len = kt.shape[2]
    per_req = lambda shape: pl.BlockSpec((SAMPLE_REQS_PER_STEP,) + shape, lambda i: (i, 0, 0))
    return pl.pallas_call(
        _attn_sample_kernel,
        grid=(nb // SAMPLE_REQS_PER_STEP,),
        in_specs=[per_req((tt, d_attn)), per_req((tt, LANES)), per_req((tt, d_attn)), per_req((tt, d_attn)),
                  per_req((tt, LANES)), per_req((d_attn, p_len)), per_req((d_attn, p_len)),
                  per_req((N_HEADS, p_len)), _const_spec(ltri.shape)],
        out_specs=per_req((tt, d_attn)),
        out_shape=jax.ShapeDtypeStruct((nb, tt, d_attn), BF16),
        compiler_params=pltpu.CompilerParams(dimension_semantics=("arbitrary",),
                                             vmem_limit_bytes=VMEM_LIMIT_BYTES),
        name="attn_sample",
    )(q, qb, kn, vn, kbn, kt, vt, clft, ltri)


def _outproj_ffn_body(x_ref, co, at, g1_ref, sh2_ref, sc2_ref, g2_ref, n2g_ref,
                      wo_ref, wg_ref, wu_ref, wd_ref, y_ref, a_ref, vpu_slot=None):
    nb, tt, d = x_ref.shape
    r = nb * tt
    d_conv = co.shape[1]
    mix = _dot(co, wo_ref[0:d_conv, :]) + _dot(at, wo_ref[d_conv:, :])
    x1 = x_ref[...] + g1_ref[:, 0] * mix.reshape(nb, tt, d)
    hb = _adaln_rmsnorm(x1, n2g_ref[...], sc2_ref[:, 0], sh2_ref[:, 0]).reshape(r, d).astype(BF16)
    lead = x1.reshape(r, d)[0:SUBLANES, 0:d_conv]
    n_ff = wg_ref.shape[1] // FF_CHUNK
    chunk = lambda c: slice(c * FF_CHUNK, (c + 1) * FF_CHUNK)
    for c in range(n_ff):
        zero = vpu_slot(c, lead) if vpu_slot is not None else None
        g, u = _dot(hb, wg_ref[:, chunk(c)]), _dot(hb, wu_ref[:, chunk(c)])
        if zero is not None:
            zero = jnp.concatenate([zero] * (FF_CHUNK // LANES), axis=1)
            g = jnp.concatenate([g[:SUBLANES] + zero, g[SUBLANES:]], axis=0)
        a = (g * jax.nn.sigmoid(g)) * u
        a_ref[:, chunk(c)] = a.astype(BF16)
        lead = jnp.concatenate([a[0:SUBLANES]] * (d_conv // FF_CHUNK), axis=1)
    ffn = _dot(a_ref[...], wd_ref[...])
    y_ref[...] = x1 + g2_ref[:, 0] * ffn.reshape(nb, tt, d)


def _outproj_ffn_kernel(x_ref, co_ref, at_ref, g1_ref, sh2_ref, sc2_ref, g2_ref, n2g_ref,
                        wo_ref, wg_ref, wu_ref, wd_ref, y_ref, a_ref):
    nb, tt, _ = x_ref.shape
    co = co_ref[...].reshape(nb * tt, co_ref.shape[2])
    at = at_ref[...].reshape(nb * tt, at_ref.shape[2])
    _outproj_ffn_body(x_ref, co, at, g1_ref, sh2_ref, sc2_ref, g2_ref, n2g_ref,
                      wo_ref, wg_ref, wu_ref, wd_ref, y_ref, a_ref)


def _exact_zero(v):
    return jnp.minimum(jnp.abs(v), 0.0)


def _outproj_ffn_conv_kernel(x_ref, at_ref, un_ref, u0_ref, g1_ref, sh2_ref, sc2_ref, g2_ref, n2g_ref,
                             wo_ref, wg_ref, wu_ref, wd_ref, cw_ref, cb_ref, lng_ref, lnb_ref,
                             y_ref, a_ref, shift_ref, co_ref, wb_ref, *, tiles_per_seq):
    g = pl.program_id(0)
    tm = x_ref.shape[1]
    d_conv = co_ref.shape[1]
    span = tm + HIST - SUBLANES
    n_conv = tm // CONV_ROWS
    n_ff = wg_ref.shape[1] // FF_CHUNK
    cb, ln_g, ln_b = cb_ref[...], lng_ref[...], lnb_ref[...]

    def stage(u, hist):
        shift_ref[0, 0:HIST, :] = hist
        shift_ref[0, HIST:HIST + tm, :] = u
        for r in range(1, SUBLANES):
            shift_ref[r, 0:span, :] = shift_ref[0, r:r + span, :]

    def conv_chunk(c, bias):
        def tap(k):
            off = k + HIST - (CONV_WIDTH - 1)
            row = c * CONV_ROWS + off - off % SUBLANES
            return shift_ref[off % SUBLANES, row:row + CONV_ROWS, :]
        weight = lambda k: jnp.concatenate([wb_ref[k]] * (CONV_ROWS // SUBLANES), axis=0)
        y = _conv_ln_swish(tap, weight, bias, ln_g, ln_b)
        co_ref[c * CONV_ROWS:(c + 1) * CONV_ROWS, :] = y.astype(BF16)
        return y

    @pl.when(g == 0)
    def _():
        for k in range(CONV_WIDTH):
            wb_ref[k] = jnp.broadcast_to(cw_ref[k:k + 1, :], (SUBLANES, d_conv))
        stage(u0_ref[0], jnp.zeros((HIST, d_conv), F32))
        for c in range(n_conv):
            conv_chunk(c, cb)

    def conv_slot(c, lead):
        if c == 0:
            tail = shift_ref[0, tm:tm + HIST, :]
            starts_seq = (g + 1) % tiles_per_seq == 0
            stage(un_ref[0], jnp.where(starts_seq, 0.0, tail))
        bias = jnp.concatenate([cb + _exact_zero(lead)] * (CONV_ROWS // SUBLANES), axis=0)
        zero = jnp.zeros((SUBLANES, LANES), F32)
        for i in range(c * n_conv // n_ff, (c + 1) * n_conv // n_ff):
            zero = zero + _exact_zero(conv_chunk(i, bias)[0:SUBLANES, 0:LANES])
        return zero

    co = co_ref[...]
    _outproj_ffn_body(x_ref, co, at_ref[0], g1_ref, sh2_ref, sc2_ref, g2_ref, n2g_ref,
                      wo_ref, wg_ref, wu_ref, wd_ref, y_ref, a_ref, vpu_slot=conv_slot)


def _outproj_ffn_conv(x, at, u, mod4, mod_row0, n2g, wo, wg, wu, wd, conv_w, conv_b, ln_g, ln_b):
    b, s, d = x.shape
    tm = ROW_TILE
    d_conv = u.shape[2]
    d_attn = at.shape[2]
    tps = s // tm
    n_tiles = b * tps
    tile = lambda g: (g // tps, g % tps, 0)
    next_tile = lambda g: tile(jnp.minimum(g + 1, n_tiles - 1))
    mod_spec = lambda j: pl.BlockSpec((1, 1, 1, d), lambda g: (mod_row0 + g // tps, j, 0, 0))
    consts = (n2g, wo, wg, wu, wd, conv_w, conv_b, ln_g, ln_b)
    return pl.pallas_call(
        functools.partial(_outproj_ffn_conv_kernel, tiles_per_seq=tps),
        grid=(n_tiles,),
        in_specs=[pl.BlockSpec((1, tm, d), tile), pl.BlockSpec((1, tm, d_attn), tile),
                  pl.BlockSpec((1, tm, d_conv), next_tile),
                  pl.BlockSpec((1, tm, d_conv), lambda g: (0, 0, 0), pipeline_mode=pl.Buffered(1)),
                  mod_spec(2), mod_spec(3), mod_spec(4), mod_spec(5)] + [_const_spec(c.shape) for c in consts],
        out_specs=pl.BlockSpec((1, tm, d), tile),
        out_shape=jax.ShapeDtypeStruct((b, s, d), F32),
        scratch_shapes=[pltpu.VMEM((tm, wg.shape[1]), BF16),
                        pltpu.VMEM((SUBLANES, tm + HIST, d_conv), F32),
                        pltpu.VMEM((tm, d_conv), BF16),
                        pltpu.VMEM((CONV_WIDTH, SUBLANES, d_conv), F32)],
        compiler_params=pltpu.CompilerParams(dimension_semantics=("arbitrary",),
                                             vmem_limit_bytes=VMEM_LIMIT_BYTES),
        name="outproj_ffn_conv",
    )(x, at, u, u, mod4, mod4, mod4, mod4, *consts)


def _outproj_ffn(x, co, at, mod4, n2g, wo, wg, wu, wd, bb, tt):
    nb, s, d = x.shape
    d_conv = co.shape[2]
    d_attn = at.shape[2]
    mod_spec = lambda j: pl.BlockSpec((bb, 1, 1, d), lambda i, t: (i, j, 0, 0))
    row_spec = lambda w: pl.BlockSpec((bb, tt, w), lambda i, t: (i, t, 0))
    return pl.pallas_call(
        _outproj_ffn_kernel,
        grid=(nb // bb, s // tt),
        in_specs=[row_spec(d), row_spec(d_conv), row_spec(d_attn),
                  mod_spec(2), mod_spec(3), mod_spec(4), mod_spec(5),
                  _const_spec(n2g.shape), _const_spec(wo.shape), _const_spec(wg.shape),
                  _const_spec(wu.shape), _const_spec(wd.shape)],
        out_specs=row_spec(d),
        out_shape=jax.ShapeDtypeStruct((nb, s, d), F32),
        scratch_shapes=[pltpu.VMEM((bb * tt, wg.shape[1]), BF16)],
        compiler_params=pltpu.CompilerParams(dimension_semantics=("arbitrary", "arbitrary"),
                                             vmem_limit_bytes=VMEM_LIMIT_BYTES),
        name="outproj_ffn",
    )(x, co, at, mod4, mod4, mod4, mod4, n2g, wo, wg, wu, wd)


def _tri(n, rel):
    i = lax.broadcasted_iota(jnp.int32, (n, n), 0)
    j = lax.broadcasted_iota(jnp.int32, (n, n), 1)
    return rel(i, j).astype(BF16)


def _layer(xp, xs, cache_k, cache_v, cache_logf, state_conv, c_all, w_ada, b_ada, norm1_g, w_in, b_f,
           q_norm_g, k_norm_g, conv_w, conv_b, conv_ln_g, conv_ln_b, w_out, norm2_g, w_gate, w_up, w_down):
    b, s, d = xp.shape
    nb, tt, _ = xs.shape
    d_conv = conv_w.shape[1]
    d_attn = N_HEADS * HEAD_DIM
    d_ff = w_gate.shape[1]
    p_len = cache_k.shape[1]

    mod4 = _modulation(c_all, w_ada, b_ada).reshape(nb + b, N_MOD, 1, d)

    wglu = w_in[:, :2 * d_conv].astype(BF16)
    wqkvf_t = jnp.pad(w_in[:, 2 * d_conv:].T, ((0, 2 * SUBLANES - N_HEADS), (0, 0))).astype(BF16)
    bf_col = b_f.reshape(N_HEADS, 1)
    qg_col = q_norm_g.reshape(d_attn, 1)
    kg_col = k_norm_g.reshape(d_attn, 1)
    row = lambda v: v.reshape(1, -1)
    n1g, n2g, cb, ln_g, ln_b = row(norm1_g), row(norm2_g), row(conv_b), row(conv_ln_g), row(conv_ln_b)
    wo = w_out.astype(BF16)
    assert d_ff % FF_CHUNK == 0
    wg, wu, wd = w_gate.astype(BF16), w_up.astype(BF16), w_down.astype(BF16)
    utri = _tri(MXU_DIM, lambda i, j: i <= j)
    ltri = _tri(MXU_DIM, lambda i, j: i > j)
    bdtri = _tri(nb * tt, lambda i, j: (i // tt == j // tt) & (j <= i))

    kt, vt, lft, cst, qt, qft, k, kb, u = _inproj_prompt(
        xp, mod4, nb, n1g, wglu, wqkvf_t, bf_col, qg_col, kg_col, utri)
    at = _attn_prompt(qt, qft, k, kb, vt)
    yp = _outproj_ffn_conv(xp, at, u, mod4, nb, n2g, wo, wg, wu, wd, conv_w, cb, ln_g, ln_b)
    k_p = kt.reshape(b, N_HEADS, HEAD_DIM, s).transpose(0, 3, 1, 2)
    v_p = vt.reshape(b, N_HEADS, HEAD_DIM, s).transpose(0, 3, 1, 2)
    lf_p = lft.transpose(0, 2, 1)

    state_t = state_conv.transpose(1, 0, 2)
    k_s, v_s, lf_s, nst, q_s, qb_s, kb_s, co_s = _inproj_sample(
        xs, mod4, n1g, wglu, wqkvf_t, bf_col, qg_col, kg_col, bdtri, conv_w, cb, ln_g, ln_b, state_t)
    r3 = lambda a: a.reshape(nb, tt, a.shape[-1])
    ckt = cache_k.transpose(0, 2, 3, 1).reshape(nb, d_attn, p_len)
    cvt = cache_v.transpose(0, 2, 3, 1).reshape(nb, d_attn, p_len)
    clft = cache_logf.transpose(0, 2, 1)
    at_s = _attn_sample(r3(q_s), r3(qb_s), r3(k_s), r3(v_s), r3(kb_s), ckt, cvt, clft, ltri)
    ys = _outproj_ffn(xs, r3(co_s), at_s, mod4, n2g, wo, wg, wu, wd, nb, tt)
    k_sn = k_s.reshape(nb, tt, N_HEADS, HEAD_DIM)
    v_sn = v_s.reshape(nb, tt, N_HEADS, HEAD_DIM)
    lf_sn = lf_s[:, :N_HEADS].reshape(nb, tt, N_HEADS)
    return yp, ys, (k_p, v_p, lf_p, cst), (k_sn, v_sn, lf_sn, nst.transpose(1, 0, 2))


def kernel(x_prompt, x_sample, cache_k, cache_v, cache_logf, state_conv, c_prompt, c_sample, w_ada, b_ada,
           norm1_g, w_in, b_f, q_norm_g, k_norm_g, conv_w, conv_b, conv_ln_g, conv_ln_b, w_out, norm2_g,
           w_gate, w_up, w_down):
    depth = w_ada.shape[0]
    c_all = jnp.concatenate([c_sample, c_prompt], axis=0)
    yp, ys = x_prompt, x_sample
    st_p, st_s = [], []
    for l in range(depth):
        yp, ys, sp, ss = _layer(
            yp, ys, cache_k[l], cache_v[l], cache_logf[l], state_conv[l], c_all, w_ada[l], b_ada[l],
            norm1_g[l], w_in[l], b_f[l], q_norm_g[l], k_norm_g[l], conv_w[l], conv_b[l], conv_ln_g[l],
            conv_ln_b[l], w_out[l], norm2_g[l], w_gate[l], w_up[l], w_down[l])
        st_p.append(sp)
        st_s.append(ss)
    stack = lambda xs: xs[0][None] if len(xs) == 1 else jnp.stack(xs)
    outs_p = [stack([s[i] for s in st_p]) for i in range(4)]
    outs_s = [stack([s[i] for s in st_s]) for i in range(4)]
    return (yp, ys, *outs_p, *outs_s)
```

```python
import functools

import jax
import jax.numpy as jnp
from jax import lax
from jax.experimental import pallas as pl
from jax.experimental.pallas import tpu as pltpu

F32 = jnp.float32
BF16 = jnp.bfloat16

N_HEADS = 8
HEAD_DIM = 64
CONV_WIDTH = 31
EPS = 1e-6
NEG_INF = -1e30

LANES = 128
SUBLANES = 8
MXU_DIM = 256
VMEM_LIMIT_BYTES = 56 * 1024 * 1024

N_PARTS = 3
N_FEAT = N_PARTS * N_HEADS
HIST = 32
CONV_ROWS = 32

ROW_TILE = 512
INPROJ_TILE = 1024
Q_TILE = 256
ATTN_LAG = 2
DENOM_ROWS = 16
LOG2E = 1.4426950408889634
FF_CHUNK = 256
SAMPLE_REQS_PER_STEP = 4
N_MOD = 6
MOD_TILE = 1536


def _dot(a, b):
    return jnp.dot(a, b, preferred_element_type=F32)


def _dot_nt(a, b):
    return lax.dot_general(a, b, (((1,), (1,)), ((), ())), preferred_element_type=F32)


def _split3(x):
    hi = x.astype(BF16).astype(F32)
    r = x - hi
    mid = r.astype(BF16).astype(F32)
    lo = (r - mid).astype(BF16).astype(F32)
    return hi, mid, lo


def _log_sigmoid(x):
    return jnp.minimum(x, 0.0) - jnp.log1p(jnp.exp(-jnp.abs(x)))


def _adaln_rmsnorm(x, g, scale, shift):
    y = x * lax.rsqrt(jnp.mean(x * x, axis=-1, keepdims=True) + EPS)
    return (y * g) * (1.0 + scale) + shift


def _const_spec(shape):
    n = len(shape)
    return pl.BlockSpec(shape, lambda *_: (0,) * n, pipeline_mode=pl.Buffered(1))


def _modulation_kernel(c_ref, w_ref, b_ref, o_ref):
    c = c_ref[...]
    a = (c * jax.nn.sigmoid(c)).astype(BF16)
    o_ref[...] = _dot(a, w_ref[...].astype(BF16)) + b_ref[...]


def _modulation(c, w_ada, b_ada):
    nb, d = c.shape
    n = w_ada.shape[1]
    tn = MOD_TILE
    return pl.pallas_call(
        _modulation_kernel,
        grid=(n // tn,),
        in_specs=[pl.BlockSpec((nb, d), lambda j: (0, 0)),
                  pl.BlockSpec((d, tn), lambda j: (0, j)),
                  pl.BlockSpec((1, tn), lambda j: (0, j))],
        out_specs=pl.BlockSpec((nb, tn), lambda j: (0, j)),
        out_shape=jax.ShapeDtypeStruct((nb, n), F32),
        compiler_params=pltpu.CompilerParams(dimension_semantics=("arbitrary",),
                                             vmem_limit_bytes=VMEM_LIMIT_BYTES),
        name="modulation",
    )(c, w_ada, b_ada.reshape(1, n))


def _qkvf_feature_major(hb, wqkvf_t, qg_col, kg_col, bf_col):
    d_attn = N_HEADS * HEAD_DIM
    zt = _dot_nt(wqkvf_t, hb)
    r = zt.shape[1]

    def head_rms(z, g_col):
        z3 = z.reshape(N_HEADS, HEAD_DIM, r)
        ms = jnp.mean(z3 * z3, axis=1, keepdims=True)
        return (z3 * lax.rsqrt(ms + EPS)).reshape(d_attn, r) * g_col

    q_t = head_rms(zt[0:d_attn], qg_col)
    k_t = head_rms(zt[d_attn:2 * d_attn], kg_col)
    v_t = zt[2 * d_attn:3 * d_attn]
    lf_t = _log_sigmoid(zt[3 * d_attn:3 * d_attn + N_HEADS] + bf_col)
    return q_t, k_t, v_t, lf_t


def _conv_ln_swish(load_rows, weight, cb, ln_g, ln_b):
    acc = cb + weight(0) * load_rows(0)
    for k in range(1, CONV_WIDTH):
        acc = acc + weight(k) * load_rows(k)
    mu = jnp.mean(acc, axis=-1, keepdims=True)
    cen = acc - mu
    var = jnp.mean(cen * cen, axis=-1, keepdims=True)
    y = cen * lax.rsqrt(var + EPS) * ln_g + ln_b
    return y * jax.nn.sigmoid(y)


def _inproj_prompt_kernel(x_ref, sh_ref, sc_ref, n1g_ref, wglu_ref, wqkvf_ref, bf_ref, qg_ref, kg_ref,
                          utri_ref,
                          kt_ref, vt_ref, lft_ref, cst_ref, qt_ref, qft_ref, k_ref, kb_ref, u_ref,
                          carry_ref):
    t = pl.program_id(1)
    tm = x_ref.shape[1]
    d_conv = u_ref.shape[2]

    @pl.when(t == 0)
    def _():
        carry_ref[...] = jnp.zeros(carry_ref.shape, F32)

    carry = carry_ref[:, 0:1]
    for h in range(tm // ROW_TILE):
        rows = slice(h * ROW_TILE, (h + 1) * ROW_TILE)
        hb = _adaln_rmsnorm(x_ref[0, rows, :], n1g_ref[...], sc_ref[0, 0], sh_ref[0, 0]).astype(BF16)

        q_t, k_t, v_t, lf_t = _qkvf_feature_major(hb, wqkvf_ref[...], qg_ref[...], kg_ref[...], bf_ref[...])
        kt_ref[0, :, rows] = k_t
        vt_ref[0, :, rows] = v_t
        lft_ref[0, :, rows] = lf_t

        zg = _dot(hb, wglu_ref[...])
        u_ref[0, rows, :] = zg[:, :d_conv] * jax.nn.sigmoid(zg[:, d_conv:])

        nblk = ROW_TILE // MXU_DIM
        parts = jnp.concatenate(_split3(lf_t), axis=0)
        stacked = jnp.concatenate([parts[:, i * MXU_DIM:(i + 1) * MXU_DIM] for i in range(nblk)], axis=0)
        local = _dot(stacked.astype(BF16), utri_ref[...])
        cums = []
        for i in range(nblk):
            loc = local[i * N_FEAT:(i + 1) * N_FEAT]
            cums.append(loc + carry)
            carry = carry + loc[:, MXU_DIM - 1:MXU_DIM]
        cum_parts = jnp.concatenate(cums, axis=1)
        cum_t = (cum_parts[0:8] + cum_parts[8:16] + cum_parts[16:24]) * LOG2E

        c_hi, c_mid, c_lo = _split3(cum_t)
        ones = jnp.ones((N_FEAT, ROW_TILE), F32)
        zeros = jnp.zeros((LANES - 2 * N_FEAT, ROW_TILE), F32)
        kfeat_t = jnp.concatenate([ones, -c_hi, -c_mid, -c_lo, zeros], axis=0)
        qft_ref[0, :, rows] = jnp.concatenate([c_hi, c_mid, c_lo, ones, zeros], axis=0).astype(BF16)
        qt_ref[0, :, rows] = (q_t * (HEAD_DIM ** -0.5 * LOG2E)).astype(BF16)
        k_ref[0, rows, :] = k_t.T.astype(BF16)
        kb_ref[0, rows, :] = kfeat_t.T.astype(BF16)
    carry_ref[...] = jnp.broadcast_to(carry, carry_ref.shape)

    @pl.when(t == pl.num_programs(1) - 1)
    def _():
        cst_ref[0] = u_ref[0, tm - (CONV_WIDTH - 1):tm, :]


def _inproj_prompt(x, mod4, mod_row0, n1g, wglu, wqkvf_t, bf_col, qg_col, kg_col, utri):
    b, s, d = x.shape
    tm = INPROJ_TILE
    d_conv = wglu.shape[1] // 2
    d_attn = N_HEADS * HEAD_DIM
    mod_spec = lambda j: pl.BlockSpec((1, 1, 1, d), lambda i, t: (mod_row0 + i, j, 0, 0))
    out_shape = (
        jax.ShapeDtypeStruct((b, d_attn, s), F32),
        jax.ShapeDtypeStruct((b, d_attn, s), F32),
        jax.ShapeDtypeStruct((b, N_HEADS, s), F32),
        jax.ShapeDtypeStruct((b, CONV_WIDTH - 1, d_conv), F32),
        jax.ShapeDtypeStruct((b, d_attn, s), BF16),
        jax.ShapeDtypeStruct((b, LANES, s), BF16),
        jax.ShapeDtypeStruct((b, s, d_attn), BF16),
        jax.ShapeDtypeStruct((b, s, LANES), BF16),
        jax.ShapeDtypeStruct((b, s, d_conv), F32),
    )
    out_specs = (
        pl.BlockSpec((1, d_attn, tm), lambda i, t: (i, 0, t)),
        pl.BlockSpec((1, d_attn, tm), lambda i, t: (i, 0, t)),
        pl.BlockSpec((1, N_HEADS, tm), lambda i, t: (i, 0, t)),
        pl.BlockSpec((1, CONV_WIDTH - 1, d_conv), lambda i, t: (i, 0, 0)),
        pl.BlockSpec((1, d_attn, tm), lambda i, t: (i, 0, t)),
        pl.BlockSpec((1, LANES, tm), lambda i, t: (i, 0, t)),
        pl.BlockSpec((1, tm, d_attn), lambda i, t: (i, t, 0)),
        pl.BlockSpec((1, tm, LANES), lambda i, t: (i, t, 0)),
        pl.BlockSpec((1, tm, d_conv), lambda i, t: (i, t, 0)),
    )
    in_specs = [
        pl.BlockSpec((1, tm, d), lambda i, t: (i, t, 0)),
        mod_spec(0), mod_spec(1),
        _const_spec(n1g.shape), _const_spec(wglu.shape), _const_spec(wqkvf_t.shape),
        _const_spec(bf_col.shape), _const_spec(qg_col.shape), _const_spec(kg_col.shape),
        _const_spec(utri.shape),
    ]
    return pl.pallas_call(
        _inproj_prompt_kernel,
        grid=(b, s // tm),
        in_specs=in_specs,
        out_specs=out_specs,
        out_shape=out_shape,
        scratch_shapes=[pltpu.VMEM((N_FEAT, LANES), F32)],
        compiler_params=pltpu.CompilerParams(dimension_semantics=("arbitrary", "arbitrary"),
                                             vmem_limit_bytes=VMEM_LIMIT_BYTES),
        name="inproj_prompt",
    )(x, mod4, mod4, n1g, wglu, wqkvf_t, bf_col, qg_col, kg_col, utri)


def _attn_prompt_kernel(qt_ref, qft_ref, k_ref, kb_ref, vt_ref, o_ref,
                        kp_ref, vp_ref, q2_all, m_all, acc_all, sc_all):
    step = pl.program_id(1)
    tq = qt_ref.shape[2] // 2
    n_pairs, nblk = vp_ref.shape[0], vp_ref.shape[1]

    @pl.when(step == 0)
    def _():
        ones = jnp.ones((vp_ref.shape[2] - LANES, tq), BF16)
        for p in range(n_pairs):
            kp_ref[p, :, 0:LANES] = k_ref[0, :, p * LANES:(p + 1) * LANES]
            kp_ref[p, :, LANES:2 * LANES] = kb_ref[0]
            for i in range(nblk):
                vp_ref[p, i, 0:LANES, :] = vt_ref[0, p * LANES:(p + 1) * LANES, i * tq:(i + 1) * tq].astype(BF16)
                vp_ref[p, i, LANES:, :] = ones

    frow = lax.broadcasted_iota(jnp.int32, (LANES, tq), 0)

    def query_block(sub):
        q2_ref, m_ref, acc_ref, sc_ref = q2_all.at[sub], m_all.at[sub], acc_all.at[sub], sc_all.at[sub]
        def scores(p, j):
            start = pl.multiple_of(j * tq, tq)
            return _dot(kp_ref[p, pl.ds(start, tq), :], q2_ref[p])

        def accumulate(p, j, masked, s):
            if masked:
                key = lax.broadcasted_iota(jnp.int32, s.shape, 0)
                qry = lax.broadcasted_iota(jnp.int32, s.shape, 1)
                qry = jnp.where(qry >= tq, qry - tq, qry)
                s = jnp.where(key <= qry, s, NEG_INF)
            m_old = m_ref[p]
            m_new = jnp.maximum(m_old, jnp.max(s, axis=0, keepdims=True))
            pr = jnp.exp2(s - m_new).astype(BF16)
            acc_ref[p] = jnp.exp2(m_old - m_new) * acc_ref[p] + _dot(vp_ref[p, j], pr)
            m_ref[p] = m_new

        def run_units(units, next_block):
            pending = {}
            for i in range(len(units) + ATTN_LAG):
                if i < ATTN_LAG:
                    pending[i] = sc_ref[i]
                elif i < len(units):
                    p, j, _ = units[i]
                    pending[i] = scores(p, j)
                elif next_block is not None:
                    sc_ref[i - len(units)] = scores(i - len(units), next_block)
                if i >= ATTN_LAG:
                    p, j, masked = units[i - ATTN_LAG]
                    accumulate(p, j, masked, pending.pop(i - ATTN_LAG))

        def block(j, masked):
            return [(p, j, masked) for p in range(n_pairs)]

        if True:
            qi = 2 * step + sub
            cols = slice(sub * tq, (sub + 1) * tq)

            feat = qft_ref[0, :, cols].astype(F32)
            for p in range(n_pairs):
                qp = qt_ref[0, p * LANES:(p + 1) * LANES, cols].astype(F32)
                heads = []
                for j in range(2):
                    qm = jnp.where((frow >= j * HEAD_DIM) & (frow < (j + 1) * HEAD_DIM), qp, 0.0)
                    fm = jnp.where((frow % N_HEADS) == 2 * p + j, feat, 0.0)
                    heads.append(jnp.concatenate([qm, fm], axis=0))
                q2_ref[p] = jnp.concatenate(heads, axis=1).astype(BF16)

            m_ref[...] = jnp.full(m_ref.shape, NEG_INF, F32)
            acc_ref[...] = jnp.zeros(acc_ref.shape, F32)
            for p in range(ATTN_LAG):
                sc_ref[p] = scores(p, 0)

            def body(i, carry):
                run_units(block(2 * i, False) + block(2 * i + 1, False), 2 * i + 2)
                return carry

            lax.fori_loop(0, step, body, 0)
            if sub == 0:
                run_units(block(qi, True), None)
            else:
                run_units(block(qi - 1, False) + block(qi, True), None)

            for p in range(n_pairs):
                acc = acc_ref[p]
                out_t = acc[0:LANES] / acc[LANES:LANES + 1]
                pair_t = jnp.where(frow < HEAD_DIM, out_t[:, :tq], out_t[:, tq:])
                o_ref[0, cols, p * LANES:(p + 1) * LANES] = pair_t.T.astype(BF16)

    query_block(0)
    query_block(1)


def _attn_prompt(qt, qft, k, kb, vt):
    b, d_attn, s = qt.shape
    tq = Q_TILE
    nblk = s // tq
    n_pairs = d_attn // LANES
    assert nblk % 2 == 0 and ATTN_LAG <= n_pairs
    return pl.pallas_call(
        _attn_prompt_kernel,
        grid=(b, nblk // 2),
        in_specs=[
            pl.BlockSpec((1, d_attn, 2 * tq), lambda i, j: (i, 0, j)),
            pl.BlockSpec((1, LANES, 2 * tq), lambda i, j: (i, 0, j)),
            pl.BlockSpec((1, s, d_attn), lambda i, j: (i, 0, 0)),
            pl.BlockSpec((1, s, LANES), lambda i, j: (i, 0, 0)),
            pl.BlockSpec((1, d_attn, s), lambda i, j: (i, 0, 0)),
        ],
        out_specs=pl.BlockSpec((1, 2 * tq, d_attn), lambda i, j: (i, j, 0)),
        out_shape=jax.ShapeDtypeStruct((b, s, d_attn), BF16),
        scratch_shapes=[
            pltpu.VMEM((n_pairs, s, 2 * LANES), BF16),
            pltpu.VMEM((n_pairs, nblk, LANES + DENOM_ROWS, tq), BF16),
            pltpu.VMEM((2, n_pairs, 2 * LANES, 2 * tq), BF16),
            pltpu.VMEM((2, n_pairs, 1, 2 * tq), F32),
            pltpu.VMEM((2, n_pairs, LANES + DENOM_ROWS, 2 * tq), F32),
            pltpu.VMEM((2, ATTN_LAG, tq, 2 * tq), F32),
        ],
        compiler_params=pltpu.CompilerParams(dimension_semantics=("arbitrary", "arbitrary"),
                                             vmem_limit_bytes=VMEM_LIMIT_BYTES),
        name="attn_prompt",
    )(qt, qft, k, kb, vt)


def _inproj_sample_kernel(x_ref, sh_ref, sc_ref, n1g_ref, wglu_ref, wqkvf_ref, bf_ref, qg_ref, kg_ref,
                          bdtri_ref, cw_ref, cb_ref, lng_ref, lnb_ref, st_ref,
                          k_ref, v_ref, lf_ref, nst_ref, q_ref, qb_ref, kb_ref, co_ref,
                          u_ref, hist_ref, cof_ref):
    nb, tt, d = x_ref.shape
    r = nb * tt
    d_conv = co_ref.shape[1]
    n_state = CONV_WIDTH - 1

    h = _adaln_rmsnorm(x_ref[...], n1g_ref[...], sc_ref[:, 0], sh_ref[:, 0])
    hb = h.reshape(r, d).astype(BF16)

    n_chunks = d_conv // LANES
    zg = _dot(hb, wglu_ref[...])
    u = zg[:, :d_conv] * jax.nn.sigmoid(zg[:, d_conv:])
    for c in range(n_chunks):
        u_ref[c] = u[:, c * LANES:(c + 1) * LANES]

    q_t, k_t, v_t, lf_t = _qkvf_feature_major(hb, wqkvf_ref[...], qg_ref[...], kg_ref[...], bf_ref[...])
    k_ref[...] = k_t.T
    v_ref[...] = v_t.T
    q_ref[...] = (q_t * (HEAD_DIM ** -0.5)).T.astype(BF16)
    lf = jnp.concatenate([lf_t, jnp.zeros((LANES - N_HEADS, r), F32)], axis=0).T
    lf_ref[...] = lf

    lane = lax.broadcasted_iota(jnp.int32, (r, LANES), 1)
    hi, mid, lo = _split3(lf)
    packed = hi + pltpu.roll(mid, N_HEADS, 1) + pltpu.roll(lo, 2 * N_HEADS, 1)
    c = _dot(bdtri_ref[...], packed.astype(BF16))
    cn = jnp.where(lane < N_HEADS,
                   c + pltpu.roll(c, LANES - N_HEADS, 1) + pltpu.roll(c, LANES - 2 * N_HEADS, 1), 0.0)
    hi, mid, lo = _split3(cn)
    p = hi + pltpu.roll(mid, N_HEADS, 1) + pltpu.roll(lo, 2 * N_HEADS, 1)
    qb_ref[...] = (p + jnp.where((lane >= N_FEAT) & (lane < 2 * N_FEAT), 1.0, 0.0)).astype(BF16)
    kb_ref[...] = (jnp.where(lane < N_FEAT, 1.0, 0.0) - pltpu.roll(p, N_FEAT, 1)).astype(BF16)

    hist_ref[0:n_state] = st_ref[...]
    for t in range(tt):
        for c in range(n_chunks):
            hist_ref[n_state + t, :, c * LANES:(c + 1) * LANES] = u_ref[c, pl.ds(t, nb, stride=tt), :]
    nst_ref[...] = hist_ref[tt:tt + n_state]
    cb, ln_g, ln_b = cb_ref[...], lng_ref[...], lnb_ref[...]
    for t in range(tt):
        y = _conv_ln_swish(lambda k: hist_ref[t + k], lambda k: cw_ref[k:k + 1, :], cb, ln_g, ln_b)
        for c in range(n_chunks):
            cof_ref[c, pl.ds(t, nb, stride=tt), :] = y[:, c * LANES:(c + 1) * LANES]
    co_ref[...] = jnp.concatenate([cof_ref[c] for c in range(n_chunks)], axis=1).astype(BF16)


def _inproj_sample(x, mod4, n1g, wglu, wqkvf_t, bf_col, qg_col, kg_col, bdtri, conv_w, conv_b, ln_g, ln_b,
                   state_t):
    nb, tt, d = x.shape
    r = nb * tt
    d_conv = conv_w.shape[1]
    d_attn = N_HEADS * HEAD_DIM
    n_state = CONV_WIDTH - 1
    mod_spec = lambda j: pl.BlockSpec((nb, 1, 1, d), lambda i: (0, j, 0, 0))
    full = lambda shape: pl.BlockSpec(shape, lambda i: (0,) * len(shape))
    out_shape = (
        jax.ShapeDtypeStruct((r, d_attn), F32),
        jax.ShapeDtypeStruct((r, d_attn), F32),
        jax.ShapeDtypeStruct((r, LANES), F32),
        jax.ShapeDtypeStruct((n_state, nb, d_conv), F32),
        jax.ShapeDtypeStruct((r, d_attn), BF16),
        jax.ShapeDtypeStruct((r, LANES), BF16),
        jax.ShapeDtypeStruct((r, LANES), BF16),
        jax.ShapeDtypeStruct((r, d_conv), BF16),
    )
    args = (x, mod4, mod4, n1g, wglu, wqkvf_t, bf_col, qg_col, kg_col, bdtri, conv_w, conv_b, ln_g, ln_b,
            state_t)
    in_specs = [full(x.shape), mod_spec(0), mod_spec(1)] + [full(a.shape) for a in args[3:]]
    return pl.pallas_call(
        _inproj_sample_kernel,
        grid=(1,),
        in_specs=in_specs,
        out_specs=tuple(full(o.shape) for o in out_shape),
        out_shape=out_shape,
        scratch_shapes=[pltpu.VMEM((d_conv // LANES, r, LANES), F32),
                        pltpu.VMEM((n_state + tt, nb, d_conv), F32),
                        pltpu.VMEM((d_conv // LANES, r, LANES), F32)],
        compiler_params=pltpu.CompilerParams(dimension_semantics=("arbitrary",),
                                             vmem_limit_bytes=VMEM_LIMIT_BYTES),
        name="inproj_sample",
    )(*args)


def _attn_sample_kernel(*refs):
    for i in range(refs[0].shape[0]):
        _attn_sample_request(i, *refs)


def _attn_sample_request(i, q_ref, qb_ref, kn_ref, vn_ref, kbn_ref, kt_ref, vt_ref, clft_ref, ltri_ref, o_ref):
    tt = q_ref.shape[1]
    d_attn = q_ref.shape[2]
    p_len = kt_ref.shape[2]
    nblk = p_len // MXU_DIM
    rows = N_HEADS * tt

    parts = jnp.concatenate(_split3(clft_ref[i]), axis=0)
    blocks = [parts[:, blk * MXU_DIM:(blk + 1) * MXU_DIM] for blk in range(nblk)]
    local = _dot(jnp.concatenate(blocks, axis=0).astype(BF16), ltri_ref[...])
    off = jnp.zeros((N_FEAT, 1), F32)
    sufs = [None] * nblk
    for blk in reversed(range(nblk)):
        loc = local[blk * N_FEAT:(blk + 1) * N_FEAT]
        sufs[blk] = loc + off
        off = off + loc[:, 0:1] + blocks[blk][:, 0:1]
    suf_parts = jnp.concatenate(sufs, axis=1)
    ck_rel = suf_parts[0:8] + suf_parts[8:16] + suf_parts[16:24]
    c_hi, c_mid, c_lo = _split3(ck_rel)
    kbt = jnp.concatenate([jnp.ones((N_FEAT, p_len), F32), c_hi, c_mid, c_lo,
                           jnp.zeros((LANES - 2 * N_FEAT, p_len), F32)], axis=0)
    k_all = jnp.concatenate([kt_ref[i].astype(BF16), kbt.astype(BF16)], axis=0)

    lane_q = lax.broadcasted_iota(jnp.int32, (tt, d_attn), 1)
    lane_b = lax.broadcasted_iota(jnp.int32, (tt, LANES), 1)
    q = q_ref[i].astype(F32)
    qb = qb_ref[i].astype(F32)
    stack = []
    for h in range(N_HEADS):
        qm = jnp.where((lane_q >= h * HEAD_DIM) & (lane_q < (h + 1) * HEAD_DIM), q, 0.0)
        bm = jnp.where((lane_b % N_HEADS) == h, qb, 0.0)
        stack.append(jnp.concatenate([qm, bm], axis=1))
    qs = jnp.concatenate(stack, axis=0).astype(BF16)

    s_c = _dot(qs, k_all)

    pad = jnp.zeros((LANES - tt, d_attn + LANES), F32)
    kn = jnp.concatenate([jnp.concatenate([kn_ref[i], kbn_ref[i].astype(F32)], axis=1), pad], axis=0)
    s_n = _dot_nt(qs, kn.astype(BF16))
    row = lax.broadcasted_iota(jnp.int32, s_n.shape, 0)
    col = lax.broadcasted_iota(jnp.int32, s_n.shape, 1)
    s_n = jnp.where(col <= (row % tt), s_n, NEG_INF)

    m = jnp.maximum(jnp.max(s_c, axis=-1, keepdims=True), jnp.max(s_n, axis=-1, keepdims=True))
    p_c = jnp.exp(s_c - m)
    p_n = jnp.exp(s_n - m)
    l = jnp.sum(p_c, axis=-1, keepdims=True) + jnp.sum(p_n, axis=-1, keepdims=True)
    vn = jnp.concatenate([vn_ref[i], jnp.zeros((LANES - tt, d_attn), F32)], axis=0).astype(BF16)
    o = _dot_nt(p_c.astype(BF16), vt_ref[i].astype(BF16)) + _dot(p_n.astype(BF16), vn)
    o = o / l

    out = jnp.zeros((tt, d_attn), F32)
    for h in range(N_HEADS):
        out = jnp.where((lane_q >= h * HEAD_DIM) & (lane_q < (h + 1) * HEAD_DIM), o[h * tt:(h + 1) * tt], out)
    o_ref[i] = out.astype(BF16)


def _attn_sample(q, qb, kn, vn, kbn, kt, vt, clft, ltri):
    nb, tt, d_attn = q.shape
    p_len = kt.shape[2]
    per_req = lambda shape: pl.BlockSpec((SAMPLE_REQS_PER_STEP,) + shape, lambda i: (i, 0, 0))
    return pl.pallas_call(
        _attn_sample_kernel,
        grid=(nb // SAMPLE_REQS_PER_STEP,),
        in_specs=[per_req((tt, d_attn)), per_req((tt, LANES)), per_req((tt, d_attn)), per_req((tt, d_attn)),
                  per_req((tt, LANES)), per_req((d_attn, p_len)), per_req((d_attn, p_len)),
                  per_req((N_HEADS, p_len)), _const_spec(ltri.shape)],
        out_specs=per_req((tt, d_attn)),
        out_shape=jax.ShapeDtypeStruct((nb, tt, d_attn), BF16),
        compiler_params=pltpu.CompilerParams(dimension_semantics=("arbitrary",),
                                             vmem_limit_bytes=VMEM_LIMIT_BYTES),
        name="attn_sample",
    )(q, qb, kn, vn, kbn, kt, vt, clft, ltri)


def _outproj_ffn_body(x_ref, co, at, g1_ref, sh2_ref, sc2_ref, g2_ref, n2g_ref,
                      wo_ref, wg_ref, wu_ref, wd_ref, y_ref, a_ref, vpu_slot=None):
    nb, tt, d = x_ref.shape
    r = nb * tt
    d_conv = co.shape[1]
    mix = _dot(co, wo_ref[0:d_conv, :]) + _dot(at, wo_ref[d_conv:, :])
    x1 = x_ref[...] + g1_ref[:, 0] * mix.reshape(nb, tt, d)
    hb = _adaln_rmsnorm(x1, n2g_ref[...], sc2_ref[:, 0], sh2_ref[:, 0]).reshape(r, d).astype(BF16)
    lead = x1.reshape(r, d)[0:SUBLANES, 0:d_conv]
    n_ff = wg_ref.shape[1] // FF_CHUNK
    chunk = lambda c: slice(c * FF_CHUNK, (c + 1) * FF_CHUNK)
    for c in range(n_ff):
        zero = vpu_slot(c, lead) if vpu_slot is not None else None
        g, u = _dot(hb, wg_ref[:, chunk(c)]), _dot(hb, wu_ref[:, chunk(c)])
        if zero is not None:
            zero = jnp.concatenate([zero] * (FF_CHUNK // LANES), axis=1)
            g = jnp.concatenate([g[:SUBLANES] + zero, g[SUBLANES:]], axis=0)
        a = (g * jax.nn.sigmoid(g)) * u
        a_ref[:, chunk(c)] = a.astype(BF16)
        lead = jnp.concatenate([a[0:SUBLANES]] * (d_conv // FF_CHUNK), axis=1)
    ffn = _dot(a_ref[...], wd_ref[...])
    y_ref[...] = x1 + g2_ref[:, 0] * ffn.reshape(nb, tt, d)


def _outproj_ffn_kernel(x_ref, co_ref, at_ref, g1_ref, sh2_ref, sc2_ref, g2_ref, n2g_ref,
                        wo_ref, wg_ref, wu_ref, wd_ref, y_ref, a_ref):
    nb, tt, _ = x_ref.shape
    co = co_ref[...].reshape(nb * tt, co_ref.shape[2])
    at = at_ref[...].reshape(nb * tt, at_ref.shape[2])
    _outproj_ffn_body(x_ref, co, at, g1_ref, sh2_ref, sc2_ref, g2_ref, n2g_ref,
                      wo_ref, wg_ref, wu_ref, wd_ref, y_ref, a_ref)


def _exact_zero(v):
    return jnp.minimum(jnp.abs(v), 0.0)


def _outproj_ffn_conv_kernel(x_ref, at_ref, un_ref, u0_ref, g1_ref, sh2_ref, sc2_ref, g2_ref, n2g_ref,
                             wo_ref, wg_ref, wu_ref, wd_ref, cw_ref, cb_ref, lng_ref, lnb_ref,
                             y_ref, a_ref, shift_ref, co_ref, wb_ref, *, tiles_per_seq):
    g = pl.program_id(0)
    tm = x_ref.shape[1]
    d_conv = co_ref.shape[1]
    span = tm + HIST - SUBLANES
    n_conv = tm // CONV_ROWS
    n_ff = wg_ref.shape[1] // FF_CHUNK
    cb, ln_g, ln_b = cb_ref[...], lng_ref[...], lnb_ref[...]

    def stage(u, hist):
        shift_ref[0, 0:HIST, :] = hist
        shift_ref[0, HIST:HIST + tm, :] = u
        for r in range(1, SUBLANES):
            shift_ref[r, 0:span, :] = shift_ref[0, r:r + span, :]

    def conv_chunk(c, bias):
        def tap(k):
            off = k + HIST - (CONV_WIDTH - 1)
            row = c * CONV_ROWS + off - off % SUBLANES
            return shift_ref[off % SUBLANES, row:row + CONV_ROWS, :]
        weight = lambda k: jnp.concatenate([wb_ref[k]] * (CONV_ROWS // SUBLANES), axis=0)
        y = _conv_ln_swish(tap, weight, bias, ln_g, ln_b)
        co_ref[c * CONV_ROWS:(c + 1) * CONV_ROWS, :] = y.astype(BF16)
        return y

    @pl.when(g == 0)
    def _():
        for k in range(CONV_WIDTH):
            wb_ref[k] = jnp.broadcast_to(cw_ref[k:k + 1, :], (SUBLANES, d_conv))
        stage(u0_ref[0], jnp.zeros((HIST, d_conv), F32))
        for c in range(n_conv):
            conv_chunk(c, cb)

    def conv_slot(c, lead):
        if c == 0:
            tail = shift_ref[0, tm:tm + HIST, :]
            starts_seq = (g + 1) % tiles_per_seq == 0
            stage(un_ref[0], jnp.where(starts_seq, 0.0, tail))
        bias = jnp.concatenate([cb + _exact_zero(lead)] * (CONV_ROWS // SUBLANES), axis=0)
        zero = jnp.zeros((SUBLANES, LANES), F32)
        for i in range(c * n_conv // n_ff, (c + 1) * n_conv // n_ff):
            zero = zero + _exact_zero(conv_chunk(i, bias)[0:SUBLANES, 0:LANES])
        return zero

    co = co_ref[...]
    _outproj_ffn_body(x_ref, co, at_ref[0], g1_ref, sh2_ref, sc2_ref, g2_ref, n2g_ref,
                      wo_ref, wg_ref, wu_ref, wd_ref, y_ref, a_ref, vpu_slot=conv_slot)


def _outproj_ffn_conv(x, at, u, mod4, mod_row0, n2g, wo, wg, wu, wd, conv_w, conv_b, ln_g, ln_b):
    b, s, d = x.shape
    tm = ROW_TILE
    d_conv = u.shape[2]
    d_attn = at.shape[2]
    tps = s // tm
    n_tiles = b * tps
    tile = lambda g: (g // tps, g % tps, 0)
    next_tile = lambda g: tile(jnp.minimum(g + 1, n_tiles - 1))
    mod_spec = lambda j: pl.BlockSpec((1, 1, 1, d), lambda g: (mod_row0 + g // tps, j, 0, 0))
    consts = (n2g, wo, wg, wu, wd, conv_w, conv_b, ln_g, ln_b)
    return pl.pallas_call(
        functools.partial(_outproj_ffn_conv_kernel, tiles_per_seq=tps),
        grid=(n_tiles,),
        in_specs=[pl.BlockSpec((1, tm, d), tile), pl.BlockSpec((1, tm, d_attn), tile),
                  pl.BlockSpec((1, tm, d_conv), next_tile),
                  pl.BlockSpec((1, tm, d_conv), lambda g: (0, 0, 0), pipeline_mode=pl.Buffered(1)),
                  mod_spec(2), mod_spec(3), mod_spec(4), mod_spec(5)] + [_const_spec(c.shape) for c in consts],
        out_specs=pl.BlockSpec((1, tm, d), tile),
        out_shape=jax.ShapeDtypeStruct((b, s, d), F32),
        scratch_shapes=[pltpu.VMEM((tm, wg.shape[1]), BF16),
                        pltpu.VMEM((SUBLANES, tm + HIST, d_conv), F32),
                        pltpu.VMEM((tm, d_conv), BF16),
                        pltpu.VMEM((CONV_WIDTH, SUBLANES, d_conv), F32)],
        compiler_params=pltpu.CompilerParams(dimension_semantics=("arbitrary",),
                                             vmem_limit_bytes=VMEM_LIMIT_BYTES),
        name="outproj_ffn_conv",
    )(x, at, u, u, mod4, mod4, mod4, mod4, *consts)


def _outproj_ffn(x, co, at, mod4, n2g, wo, wg, wu, wd, bb, tt):
    nb, s, d = x.shape
    d_conv = co.shape[2]
    d_attn = at.shape[2]
    mod_spec = lambda j: pl.BlockSpec((bb, 1, 1, d), lambda i, t: (i, j, 0, 0))
    row_spec = lambda w: pl.BlockSpec((bb, tt, w), lambda i, t: (i, t, 0))
    return pl.pallas_call(
        _outproj_ffn_kernel,
        grid=(nb // bb, s // tt),
        in_specs=[row_spec(d), row_spec(d_conv), row_spec(d_attn),
                  mod_spec(2), mod_spec(3), mod_spec(4), mod_spec(5),
                  _const_spec(n2g.shape), _const_spec(wo.shape), _const_spec(wg.shape),
                  _const_spec(wu.shape), _const_spec(wd.shape)],
        out_specs=row_spec(d),
        out_shape=jax.ShapeDtypeStruct((nb, s, d), F32),
        scratch_shapes=[pltpu.VMEM((bb * tt, wg.shape[1]), BF16)],
        compiler_params=pltpu.CompilerParams(dimension_semantics=("arbitrary", "arbitrary"),
                                             vmem_limit_bytes=VMEM_LIMIT_BYTES),
        name="outproj_ffn",
    )(x, co, at, mod4, mod4, mod4, mod4, n2g, wo, wg, wu, wd)


def _tri(n, rel):
    i = lax.broadcasted_iota(jnp.int32, (n, n), 0)
    j = lax.broadcasted_iota(jnp.int32, (n, n), 1)
    return rel(i, j).astype(BF16)


def _layer(xp, xs, cache_k, cache_v, cache_logf, state_conv, c_all, w_ada, b_ada, norm1_g, w_in, b_f,
           q_norm_g, k_norm_g, conv_w, conv_b, conv_ln_g, conv_ln_b, w_out, norm2_g, w_gate, w_up, w_down):
    b, s, d = xp.shape
    nb, tt, _ = xs.shape
    d_conv = conv_w.shape[1]
    d_attn = N_HEADS * HEAD_DIM
    d_ff = w_gate.shape[1]
    p_len = cache_k.shape[1]

    mod4 = _modulation(c_all, w_ada, b_ada).reshape(nb + b, N_MOD, 1, d)

    wglu = w_in[:, :2 * d_conv].astype(BF16)
    wqkvf_t = jnp.pad(w_in[:, 2 * d_conv:].T, ((0, 2 * SUBLANES - N_HEADS), (0, 0))).astype(BF16)
    bf_col = b_f.reshape(N_HEADS, 1)
    qg_col = q_norm_g.reshape(d_attn, 1)
    kg_col = k_norm_g.reshape(d_attn, 1)
    row = lambda v: v.reshape(1, -1)
    n1g, n2g, cb, ln_g, ln_b = row(norm1_g), row(norm2_g), row(conv_b), row(conv_ln_g), row(conv_ln_b)
    wo = w_out.astype(BF16)
    assert d_ff % FF_CHUNK == 0
    wg, wu, wd = w_gate.astype(BF16), w_up.astype(BF16), w_down.astype(BF16)
    utri = _tri(MXU_DIM, lambda i, j: i <= j)
    ltri = _tri(MXU_DIM, lambda i, j: i > j)
    bdtri = _tri(nb * tt, lambda i, j: (i // tt == j // tt) & (j <= i))

    kt, vt, lft, cst, qt, qft, k, kb, u = _inproj_prompt(
        xp, mod4, nb, n1g, wglu, wqkvf_t, bf_col, qg_col, kg_col, utri)
    at = _attn_prompt(qt, qft, k, kb, vt)
    yp = _outproj_ffn_conv(xp, at, u, mod4, nb, n2g, wo, wg, wu, wd, conv_w, cb, ln_g, ln_b)
    k_p = kt.reshape(b, N_HEADS, HEAD_DIM, s).transpose(0, 3, 1, 2)
    v_p = vt.reshape(b, N_HEADS, HEAD_DIM, s).transpose(0, 3, 1, 2)
    lf_p = lft.transpose(0, 2, 1)

    state_t = state_conv.transpose(1, 0, 2)
    k_s, v_s, lf_s, nst, q_s, qb_s, kb_s, co_s = _inproj_sample(
        xs, mod4, n1g, wglu, wqkvf_t, bf_col, qg_col, kg_col, bdtri, conv_w, cb, ln_g, ln_b, state_t)
    r3 = lambda a: a.reshape(nb, tt, a.shape[-1])
    ckt = cache_k.transpose(0, 2, 3, 1).reshape(nb, d_attn, p_len)
    cvt = cache_v.transpose(0, 2, 3, 1).reshape(nb, d_attn, p_len)
    clft = cache_logf.transpose(0, 2, 1)
    at_s = _attn_sample(r3(q_s), r3(qb_s), r3(k_s), r3(v_s), r3(kb_s), ckt, cvt, clft, ltri)
    ys = _outproj_ffn(xs, r3(co_s), at_s, mod4, n2g, wo, wg, wu, wd, nb, tt)
    k_sn = k_s.reshape(nb, tt, N_HEADS, HEAD_DIM)
    v_sn = v_s.reshape(nb, tt, N_HEADS, HEAD_DIM)
    lf_sn = lf_s[:, :N_HEADS].reshape(nb, tt, N_HEADS)
    return yp, ys, (k_p, v_p, lf_p, cst), (k_sn, v_sn, lf_sn, nst.transpose(1, 0, 2))


def kernel(x_prompt, x_sample, cache_k, cache_v, cache_logf, state_conv, c_prompt, c_sample, w_ada, b_ada,
           norm1_g, w_in, b_f, q_norm_g, k_norm_g, conv_w, conv_b, conv_ln_g, conv_ln_b, w_out, norm2_g,
           w_gate, w_up, w_down):
    depth = w_ada.shape[0]
    c_all = jnp.concatenate([c_sample, c_prompt], axis=0)
    yp, ys = x_prompt, x_sample
    st_p, st_s = [], []
    for l in range(depth):
        yp, ys, sp, ss = _layer(
            yp, ys, cache_k[l], cache_v[l], cache_logf[l], state_conv[l], c_all, w_ada[l], b_ada[l],
            norm1_g[l], w_in[l], b_f[l], q_norm_g[l], k_norm_g[l], conv_w[l], conv_b[l], conv_ln_g[l],
            conv_ln_b[l], w_out[l], norm2_g[l], w_gate[l], w_up[l], w_down[l])
        st_p.append(sp)
        st_s.append(ss)
    stack = lambda xs: xs[0][None] if len(xs) == 1 else jnp.stack(xs)
    outs_p = [stack([s[i] for s in st_p]) for i in range(4)]
    outs_s = [stack([s[i] for s in st_s]) for i in range(4)]
    return (yp, ys, *outs_p, *outs_s)
```

```python
import functools

import jax
import jax.numpy as jnp
from jax import lax
from jax.experimental import pallas as pl
from jax.experimental.pallas import tpu as pltpu

F32 = jnp.float32
BF16 = jnp.bfloat16

N_HEADS = 8
HEAD_DIM = 64
CONV_WIDTH = 31
EPS = 1e-6
NEG_INF = -1e30

LANES = 128
SUBLANES = 8
MXU_DIM = 256
VMEM_LIMIT_BYTES = 56 * 1024 * 1024

N_PARTS = 3
N_FEAT = N_PARTS * N_HEADS
HIST = 32
CONV_ROWS = 32

ROW_TILE = 512
INPROJ_TILE = 1024
Q_TILE = 256
Q_BLOCKS_PER_STEP = 4
ATTN_LAG = 2
DENOM_ROWS = 16
LOG2E = 1.4426950408889634
FF_CHUNK = 256
SAMPLE_REQS_PER_STEP = 4
N_MOD = 6
MOD_TILE = 1536


def _dot(a, b):
    return jnp.dot(a, b, preferred_element_type=F32)


def _dot_nt(a, b):
    return lax.dot_general(a, b, (((1,), (1,)), ((), ())), preferred_element_type=F32)


def _split3(x):
    hi = x.astype(BF16).astype(F32)
    r = x - hi
    mid = r.astype(BF16).astype(F32)
    lo = (r - mid).astype(BF16).astype(F32)
    return hi, mid, lo


def _log_sigmoid(x):
    return jnp.minimum(x, 0.0) - jnp.log1p(jnp.exp(-jnp.abs(x)))


def _adaln_rmsnorm(x, g, scale, shift):
    y = x * lax.rsqrt(jnp.mean(x * x, axis=-1, keepdims=True) + EPS)
    return (y * g) * (1.0 + scale) + shift


def _const_spec(shape):
    n = len(shape)
    return pl.BlockSpec(shape, lambda *_: (0,) * n, pipeline_mode=pl.Buffered(1))


def _modulation_kernel(c_ref, w_ref, b_ref, o_ref):
    c = c_ref[...]
    a = (c * jax.nn.sigmoid(c)).astype(BF16)
    o_ref[...] = _dot(a, w_ref[...].astype(BF16)) + b_ref[...]


def _modulation(c, w_ada, b_ada):
    nb, d = c.shape
    n = w_ada.shape[1]
    tn = MOD_TILE
    return pl.pallas_call(
        _modulation_kernel,
        grid=(n // tn,),
        in_specs=[pl.BlockSpec((nb, d), lambda j: (0, 0)),
                  pl.BlockSpec((d, tn), lambda j: (0, j)),
                  pl.BlockSpec((1, tn), lambda j: (0, j))],
        out_specs=pl.BlockSpec((nb, tn), lambda j: (0, j)),
        out_shape=jax.ShapeDtypeStruct((nb, n), F32),
        compiler_params=pltpu.CompilerParams(dimension_semantics=("arbitrary",),
                                             vmem_limit_bytes=VMEM_LIMIT_BYTES),
        name="modulation",
    )(c, w_ada, b_ada.reshape(1, n))


def _qkvf_feature_major(hb, wqkvf_t, qg_col, kg_col, bf_col):
    d_attn = N_HEADS * HEAD_DIM
    zt = _dot_nt(wqkvf_t, hb)
    r = zt.shape[1]

    def head_rms(z, g_col):
        z3 = z.reshape(N_HEADS, HEAD_DIM, r)
        ms = jnp.mean(z3 * z3, axis=1, keepdims=True)
        return (z3 * lax.rsqrt(ms + EPS)).reshape(d_attn, r) * g_col

    q_t = head_rms(zt[0:d_attn], qg_col)
    k_t = head_rms(zt[d_attn:2 * d_attn], kg_col)
    v_t = zt[2 * d_attn:3 * d_attn]
    lf_t = _log_sigmoid(zt[3 * d_attn:3 * d_attn + N_HEADS] + bf_col)
    return q_t, k_t, v_t, lf_t


def _conv_ln_swish(load_rows, weight, cb, ln_g, ln_b):
    acc = cb + weight(0) * load_rows(0)
    for k in range(1, CONV_WIDTH):
        acc = acc + weight(k) * load_rows(k)
    mu = jnp.mean(acc, axis=-1, keepdims=True)
    cen = acc - mu
    var = jnp.mean(cen * cen, axis=-1, keepdims=True)
    y = cen * lax.rsqrt(var + EPS) * ln_g + ln_b
    return y * jax.nn.sigmoid(y)


def _inproj_prompt_kernel(x_ref, sh_ref, sc_ref, n1g_ref, wglu_ref, wqkvf_ref, bf_ref, qg_ref, kg_ref,
                          utri_ref,
                          kt_ref, vt_ref, lft_ref, cst_ref, qt_ref, qft_ref, k_ref, kb_ref, u_ref,
                          carry_ref):
    t = pl.program_id(1)
    tm = x_ref.shape[1]
    d_conv = u_ref.shape[2]

    @pl.when(t == 0)
    def _():
        carry_ref[...] = jnp.zeros(carry_ref.shape, F32)

    carry = carry_ref[:, 0:1]
    for h in range(tm // ROW_TILE):
        rows = slice(h * ROW_TILE, (h + 1) * ROW_TILE)
        hb = _adaln_rmsnorm(x_ref[0, rows, :], n1g_ref[...], sc_ref[0, 0], sh_ref[0, 0]).astype(BF16)

        q_t, k_t, v_t, lf_t = _qkvf_feature_major(hb, wqkvf_ref[...], qg_ref[...], kg_ref[...], bf_ref[...])
        kt_ref[0, :, rows] = k_t
        vt_ref[0, :, rows] = v_t
        lft_ref[0, :, rows] = lf_t

        zg = _dot(hb, wglu_ref[...])
        u_ref[0, rows, :] = zg[:, :d_conv] * jax.nn.sigmoid(zg[:, d_conv:])

        nblk = ROW_TILE // MXU_DIM
        parts = jnp.concatenate(_split3(lf_t), axis=0)
        stacked = jnp.concatenate([parts[:, i * MXU_DIM:(i + 1) * MXU_DIM] for i in range(nblk)], axis=0)
        local = _dot(stacked.astype(BF16), utri_ref[...])
        cums = []
        for i in range(nblk):
            loc = local[i * N_FEAT:(i + 1) * N_FEAT]
            cums.append(loc + carry)
            carry = carry + loc[:, MXU_DIM - 1:MXU_DIM]
        cum_parts = jnp.concatenate(cums, axis=1)
        cum_t = (cum_parts[0:8] + cum_parts[8:16] + cum_parts[16:24]) * LOG2E

        c_hi, c_mid, c_lo = _split3(cum_t)
        ones = jnp.ones((N_FEAT, ROW_TILE), F32)
        zeros = jnp.zeros((LANES - 2 * N_FEAT, ROW_TILE), F32)
        kfeat_t = jnp.concatenate([ones, -c_hi, -c_mid, -c_lo, zeros], axis=0)
        qft_ref[0, :, rows] = jnp.concatenate([c_hi, c_mid, c_lo, ones, zeros], axis=0).astype(BF16)
        qt_ref[0, :, rows] = (q_t * (HEAD_DIM ** -0.5 * LOG2E)).astype(BF16)
        k_ref[0, rows, :] = k_t.T.astype(BF16)
        kb_ref[0, rows, :] = kfeat_t.T.astype(BF16)
    carry_ref[...] = jnp.broadcast_to(carry, carry_ref.shape)

    @pl.when(t == pl.num_programs(1) - 1)
    def _():
        cst_ref[0] = u_ref[0, tm - (CONV_WIDTH - 1):tm, :]


def _inproj_prompt(x, mod4, mod_row0, n1g, wglu, wqkvf_t, bf_col, qg_col, kg_col, utri):
    b, s, d = x.shape
    tm = INPROJ_TILE
    d_conv = wglu.shape[1] // 2
    d_attn = N_HEADS * HEAD_DIM
    mod_spec = lambda j: pl.BlockSpec((1, 1, 1, d), lambda i, t: (mod_row0 + i, j, 0, 0))
    out_shape = (
        jax.ShapeDtypeStruct((b, d_attn, s), F32),
        jax.ShapeDtypeStruct((b, d_attn, s), F32),
        jax.ShapeDtypeStruct((b, N_HEADS, s), F32),
        jax.ShapeDtypeStruct((b, CONV_WIDTH - 1, d_conv), F32),
        jax.ShapeDtypeStruct((b, d_attn, s), BF16),
        jax.ShapeDtypeStruct((b, LANES, s), BF16),
        jax.ShapeDtypeStruct((b, s, d_attn), BF16),
        jax.ShapeDtypeStruct((b, s, LANES), BF16),
        jax.ShapeDtypeStruct((b, s, d_conv), F32),
    )
    out_specs = (
        pl.BlockSpec((1, d_attn, tm), lambda i, t: (i, 0, t)),
        pl.BlockSpec((1, d_attn, tm), lambda i, t: (i, 0, t)),
        pl.BlockSpec((1, N_HEADS, tm), lambda i, t: (i, 0, t)),
        pl.BlockSpec((1, CONV_WIDTH - 1, d_conv), lambda i, t: (i, 0, 0)),
        pl.BlockSpec((1, d_attn, tm), lambda i, t: (i, 0, t)),
        pl.BlockSpec((1, LANES, tm), lambda i, t: (i, 0, t)),
        pl.BlockSpec((1, tm, d_attn), lambda i, t: (i, t, 0)),
        pl.BlockSpec((1, tm, LANES), lambda i, t: (i, t, 0)),
        pl.BlockSpec((1, tm, d_conv), lambda i, t: (i, t, 0)),
    )
    in_specs = [
        pl.BlockSpec((1, tm, d), lambda i, t: (i, t, 0)),
        mod_spec(0), mod_spec(1),
        _const_spec(n1g.shape), _const_spec(wglu.shape), _const_spec(wqkvf_t.shape),
        _const_spec(bf_col.shape), _const_spec(qg_col.shape), _const_spec(kg_col.shape),
        _const_spec(utri.shape),
    ]
    return pl.pallas_call(
        _inproj_prompt_kernel,
        grid=(b, s // tm),
        in_specs=in_specs,
        out_specs=out_specs,
        out_shape=out_shape,
        scratch_shapes=[pltpu.VMEM((N_FEAT, LANES), F32)],
        compiler_params=pltpu.CompilerParams(dimension_semantics=("arbitrary", "arbitrary"),
                                             vmem_limit_bytes=VMEM_LIMIT_BYTES),
        name="inproj_prompt",
    )(x, mod4, mod4, n1g, wglu, wqkvf_t, bf_col, qg_col, kg_col, utri)


def _attn_prompt_kernel(qt_ref, qft_ref, k_ref, kb_ref, vt_ref, o_ref,
                        kp_ref, vp_ref, q2_all, m_all, acc_all, sc_all):
    step = pl.program_id(1)
    tq = Q_TILE
    n_sub = qt_ref.shape[2] // tq
    n_pairs, nblk = vp_ref.shape[0], vp_ref.shape[1]

    @pl.when(step == 0)
    def _():
        ones = jnp.ones((vp_ref.shape[2] - LANES, tq), BF16)
        for p in range(n_pairs):
            kp_ref[p, :, 0:LANES] = k_ref[0, :, p * LANES:(p + 1) * LANES]
            kp_ref[p, :, LANES:2 * LANES] = kb_ref[0]
            for i in range(nblk):
                vp_ref[p, i, 0:LANES, :] = vt_ref[0, p * LANES:(p + 1) * LANES, i * tq:(i + 1) * tq].astype(BF16)
                vp_ref[p, i, LANES:, :] = ones

    frow = lax.broadcasted_iota(jnp.int32, (LANES, tq), 0)

    def query_block(sub):
        q2_ref, m_ref, acc_ref, sc_ref = q2_all.at[sub], m_all.at[sub], acc_all.at[sub], sc_all.at[sub]
        qi = n_sub * step + sub
        cols = slice(sub * tq, (sub + 1) * tq)

        def scores(p, j):
            start = pl.multiple_of(j * tq, tq)
            return _dot(kp_ref[p, pl.ds(start, tq), :], q2_ref[p])

        def accumulate(p, j, masked, s):
            if masked:
                key = lax.broadcasted_iota(jnp.int32, s.shape, 0)
                qry = lax.broadcasted_iota(jnp.int32, s.shape, 1)
                qry = jnp.where(qry >= tq, qry - tq, qry)
                s = jnp.where(key <= qry, s, NEG_INF)
            m_old = m_ref[p]
            m_new = jnp.maximum(m_old, jnp.max(s, axis=0, keepdims=True))
            pr = jnp.exp2(s - m_new).astype(BF16)
            acc_ref[p] = jnp.exp2(m_old - m_new) * acc_ref[p] + _dot(vp_ref[p, j], pr)
            m_ref[p] = m_new

        def run_units(units, next_block):
            pending = {}
            for i in range(len(units) + ATTN_LAG):
                if i < ATTN_LAG:
                    pending[i] = sc_ref[i]
                elif i < len(units):
                    p, j, _ = units[i]
                    pending[i] = scores(p, j)
                elif next_block is not None:
                    sc_ref[i - len(units)] = scores(i - len(units), next_block)
                if i >= ATTN_LAG:
                    p, j, masked = units[i - ATTN_LAG]
                    accumulate(p, j, masked, pending.pop(i - ATTN_LAG))

        def block(j, masked):
            return [(p, j, masked) for p in range(n_pairs)]

        feat = qft_ref[0, :, cols].astype(F32)
        for p in range(n_pairs):
            qp = qt_ref[0, p * LANES:(p + 1) * LANES, cols].astype(F32)
            heads = []
            for j in range(2):
                qm = jnp.where((frow >= j * HEAD_DIM) & (frow < (j + 1) * HEAD_DIM), qp, 0.0)
                fm = jnp.where((frow % N_HEADS) == 2 * p + j, feat, 0.0)
                heads.append(jnp.concatenate([qm, fm], axis=0))
            q2_ref[p] = jnp.concatenate(heads, axis=1).astype(BF16)

        m_ref[...] = jnp.full(m_ref.shape, NEG_INF, F32)
        acc_ref[...] = jnp.zeros(acc_ref.shape, F32)
        for p in range(ATTN_LAG):
            sc_ref[p] = scores(p, 0)

        def body(i, carry):
            run_units(block(2 * i, False) + block(2 * i + 1, False), 2 * i + 2)
            return carry

        lax.fori_loop(0, (n_sub // 2) * step + sub // 2, body, 0)
        if sub % 2 == 0:
            run_units(block(qi, True), None)
        else:
            run_units(block(qi - 1, False) + block(qi, True), None)

        for p in range(n_pairs):
            acc = acc_ref[p]
            out_t = acc[0:LANES] / acc[LANES:LANES + 1]
            pair_t = jnp.where(frow < HEAD_DIM, out_t[:, :tq], out_t[:, tq:])
            o_ref[0, cols, p * LANES:(p + 1) * LANES] = pair_t.T.astype(BF16)

    for sub in range(n_sub):
        query_block(sub)


def _attn_prompt(qt, qft, k, kb, vt):
    b, d_attn, s = qt.shape
    tq, n_sub = Q_TILE, Q_BLOCKS_PER_STEP
    nblk = s // tq
    n_pairs = d_attn // LANES
    assert nblk % n_sub == 0 and n_sub % 2 == 0 and ATTN_LAG <= n_pairs
    return pl.pallas_call(
        _attn_prompt_kernel,
        grid=(b, nblk // n_sub),
        in_specs=[
            pl.BlockSpec((1, d_attn, n_sub * tq), lambda i, j: (i, 0, j)),
            pl.BlockSpec((1, LANES, n_sub * tq), lambda i, j: (i, 0, j)),
            pl.BlockSpec((1, s, d_attn), lambda i, j: (i, 0, 0)),
            pl.BlockSpec((1, s, LANES), lambda i, j: (i, 0, 0)),
            pl.BlockSpec((1, d_attn, s), lambda i, j: (i, 0, 0)),
        ],
        out_specs=pl.BlockSpec((1, n_sub * tq, d_attn), lambda i, j: (i, j, 0)),
        out_shape=jax.ShapeDtypeStruct((b, s, d_attn), BF16),
        scratch_shapes=[
            pltpu.VMEM((n_pairs, s, 2 * LANES), BF16),
            pltpu.VMEM((n_pairs, nblk, LANES + DENOM_ROWS, tq), BF16),
            pltpu.VMEM((n_sub, n_pairs, 2 * LANES, 2 * tq), BF16),
            pltpu.VMEM((n_sub, n_pairs, 1, 2 * tq), F32),
            pltpu.VMEM((n_sub, n_pairs, LANES + DENOM_ROWS, 2 * tq), F32),
            pltpu.VMEM((n_sub, ATTN_LAG, tq, 2 * tq), F32),
        ],
        compiler_params=pltpu.CompilerParams(dimension_semantics=("arbitrary", "arbitrary"),
                                             vmem_limit_bytes=VMEM_LIMIT_BYTES),
        name="attn_prompt",
    )(qt, qft, k, kb, vt)


def _inproj_sample_kernel(x_ref, sh_ref, sc_ref, n1g_ref, wglu_ref, wqkvf_ref, bf_ref, qg_ref, kg_ref,
                          bdtri_ref, cw_ref, cb_ref, lng_ref, lnb_ref, st_ref,
                          k_ref, v_ref, lf_ref, nst_ref, q_ref, qb_ref, kb_ref, co_ref,
                          u_ref, hist_ref, cof_ref):
    nb, tt, d = x_ref.shape
    r = nb * tt
    d_conv = co_ref.shape[1]
    n_state = CONV_WIDTH - 1

    h = _adaln_rmsnorm(x_ref[...], n1g_ref[...], sc_ref[:, 0], sh_ref[:, 0])
    hb = h.reshape(r, d).astype(BF16)

    n_chunks = d_conv // LANES
    zg = _dot(hb, wglu_ref[...])
    u = zg[:, :d_conv] * jax.nn.sigmoid(zg[:, d_conv:])
    for c in range(n_chunks):
        u_ref[c] = u[:, c * LANES:(c + 1) * LANES]

    q_t, k_t, v_t, lf_t = _qkvf_feature_major(hb, wqkvf_ref[...], qg_ref[...], kg_ref[...], bf_ref[...])
    k_ref[...] = k_t.T
    v_ref[...] = v_t.T
    q_ref[...] = (q_t * (HEAD_DIM ** -0.5)).T.astype(BF16)
    lf = jnp.concatenate([lf_t, jnp.zeros((LANES - N_HEADS, r), F32)], axis=0).T
    lf_ref[...] = lf

    lane = lax.broadcasted_iota(jnp.int32, (r, LANES), 1)
    hi, mid, lo = _split3(lf)
    packed = hi + pltpu.roll(mid, N_HEADS, 1) + pltpu.roll(lo, 2 * N_HEADS, 1)
    c = _dot(bdtri_ref[...], packed.astype(BF16))
    cn = jnp.where(lane < N_HEADS,
                   c + pltpu.roll(c, LANES - N_HEADS, 1) + pltpu.roll(c, LANES - 2 * N_HEADS, 1), 0.0)
    hi, mid, lo = _split3(cn)
    p = hi + pltpu.roll(mid, N_HEADS, 1) + pltpu.roll(lo, 2 * N_HEADS, 1)
    qb_ref[...] = (p + jnp.where((lane >= N_FEAT) & (lane < 2 * N_FEAT), 1.0, 0.0)).astype(BF16)
    kb_ref[...] = (jnp.where(lane < N_FEAT, 1.0, 0.0) - pltpu.roll(p, N_FEAT, 1)).astype(BF16)

    hist_ref[0:n_state] = st_ref[...]
    for t in range(tt):
        for c in range(n_chunks):
            hist_ref[n_state + t, :, c * LANES:(c + 1) * LANES] = u_ref[c, pl.ds(t, nb, stride=tt), :]
    nst_ref[...] = hist_ref[tt:tt + n_state]
    cb, ln_g, ln_b = cb_ref[...], lng_ref[...], lnb_ref[...]
    for t in range(tt):
        y = _conv_ln_swish(lambda k: hist_ref[t + k], lambda k: cw_ref[k:k + 1, :], cb, ln_g, ln_b)
        for c in range(n_chunks):
            cof_ref[c, pl.ds(t, nb, stride=tt), :] = y[:, c * LANES:(c + 1) * LANES]
    co_ref[...] = jnp.concatenate([cof_ref[c] for c in range(n_chunks)], axis=1).astype(BF16)


def _inproj_sample(x, mod4, n1g, wglu, wqkvf_t, bf_col, qg_col, kg_col, bdtri, conv_w, conv_b, ln_g, ln_b,
                   state_t):
    nb, tt, d = x.shape
    r = nb * tt
    d_conv = conv_w.shape[1]
    d_attn = N_HEADS * HEAD_DIM
    n_state = CONV_WIDTH - 1
    mod_spec = lambda j: pl.BlockSpec((nb, 1, 1, d), lambda i: (0, j, 0, 0))
    full = lambda shape: pl.BlockSpec(shape, lambda i: (0,) * len(shape))
    out_shape = (
        jax.ShapeDtypeStruct((r, d_attn), F32),
        jax.ShapeDtypeStruct((r, d_attn), F32),
        jax.ShapeDtypeStruct((r, LANES), F32),
        jax.ShapeDtypeStruct((n_state, nb, d_conv), F32),
        jax.ShapeDtypeStruct((r, d_attn), BF16),
        jax.ShapeDtypeStruct((r, LANES), BF16),
        jax.ShapeDtypeStruct((r, LANES), BF16),
        jax.ShapeDtypeStruct((r, d_conv), BF16),
    )
    args = (x, mod4, mod4, n1g, wglu, wqkvf_t, bf_col, qg_col, kg_col, bdtri, conv_w, conv_b, ln_g, ln_b,
            state_t)
    in_specs = [full(x.shape), mod_spec(0), mod_spec(1)] + [full(a.shape) for a in args[3:]]
    return pl.pallas_call(
        _inproj_sample_kernel,
        grid=(1,),
        in_specs=in_specs,
        out_specs=tuple(full(o.shape) for o in out_shape),
        out_shape=out_shape,
        scratch_shapes=[pltpu.VMEM((d_conv // LANES, r, LANES), F32),
                        pltpu.VMEM((n_state + tt, nb, d_conv), F32),
                        pltpu.VMEM((d_conv // LANES, r, LANES), F32)],
        compiler_params=pltpu.CompilerParams(dimension_semantics=("arbitrary",),
                                             vmem_limit_bytes=VMEM_LIMIT_BYTES),
        name="inproj_sample",
    )(*args)


def _attn_sample_kernel(*refs):
    for i in range(refs[0].shape[0]):
        _attn_sample_request(i, *refs)


def _attn_sample_request(i, q_ref, qb_ref, kn_ref, vn_ref, kbn_ref, kt_ref, vt_ref, clft_ref, ltri_ref, o_ref):
    tt = q_ref.shape[1]
    d_attn = q_ref.shape[2]
    p_len = kt_ref.shape[2]
    nblk = p_len // MXU_DIM
    rows = N_HEADS * tt

    parts = jnp.concatenate(_split3(clft_ref[i]), axis=0)
    blocks = [parts[:, blk * MXU_DIM:(blk + 1) * MXU_DIM] for blk in range(nblk)]
    local = _dot(jnp.concatenate(blocks, axis=0).astype(BF16), ltri_ref[...])
    off = jnp.zeros((N_FEAT, 1), F32)
    sufs = [None] * nblk
    for blk in reversed(range(nblk)):
        loc = local[blk * N_FEAT:(blk + 1) * N_FEAT]
        sufs[blk] = loc + off
        off = off + loc[:, 0:1] + blocks[blk][:, 0:1]
    suf_parts = jnp.concatenate(sufs, axis=1)
    ck_rel = suf_parts[0:8] + suf_parts[8:16] + suf_parts[16:24]
    c_hi, c_mid, c_lo = _split3(ck_rel)
    kbt = jnp.concatenate([jnp.ones((N_FEAT, p_len), F32), c_hi, c_mid, c_lo,
                           jnp.zeros((LANES - 2 * N_FEAT, p_len), F32)], axis=0)
    k_all = jnp.concatenate([kt_ref[i].astype(BF16), kbt.astype(BF16)], axis=0)

    lane_q = lax.broadcasted_iota(jnp.int32, (tt, d_attn), 1)
    lane_b = lax.broadcasted_iota(jnp.int32, (tt, LANES), 1)
    q = q_ref[i].astype(F32)
    qb = qb_ref[i].astype(F32)
    stack = []
    for h in range(N_HEADS):
        qm = jnp.where((lane_q >= h * HEAD_DIM) & (lane_q < (h + 1) * HEAD_DIM), q, 0.0)
        bm = jnp.where((lane_b % N_HEADS) == h, qb, 0.0)
        stack.append(jnp.concatenate([qm, bm], axis=1))
    qs = jnp.concatenate(stack, axis=0).astype(BF16)

    s_c = _dot(qs, k_all)

    pad = jnp.zeros((LANES - tt, d_attn + LANES), F32)
    kn = jnp.concatenate([jnp.concatenate([kn_ref[i], kbn_ref[i].astype(F32)], axis=1), pad], axis=0)
    s_n = _dot_nt(qs, kn.astype(BF16))
    row = lax.broadcasted_iota(jnp.int32, s_n.shape, 0)
    col = lax.broadcasted_iota(jnp.int32, s_n.shape, 1)
    s_n = jnp.where(col <= (row % tt), s_n, NEG_INF)

    m = jnp.maximum(jnp.max(s_c, axis=-1, keepdims=True), jnp.max(s_n, axis=-1, keepdims=True))
    p_c = jnp.exp(s_c - m)
    p_n = jnp.exp(s_n - m)
    l = jnp.sum(p_c, axis=-1, keepdims=True) + jnp.sum(p_n, axis=-1, keepdims=True)
    vn = jnp.concatenate([vn_ref[i], jnp.zeros((LANES - tt, d_attn), F32)], axis=0).astype(BF16)
    o = _dot_nt(p_c.astype(BF16), vt_ref[i].astype(BF16)) + _dot(p_n.astype(BF16), vn)
    o = o / l

    out = jnp.zeros((tt, d_attn), F32)
    for h in range(N_HEADS):
        out = jnp.where((lane_q >= h * HEAD_DIM) & (lane_q < (h + 1) * HEAD_DIM), o[h * tt:(h + 1) * tt], out)
    o_ref[i] = out.astype(BF16)


def _attn_sample(q, qb, kn, vn, kbn, kt, vt, clft, ltri):
    nb, tt, d_attn = q.shape
    p_len = kt.shape[2]
    per_req = lambda shape: pl.BlockSpec((SAMPLE_REQS_PER_STEP,) + shape, lambda i: (i, 0, 0))
    return pl.pallas_call(
        _attn_sample_kernel,
        grid=(nb // SAMPLE_REQS_PER_STEP,),
        in_specs=[per_req((tt, d_attn)), per_req((tt, LANES)), per_req((tt, d_attn)), per_req((tt, d_attn)),
                  per_req((tt, LANES)), per_req((d_attn, p_len)), per_req((d_attn, p_len)),
                  per_req((N_HEADS, p_len)), _const_spec(ltri.shape)],
        out_specs=per_req((tt, d_attn)),
        out_shape=jax.ShapeDtypeStruct((nb, tt, d_attn), BF16),
        compiler_params=pltpu.CompilerParams(dimension_semantics=("arbitrary",),
                                             vmem_limit_bytes=VMEM_LIMIT_BYTES),
        name="attn_sample",
    )(q, qb, kn, vn, kbn, kt, vt, clft, ltri)


def _outproj_ffn_body(x_ref, co, at, g1_ref, sh2_ref, sc2_ref, g2_ref, n2g_ref,
                      wo_ref, wg_ref, wu_ref, wd_ref, y_ref, a_ref, vpu_slot=None):
    nb, tt, d = x_ref.shape
    r = nb * tt
    d_conv = co.shape[1]
    mix = _dot(co, wo_ref[0:d_conv, :]) + _dot(at, wo_ref[d_conv:, :])
    x1 = x_ref[...] + g1_ref[:, 0] * mix.reshape(nb, tt, d)
    hb = _adaln_rmsnorm(x1, n2g_ref[...], sc2_ref[:, 0], sh2_ref[:, 0]).reshape(r, d).astype(BF16)
    lead = x1.reshape(r, d)[0:SUBLANES, 0:d_conv]
    n_ff = wg_ref.shape[1] // FF_CHUNK
    chunk = lambda c: slice(c * FF_CHUNK, (c + 1) * FF_CHUNK)
    for c in range(n_ff):
        zero = vpu_slot(c, lead) if vpu_slot is not None else None
        g, u = _dot(hb, wg_ref[:, chunk(c)]), _dot(hb, wu_ref[:, chunk(c)])
        if zero is not None:
            zero = jnp.concatenate([zero] * (FF_CHUNK // LANES), axis=1)
            g = jnp.concatenate([g[:SUBLANES] + zero, g[SUBLANES:]], axis=0)
        a = (g * jax.nn.sigmoid(g)) * u
        a_ref[:, chunk(c)] = a.astype(BF16)
        lead = jnp.concatenate([a[0:SUBLANES]] * (d_conv // FF_CHUNK), axis=1)
    ffn = _dot(a_ref[...], wd_ref[...])
    y_ref[...] = x1 + g2_ref[:, 0] * ffn.reshape(nb, tt, d)


def _outproj_ffn_kernel(x_ref, co_ref, at_ref, g1_ref, sh2_ref, sc2_ref, g2_ref, n2g_ref,
                        wo_ref, wg_ref, wu_ref, wd_ref, y_ref, a_ref):
    nb, tt, _ = x_ref.shape
    co = co_ref[...].reshape(nb * tt, co_ref.shape[2])
    at = at_ref[...].reshape(nb * tt, at_ref.shape[2])
    _outproj_ffn_body(x_ref, co, at, g1_ref, sh2_ref, sc2_ref, g2_ref, n2g_ref,
                      wo_ref, wg_ref, wu_ref, wd_ref, y_ref, a_ref)


def _exact_zero(v):
    return jnp.minimum(jnp.abs(v), 0.0)


def _outproj_ffn_conv_kernel(x_ref, at_ref, un_ref, u0_ref, g1_ref, sh2_ref, sc2_ref, g2_ref, n2g_ref,
                             wo_ref, wg_ref, wu_ref, wd_ref, cw_ref, cb_ref, lng_ref, lnb_ref,
                             y_ref, a_ref, shift_ref, co_ref, wb_ref, *, tiles_per_seq):
    g = pl.program_id(0)
    tm = x_ref.shape[1]
    d_conv = co_ref.shape[1]
    span = tm + HIST - SUBLANES
    n_conv = tm // CONV_ROWS
    n_ff = wg_ref.shape[1] // FF_CHUNK
    cb, ln_g, ln_b = cb_ref[...], lng_ref[...], lnb_ref[...]

    def stage(u, hist):
        shift_ref[0, 0:HIST, :] = hist
        shift_ref[0, HIST:HIST + tm, :] = u
        for r in range(1, SUBLANES):
            shift_ref[r, 0:span, :] = shift_ref[0, r:r + span, :]

    def conv_chunk(c, bias):
        def tap(k):
            off = k + HIST - (CONV_WIDTH - 1)
            row = c * CONV_ROWS + off - off % SUBLANES
            return shift_ref[off % SUBLANES, row:row + CONV_ROWS, :]
        weight = lambda k: jnp.concatenate([wb_ref[k]] * (CONV_ROWS // SUBLANES), axis=0)
        y = _conv_ln_swish(tap, weight, bias, ln_g, ln_b)
        co_ref[c * CONV_ROWS:(c + 1) * CONV_ROWS, :] = y.astype(BF16)
        return y

    @pl.when(g == 0)
    def _():
        for k in range(CONV_WIDTH):
            wb_ref[k] = jnp.broadcast_to(cw_ref[k:k + 1, :], (SUBLANES, d_conv))
        stage(u0_ref[0], jnp.zeros((HIST, d_conv), F32))
        for c in range(n_conv):
            conv_chunk(c, cb)

    def conv_slot(c, lead):
        if c == 0:
            tail = shift_ref[0, tm:tm + HIST, :]
            starts_seq = (g + 1) % tiles_per_seq == 0
            stage(un_ref[0], jnp.where(starts_seq, 0.0, tail))
        bias = jnp.concatenate([cb + _exact_zero(lead)] * (CONV_ROWS // SUBLANES), axis=0)
        zero = jnp.zeros((SUBLANES, LANES), F32)
        for i in range(c * n_conv // n_ff, (c + 1) * n_conv // n_ff):
            zero = zero + _exact_zero(conv_chunk(i, bias)[0:SUBLANES, 0:LANES])
        return zero

    co = co_ref[...]
    _outproj_ffn_body(x_ref, co, at_ref[0], g1_ref, sh2_ref, sc2_ref, g2_ref, n2g_ref,
                      wo_ref, wg_ref, wu_ref, wd_ref, y_ref, a_ref, vpu_slot=conv_slot)


def _outproj_ffn_conv(x, at, u, mod4, mod_row0, n2g, wo, wg, wu, wd, conv_w, conv_b, ln_g, ln_b):
    b, s, d = x.shape
    tm = ROW_TILE
    d_conv = u.shape[2]
    d_attn = at.shape[2]
    tps = s // tm
    n_tiles = b * tps
    tile = lambda g: (g // tps, g % tps, 0)
    next_tile = lambda g: tile(jnp.minimum(g + 1, n_tiles - 1))
    mod_spec = lambda j: pl.BlockSpec((1, 1, 1, d), lambda g: (mod_row0 + g // tps, j, 0, 0))
    consts = (n2g, wo, wg, wu, wd, conv_w, conv_b, ln_g, ln_b)
    return pl.pallas_call(
        functools.partial(_outproj_ffn_conv_kernel, tiles_per_seq=tps),
        grid=(n_tiles,),
        in_specs=[pl.BlockSpec((1, tm, d), tile), pl.BlockSpec((1, tm, d_attn), tile),
                  pl.BlockSpec((1, tm, d_conv), next_tile),
                  pl.BlockSpec((1, tm, d_conv), lambda g: (0, 0, 0), pipeline_mode=pl.Buffered(1)),
                  mod_spec(2), mod_spec(3), mod_spec(4), mod_spec(5)] + [_const_spec(c.shape) for c in consts],
        out_specs=pl.BlockSpec((1, tm, d), tile),
        out_shape=jax.ShapeDtypeStruct((b, s, d), F32),
        scratch_shapes=[pltpu.VMEM((tm, wg.shape[1]), BF16),
                        pltpu.VMEM((SUBLANES, tm + HIST, d_conv), F32),
                        pltpu.VMEM((tm, d_conv), BF16),
                        pltpu.VMEM((CONV_WIDTH, SUBLANES, d_conv), F32)],
        compiler_params=pltpu.CompilerParams(dimension_semantics=("arbitrary",),
                                             vmem_limit_bytes=VMEM_LIMIT_BYTES),
        name="outproj_ffn_conv",
    )(x, at, u, u, mod4, mod4, mod4, mod4, *consts)


def _outproj_ffn(x, co, at, mod4, n2g, wo, wg, wu, wd, bb, tt):
    nb, s, d = x.shape
    d_conv = co.shape[2]
    d_attn = at.shape[2]
    mod_spec = lambda j: pl.BlockSpec((bb, 1, 1, d), lambda i, t: (i, j, 0, 0))
    row_spec = lambda w: pl.BlockSpec((bb, tt, w), lambda i, t: (i, t, 0))
    return pl.pallas_call(
        _outproj_ffn_kernel,
        grid=(nb // bb, s // tt),
        in_specs=[row_spec(d), row_spec(d_conv), row_spec(d_attn),
                  mod_spec(2), mod_spec(3), mod_spec(4), mod_spec(5),
                  _const_spec(n2g.shape), _const_spec(wo.shape), _const_spec(wg.shape),
                  _const_spec(wu.shape), _const_spec(wd.shape)],
        out_specs=row_spec(d),
        out_shape=jax.ShapeDtypeStruct((nb, s, d), F32),
        scratch_shapes=[pltpu.VMEM((bb * tt, wg.shape[1]), BF16)],
        compiler_params=pltpu.CompilerParams(dimension_semantics=("arbitrary", "arbitrary"),
                                             vmem_limit_bytes=VMEM_LIMIT_BYTES),
        name="outproj_ffn",
    )(x, co, at, mod4, mod4, mod4, mod4, n2g, wo, wg, wu, wd)


def _tri(n, rel):
    i = lax.broadcasted_iota(jnp.int32, (n, n), 0)
    j = lax.broadcasted_iota(jnp.int32, (n, n), 1)
    return rel(i, j).astype(BF16)


def _layer(xp, xs, cache_k, cache_v, cache_logf, state_conv, c_all, w_ada, b_ada, norm1_g, w_in, b_f,
           q_norm_g, k_norm_g, conv_w, conv_b, conv_ln_g, conv_ln_b, w_out, norm2_g, w_gate, w_up, w_down):
    b, s, d = xp.shape
    nb, tt, _ = xs.shape
    d_conv = conv_w.shape[1]
    d_attn = N_HEADS * HEAD_DIM
    d_ff = w_gate.shape[1]
    p_len = cache_k.shape[1]

    mod4 = _modulation(c_all, w_ada, b_ada).reshape(nb + b, N_MOD, 1, d)

    wglu = w_in[:, :2 * d_conv].astype(BF16)
    wqkvf_t = jnp.pad(w_in[:, 2 * d_conv:].T, ((0, 2 * SUBLANES - N_HEADS), (0, 0))).astype(BF16)
    bf_col = b_f.reshape(N_HEADS, 1)
    qg_col = q_norm_g.reshape(d_attn, 1)
    kg_col = k_norm_g.reshape(d_attn, 1)
    row = lambda v: v.reshape(1, -1)
    n1g, n2g, cb, ln_g, ln_b = row(norm1_g), row(norm2_g), row(conv_b), row(conv_ln_g), row(conv_ln_b)
    wo = w_out.astype(BF16)
    assert d_ff % FF_CHUNK == 0
    wg, wu, wd = w_gate.astype(BF16), w_up.astype(BF16), w_down.astype(BF16)
    utri = _tri(MXU_DIM, lambda i, j: i <= j)
    ltri = _tri(MXU_DIM, lambda i, j: i > j)
    bdtri = _tri(nb * tt, lambda i, j: (i // tt == j // tt) & (j <= i))

    kt, vt, lft, cst, qt, qft, k, kb, u = _inproj_prompt(
        xp, mod4, nb, n1g, wglu, wqkvf_t, bf_col, qg_col, kg_col, utri)
    at = _attn_prompt(qt, qft, k, kb, vt)
    yp = _outproj_ffn_conv(xp, at, u, mod4, nb, n2g, wo, wg, wu, wd, conv_w, cb, ln_g, ln_b)
    k_p = kt.reshape(b, N_HEADS, HEAD_DIM, s).transpose(0, 3, 1, 2)
    v_p = vt.reshape(b, N_HEADS, HEAD_DIM, s).transpose(0, 3, 1, 2)
    lf_p = lft.transpose(0, 2, 1)

    state_t = state_conv.transpose(1, 0, 2)
    k_s, v_s, lf_s, nst, q_s, qb_s, kb_s, co_s = _inproj_sample(
        xs, mod4, n1g, wglu, wqkvf_t, bf_col, qg_col, kg_col, bdtri, conv_w, cb, ln_g, ln_b, state_t)
    r3 = lambda a: a.reshape(nb, tt, a.shape[-1])
    ckt = cache_k.transpose(0, 2, 3, 1).reshape(nb, d_attn, p_len)
    cvt = cache_v.transpose(0, 2, 3, 1).reshape(nb, d_attn, p_len)
    clft = cache_logf.transpose(0, 2, 1)
    at_s = _attn_sample(r3(q_s), r3(qb_s), r3(k_s), r3(v_s), r3(kb_s), ckt, cvt, clft, ltri)
    ys = _outproj_ffn(xs, r3(co_s), at_s, mod4, n2g, wo, wg, wu, wd, nb, tt)
    k_sn = k_s.reshape(nb, tt, N_HEADS, HEAD_DIM)
    v_sn = v_s.reshape(nb, tt, N_HEADS, HEAD_DIM)
    lf_sn = lf_s[:, :N_HEADS].reshape(nb, tt, N_HEADS)
    return yp, ys, (k_p, v_p, lf_p, cst), (k_sn, v_sn, lf_sn, nst.transpose(1, 0, 2))


def kernel(x_prompt, x_sample, cache_k, cache_v, cache_logf, state_conv, c_prompt, c_sample, w_ada, b_ada,
           norm1_g, w_in, b_f, q_norm_g, k_norm_g, conv_w, conv_b, conv_ln_g, conv_ln_b, w_out, norm2_g,
           w_gate, w_up, w_down):
    depth = w_ada.shape[0]
    c_all = jnp.concatenate([c_sample, c_prompt], axis=0)
    yp, ys = x_prompt, x_sample
    st_p, st_s = [], []
    for l in range(depth):
        yp, ys, sp, ss = _layer(
            yp, ys, cache_k[l], cache_v[l], cache_logf[l], state_conv[l], c_all, w_ada[l], b_ada[l],
            norm1_g[l], w_in[l], b_f[l], q_norm_g[l], k_norm_g[l], conv_w[l], conv_b[l], conv_ln_g[l],
            conv_ln_b[l], w_out[l], norm2_g[l], w_gate[l], w_up[l], w_down[l])
        st_p.append(sp)
        st_s.append(ss)
    stack = lambda xs: xs[0][None] if len(xs) == 1 else jnp.stack(xs)
    outs_p = [stack([s[i] for s in st_p]) for i in range(4)]
    outs_s = [stack([s[i] for s in st_s]) for i in range(4)]
    return (yp, ys, *outs_p, *outs_s)
```

```python
import functools

import jax
import jax.numpy as jnp
from jax import lax
from jax.experimental import pallas as pl
from jax.experimental.pallas import tpu as pltpu

F32 = jnp.float32
BF16 = jnp.bfloat16

N_HEADS = 8
HEAD_DIM = 64
CONV_WIDTH = 31
EPS = 1e-6
NEG_INF = -1e30

LANES = 128
SUBLANES = 8
MXU_DIM = 256
VMEM_LIMIT_BYTES = 56 * 1024 * 1024

N_PARTS = 3
N_FEAT = N_PARTS * N_HEADS
HIST = 32
CONV_ROWS = 32

ROW_TILE = 512
INPROJ_TILE = 1024
Q_TILE = 256
Q_BLOCKS_PER_STEP = 8
ATTN_LAG = 2
DENOM_ROWS = 16
LOG2E = 1.4426950408889634
FF_CHUNK = 256
SAMPLE_REQS_PER_STEP = 4
N_MOD = 6
MOD_TILE = 1536


def _dot(a, b):
    return jnp.dot(a, b, preferred_element_type=F32)


def _dot_nt(a, b):
    return lax.dot_general(a, b, (((1,), (1,)), ((), ())), preferred_element_type=F32)


def _split3(x):
    hi = x.astype(BF16).astype(F32)
    r = x - hi
    mid = r.astype(BF16).astype(F32)
    lo = (r - mid).astype(BF16).astype(F32)
    return hi, mid, lo


def _log_sigmoid(x):
    return jnp.minimum(x, 0.0) - jnp.log1p(jnp.exp(-jnp.abs(x)))


def _adaln_rmsnorm(x, g, scale, shift):
    y = x * lax.rsqrt(jnp.mean(x * x, axis=-1, keepdims=True) + EPS)
    return (y * g) * (1.0 + scale) + shift


def _const_spec(shape):
    n = len(shape)
    return pl.BlockSpec(shape, lambda *_: (0,) * n, pipeline_mode=pl.Buffered(1))


def _modulation_kernel(c_ref, w_ref, b_ref, o_ref):
    c = c_ref[...]
    a = (c * jax.nn.sigmoid(c)).astype(BF16)
    o_ref[...] = _dot(a, w_ref[...].astype(BF16)) + b_ref[...]


def _modulation(c, w_ada, b_ada):
    nb, d = c.shape
    n = w_ada.shape[1]
    tn = MOD_TILE
    return pl.pallas_call(
        _modulation_kernel,
        grid=(n // tn,),
        in_specs=[pl.BlockSpec((nb, d), lambda j: (0, 0)),
                  pl.BlockSpec((d, tn), lambda j: (0, j)),
                  pl.BlockSpec((1, tn), lambda j: (0, j))],
        out_specs=pl.BlockSpec((nb, tn), lambda j: (0, j)),
        out_shape=jax.ShapeDtypeStruct((nb, n), F32),
        compiler_params=pltpu.CompilerParams(dimension_semantics=("arbitrary",),
                                             vmem_limit_bytes=VMEM_LIMIT_BYTES),
        name="modulation",
    )(c, w_ada, b_ada.reshape(1, n))


def _qkvf_feature_major(hb, wqkvf_t, qg_col, kg_col, bf_col):
    d_attn = N_HEADS * HEAD_DIM
    zt = _dot_nt(wqkvf_t, hb)
    r = zt.shape[1]

    def head_rms(z, g_col):
        z3 = z.reshape(N_HEADS, HEAD_DIM, r)
        ms = jnp.mean(z3 * z3, axis=1, keepdims=True)
        return (z3 * lax.rsqrt(ms + EPS)).reshape(d_attn, r) * g_col

    q_t = head_rms(zt[0:d_attn], qg_col)
    k_t = head_rms(zt[d_attn:2 * d_attn], kg_col)
    v_t = zt[2 * d_attn:3 * d_attn]
    lf_t = _log_sigmoid(zt[3 * d_attn:3 * d_attn + N_HEADS] + bf_col)
    return q_t, k_t, v_t, lf_t


def _conv_ln_swish(load_rows, weight, cb, ln_g, ln_b):
    acc = cb + weight(0) * load_rows(0)
    for k in range(1, CONV_WIDTH):
        acc = acc + weight(k) * load_rows(k)
    mu = jnp.mean(acc, axis=-1, keepdims=True)
    cen = acc - mu
    var = jnp.mean(cen * cen, axis=-1, keepdims=True)
    y = cen * lax.rsqrt(var + EPS) * ln_g + ln_b
    return y * jax.nn.sigmoid(y)


def _inproj_prompt_kernel(x_ref, sh_ref, sc_ref, n1g_ref, wglu_ref, wqkvf_ref, bf_ref, qg_ref, kg_ref,
                          utri_ref,
                          kt_ref, vt_ref, lft_ref, cst_ref, qt_ref, qft_ref, k_ref, kb_ref, u_ref,
                          carry_ref):
    t = pl.program_id(1)
    tm = x_ref.shape[1]
    d_conv = u_ref.shape[2]

    @pl.when(t == 0)
    def _():
        carry_ref[...] = jnp.zeros(carry_ref.shape, F32)

    carry = carry_ref[:, 0:1]
    for h in range(tm // ROW_TILE):
        rows = slice(h * ROW_TILE, (h + 1) * ROW_TILE)
        hb = _adaln_rmsnorm(x_ref[0, rows, :], n1g_ref[...], sc_ref[0, 0], sh_ref[0, 0]).astype(BF16)

        q_t, k_t, v_t, lf_t = _qkvf_feature_major(hb, wqkvf_ref[...], qg_ref[...], kg_ref[...], bf_ref[...])
        kt_ref[0, :, rows] = k_t
        vt_ref[0, :, rows] = v_t
        lft_ref[0, :, rows] = lf_t

        zg = _dot(hb, wglu_ref[...])
        u_ref[0, rows, :] = zg[:, :d_conv] * jax.nn.sigmoid(zg[:, d_conv:])

        nblk = ROW_TILE // MXU_DIM
        parts = jnp.concatenate(_split3(lf_t), axis=0)
        stacked = jnp.concatenate([parts[:, i * MXU_DIM:(i + 1) * MXU_DIM] for i in range(nblk)], axis=0)
        local = _dot(stacked.astype(BF16), utri_ref[...])
        cums = []
        for i in range(nblk):
            loc = local[i * N_FEAT:(i + 1) * N_FEAT]
            cums.append(loc + carry)
            carry = carry + loc[:, MXU_DIM - 1:MXU_DIM]
        cum_parts = jnp.concatenate(cums, axis=1)
        cum_t = (cum_parts[0:8] + cum_parts[8:16] + cum_parts[16:24]) * LOG2E

        c_hi, c_mid, c_lo = _split3(cum_t)
        ones = jnp.ones((N_FEAT, ROW_TILE), F32)
        zeros = jnp.zeros((LANES - 2 * N_FEAT, ROW_TILE), F32)
        kfeat_t = jnp.concatenate([ones, -c_hi, -c_mid, -c_lo, zeros], axis=0)
        qft_ref[0, :, rows] = jnp.concatenate([c_hi, c_mid, c_lo, ones, zeros], axis=0).astype(BF16)
        qt_ref[0, :, rows] = (q_t * (HEAD_DIM ** -0.5 * LOG2E)).astype(BF16)
        k_ref[0, rows, :] = k_t.T.astype(BF16)
        kb_ref[0, rows, :] = kfeat_t.T.astype(BF16)
    carry_ref[...] = jnp.broadcast_to(carry, carry_ref.shape)

    @pl.when(t == pl.num_programs(1) - 1)
    def _():
        cst_ref[0] = u_ref[0, tm - (CONV_WIDTH - 1):tm, :]


def _inproj_prompt(x, mod4, mod_row0, n1g, wglu, wqkvf_t, bf_col, qg_col, kg_col, utri):
    b, s, d = x.shape
    tm = INPROJ_TILE
    d_conv = wglu.shape[1] // 2
    d_attn = N_HEADS * HEAD_DIM
    mod_spec = lambda j: pl.BlockSpec((1, 1, 1, d), lambda i, t: (mod_row0 + i, j, 0, 0))
    out_shape = (
        jax.ShapeDtypeStruct((b, d_attn, s), F32),
        jax.ShapeDtypeStruct((b, d_attn, s), F32),
        jax.ShapeDtypeStruct((b, N_HEADS, s), F32),
        jax.ShapeDtypeStruct((b, CONV_WIDTH - 1, d_conv), F32),
        jax.ShapeDtypeStruct((b, d_attn, s), BF16),
        jax.ShapeDtypeStruct((b, LANES, s), BF16),
        jax.ShapeDtypeStruct((b, s, d_attn), BF16),
        jax.ShapeDtypeStruct((b, s, LANES), BF16),
        jax.ShapeDtypeStruct((b, s, d_conv), F32),
    )
    out_specs = (
        pl.BlockSpec((1, d_attn, tm), lambda i, t: (i, 0, t)),
        pl.BlockSpec((1, d_attn, tm), lambda i, t: (i, 0, t)),
        pl.BlockSpec((1, N_HEADS, tm), lambda i, t: (i, 0, t)),
        pl.BlockSpec((1, CONV_WIDTH - 1, d_conv), lambda i, t: (i, 0, 0)),
        pl.BlockSpec((1, d_attn, tm), lambda i, t: (i, 0, t)),
        pl.BlockSpec((1, LANES, tm), lambda i, t: (i, 0, t)),
        pl.BlockSpec((1, tm, d_attn), lambda i, t: (i, t, 0)),
        pl.BlockSpec((1, tm, LANES), lambda i, t: (i, t, 0)),
        pl.BlockSpec((1, tm, d_conv), lambda i, t: (i, t, 0)),
    )
    in_specs = [
        pl.BlockSpec((1, tm, d), lambda i, t: (i, t, 0)),
        mod_spec(0), mod_spec(1),
        _const_spec(n1g.shape), _const_spec(wglu.shape), _const_spec(wqkvf_t.shape),
        _const_spec(bf_col.shape), _const_spec(qg_col.shape), _const_spec(kg_col.shape),
        _const_spec(utri.shape),
    ]
    return pl.pallas_call(
        _inproj_prompt_kernel,
        grid=(b, s // tm),
        in_specs=in_specs,
        out_specs=out_specs,
        out_shape=out_shape,
        scratch_shapes=[pltpu.VMEM((N_FEAT, LANES), F32)],
        compiler_params=pltpu.CompilerParams(dimension_semantics=("arbitrary", "arbitrary"),
                                             vmem_limit_bytes=VMEM_LIMIT_BYTES),
        name="inproj_prompt",
    )(x, mod4, mod4, n1g, wglu, wqkvf_t, bf_col, qg_col, kg_col, utri)


def _attn_prompt_kernel(qt_ref, qft_ref, k_ref, kb_ref, vt_ref, o_ref,
                        kp_ref, vp_ref, q2_all, m_all, acc_all, sc_all):
    step = pl.program_id(1)
    tq = Q_TILE
    n_sub = qt_ref.shape[2] // tq
    n_pairs, nblk = vp_ref.shape[0], vp_ref.shape[1]

    @pl.when(step == 0)
    def _():
        ones = jnp.ones((vp_ref.shape[2] - LANES, tq), BF16)
        for p in range(n_pairs):
            kp_ref[p, :, 0:LANES] = k_ref[0, :, p * LANES:(p + 1) * LANES]
            kp_ref[p, :, LANES:2 * LANES] = kb_ref[0]
            for i in range(nblk):
                vp_ref[p, i, 0:LANES, :] = vt_ref[0, p * LANES:(p + 1) * LANES, i * tq:(i + 1) * tq].astype(BF16)
                vp_ref[p, i, LANES:, :] = ones

    frow = lax.broadcasted_iota(jnp.int32, (LANES, tq), 0)

    def query_block(sub):
        q2_ref, m_ref, acc_ref, sc_ref = q2_all.at[sub], m_all.at[sub], acc_all.at[sub], sc_all.at[sub]
        qi = n_sub * step + sub
        cols = slice(sub * tq, (sub + 1) * tq)

        def scores(p, j):
            start = pl.multiple_of(j * tq, tq)
            return _dot(kp_ref[p, pl.ds(start, tq), :], q2_ref[p])

        def accumulate(p, j, masked, s):
            if masked:
                key = lax.broadcasted_iota(jnp.int32, s.shape, 0)
                qry = lax.broadcasted_iota(jnp.int32, s.shape, 1)
                qry = jnp.where(qry >= tq, qry - tq, qry)
                s = jnp.where(key <= qry, s, NEG_INF)
            m_old = m_ref[p]
            m_new = jnp.maximum(m_old, jnp.max(s, axis=0, keepdims=True))
            pr = jnp.exp2(s - m_new).astype(BF16)
            acc_ref[p] = jnp.exp2(m_old - m_new) * acc_ref[p] + _dot(vp_ref[p, j], pr)
            m_ref[p] = m_new

        def run_units(units, next_block):
            pending = {}
            for i in range(len(units) + ATTN_LAG):
                if i < ATTN_LAG:
                    pending[i] = sc_ref[i]
                elif i < len(units):
                    p, j, _ = units[i]
                    pending[i] = scores(p, j)
                elif next_block is not None:
                    sc_ref[i - len(units)] = scores(i - len(units), next_block)
                if i >= ATTN_LAG:
                    p, j, masked = units[i - ATTN_LAG]
                    accumulate(p, j, masked, pending.pop(i - ATTN_LAG))

        def block(j, masked):
            return [(p, j, masked) for p in range(n_pairs)]

        feat = qft_ref[0, :, cols].astype(F32)
        for p in range(n_pairs):
            qp = qt_ref[0, p * LANES:(p + 1) * LANES, cols].astype(F32)
            heads = []
            for j in range(2):
                qm = jnp.where((frow >= j * HEAD_DIM) & (frow < (j + 1) * HEAD_DIM), qp, 0.0)
                fm = jnp.where((frow % N_HEADS) == 2 * p + j, feat, 0.0)
                heads.append(jnp.concatenate([qm, fm], axis=0))
            q2_ref[p] = jnp.concatenate(heads, axis=1).astype(BF16)

        m_ref[...] = jnp.full(m_ref.shape, NEG_INF, F32)
        acc_ref[...] = jnp.zeros(acc_ref.shape, F32)
        for p in range(ATTN_LAG):
            sc_ref[p] = scores(p, 0)

        def body(i, carry):
            run_units(block(2 * i, False) + block(2 * i + 1, False), 2 * i + 2)
            return carry

        lax.fori_loop(0, (n_sub // 2) * step + sub // 2, body, 0)
        if sub % 2 == 0:
            run_units(block(qi, True), None)
        else:
            run_units(block(qi - 1, False) + block(qi, True), None)

        for p in range(n_pairs):
            acc = acc_ref[p]
            out_t = acc[0:LANES] / acc[LANES:LANES + 1]
            pair_t = jnp.where(frow < HEAD_DIM, out_t[:, :tq], out_t[:, tq:])
            o_ref[0, cols, p * LANES:(p + 1) * LANES] = pair_t.T.astype(BF16)

    for sub in range(n_sub):
        query_block(sub)


def _attn_prompt(qt, qft, k, kb, vt):
    b, d_attn, s = qt.shape
    tq, n_sub = Q_TILE, Q_BLOCKS_PER_STEP
    nblk = s // tq
    n_pairs = d_attn // LANES
    assert nblk % n_sub == 0 and n_sub % 2 == 0 and ATTN_LAG <= n_pairs
    return pl.pallas_call(
        _attn_prompt_kernel,
        grid=(b, nblk // n_sub),
        in_specs=[
            pl.BlockSpec((1, d_attn, n_sub * tq), lambda i, j: (i, 0, j)),
            pl.BlockSpec((1, LANES, n_sub * tq), lambda i, j: (i, 0, j)),
            pl.BlockSpec((1, s, d_attn), lambda i, j: (i, 0, 0)),
            pl.BlockSpec((1, s, LANES), lambda i, j: (i, 0, 0)),
            pl.BlockSpec((1, d_attn, s), lambda i, j: (i, 0, 0)),
        ],
        out_specs=pl.BlockSpec((1, n_sub * tq, d_attn), lambda i, j: (i, j, 0)),
        out_shape=jax.ShapeDtypeStruct((b, s, d_attn), BF16),
        scratch_shapes=[
            pltpu.VMEM((n_pairs, s, 2 * LANES), BF16),
            pltpu.VMEM((n_pairs, nblk, LANES + DENOM_ROWS, tq), BF16),
            pltpu.VMEM((n_sub, n_pairs, 2 * LANES, 2 * tq), BF16),
            pltpu.VMEM((n_sub, n_pairs, 1, 2 * tq), F32),
            pltpu.VMEM((n_sub, n_pairs, LANES + DENOM_ROWS, 2 * tq), F32),
            pltpu.VMEM((n_sub, ATTN_LAG, tq, 2 * tq), F32),
        ],
        compiler_params=pltpu.CompilerParams(dimension_semantics=("arbitrary", "arbitrary"),
                                             vmem_limit_bytes=VMEM_LIMIT_BYTES),
        name="attn_prompt",
    )(qt, qft, k, kb, vt)


def _inproj_sample_kernel(x_ref, sh_ref, sc_ref, n1g_ref, wglu_ref, wqkvf_ref, bf_ref, qg_ref, kg_ref,
                          bdtri_ref, cw_ref, cb_ref, lng_ref, lnb_ref, st_ref,
                          k_ref, v_ref, lf_ref, nst_ref, q_ref, qb_ref, kb_ref, co_ref,
                          u_ref, hist_ref, cof_ref):
    nb, tt, d = x_ref.shape
    r = nb * tt
    d_conv = co_ref.shape[1]
    n_state = CONV_WIDTH - 1

    h = _adaln_rmsnorm(x_ref[...], n1g_ref[...], sc_ref[:, 0], sh_ref[:, 0])
    hb = h.reshape(r, d).astype(BF16)

    n_chunks = d_conv // LANES
    zg = _dot(hb, wglu_ref[...])
    u = zg[:, :d_conv] * jax.nn.sigmoid(zg[:, d_conv:])
    for c in range(n_chunks):
        u_ref[c] = u[:, c * LANES:(c + 1) * LANES]

    q_t, k_t, v_t, lf_t = _qkvf_feature_major(hb, wqkvf_ref[...], qg_ref[...], kg_ref[...], bf_ref[...])
    k_ref[...] = k_t.T
    v_ref[...] = v_t.T
    q_ref[...] = (q_t * (HEAD_DIM ** -0.5)).T.astype(BF16)
    lf = jnp.concatenate([lf_t, jnp.zeros((LANES - N_HEADS, r), F32)], axis=0).T
    lf_ref[...] = lf

    lane = lax.broadcasted_iota(jnp.int32, (r, LANES), 1)
    hi, mid, lo = _split3(lf)
    packed = hi + pltpu.roll(mid, N_HEADS, 1) + pltpu.roll(lo, 2 * N_HEADS, 1)
    c = _dot(bdtri_ref[...], packed.astype(BF16))
    cn = jnp.where(lane < N_HEADS,
                   c + pltpu.roll(c, LANES - N_HEADS, 1) + pltpu.roll(c, LANES - 2 * N_HEADS, 1), 0.0)
    hi, mid, lo = _split3(cn)
    p = hi + pltpu.roll(mid, N_HEADS, 1) + pltpu.roll(lo, 2 * N_HEADS, 1)
    qb_ref[...] = (p + jnp.where((lane >= N_FEAT) & (lane < 2 * N_FEAT), 1.0, 0.0)).astype(BF16)
    kb_ref[...] = (jnp.where(lane < N_FEAT, 1.0, 0.0) - pltpu.roll(p, N_FEAT, 1)).astype(BF16)

    hist_ref[0:n_state] = st_ref[...]
    for t in range(tt):
        for c in range(n_chunks):
            hist_ref[n_state + t, :, c * LANES:(c + 1) * LANES] = u_ref[c, pl.ds(t, nb, stride=tt), :]
    nst_ref[...] = hist_ref[tt:tt + n_state]
    cb, ln_g, ln_b = cb_ref[...], lng_ref[...], lnb_ref[...]
    for t in range(tt):
        y = _conv_ln_swish(lambda k: hist_ref[t + k], lambda k: cw_ref[k:k + 1, :], cb, ln_g, ln_b)
        for c in range(n_chunks):
            cof_ref[c, pl.ds(t, nb, stride=tt), :] = y[:, c * LANES:(c + 1) * LANES]
    co_ref[...] = jnp.concatenate([cof_ref[c] for c in range(n_chunks)], axis=1).astype(BF16)


def _inproj_sample(x, mod4, n1g, wglu, wqkvf_t, bf_col, qg_col, kg_col, bdtri, conv_w, conv_b, ln_g, ln_b,
                   state_t):
    nb, tt, d = x.shape
    r = nb * tt
    d_conv = conv_w.shape[1]
    d_attn = N_HEADS * HEAD_DIM
    n_state = CONV_WIDTH - 1
    mod_spec = lambda j: pl.BlockSpec((nb, 1, 1, d), lambda i: (0, j, 0, 0))
    full = lambda shape: pl.BlockSpec(shape, lambda i: (0,) * len(shape))
    out_shape = (
        jax.ShapeDtypeStruct((r, d_attn), F32),
        jax.ShapeDtypeStruct((r, d_attn), F32),
        jax.ShapeDtypeStruct((r, LANES), F32),
        jax.ShapeDtypeStruct((n_state, nb, d_conv), F32),
        jax.ShapeDtypeStruct((r, d_attn), BF16),
        jax.ShapeDtypeStruct((r, LANES), BF16),
        jax.ShapeDtypeStruct((r, LANES), BF16),
        jax.ShapeDtypeStruct((r, d_conv), BF16),
    )
    args = (x, mod4, mod4, n1g, wglu, wqkvf_t, bf_col, qg_col, kg_col, bdtri, conv_w, conv_b, ln_g, ln_b,
            state_t)
    in_specs = [full(x.shape), mod_spec(0), mod_spec(1)] + [full(a.shape) for a in args[3:]]
    return pl.pallas_call(
        _inproj_sample_kernel,
        grid=(1,),
        in_specs=in_specs,
        out_specs=tuple(full(o.shape) for o in out_shape),
        out_shape=out_shape,
        scratch_shapes=[pltpu.VMEM((d_conv // LANES, r, LANES), F32),
                        pltpu.VMEM((n_state + tt, nb, d_conv), F32),
                        pltpu.VMEM((d_conv // LANES, r, LANES), F32)],
        compiler_params=pltpu.CompilerParams(dimension_semantics=("arbitrary",),
                                             vmem_limit_bytes=VMEM_LIMIT_BYTES),
        name="inproj_sample",
    )(*args)


def _attn_sample_kernel(*refs):
    for i in range(refs[0].shape[0]):
        _attn_sample_request(i, *refs)


def _attn_sample_request(i, q_ref, qb_ref, kn_ref, vn_ref, kbn_ref, kt_ref, vt_ref, clft_ref, ltri_ref, o_ref):
    tt = q_ref.shape[1]
    d_attn = q_ref.shape[2]
    p_len = kt_ref.shape[2]
    nblk = p_len // MXU_DIM
    rows = N_HEADS * tt

    parts = jnp.concatenate(_split3(clft_ref[i]), axis=0)
    blocks = [parts[:, blk * MXU_DIM:(blk + 1) * MXU_DIM] for blk in range(nblk)]
    local = _dot(jnp.concatenate(blocks, axis=0).astype(BF16), ltri_ref[...])
    off = jnp.zeros((N_FEAT, 1), F32)
    sufs = [None] * nblk
    for blk in reversed(range(nblk)):
        loc = local[blk * N_FEAT:(blk + 1) * N_FEAT]
        sufs[blk] = loc + off
        off = off + loc[:, 0:1] + blocks[blk][:, 0:1]
    suf_parts = jnp.concatenate(sufs, axis=1)
    ck_rel = suf_parts[0:8] + suf_parts[8:16] + suf_parts[16:24]
    c_hi, c_mid, c_lo = _split3(ck_rel)
    kbt = jnp.concatenate([jnp.ones((N_FEAT, p_len), F32), c_hi, c_mid, c_lo,
                           jnp.zeros((LANES - 2 * N_FEAT, p_len), F32)], axis=0)
    k_all = jnp.concatenate([kt_ref[i].astype(BF16), kbt.astype(BF16)], axis=0)

    lane_q = lax.broadcasted_iota(jnp.int32, (tt, d_attn), 1)
    lane_b = lax.broadcasted_iota(jnp.int32, (tt, LANES), 1)
    q = q_ref[i].astype(F32)
    qb = qb_ref[i].astype(F32)
    stack = []
    for h in range(N_HEADS):
        qm = jnp.where((lane_q >= h * HEAD_DIM) & (lane_q < (h + 1) * HEAD_DIM), q, 0.0)
        bm = jnp.where((lane_b % N_HEADS) == h, qb, 0.0)
        stack.append(jnp.concatenate([qm, bm], axis=1))
    qs = jnp.concatenate(stack, axis=0).astype(BF16)

    s_c = _dot(qs, k_all)

    pad = jnp.zeros((LANES - tt, d_attn + LANES), F32)
    kn = jnp.concatenate([jnp.concatenate([kn_ref[i], kbn_ref[i].astype(F32)], axis=1), pad], axis=0)
    s_n = _dot_nt(qs, kn.astype(BF16))
    row = lax.broadcasted_iota(jnp.int32, s_n.shape, 0)
    col = lax.broadcasted_iota(jnp.int32, s_n.shape, 1)
    s_n = jnp.where(col <= (row % tt), s_n, NEG_INF)

    m = jnp.maximum(jnp.max(s_c, axis=-1, keepdims=True), jnp.max(s_n, axis=-1, keepdims=True))
    p_c = jnp.exp(s_c - m)
    p_n = jnp.exp(s_n - m)
    l = jnp.sum(p_c, axis=-1, keepdims=True) + jnp.sum(p_n, axis=-1, keepdims=True)
    vn = jnp.concatenate([vn_ref[i], jnp.zeros((LANES - tt, d_attn), F32)], axis=0).astype(BF16)
    o = _dot_nt(p_c.astype(BF16), vt_ref[i].astype(BF16)) + _dot(p_n.astype(BF16), vn)
    o = o / l

    out = jnp.zeros((tt, d_attn), F32)
    for h in range(N_HEADS):
        out = jnp.where((lane_q >= h * HEAD_DIM) & (lane_q < (h + 1) * HEAD_DIM), o[h * tt:(h + 1) * tt], out)
    o_ref[i] = out.astype(BF16)


def _attn_sample(q, qb, kn, vn, kbn, kt, vt, clft, ltri):
    nb, tt, d_attn = q.shape
    p_len = kt.shape[2]
    per_req = lambda shape: pl.BlockSpec((SAMPLE_REQS_PER_STEP,) + shape, lambda i: (i, 0, 0))
    return pl.pallas_call(
        _attn_sample_kernel,
        grid=(nb // SAMPLE_REQS_PER_STEP,),
        in_specs=[per_req((tt, d_attn)), per_req((tt, LANES)), per_req((tt, d_attn)), per_req((tt, d_attn)),
                  per_req((tt, LANES)), per_req((d_attn, p_len)), per_req((d_attn, p_len)),
                  per_req((N_HEADS, p_len)), _const_spec(ltri.shape)],
        out_specs=per_req((tt, d_attn)),
        out_shape=jax.ShapeDtypeStruct((nb, tt, d_attn), BF16),
        compiler_params=pltpu.CompilerParams(dimension_semantics=("arbitrary",),
                                             vmem_limit_bytes=VMEM_LIMIT_BYTES),
        name="attn_sample",
    )(q, qb, kn, vn, kbn, kt, vt, clft, ltri)


def _outproj_ffn_body(x_ref, co, at, g1_ref, sh2_ref, sc2_ref, g2_ref, n2g_ref,
                      wo_ref, wg_ref, wu_ref, wd_ref, y_ref, a_ref, vpu_slot=None):
    nb, tt, d = x_ref.shape
    r = nb * tt
    d_conv = co.shape[1]
    mix = _dot(co, wo_ref[0:d_conv, :]) + _dot(at, wo_ref[d_conv:, :])
    x1 = x_ref[...] + g1_ref[:, 0] * mix.reshape(nb, tt, d)
    hb = _adaln_rmsnorm(x1, n2g_ref[...], sc2_ref[:, 0], sh2_ref[:, 0]).reshape(r, d).astype(BF16)
    lead = x1.reshape(r, d)[0:SUBLANES, 0:d_conv]
    n_ff = wg_ref.shape[1] // FF_CHUNK
    chunk = lambda c: slice(c * FF_CHUNK, (c + 1) * FF_CHUNK)
    for c in range(n_ff):
        zero = vpu_slot(c, lead) if vpu_slot is not None else None
        g, u = _dot(hb, wg_ref[:, chunk(c)]), _dot(hb, wu_ref[:, chunk(c)])
        if zero is not None:
            zero = jnp.concatenate([zero] * (FF_CHUNK // LANES), axis=1)
            g = jnp.concatenate([g[:SUBLANES] + zero, g[SUBLANES:]], axis=0)
        a = (g * jax.nn.sigmoid(g)) * u
        a_ref[:, chunk(c)] = a.astype(BF16)
        lead = jnp.concatenate([a[0:SUBLANES]] * (d_conv // FF_CHUNK), axis=1)
    ffn = _dot(a_ref[...], wd_ref[...])
    y_ref[...] = x1 + g2_ref[:, 0] * ffn.reshape(nb, tt, d)


def _outproj_ffn_kernel(x_ref, co_ref, at_ref, g1_ref, sh2_ref, sc2_ref, g2_ref, n2g_ref,
                        wo_ref, wg_ref, wu_ref, wd_ref, y_ref, a_ref):
    nb, tt, _ = x_ref.shape
    co = co_ref[...].reshape(nb * tt, co_ref.shape[2])
    at = at_ref[...].reshape(nb * tt, at_ref.shape[2])
    _outproj_ffn_body(x_ref, co, at, g1_ref, sh2_ref, sc2_ref, g2_ref, n2g_ref,
                      wo_ref, wg_ref, wu_ref, wd_ref, y_ref, a_ref)


def _exact_zero(v):
    return jnp.minimum(jnp.abs(v), 0.0)


def _outproj_ffn_conv_kernel(x_ref, at_ref, un_ref, u0_ref, g1_ref, sh2_ref, sc2_ref, g2_ref, n2g_ref,
                             wo_ref, wg_ref, wu_ref, wd_ref, cw_ref, cb_ref, lng_ref, lnb_ref,
                             y_ref, a_ref, shift_ref, co_ref, wb_ref, *, tiles_per_seq):
    g = pl.program_id(0)
    tm = x_ref.shape[1]
    d_conv = co_ref.shape[1]
    span = tm + HIST - SUBLANES
    n_conv = tm // CONV_ROWS
    n_ff = wg_ref.shape[1] // FF_CHUNK
    cb, ln_g, ln_b = cb_ref[...], lng_ref[...], lnb_ref[...]

    def stage(u, hist):
        shift_ref[0, 0:HIST, :] = hist
        shift_ref[0, HIST:HIST + tm, :] = u
        for r in range(1, SUBLANES):
            shift_ref[r, 0:span, :] = shift_ref[0, r:r + span, :]

    def conv_chunk(c, bias):
        def tap(k):
            off = k + HIST - (CONV_WIDTH - 1)
            row = c * CONV_ROWS + off - off % SUBLANES
            return shift_ref[off % SUBLANES, row:row + CONV_ROWS, :]
        weight = lambda k: jnp.concatenate([wb_ref[k]] * (CONV_ROWS // SUBLANES), axis=0)
        y = _conv_ln_swish(tap, weight, bias, ln_g, ln_b)
        co_ref[c * CONV_ROWS:(c + 1) * CONV_ROWS, :] = y.astype(BF16)
        return y

    @pl.when(g == 0)
    def _():
        for k in range(CONV_WIDTH):
            wb_ref[k] = jnp.broadcast_to(cw_ref[k:k + 1, :], (SUBLANES, d_conv))
        stage(u0_ref[0], jnp.zeros((HIST, d_conv), F32))
        for c in range(n_conv):
            conv_chunk(c, cb)

    def conv_slot(c, lead):
        if c == 0:
            tail = shift_ref[0, tm:tm + HIST, :]
            starts_seq = (g + 1) % tiles_per_seq == 0
            stage(un_ref[0], jnp.where(starts_seq, 0.0, tail))
        bias = jnp.concatenate([cb + _exact_zero(lead)] * (CONV_ROWS // SUBLANES), axis=0)
        zero = jnp.zeros((SUBLANES, LANES), F32)
        for i in range(c * n_conv // n_ff, (c + 1) * n_conv // n_ff):
            zero = zero + _exact_zero(conv_chunk(i, bias)[0:SUBLANES, 0:LANES])
        return zero

    co = co_ref[...]
    _outproj_ffn_body(x_ref, co, at_ref[0], g1_ref, sh2_ref, sc2_ref, g2_ref, n2g_ref,
                      wo_ref, wg_ref, wu_ref, wd_ref, y_ref, a_ref, vpu_slot=conv_slot)


def _outproj_ffn_conv(x, at, u, mod4, mod_row0, n2g, wo, wg, wu, wd, conv_w, conv_b, ln_g, ln_b):
    b, s, d = x.shape
    tm = ROW_TILE
    d_conv = u.shape[2]
    d_attn = at.shape[2]
    tps = s // tm
    n_tiles = b * tps
    tile = lambda g: (g // tps, g % tps, 0)
    next_tile = lambda g: tile(jnp.minimum(g + 1, n_tiles - 1))
    mod_spec = lambda j: pl.BlockSpec((1, 1, 1, d), lambda g: (mod_row0 + g // tps, j, 0, 0))
    consts = (n2g, wo, wg, wu, wd, conv_w, conv_b, ln_g, ln_b)
    return pl.pallas_call(
        functools.partial(_outproj_ffn_conv_kernel, tiles_per_seq=tps),
        grid=(n_tiles,),
        in_specs=[pl.BlockSpec((1, tm, d), tile), pl.BlockSpec((1, tm, d_attn), tile),
                  pl.BlockSpec((1, tm, d_conv), next_tile),
                  pl.BlockSpec((1, tm, d_conv), lambda g: (0, 0, 0), pipeline_mode=pl.Buffered(1)),
                  mod_spec(2), mod_spec(3), mod_spec(4), mod_spec(5)] + [_const_spec(c.shape) for c in consts],
        out_specs=pl.BlockSpec((1, tm, d), tile),
        out_shape=jax.ShapeDtypeStruct((b, s, d), F32),
        scratch_shapes=[pltpu.VMEM((tm, wg.shape[1]), BF16),
                        pltpu.VMEM((SUBLANES, tm + HIST, d_conv), F32),
                        pltpu.VMEM((tm, d_conv), BF16),
                        pltpu.VMEM((CONV_WIDTH, SUBLANES, d_conv), F32)],
        compiler_params=pltpu.CompilerParams(dimension_semantics=("arbitrary",),
                                             vmem_limit_bytes=VMEM_LIMIT_BYTES),
        name="outproj_ffn_conv",
    )(x, at, u, u, mod4, mod4, mod4, mod4, *consts)


def _outproj_ffn(x, co, at, mod4, n2g, wo, wg, wu, wd, bb, tt):
    nb, s, d = x.shape
    d_conv = co.shape[2]
    d_attn = at.shape[2]
    mod_spec = lambda j: pl.BlockSpec((bb, 1, 1, d), lambda i, t: (i, j, 0, 0))
    row_spec = lambda w: pl.BlockSpec((bb, tt, w), lambda i, t: (i, t, 0))
    return pl.pallas_call(
        _outproj_ffn_kernel,
        grid=(nb // bb, s // tt),
        in_specs=[row_spec(d), row_spec(d_conv), row_spec(d_attn),
                  mod_spec(2), mod_spec(3), mod_spec(4), mod_spec(5),
                  _const_spec(n2g.shape), _const_spec(wo.shape), _const_spec(wg.shape),
                  _const_spec(wu.shape), _const_spec(wd.shape)],
        out_specs=row_spec(d),
        out_shape=jax.ShapeDtypeStruct((nb, s, d), F32),
        scratch_shapes=[pltpu.VMEM((bb * tt, wg.shape[1]), BF16)],
        compiler_params=pltpu.CompilerParams(dimension_semantics=("arbitrary", "arbitrary"),
                                             vmem_limit_bytes=VMEM_LIMIT_BYTES),
        name="outproj_ffn",
    )(x, co, at, mod4, mod4, mod4, mod4, n2g, wo, wg, wu, wd)


def _tri(n, rel):
    i = lax.broadcasted_iota(jnp.int32, (n, n), 0)
    j = lax.broadcasted_iota(jnp.int32, (n, n), 1)
    return rel(i, j).astype(BF16)


def _layer(xp, xs, cache_k, cache_v, cache_logf, state_conv, c_all, w_ada, b_ada, norm1_g, w_in, b_f,
           q_norm_g, k_norm_g, conv_w, conv_b, conv_ln_g, conv_ln_b, w_out, norm2_g, w_gate, w_up, w_down):
    b, s, d = xp.shape
    nb, tt, _ = xs.shape
    d_conv = conv_w.shape[1]
    d_attn = N_HEADS * HEAD_DIM
    d_ff = w_gate.shape[1]
    p_len = cache_k.shape[1]

    mod4 = _modulation(c_all, w_ada, b_ada).reshape(nb + b, N_MOD, 1, d)

    wglu = w_in[:, :2 * d_conv].astype(BF16)
    wqkvf_t = jnp.pad(w_in[:, 2 * d_conv:].T, ((0, 2 * SUBLANES - N_HEADS), (0, 0))).astype(BF16)
    bf_col = b_f.reshape(N_HEADS, 1)
    qg_col = q_norm_g.reshape(d_attn, 1)
    kg_col = k_norm_g.reshape(d_attn, 1)
    row = lambda v: v.reshape(1, -1)
    n1g, n2g, cb, ln_g, ln_b = row(norm1_g), row(norm2_g), row(conv_b), row(conv_ln_g), row(conv_ln_b)
    wo = w_out.astype(BF16)
    assert d_ff % FF_CHUNK == 0
    wg, wu, wd = w_gate.astype(BF16), w_up.astype(BF16), w_down.astype(BF16)
    utri = _tri(MXU_DIM, lambda i, j: i <= j)
    ltri = _tri(MXU_DIM, lambda i, j: i > j)
    bdtri = _tri(nb * tt, lambda i, j: (i // tt == j // tt) & (j <= i))

    kt, vt, lft, cst, qt, qft, k, kb, u = _inproj_prompt(
        xp, mod4, nb, n1g, wglu, wqkvf_t, bf_col, qg_col, kg_col, utri)
    at = _attn_prompt(qt, qft, k, kb, vt)
    yp = _outproj_ffn_conv(xp, at, u, mod4, nb, n2g, wo, wg, wu, wd, conv_w, cb, ln_g, ln_b)
    k_p = kt.reshape(b, N_HEADS, HEAD_DIM, s).transpose(0, 3, 1, 2)
    v_p = vt.reshape(b, N_HEADS, HEAD_DIM, s).transpose(0, 3, 1, 2)
    lf_p = lft.transpose(0, 2, 1)

    state_t = state_conv.transpose(1, 0, 2)
    k_s, v_s, lf_s, nst, q_s, qb_s, kb_s, co_s = _inproj_sample(
        xs, mod4, n1g, wglu, wqkvf_t, bf_col, qg_col, kg_col, bdtri, conv_w, cb, ln_g, ln_b, state_t)
    r3 = lambda a: a.reshape(nb, tt, a.shape[-1])
    ckt = cache_k.transpose(0, 2, 3, 1).reshape(nb, d_attn, p_len)
    cvt = cache_v.transpose(0, 2, 3, 1).reshape(nb, d_attn, p_len)
    clft = cache_logf.transpose(0, 2, 1)
    at_s = _attn_sample(r3(q_s), r3(qb_s), r3(k_s), r3(v_s), r3(kb_s), ckt, cvt, clft, ltri)
    ys = _outproj_ffn(xs, r3(co_s), at_s, mod4, n2g, wo, wg, wu, wd, nb, tt)
    k_sn = k_s.reshape(nb, tt, N_HEADS, HEAD_DIM)
    v_sn = v_s.reshape(nb, tt, N_HEADS, HEAD_DIM)
    lf_sn = lf_s[:, :N_HEADS].reshape(nb, tt, N_HEADS)
    return yp, ys, (k_p, v_p, lf_p, cst), (k_sn, v_sn, lf_sn, nst.transpose(1, 0, 2))


def kernel(x_prompt, x_sample, cache_k, cache_v, cache_logf, state_conv, c_prompt, c_sample, w_ada, b_ada,
           norm1_g, w_in, b_f, q_norm_g, k_norm_g, conv_w, conv_b, conv_ln_g, conv_ln_b, w_out, norm2_g,
           w_gate, w_up, w_down):
    depth = w_ada.shape[0]
    c_all = jnp.concatenate([c_sample, c_prompt], axis=0)
    yp, ys = x_prompt, x_sample
    st_p, st_s = [], []
    for l in range(depth):
        yp, ys, sp, ss = _layer(
            yp, ys, cache_k[l], cache_v[l], cache_logf[l], state_conv[l], c_all, w_ada[l], b_ada[l],
            norm1_g[l], w_in[l], b_f[l], q_norm_g[l], k_norm_g[l], conv_w[l], conv_b[l], conv_ln_g[l],
            conv_ln_b[l], w_out[l], norm2_g[l], w_gate[l], w_up[l], w_down[l])
        st_p.append(sp)
        st_s.append(ss)
    stack = lambda xs: xs[0][None] if len(xs) == 1 else jnp.stack(xs)
    outs_p = [stack([s[i] for s in st_p]) for i in range(4)]
    outs_s = [stack([s[i] for s in st_s]) for i in range(4)]
    return (yp, ys, *outs_p, *outs_s)
```

```python
import functools

import jax
import jax.numpy as jnp
from jax import lax
from jax.experimental import pallas as pl
from jax.experimental.pallas import tpu as pltpu

F32 = jnp.float32
BF16 = jnp.bfloat16

N_HEADS = 8
HEAD_DIM = 64
CONV_WIDTH = 31
EPS = 1e-6
NEG_INF = -1e30

LANES = 128
SUBLANES = 8
MXU_DIM = 256
VMEM_LIMIT_BYTES = 56 * 1024 * 1024

N_PARTS = 3
N_FEAT = N_PARTS * N_HEADS
HIST = 32
CONV_ROWS = 32

ROW_TILE = 512
INPROJ_TILE = 1024
Q_TILE = 256
Q_BLOCKS_PER_STEP = 8
ATTN_LAG = 2
DENOM_ROWS = 16
LOG2E = 1.4426950408889634
FF_CHUNK = 256
SAMPLE_REQS_PER_STEP = 4
N_MOD = 6
MOD_TILE = 1536


def _dot(a, b):
    return jnp.dot(a, b, preferred_element_type=F32)


def _dot_nt(a, b):
    return lax.dot_general(a, b, (((1,), (1,)), ((), ())), preferred_element_type=F32)


def _split3(x):
    hi = x.astype(BF16).astype(F32)
    r = x - hi
    mid = r.astype(BF16).astype(F32)
    lo = (r - mid).astype(BF16).astype(F32)
    return hi, mid, lo


def _log_sigmoid(x):
    return jnp.minimum(x, 0.0) - jnp.log1p(jnp.exp(-jnp.abs(x)))


def _adaln_rmsnorm(x, g, scale, shift):
    y = x * lax.rsqrt(jnp.mean(x * x, axis=-1, keepdims=True) + EPS)
    return (y * g) * (1.0 + scale) + shift


def _const_spec(shape):
    n = len(shape)
    return pl.BlockSpec(shape, lambda *_: (0,) * n, pipeline_mode=pl.Buffered(1))


def _modulation_kernel(c_ref, w_ref, b_ref, o_ref):
    c = c_ref[...]
    a = (c * jax.nn.sigmoid(c)).astype(BF16)
    o_ref[...] = _dot(a, w_ref[...].astype(BF16)) + b_ref[...]


def _modulation(c, w_ada, b_ada):
    nb, d = c.shape
    n = w_ada.shape[1]
    tn = MOD_TILE
    return pl.pallas_call(
        _modulation_kernel,
        grid=(n // tn,),
        in_specs=[pl.BlockSpec((nb, d), lambda j: (0, 0)),
                  pl.BlockSpec((d, tn), lambda j: (0, j)),
                  pl.BlockSpec((1, tn), lambda j: (0, j))],
        out_specs=pl.BlockSpec((nb, tn), lambda j: (0, j)),
        out_shape=jax.ShapeDtypeStruct((nb, n), F32),
        compiler_params=pltpu.CompilerParams(dimension_semantics=("arbitrary",),
                                             vmem_limit_bytes=VMEM_LIMIT_BYTES),
        name="modulation",
    )(c, w_ada, b_ada.reshape(1, n))


def _qkvf_feature_major(hb, wqkvf_t, qg_col, kg_col, bf_col):
    d_attn = N_HEADS * HEAD_DIM
    zt = _dot_nt(wqkvf_t, hb)
    r = zt.shape[1]

    def head_rms(z, g_col):
        z3 = z.reshape(N_HEADS, HEAD_DIM, r)
        ms = jnp.mean(z3 * z3, axis=1, keepdims=True)
        return (z3 * lax.rsqrt(ms + EPS)).reshape(d_attn, r) * g_col

    q_t = head_rms(zt[0:d_attn], qg_col)
    k_t = head_rms(zt[d_attn:2 * d_attn], kg_col)
    v_t = zt[2 * d_attn:3 * d_attn]
    lf_t = _log_sigmoid(zt[3 * d_attn:3 * d_attn + N_HEADS] + bf_col)
    return q_t, k_t, v_t, lf_t


def _conv_ln_swish(load_rows, weight, cb, ln_g, ln_b):
    acc = cb + weight(0) * load_rows(0)
    for k in range(1, CONV_WIDTH):
        acc = acc + weight(k) * load_rows(k)
    mu = jnp.mean(acc, axis=-1, keepdims=True)
    cen = acc - mu
    var = jnp.mean(cen * cen, axis=-1, keepdims=True)
    y = cen * lax.rsqrt(var + EPS) * ln_g + ln_b
    return y * jax.nn.sigmoid(y)


def _inproj_prompt_kernel(x_ref, sh_ref, sc_ref, n1g_ref, wglu_ref, wqkvf_ref, bf_ref, qg_ref, kg_ref,
                          utri_ref,
                          kt_ref, vt_ref, lft_ref, cst_ref, qt_ref, qft_ref, k_ref, kb_ref, u_ref,
                          carry_ref):
    t = pl.program_id(1)
    tm = x_ref.shape[1]
    d_conv = u_ref.shape[2]

    @pl.when(t == 0)
    def _():
        carry_ref[...] = jnp.zeros(carry_ref.shape, F32)

    carry = carry_ref[:, 0:1]
    for h in range(tm // ROW_TILE):
        rows = slice(h * ROW_TILE, (h + 1) * ROW_TILE)
        hb = _adaln_rmsnorm(x_ref[0, rows, :], n1g_ref[...], sc_ref[0, 0], sh_ref[0, 0]).astype(BF16)

        q_t, k_t, v_t, lf_t = _qkvf_feature_major(hb, wqkvf_ref[...], qg_ref[...], kg_ref[...], bf_ref[...])
        kt_ref[0, :, rows] = k_t
        vt_ref[0, :, rows] = v_t
        lft_ref[0, :, rows] = lf_t

        zg = _dot(hb, wglu_ref[...])
        u_ref[0, rows, :] = zg[:, :d_conv] * jax.nn.sigmoid(zg[:, d_conv:])

        nblk = ROW_TILE // MXU_DIM
        parts = jnp.concatenate(_split3(lf_t), axis=0)
        stacked = jnp.concatenate([parts[:, i * MXU_DIM:(i + 1) * MXU_DIM] for i in range(nblk)], axis=0)
        local = _dot(stacked.astype(BF16), utri_ref[...])
        cums = []
        for i in range(nblk):
            loc = local[i * N_FEAT:(i + 1) * N_FEAT]
            cums.append(loc + carry)
            carry = carry + loc[:, MXU_DIM - 1:MXU_DIM]
        cum_parts = jnp.concatenate(cums, axis=1)
        cum_t = (cum_parts[0:8] + cum_parts[8:16] + cum_parts[16:24]) * LOG2E

        c_hi, c_mid, c_lo = _split3(cum_t)
        ones = jnp.ones((N_FEAT, ROW_TILE), F32)
        zeros = jnp.zeros((LANES - 2 * N_FEAT, ROW_TILE), F32)
        kfeat_t = jnp.concatenate([ones, -c_hi, -c_mid, -c_lo, zeros], axis=0)
        qft_ref[0, :, rows] = jnp.concatenate([c_hi, c_mid, c_lo, ones, zeros], axis=0).astype(BF16)
        qt_ref[0, :, rows] = (q_t * (HEAD_DIM ** -0.5 * LOG2E)).astype(BF16)
        k_ref[0, rows, :] = k_t.T.astype(BF16)
        kb_ref[0, rows, :] = kfeat_t.T.astype(BF16)
    carry_ref[...] = jnp.broadcast_to(carry, carry_ref.shape)

    @pl.when(t == pl.num_programs(1) - 1)
    def _():
        cst_ref[0] = u_ref[0, tm - (CONV_WIDTH - 1):tm, :]


def _inproj_prompt(x, mod4, mod_row0, n1g, wglu, wqkvf_t, bf_col, qg_col, kg_col, utri):
    b, s, d = x.shape
    tm = INPROJ_TILE
    d_conv = wglu.shape[1] // 2
    d_attn = N_HEADS * HEAD_DIM
    mod_spec = lambda j: pl.BlockSpec((1, 1, 1, d), lambda i, t: (mod_row0 + i, j, 0, 0))
    out_shape = (
        jax.ShapeDtypeStruct((b, d_attn, s), F32),
        jax.ShapeDtypeStruct((b, d_attn, s), F32),
        jax.ShapeDtypeStruct((b, N_HEADS, s), F32),
        jax.ShapeDtypeStruct((b, CONV_WIDTH - 1, d_conv), F32),
        jax.ShapeDtypeStruct((b, d_attn, s), BF16),
        jax.ShapeDtypeStruct((b, LANES, s), BF16),
        jax.ShapeDtypeStruct((b, s, d_attn), BF16),
        jax.ShapeDtypeStruct((b, s, LANES), BF16),
        jax.ShapeDtypeStruct((b, s, d_conv), F32),
    )
    out_specs = (
        pl.BlockSpec((1, d_attn, tm), lambda i, t: (i, 0, t)),
        pl.BlockSpec((1, d_attn, tm), lambda i, t: (i, 0, t)),
        pl.BlockSpec((1, N_HEADS, tm), lambda i, t: (i, 0, t)),
        pl.BlockSpec((1, CONV_WIDTH - 1, d_conv), lambda i, t: (i, 0, 0)),
        pl.BlockSpec((1, d_attn, tm), lambda i, t: (i, 0, t)),
        pl.BlockSpec((1, LANES, tm), lambda i, t: (i, 0, t)),
        pl.BlockSpec((1, tm, d_attn), lambda i, t: (i, t, 0)),
        pl.BlockSpec((1, tm, LANES), lambda i, t: (i, t, 0)),
        pl.BlockSpec((1, tm, d_conv), lambda i, t: (i, t, 0)),
    )
    in_specs = [
        pl.BlockSpec((1, tm, d), lambda i, t: (i, t, 0)),
        mod_spec(0), mod_spec(1),
        _const_spec(n1g.shape), _const_spec(wglu.shape), _const_spec(wqkvf_t.shape),
        _const_spec(bf_col.shape), _const_spec(qg_col.shape), _const_spec(kg_col.shape),
        _const_spec(utri.shape),
    ]
    return pl.pallas_call(
        _inproj_prompt_kernel,
        grid=(b, s // tm),
        in_specs=in_specs,
        out_specs=out_specs,
        out_shape=out_shape,
        scratch_shapes=[pltpu.VMEM((N_FEAT, LANES), F32)],
        compiler_params=pltpu.CompilerParams(dimension_semantics=("arbitrary", "arbitrary"),
                                             vmem_limit_bytes=VMEM_LIMIT_BYTES),
        name="inproj_prompt",
    )(x, mod4, mod4, n1g, wglu, wqkvf_t, bf_col, qg_col, kg_col, utri)


def _attn_prompt_kernel(qt_ref, qft_ref, k_ref, kb_ref, vt_ref, o_ref,
                        kp_ref, vp_ref, q2_all, m_all, acc_all, sc_all):
    step = pl.program_id(1)
    tq = Q_TILE
    n_sub = qt_ref.shape[2] // tq
    n_pairs, nblk = vp_ref.shape[0], vp_ref.shape[1]

    @pl.when(step == 0)
    def _():
        ones = jnp.ones((DENOM_ROWS, tq), BF16)
        for p in range(n_pairs):
            kp_ref[p, :, 0:LANES] = k_ref[0, :, p * LANES:(p + 1) * LANES]
            kp_ref[p, :, LANES:2 * LANES] = kb_ref[0]
            for i in range(nblk):
                for h in range(2):
                    feats = slice(p * LANES + h * HEAD_DIM, p * LANES + (h + 1) * HEAD_DIM)
                    vp_ref[p, i, h, 0:HEAD_DIM, :] = vt_ref[0, feats, i * tq:(i + 1) * tq].astype(BF16)
                    vp_ref[p, i, h, HEAD_DIM:, :] = ones

    frow = lax.broadcasted_iota(jnp.int32, (LANES, tq), 0)

    def query_block(sub):
        q2_ref, m_ref, acc_ref, sc_ref = q2_all.at[sub], m_all.at[sub], acc_all.at[sub], sc_all.at[sub]
        qi = n_sub * step + sub
        cols = slice(sub * tq, (sub + 1) * tq)

        def scores(p, j):
            start = pl.multiple_of(j * tq, tq)
            return _dot(kp_ref[p, pl.ds(start, tq), :], q2_ref[p])

        def accumulate(p, j, masked, s):
            if masked:
                key = lax.broadcasted_iota(jnp.int32, s.shape, 0)
                qry = lax.broadcasted_iota(jnp.int32, s.shape, 1)
                qry = jnp.where(qry >= tq, qry - tq, qry)
                s = jnp.where(key <= qry, s, NEG_INF)
            m_old = m_ref[p]
            m_new = jnp.maximum(m_old, jnp.max(s, axis=0, keepdims=True))
            pr = jnp.exp2(s - m_new).astype(BF16)
            pv = jnp.concatenate([_dot(vp_ref[p, j, h], pr[:, h * tq:(h + 1) * tq]) for h in range(2)], axis=1)
            acc_ref[p] = jnp.exp2(m_old - m_new) * acc_ref[p] + pv
            m_ref[p] = m_new

        def run_units(units, next_block):
            pending = {}
            for i in range(len(units) + ATTN_LAG):
                if i < ATTN_LAG:
                    pending[i] = sc_ref[i]
                elif i < len(units):
                    p, j, _ = units[i]
                    pending[i] = scores(p, j)
                elif next_block is not None:
                    sc_ref[i - len(units)] = scores(i - len(units), next_block)
                if i >= ATTN_LAG:
                    p, j, masked = units[i - ATTN_LAG]
                    accumulate(p, j, masked, pending.pop(i - ATTN_LAG))

        def block(j, masked):
            return [(p, j, masked) for p in range(n_pairs)]

        feat = qft_ref[0, :, cols].astype(F32)
        for p in range(n_pairs):
            qp = qt_ref[0, p * LANES:(p + 1) * LANES, cols].astype(F32)
            heads = []
            for j in range(2):
                qm = jnp.where((frow >= j * HEAD_DIM) & (frow < (j + 1) * HEAD_DIM), qp, 0.0)
                fm = jnp.where((frow % N_HEADS) == 2 * p + j, feat, 0.0)
                heads.append(jnp.concatenate([qm, fm], axis=0))
            q2_ref[p] = jnp.concatenate(heads, axis=1).astype(BF16)

        m_ref[...] = jnp.full(m_ref.shape, NEG_INF, F32)
        acc_ref[...] = jnp.zeros(acc_ref.shape, F32)
        for p in range(ATTN_LAG):
            sc_ref[p] = scores(p, 0)

        def body(i, carry):
            run_units(block(2 * i, False) + block(2 * i + 1, False), 2 * i + 2)
            return carry

        lax.fori_loop(0, (n_sub // 2) * step + sub // 2, body, 0)
        if sub % 2 == 0:
            run_units(block(qi, True), None)
        else:
            run_units(block(qi - 1, False) + block(qi, True), None)

        for p in range(n_pairs):
            acc = acc_ref[p]
            out_t = acc[0:HEAD_DIM] / acc[HEAD_DIM:HEAD_DIM + 1]
            pair_t = jnp.concatenate([out_t[:, :tq], out_t[:, tq:]], axis=0)
            o_ref[0, cols, p * LANES:(p + 1) * LANES] = pair_t.T.astype(BF16)

    for sub in range(n_sub):
        query_block(sub)


def _attn_prompt(qt, qft, k, kb, vt):
    b, d_attn, s = qt.shape
    tq, n_sub = Q_TILE, Q_BLOCKS_PER_STEP
    nblk = s // tq
    n_pairs = d_attn // LANES
    assert nblk % n_sub == 0 and n_sub % 2 == 0 and ATTN_LAG <= n_pairs
    return pl.pallas_call(
        _attn_prompt_kernel,
        grid=(b, nblk // n_sub),
        in_specs=[
            pl.BlockSpec((1, d_attn, n_sub * tq), lambda i, j: (i, 0, j)),
            pl.BlockSpec((1, LANES, n_sub * tq), lambda i, j: (i, 0, j)),
            pl.BlockSpec((1, s, d_attn), lambda i, j: (i, 0, 0)),
            pl.BlockSpec((1, s, LANES), lambda i, j: (i, 0, 0)),
            pl.BlockSpec((1, d_attn, s), lambda i, j: (i, 0, 0)),
        ],
        out_specs=pl.BlockSpec((1, n_sub * tq, d_attn), lambda i, j: (i, j, 0)),
        out_shape=jax.ShapeDtypeStruct((b, s, d_attn), BF16),
        scratch_shapes=[
            pltpu.VMEM((n_pairs, s, 2 * LANES), BF16),
            pltpu.VMEM((n_pairs, nblk, 2, HEAD_DIM + DENOM_ROWS, tq), BF16),
            pltpu.VMEM((n_sub, n_pairs, 2 * LANES, 2 * tq), BF16),
            pltpu.VMEM((n_sub, n_pairs, 1, 2 * tq), F32),
            pltpu.VMEM((n_sub, n_pairs, HEAD_DIM + DENOM_ROWS, 2 * tq), F32),
            pltpu.VMEM((n_sub, ATTN_LAG, tq, 2 * tq), F32),
        ],
        compiler_params=pltpu.CompilerParams(dimension_semantics=("arbitrary", "arbitrary"),
                                             vmem_limit_bytes=VMEM_LIMIT_BYTES),
        name="attn_prompt",
    )(qt, qft, k, kb, vt)


def _inproj_sample_kernel(x_ref, sh_ref, sc_ref, n1g_ref, wglu_ref, wqkvf_ref, bf_ref, qg_ref, kg_ref,
                          bdtri_ref, cw_ref, cb_ref, lng_ref, lnb_ref, st_ref,
                          k_ref, v_ref, lf_ref, nst_ref, q_ref, qb_ref, kb_ref, co_ref,
                          u_ref, hist_ref, cof_ref):
    nb, tt, d = x_ref.shape
    r = nb * tt
    d_conv = co_ref.shape[1]
    n_state = CONV_WIDTH - 1

    h = _adaln_rmsnorm(x_ref[...], n1g_ref[...], sc_ref[:, 0], sh_ref[:, 0])
    hb = h.reshape(r, d).astype(BF16)

    n_chunks = d_conv // LANES
    zg = _dot(hb, wglu_ref[...])
    u = zg[:, :d_conv] * jax.nn.sigmoid(zg[:, d_conv:])
    for c in range(n_chunks):
        u_ref[c] = u[:, c * LANES:(c + 1) * LANES]

    q_t, k_t, v_t, lf_t = _qkvf_feature_major(hb, wqkvf_ref[...], qg_ref[...], kg_ref[...], bf_ref[...])
    k_ref[...] = k_t.T
    v_ref[...] = v_t.T
    q_ref[...] = (q_t * (HEAD_DIM ** -0.5)).T.astype(BF16)
    lf = jnp.concatenate([lf_t, jnp.zeros((LANES - N_HEADS, r), F32)], axis=0).T
    lf_ref[...] = lf

    lane = lax.broadcasted_iota(jnp.int32, (r, LANES), 1)
    hi, mid, lo = _split3(lf)
    packed = hi + pltpu.roll(mid, N_HEADS, 1) + pltpu.roll(lo, 2 * N_HEADS, 1)
    c = _dot(bdtri_ref[...], packed.astype(BF16))
    cn = jnp.where(lane < N_HEADS,
                   c + pltpu.roll(c, LANES - N_HEADS, 1) + pltpu.roll(c, LANES - 2 * N_HEADS, 1), 0.0)
    hi, mid, lo = _split3(cn)
    p = hi + pltpu.roll(mid, N_HEADS, 1) + pltpu.roll(lo, 2 * N_HEADS, 1)
    qb_ref[...] = (p + jnp.where((lane >= N_FEAT) & (lane < 2 * N_FEAT), 1.0, 0.0)).astype(BF16)
    kb_ref[...] = (jnp.where(lane < N_FEAT, 1.0, 0.0) - pltpu.roll(p, N_FEAT, 1)).astype(BF16)

    hist_ref[0:n_state] = st_ref[...]
    for t in range(tt):
        for c in range(n_chunks):
            hist_ref[n_state + t, :, c * LANES:(c + 1) * LANES] = u_ref[c, pl.ds(t, nb, stride=tt), :]
    nst_ref[...] = hist_ref[tt:tt + n_state]
    cb, ln_g, ln_b = cb_ref[...], lng_ref[...], lnb_ref[...]
    for t in range(tt):
        y = _conv_ln_swish(lambda k: hist_ref[t + k], lambda k: cw_ref[k:k + 1, :], cb, ln_g, ln_b)
        for c in range(n_chunks):
            cof_ref[c, pl.ds(t, nb, stride=tt), :] = y[:, c * LANES:(c + 1) * LANES]
    co_ref[...] = jnp.concatenate([cof_ref[c] for c in range(n_chunks)], axis=1).astype(BF16)


def _inproj_sample(x, mod4, n1g, wglu, wqkvf_t, bf_col, qg_col, kg_col, bdtri, conv_w, conv_b, ln_g, ln_b,
                   state_t):
    nb, tt, d = x.shape
    r = nb * tt
    d_conv = conv_w.shape[1]
    d_attn = N_HEADS * HEAD_DIM
    n_state = CONV_WIDTH - 1
    mod_spec = lambda j: pl.BlockSpec((nb, 1, 1, d), lambda i: (0, j, 0, 0))
    full = lambda shape: pl.BlockSpec(shape, lambda i: (0,) * len(shape))
    out_shape = (
        jax.ShapeDtypeStruct((r, d_attn), F32),
        jax.ShapeDtypeStruct((r, d_attn), F32),
        jax.ShapeDtypeStruct((r, LANES), F32),
        jax.ShapeDtypeStruct((n_state, nb, d_conv), F32),
        jax.ShapeDtypeStruct((r, d_attn), BF16),
        jax.ShapeDtypeStruct((r, LANES), BF16),
        jax.ShapeDtypeStruct((r, LANES), BF16),
        jax.ShapeDtypeStruct((r, d_conv), BF16),
    )
    args = (x, mod4, mod4, n1g, wglu, wqkvf_t, bf_col, qg_col, kg_col, bdtri, conv_w, conv_b, ln_g, ln_b,
            state_t)
    in_specs = [full(x.shape), mod_spec(0), mod_spec(1)] + [full(a.shape) for a in args[3:]]
    return pl.pallas_call(
        _inproj_sample_kernel,
        grid=(1,),
        in_specs=in_specs,
        out_specs=tuple(full(o.shape) for o in out_shape),
        out_shape=out_shape,
        scratch_shapes=[pltpu.VMEM((d_conv // LANES, r, LANES), F32),
                        pltpu.VMEM((n_state + tt, nb, d_conv), F32),
                        pltpu.VMEM((d_conv // LANES, r, LANES), F32)],
        compiler_params=pltpu.CompilerParams(dimension_semantics=("arbitrary",),
                                             vmem_limit_bytes=VMEM_LIMIT_BYTES),
        name="inproj_sample",
    )(*args)


def _attn_sample_kernel(*refs):
    for i in range(refs[0].shape[0]):
        _attn_sample_request(i, *refs)


def _attn_sample_request(i, q_ref, qb_ref, kn_ref, vn_ref, kbn_ref, kt_ref, vt_ref, clft_ref, ltri_ref, o_ref):
    tt = q_ref.shape[1]
    d_attn = q_ref.shape[2]
    p_len = kt_ref.shape[2]
    nblk = p_len // MXU_DIM
    rows = N_HEADS * tt

    parts = jnp.concatenate(_split3(clft_ref[i]), axis=0)
    blocks = [parts[:, blk * MXU_DIM:(blk + 1) * MXU_DIM] for blk in range(nblk)]
    local = _dot(jnp.concatenate(blocks, axis=0).astype(BF16), ltri_ref[...])
    off = jnp.zeros((N_FEAT, 1), F32)
    sufs = [None] * nblk
    for blk in reversed(range(nblk)):
        loc = local[blk * N_FEAT:(blk + 1) * N_FEAT]
        sufs[blk] = loc + off
        off = off + loc[:, 0:1] + blocks[blk][:, 0:1]
    suf_parts = jnp.concatenate(sufs, axis=1)
    ck_rel = suf_parts[0:8] + suf_parts[8:16] + suf_parts[16:24]
    c_hi, c_mid, c_lo = _split3(ck_rel)
    kbt = jnp.concatenate([jnp.ones((N_FEAT, p_len), F32), c_hi, c_mid, c_lo,
                           jnp.zeros((LANES - 2 * N_FEAT, p_len), F32)], axis=0)
    k_all = jnp.concatenate([kt_ref[i].astype(BF16), kbt.astype(BF16)], axis=0)

    lane_q = lax.broadcasted_iota(jnp.int32, (tt, d_attn), 1)
    lane_b = lax.broadcasted_iota(jnp.int32, (tt, LANES), 1)
    q = q_ref[i].astype(F32)
    qb = qb_ref[i].astype(F32)
    stack = []
    for h in range(N_HEADS):
        qm = jnp.where((lane_q >= h * HEAD_DIM) & (lane_q < (h + 1) * HEAD_DIM), q, 0.0)
        bm = jnp.where((lane_b % N_HEADS) == h, qb, 0.0)
        stack.append(jnp.concatenate([qm, bm], axis=1))
    qs = jnp.concatenate(stack, axis=0).astype(BF16)

    s_c = _dot(qs, k_all)

    pad = jnp.zeros((LANES - tt, d_attn + LANES), F32)
    kn = jnp.concatenate([jnp.concatenate([kn_ref[i], kbn_ref[i].astype(F32)], axis=1), pad], axis=0)
    s_n = _dot_nt(qs, kn.astype(BF16))
    row = lax.broadcasted_iota(jnp.int32, s_n.shape, 0)
    col = lax.broadcasted_iota(jnp.int32, s_n.shape, 1)
    s_n = jnp.where(col <= (row % tt), s_n, NEG_INF)

    m = jnp.maximum(jnp.max(s_c, axis=-1, keepdims=True), jnp.max(s_n, axis=-1, keepdims=True))
    p_c = jnp.exp(s_c - m)
    p_n = jnp.exp(s_n - m)
    l = jnp.sum(p_c, axis=-1, keepdims=True) + jnp.sum(p_n, axis=-1, keepdims=True)
    vn = jnp.concatenate([vn_ref[i], jnp.zeros((LANES - tt, d_attn), F32)], axis=0).astype(BF16)
    o = _dot_nt(p_c.astype(BF16), vt_ref[i].astype(BF16)) + _dot(p_n.astype(BF16), vn)
    o = o / l

    out = jnp.zeros((tt, d_attn), F32)
    for h in range(N_HEADS):
        out = jnp.where((lane_q >= h * HEAD_DIM) & (lane_q < (h + 1) * HEAD_DIM), o[h * tt:(h + 1) * tt], out)
    o_ref[i] = out.astype(BF16)


def _attn_sample(q, qb, kn, vn, kbn, kt, vt, clft, ltri):
    nb, tt, d_attn = q.shape
    p_len = kt.shape[2]
    per_req = lambda shape: pl.BlockSpec((SAMPLE_REQS_PER_STEP,) + shape, lambda i: (i, 0, 0))
    return pl.pallas_call(
        _attn_sample_kernel,
        grid=(nb // SAMPLE_REQS_PER_STEP,),
        in_specs=[per_req((tt, d_attn)), per_req((tt, LANES)), per_req((tt, d_attn)), per_req((tt, d_attn)),
                  per_req((tt, LANES)), per_req((d_attn, p_len)), per_req((d_attn, p_len)),
                  per_req((N_HEADS, p_len)), _const_spec(ltri.shape)],
        out_specs=per_req((tt, d_attn)),
        out_shape=jax.ShapeDtypeStruct((nb, tt, d_attn), BF16),
        compiler_params=pltpu.CompilerParams(dimension_semantics=("arbitrary",),
                                             vmem_limit_bytes=VMEM_LIMIT_BYTES),
        name="attn_sample",
    )(q, qb, kn, vn, kbn, kt, vt, clft, ltri)


def _outproj_ffn_body(x_ref, co, at, g1_ref, sh2_ref, sc2_ref, g2_ref, n2g_ref,
                      wo_ref, wg_ref, wu_ref, wd_ref, y_ref, a_ref, vpu_slot=None):
    nb, tt, d = x_ref.shape
    r = nb * tt
    d_conv = co.shape[1]
    mix = _dot(co, wo_ref[0:d_conv, :]) + _dot(at, wo_ref[d_conv:, :])
    x1 = x_ref[...] + g1_ref[:, 0] * mix.reshape(nb, tt, d)
    hb = _adaln_rmsnorm(x1, n2g_ref[...], sc2_ref[:, 0], sh2_ref[:, 0]).reshape(r, d).astype(BF16)
    lead = x1.reshape(r, d)[0:SUBLANES, 0:d_conv]
    n_ff = wg_ref.shape[1] // FF_CHUNK
    chunk = lambda c: slice(c * FF_CHUNK, (c + 1) * FF_CHUNK)
    for c in range(n_ff):
        zero = vpu_slot(c, lead) if vpu_slot is not None else None
        g, u = _dot(hb, wg_ref[:, chunk(c)]), _dot(hb, wu_ref[:, chunk(c)])
        if zero is not None:
            zero = jnp.concatenate([zero] * (FF_CHUNK // LANES), axis=1)
            g = jnp.concatenate([g[:SUBLANES] + zero, g[SUBLANES:]], axis=0)
        a = (g * jax.nn.sigmoid(g)) * u
        a_ref[:, chunk(c)] = a.astype(BF16)
        lead = jnp.concatenate([a[0:SUBLANES]] * (d_conv // FF_CHUNK), axis=1)
    ffn = _dot(a_ref[...], wd_ref[...])
    y_ref[...] = x1 + g2_ref[:, 0] * ffn.reshape(nb, tt, d)


def _outproj_ffn_kernel(x_ref, co_ref, at_ref, g1_ref, sh2_ref, sc2_ref, g2_ref, n2g_ref,
                        wo_ref, wg_ref, wu_ref, wd_ref, y_ref, a_ref):
    nb, tt, _ = x_ref.shape
    co = co_ref[...].reshape(nb * tt, co_ref.shape[2])
    at = at_ref[...].reshape(nb * tt, at_ref.shape[2])
    _outproj_ffn_body(x_ref, co, at, g1_ref, sh2_ref, sc2_ref, g2_ref, n2g_ref,
                      wo_ref, wg_ref, wu_ref, wd_ref, y_ref, a_ref)


def _exact_zero(v):
    return jnp.minimum(jnp.abs(v), 0.0)


def _outproj_ffn_conv_kernel(x_ref, at_ref, un_ref, u0_ref, g1_ref, sh2_ref, sc2_ref, g2_ref, n2g_ref,
                             wo_ref, wg_ref, wu_ref, wd_ref, cw_ref, cb_ref, lng_ref, lnb_ref,
                             y_ref, a_ref, shift_ref, co_ref, wb_ref, *, tiles_per_seq):
    g = pl.program_id(0)
    tm = x_ref.shape[1]
    d_conv = co_ref.shape[1]
    span = tm + HIST - SUBLANES
    n_conv = tm // CONV_ROWS
    n_ff = wg_ref.shape[1] // FF_CHUNK
    cb, ln_g, ln_b = cb_ref[...], lng_ref[...], lnb_ref[...]

    def stage(u, hist):
        shift_ref[0, 0:HIST, :] = hist
        shift_ref[0, HIST:HIST + tm, :] = u
        for r in range(1, SUBLANES):
            shift_ref[r, 0:span, :] = shift_ref[0, r:r + span, :]

    def conv_chunk(c, bias):
        def tap(k):
            off = k + HIST - (CONV_WIDTH - 1)
            row = c * CONV_ROWS + off - off % SUBLANES
            return shift_ref[off % SUBLANES, row:row + CONV_ROWS, :]
        weight = lambda k: jnp.concatenate([wb_ref[k]] * (CONV_ROWS // SUBLANES), axis=0)
        y = _conv_ln_swish(tap, weight, bias, ln_g, ln_b)
        co_ref[c * CONV_ROWS:(c + 1) * CONV_ROWS, :] = y.astype(BF16)
        return y

    @pl.when(g == 0)
    def _():
        for k in range(CONV_WIDTH):
            wb_ref[k] = jnp.broadcast_to(cw_ref[k:k + 1, :], (SUBLANES, d_conv))
        stage(u0_ref[0], jnp.zeros((HIST, d_conv), F32))
        for c in range(n_conv):
            conv_chunk(c, cb)

    def conv_slot(c, lead):
        if c == 0:
            tail = shift_ref[0, tm:tm + HIST, :]
            starts_seq = (g + 1) % tiles_per_seq == 0
            stage(un_ref[0], jnp.where(starts_seq, 0.0, tail))
        bias = jnp.concatenate([cb + _exact_zero(lead)] * (CONV_ROWS // SUBLANES), axis=0)
        zero = jnp.zeros((SUBLANES, LANES), F32)
        for i in range(c * n_conv // n_ff, (c + 1) * n_conv // n_ff):
            zero = zero + _exact_zero(conv_chunk(i, bias)[0:SUBLANES, 0:LANES])
        return zero

    co = co_ref[...]
    _outproj_ffn_body(x_ref, co, at_ref[0], g1_ref, sh2_ref, sc2_ref, g2_ref, n2g_ref,
                      wo_ref, wg_ref, wu_ref, wd_ref, y_ref, a_ref, vpu_slot=conv_slot)


def _outproj_ffn_conv(x, at, u, mod4, mod_row0, n2g, wo, wg, wu, wd, conv_w, conv_b, ln_g, ln_b):
    b, s, d = x.shape
    tm = ROW_TILE
    d_conv = u.shape[2]
    d_attn = at.shape[2]
    tps = s // tm
    n_tiles = b * tps
    tile = lambda g: (g // tps, g % tps, 0)
    next_tile = lambda g: tile(jnp.minimum(g + 1, n_tiles - 1))
    mod_spec = lambda j: pl.BlockSpec((1, 1, 1, d), lambda g: (mod_row0 + g // tps, j, 0, 0))
    consts = (n2g, wo, wg, wu, wd, conv_w, conv_b, ln_g, ln_b)
    return pl.pallas_call(
        functools.partial(_outproj_ffn_conv_kernel, tiles_per_seq=tps),
        grid=(n_tiles,),
        in_specs=[pl.BlockSpec((1, tm, d), tile), pl.BlockSpec((1, tm, d_attn), tile),
                  pl.BlockSpec((1, tm, d_conv), next_tile),
                  pl.BlockSpec((1, tm, d_conv), lambda g: (0, 0, 0), pipeline_mode=pl.Buffered(1)),
                  mod_spec(2), mod_spec(3), mod_spec(4), mod_spec(5)] + [_const_spec(c.shape) for c in consts],
        out_specs=pl.BlockSpec((1, tm, d), tile),
        out_shape=jax.ShapeDtypeStruct((b, s, d), F32),
        scratch_shapes=[pltpu.VMEM((tm, wg.shape[1]), BF16),
                        pltpu.VMEM((SUBLANES, tm + HIST, d_conv), F32),
                        pltpu.VMEM((tm, d_conv), BF16),
                        pltpu.VMEM((CONV_WIDTH, SUBLANES, d_conv), F32)],
        compiler_params=pltpu.CompilerParams(dimension_semantics=("arbitrary",),
                                             vmem_limit_bytes=VMEM_LIMIT_BYTES),
        name="outproj_ffn_conv",
    )(x, at, u, u, mod4, mod4, mod4, mod4, *consts)


def _outproj_ffn(x, co, at, mod4, n2g, wo, wg, wu, wd, bb, tt):
    nb, s, d = x.shape
    d_conv = co.shape[2]
    d_attn = at.shape[2]
    mod_spec = lambda j: pl.BlockSpec((bb, 1, 1, d), lambda i, t: (i, j, 0, 0))
    row_spec = lambda w: pl.BlockSpec((bb, tt, w), lambda i, t: (i, t, 0))
    return pl.pallas_call(
        _outproj_ffn_kernel,
        grid=(nb // bb, s // tt),
        in_specs=[row_spec(d), row_spec(d_conv), row_spec(d_attn),
                  mod_spec(2), mod_spec(3), mod_spec(4), mod_spec(5),
                  _const_spec(n2g.shape), _const_spec(wo.shape), _const_spec(wg.shape),
                  _const_spec(wu.shape), _const_spec(wd.shape)],
        out_specs=row_spec(d),
        out_shape=jax.ShapeDtypeStruct((nb, s, d), F32),
        scratch_shapes=[pltpu.VMEM((bb * tt, wg.shape[1]), BF16)],
        compiler_params=pltpu.CompilerParams(dimension_semantics=("arbitrary", "arbitrary"),
                                             vmem_limit_bytes=VMEM_LIMIT_BYTES),
        name="outproj_ffn",
    )(x, co, at, mod4, mod4, mod4, mod4, n2g, wo, wg, wu, wd)


def _tri(n, rel):
    i = lax.broadcasted_iota(jnp.int32, (n, n), 0)
    j = lax.broadcasted_iota(jnp.int32, (n, n), 1)
    return rel(i, j).astype(BF16)


def _layer(xp, xs, cache_k, cache_v, cache_logf, state_conv, c_all, w_ada, b_ada, norm1_g, w_in, b_f,
           q_norm_g, k_norm_g, conv_w, conv_b, conv_ln_g, conv_ln_b, w_out, norm2_g, w_gate, w_up, w_down):
    b, s, d = xp.shape
    nb, tt, _ = xs.shape
    d_conv = conv_w.shape[1]
    d_attn = N_HEADS * HEAD_DIM
    d_ff = w_gate.shape[1]
    p_len = cache_k.shape[1]

    mod4 = _modulation(c_all, w_ada, b_ada).reshape(nb + b, N_MOD, 1, d)

    wglu = w_in[:, :2 * d_conv].astype(BF16)
    wqkvf_t = jnp.pad(w_in[:, 2 * d_conv:].T, ((0, 2 * SUBLANES - N_HEADS), (0, 0))).astype(BF16)
    bf_col = b_f.reshape(N_HEADS, 1)
    qg_col = q_norm_g.reshape(d_attn, 1)
    kg_col = k_norm_g.reshape(d_attn, 1)
    row = lambda v: v.reshape(1, -1)
    n1g, n2g, cb, ln_g, ln_b = row(norm1_g), row(norm2_g), row(conv_b), row(conv_ln_g), row(conv_ln_b)
    wo = w_out.astype(BF16)
    assert d_ff % FF_CHUNK == 0
    wg, wu, wd = w_gate.astype(BF16), w_up.astype(BF16), w_down.astype(BF16)
    utri = _tri(MXU_DIM, lambda i, j: i <= j)
    ltri = _tri(MXU_DIM, lambda i, j: i > j)
    bdtri = _tri(nb * tt, lambda i, j: (i // tt == j // tt) & (j <= i))

    kt, vt, lft, cst, qt, qft, k, kb, u = _inproj_prompt(
        xp, mod4, nb, n1g, wglu, wqkvf_t, bf_col, qg_col, kg_col, utri)
    at = _attn_prompt(qt, qft, k, kb, vt)
    yp = _outproj_ffn_conv(xp, at, u, mod4, nb, n2g, wo, wg, wu, wd, conv_w, cb, ln_g, ln_b)
    k_p = kt.reshape(b, N_HEADS, HEAD_DIM, s).transpose(0, 3, 1, 2)
    v_p = vt.reshape(b, N_HEADS, HEAD_DIM, s).transpose(0, 3, 1, 2)
    lf_p = lft.transpose(0, 2, 1)

    state_t = state_conv.transpose(1, 0, 2)
    k_s, v_s, lf_s, nst, q_s, qb_s, kb_s, co_s = _inproj_sample(
        xs, mod4, n1g, wglu, wqkvf_t, bf_col, qg_col, kg_col, bdtri, conv_w, cb, ln_g, ln_b, state_t)
    r3 = lambda a: a.reshape(nb, tt, a.shape[-1])
    ckt = cache_k.transpose(0, 2, 3, 1).reshape(nb, d_attn, p_len)
    cvt = cache_v.transpose(0, 2, 3, 1).reshape(nb, d_attn, p_len)
    clft = cache_logf.transpose(0, 2, 1)
    at_s = _attn_sample(r3(q_s), r3(qb_s), r3(k_s), r3(v_s), r3(kb_s), ckt, cvt, clft, ltri)
    ys = _outproj_ffn(xs, r3(co_s), at_s, mod4, n2g, wo, wg, wu, wd, nb, tt)
    k_sn = k_s.reshape(nb, tt, N_HEADS, HEAD_DIM)
    v_sn = v_s.reshape(nb, tt, N_HEADS, HEAD_DIM)
    lf_sn = lf_s[:, :N_HEADS].reshape(nb, tt, N_HEADS)
    return yp, ys, (k_p, v_p, lf_p, cst), (k_sn, v_sn, lf_sn, nst.transpose(1, 0, 2))


def kernel(x_prompt, x_sample, cache_k, cache_v, cache_logf, state_conv, c_prompt, c_sample, w_ada, b_ada,
           norm1_g, w_in, b_f, q_norm_g, k_norm_g, conv_w, conv_b, conv_ln_g, conv_ln_b, w_out, norm2_g,
           w_gate, w_up, w_down):
    depth = w_ada.shape[0]
    c_all = jnp.concatenate([c_sample, c_prompt], axis=0)
    yp, ys = x_prompt, x_sample
    st_p, st_s = [], []
    for l in range(depth):
        yp, ys, sp, ss = _layer(
            yp, ys, cache_k[l], cache_v[l], cache_logf[l], state_conv[l], c_all, w_ada[l], b_ada[l],
            norm1_g[l], w_in[l], b_f[l], q_norm_g[l], k_norm_g[l], conv_w[l], conv_b[l], conv_ln_g[l],
            conv_ln_b[l], w_out[l], norm2_g[l], w_gate[l], w_up[l], w_down[l])
        st_p.append(sp)
        st_s.append(ss)
    stack = lambda xs: xs[0][None] if len(xs) == 1 else jnp.stack(xs)
    outs_p = [stack([s[i] for s in st_p]) for i in range(4)]
    outs_s = [stack([s[i] for s in st_s]) for i in range(4)]
    return (yp, ys, *outs_p, *outs_s)
```
